```python
import math
import jax, jax.numpy as jnp
from jax import lax
import numpy as np

D_MODEL = 1024
BATCH = 4
SEQ = 4096
DEPTH = 2
DEC_BATCH = 128
DEC_SEQ = 1
PAST_LEN = 2048
PAGE_SIZE = 128

F32 = jnp.float32
EPS = 1e-6
NEG = -1e30
N_EVEN = (DEPTH + 1) // 2
N_ODD = DEPTH // 2

H_A = 8
D_A = 64
ROT_DIM = D_A // 4
ROPE_THETA = 500000.0
MOBA_BLOCK = 256
MOBA_TOPK = 3
Q_BLOCK = 32
G_B = 8
D_BG = 64
D_B = G_B * D_BG
CHUNK_B = 128
H_C = 4
DK_C = 64
DV_C = 128
GATE_RANK = 16
GATE_NORM = 16.0
H_D = 4
DK_D = 64
DV_D = 128
CONV_W = 4
C_CONV = 2 * H_D * DK_D + H_D * DV_D
CHUNK_LIN = 64
D_FF = 2816
N_EXPERTS = 8
TOP_K = 2
D_FF_E = 3584

SIZES_EVEN = (H_A * D_A, H_A * D_A, H_A * D_A, D_B, D_B)
SIZES_ODD = (H_C * DK_C, H_C * DK_C, H_C * DV_C, GATE_RANK, H_C * DV_C,
             C_CONV, H_D, H_D, H_D * DV_D)
IN_EVEN = sum(SIZES_EVEN)
IN_ODD = sum(SIZES_ODD)
SPLIT_EVEN = tuple(sum(SIZES_EVEN[:i + 1]) for i in range(len(SIZES_EVEN) - 1))
SPLIT_ODD = tuple(sum(SIZES_ODD[:i + 1]) for i in range(len(SIZES_ODD) - 1))
MIX_EVEN = H_A * D_A + D_B
MIX_ODD = H_C * DV_C + H_D * DV_D

kernel_name = 'hybrid_moba_gmlp_gla_gdn_moe_step'


def rmsnorm(x, g):
    xf = x.astype(F32)
    y = xf * lax.rsqrt(jnp.mean(xf * xf, axis=-1, keepdims=True) + EPS)
    return (y * g.astype(F32)).astype(x.dtype)


def layernorm(x, g, b):
    xf = x.astype(F32)
    mu = jnp.mean(xf, axis=-1, keepdims=True)
    var = jnp.mean(jnp.square(xf - mu), axis=-1, keepdims=True)
    return ((xf - mu) * lax.rsqrt(var + EPS) * g.astype(F32) + b.astype(F32)).astype(x.dtype)


def l2norm(x):
    xf = x.astype(F32)
    return (xf * lax.rsqrt(jnp.sum(xf * xf, axis=-1, keepdims=True) + EPS)).astype(x.dtype)


def swiglu(h, w_gate, w_up, w_down):
    return (jax.nn.silu(h @ w_gate) * (h @ w_up)) @ w_down


def partial_rope(x, pos):
    half = ROT_DIM // 2
    inv = ROPE_THETA ** (-jnp.arange(half, dtype=F32) / half)
    ang = pos.astype(F32)[:, None] * inv[None, :]
    cos = jnp.cos(ang)[None, :, None, :]
    sin = jnp.sin(ang)[None, :, None, :]
    xf = x.astype(F32)
    x1 = xf[..., :half]
    x2 = xf[..., half:ROT_DIM]
    out = jnp.concatenate([x1 * cos - x2 * sin, x2 * cos + x1 * sin, xf[..., ROT_DIM:]], axis=-1)
    return out.astype(x.dtype)


def moba_attention(q, k_all, v_all, q_pos):
    B, T, H, D = q.shape
    L = k_all.shape[1]
    nb = -(-L // MOBA_BLOCK)
    pad = nb * MOBA_BLOCK - L
    kp = jnp.pad(k_all, ((0, 0), (0, pad), (0, 0), (0, 0)))
    vp = jnp.pad(v_all, ((0, 0), (0, pad), (0, 0), (0, 0)))
    kb = kp.reshape(B, nb, MOBA_BLOCK, H, D).transpose(0, 3, 1, 2, 4)
    vb = vp.reshape(B, nb, MOBA_BLOCK, H, D).transpose(0, 3, 1, 2, 4)
    kmean = jnp.mean(kb.astype(F32), axis=3)
    k_sel = min(MOBA_TOPK, nb)
    scale = D_A ** -0.5
    qb = min(Q_BLOCK, T)
    nq = -(-T // qb)
    tpad = nq * qb - T
    qs = jnp.pad(q, ((0, 0), (0, tpad), (0, 0), (0, 0))).reshape(B, nq, qb, H, D).transpose(1, 0, 3, 2, 4)
    ps = jnp.pad(q_pos, (0, tpad), mode='edge').reshape(nq, qb)
    bidx = jnp.arange(B)[:, None, None, None]
    hidx = jnp.arange(H)[None, :, None, None]
    blk_ids = jnp.arange(nb)
    offs = jnp.arange(MOBA_BLOCK)

    def one_query_block(args):
        qblk, pblk = args
        own = pblk // MOBA_BLOCK
        gate = jnp.einsum('bhqd,bhnd->bhqn', qblk.astype(F32), kmean)
        eligible = blk_ids[None, :] < own[:, None]
        gate = jnp.where(eligible, gate, NEG)
        _, sel = lax.top_k(gate, k_sel)
        sel_ok = sel < own[:, None]
        own_b = jnp.broadcast_to(own[:, None], sel.shape[:-1] + (1,))
        idx = jnp.concatenate([sel, own_b], axis=-1)
        ok = jnp.concatenate([sel_ok, jnp.ones(sel.shape[:-1] + (1,), bool)], axis=-1)
        kg = kb[bidx, hidx, idx]
        vg = vb[bidx, hidx, idx]
        key_pos = idx[..., None] * MOBA_BLOCK + offs
        mask = ok[..., None] & (key_pos <= pblk[:, None, None])
        s = jnp.einsum('bhqd,bhqnkd->bhqnk', qblk, kg).astype(F32) * scale
        s = jnp.where(mask, s, NEG)
        p = jax.nn.softmax(s.reshape(s.shape[:3] + (-1,)), axis=-1).reshape(s.shape)
        return jnp.einsum('bhqnk,bhqnkd->bhqd', p.astype(vg.dtype), vg)

    out = lax.map(one_query_block, (qs, ps))
    return out.transpose(1, 0, 3, 2, 4).reshape(B, nq * qb, H, D)[:, :T]


def chunk_spatial_gating(u, v, w_sp, b_sp):
    B, T, G, DG = v.shape
    n = -(-T // CHUNK_B)
    vc = jnp.pad(v, ((0, 0), (0, n * CHUNK_B - T), (0, 0), (0, 0))).reshape(B, n, CHUNK_B, G, DG)
    w = jnp.where(jnp.tril(jnp.ones((CHUNK_B, CHUNK_B), bool)), w_sp, 0.0)
    mixed = jnp.einsum('gij,bnjgd->bnigd', w.astype(v.dtype), vc) + b_sp.T[None, None, :, :, None]
    mixed = mixed.reshape(B, n * CHUNK_B, G, DG)[:, :T]
    return u * mixed


def _to_chunks(x, c):
    B, T = x.shape[0], x.shape[1]
    n = -(-T // c)
    x = jnp.pad(x, [(0, 0), (0, n * c - T)] + [(0, 0)] * (x.ndim - 2))
    x = x.reshape((B, n, c) + x.shape[2:])
    return jnp.transpose(x, (1, 0, 3, 2) + tuple(range(4, x.ndim)))


def _from_chunks(o, T):
    n, B, H, c, V = o.shape
    return jnp.transpose(o, (1, 0, 3, 2, 4)).reshape(B, n * c, H, V)[:, :T]


def gla_chunked(q, k, v, log_a, s0):
    T = q.shape[1]
    c = min(CHUNK_LIN, T)
    tril = jnp.tril(jnp.ones((c, c), bool))
    qc, kc, vc, ac = [_to_chunks(t.astype(F32), c) for t in (q, k, v, log_a)]
    bcum = jnp.cumsum(ac, axis=3)
    b_end = bcum[:, :, :, -1:, :]
    q_in = qc * jnp.exp(bcum)
    k_in = kc * jnp.exp(-bcum)
    k_end = kc * jnp.exp(b_end - bcum)
    scores = jnp.where(tril, jnp.einsum('nbhik,nbhjk->nbhij', q_in, k_in), 0.0)
    o_intra = jnp.einsum('nbhij,nbhjv->nbhiv', scores, vc)

    def step(S, xs):
        q_i, k_i, v_i, d_i = xs
        o_i = jnp.einsum('bhik,bhkv->bhiv', q_i, S)
        S = S * d_i[..., None] + jnp.einsum('bhjk,bhjv->bhkv', k_i, v_i)
        return S, o_i

    S, o_inter = lax.scan(step, s0.astype(F32), (q_in, k_end, vc, jnp.exp(b_end[:, :, :, 0, :])))
    return _from_chunks(o_intra + o_inter, T).astype(q.dtype), S.astype(s0.dtype)


def gated_delta_chunked(q, k, v, log_a, beta, s0):
    T = q.shape[1]
    dv = v.shape[-1]
    c = min(CHUNK_LIN, T)
    tril = jnp.tril(jnp.ones((c, c), bool))
    strict = jnp.tril(jnp.ones((c, c), bool), k=-1)
    qc, kc, vc = [_to_chunks(t.astype(F32), c) for t in (q, k, v)]
    gc, bc = [_to_chunks(t.astype(F32), c) for t in (log_a, beta)]
    g = jnp.cumsum(gc, axis=3)
    decay = jnp.exp(jnp.where(tril, g[..., :, None] - g[..., None, :], NEG))
    k_beta = kc * bc[..., None]
    a_mat = jnp.where(strict, jnp.einsum('nbhik,nbhjk->nbhij', k_beta, kc) * decay, 0.0)
    rhs = jnp.concatenate([vc * bc[..., None], k_beta * jnp.exp(g)[..., None]], axis=-1)
    sol = lax.linalg.triangular_solve(a_mat + jnp.eye(c, dtype=F32), rhs, left_side=True, lower=True,
                                      unit_diagonal=True)
    u_val, w_val = sol[..., :dv], sol[..., dv:]
    qk = jnp.where(tril, jnp.einsum('nbhik,nbhjk->nbhij', qc, kc) * decay, 0.0)
    q_dec = qc * jnp.exp(g)[..., None]
    k_end = kc * jnp.exp(g[..., -1:] - g)[..., None]
    g_end = jnp.exp(g[..., -1])

    def step(S, xs):
        u_i, w_i, qk_i, qd_i, ke_i, ge_i = xs
        v_new = u_i - jnp.einsum('bhck,bhkv->bhcv', w_i, S)
        o_i = jnp.einsum('bhck,bhkv->bhcv', qd_i, S) + jnp.einsum('bhij,bhjv->bhiv', qk_i, v_new)
        S = S * ge_i[..., None, None] + jnp.einsum('bhck,bhcv->bhkv', ke_i, v_new)
        return S, o_i

    S, o = lax.scan(step, s0.astype(F32), (u_val, w_val, qk, q_dec, k_end, g_end))
    return _from_chunks(o, T).astype(q.dtype), S.astype(s0.dtype)


def short_conv(x, buf, w):
    T = x.shape[1]
    xx = jnp.concatenate([buf.astype(x.dtype), x], axis=1)
    y = xx[:, 0:T] * w[0]
    for j in range(1, CONV_W):
        y = y + xx[:, j:j + T] * w[j]
    return jax.nn.silu(y), xx[:, T:]


def moe_ffn(h, router, w_gate, w_up, w_down):
    logits = jnp.einsum('btd,de->bte', h, router).astype(F32)
    top_v, top_i = lax.top_k(logits, TOP_K)
    gates = jax.nn.softmax(top_v, axis=-1)
    combine = jnp.sum(jax.nn.one_hot(top_i, N_EXPERTS, dtype=F32) * gates[..., None], axis=-2)
    out = jnp.zeros_like(h)
    for e in range(N_EXPERTS):
        out = out + combine[..., e:e + 1].astype(h.dtype) * swiglu(h, w_gate[e], w_up[e], w_down[e])
    return out


def even_layer(x, pos, past_kv, norm_mix, w_in, ln_g, ln_b, w_sp, b_sp, w_out, norm_ffn, w_gate, w_up, w_down):
    B, T, _ = x.shape
    h = rmsnorm(x, norm_mix)
    z = h @ w_in
    q, k, v, zu, zv = jnp.split(z, SPLIT_EVEN, axis=-1)
    q = partial_rope(q.reshape(B, T, H_A, D_A), pos)
    k = partial_rope(k.reshape(B, T, H_A, D_A), pos)
    v = v.reshape(B, T, H_A, D_A)
    kv_new = jnp.stack([k, v], axis=2)
    kv_all = jnp.concatenate([past_kv.astype(kv_new.dtype), kv_new], axis=1)
    a_out = moba_attention(q, kv_all[:, :, 0], kv_all[:, :, 1], pos).reshape(B, T, H_A * D_A)
    u = jax.nn.gelu(zu, approximate=False)
    gv = layernorm(jax.nn.gelu(zv, approximate=False), ln_g, ln_b)
    b_out = chunk_spatial_gating(u.reshape(B, T, G_B, D_BG), gv.reshape(B, T, G_B, D_BG), w_sp, b_sp)
    x = x + jnp.concatenate([a_out, b_out.reshape(B, T, D_B)], axis=-1) @ w_out
    x = x + swiglu(rmsnorm(x, norm_ffn), w_gate, w_up, w_down)
    n_open = T - ((T - 1) // CHUNK_B) * CHUNK_B
    return x, kv_new, gv[:, T - n_open:]


def odd_layer(x, s_gla, s_delta, conv_buf, norm_mix, w_in, gla_w_gate2, gla_b_gate2, gla_norm, delta_conv,
              delta_a_log, delta_dt_bias, delta_norm, w_out, norm_ffn, router, w_gate, w_up, w_down):
    B, T, _ = x.shape
    h = rmsnorm(x, norm_mix)
    z = h @ w_in
    cq, ck, cv, c_lr, c_r, d_qkv, d_a, d_b, d_g = jnp.split(z, SPLIT_ODD, axis=-1)
    q = cq.reshape(B, T, H_C, DK_C) * (DK_C ** -0.5)
    k = ck.reshape(B, T, H_C, DK_C)
    v = cv.reshape(B, T, H_C, DV_C)
    log_a = jax.nn.log_sigmoid((c_lr @ gla_w_gate2 + gla_b_gate2).astype(F32)).reshape(B, T, H_C, DK_C) / GATE_NORM
    o_c, s_gla_new = gla_chunked(q, k, v, log_a, s_gla)
    o_c = (rmsnorm(o_c, gla_norm) * jax.nn.silu(c_r.reshape(B, T, H_C, DV_C))).reshape(B, T, H_C * DV_C)
    qkv, conv_new = short_conv(d_qkv, conv_buf, delta_conv)
    dq, dk, dv = jnp.split(qkv, (H_D * DK_D, 2 * H_D * DK_D), axis=-1)
    dq = l2norm(dq.reshape(B, T, H_D, DK_D)) * (DK_D ** -0.5)
    dk = l2norm(dk.reshape(B, T, H_D, DK_D))
    dv = dv.reshape(B, T, H_D, DV_D)
    beta = jax.nn.sigmoid(d_b.astype(F32))
    log_alpha = -jnp.exp(delta_a_log.astype(F32)) * jax.nn.softplus(d_a.astype(F32) + delta_dt_bias.astype(F32))
    o_d, s_delta_new = gated_delta_chunked(dq, dk, dv, log_alpha, beta, s_delta)
    o_d = (rmsnorm(o_d, delta_norm) * jax.nn.silu(d_g.reshape(B, T, H_D, DV_D))).reshape(B, T, H_D * DV_D)
    x = x + jnp.concatenate([o_c, o_d], axis=-1) @ w_out
    x = x + moe_ffn(rmsnorm(x, norm_ffn), router, w_gate, w_up, w_down)
    return x, s_gla_new, s_delta_new, conv_new


def setup_inputs(seed: int = 0) -> dict:
    key = jax.random.key(seed)
    keys = iter(jax.random.split(key, 64))

    def nrm(shape, scale):
        return jax.random.normal(next(keys), shape, F32) * scale

    def gain(shape):
        return 1.0 + nrm(shape, 0.02)

    n_pages = PAST_LEN // PAGE_SIZE
    n_used = DEC_BATCH * n_pages
    n_pool = n_used + n_used // 4
    x_prompt = nrm((BATCH, SEQ, D_MODEL), 1.0)
    x_sample = nrm((DEC_BATCH, DEC_SEQ, D_MODEL), 1.0)
    cache_kv = nrm((N_EVEN, n_pool, PAGE_SIZE, 2, H_A, D_A), 1.0)
    state_gla = nrm((N_ODD, DEC_BATCH, H_C, DK_C, DV_C), 0.1)
    state_delta = nrm((N_ODD, DEC_BATCH, H_D, DK_D, DV_D), 0.1)
    state_conv = nrm((N_ODD, DEC_BATCH, CONV_W - 1, C_CONV), 1.0)
    page_table = jax.random.permutation(next(keys), n_pool)[:n_used].reshape(DEC_BATCH, n_pages).astype(jnp.int32)
    dt = jnp.exp(jax.random.uniform(next(keys), (N_ODD, H_D), F32, math.log(1e-3), math.log(1e-1)))
    return {
        'x_prompt': x_prompt,
        'x_sample': x_sample,
        'cache_kv': cache_kv,
        'state_gla': state_gla,
        'state_delta': state_delta,
        'state_conv': state_conv,
        'page_table': page_table,
        'ev_norm_mix': gain((N_EVEN, D_MODEL)),
        'ev_w_in': nrm((N_EVEN, D_MODEL, IN_EVEN), D_MODEL ** -0.5),
        'ev_gmlp_ln_g': gain((N_EVEN, D_B)),
        'ev_gmlp_ln_b': nrm((N_EVEN, D_B), 0.02),
        'ev_w_spatial': nrm((N_EVEN, G_B, CHUNK_B, CHUNK_B), CHUNK_B ** -0.5),
        'ev_b_spatial': 1.0 + nrm((N_EVEN, G_B, CHUNK_B), 0.1),
        'ev_w_out': nrm((N_EVEN, MIX_EVEN, D_MODEL), MIX_EVEN ** -0.5),
        'ev_norm_ffn': gain((N_EVEN, D_MODEL)),
        'ev_w_gate': nrm((N_EVEN, D_MODEL, D_FF), D_MODEL ** -0.5),
        'ev_w_up': nrm((N_EVEN, D_MODEL, D_FF), D_MODEL ** -0.5),
        'ev_w_down': nrm((N_EVEN, D_FF, D_MODEL), D_FF ** -0.5),
        'od_norm_mix': gain((N_ODD, D_MODEL)),
        'od_w_in': nrm((N_ODD, D_MODEL, IN_ODD), D_MODEL ** -0.5),
        'od_gla_w_gate2': nrm((N_ODD, GATE_RANK, H_C * DK_C), GATE_RANK ** -0.5),
        'od_gla_b_gate2': nrm((N_ODD, H_C * DK_C), 0.1),
        'od_gla_norm': gain((N_ODD, DV_C)),
        'od_delta_conv': nrm((N_ODD, CONV_W, C_CONV), CONV_W ** -0.5),
        'od_delta_a_log': jnp.log(jax.random.uniform(next(keys), (N_ODD, H_D), F32, 1.0, 16.0)),
        'od_delta_dt_bias': dt + jnp.log(-jnp.expm1(-dt)),
        'od_delta_norm': gain((N_ODD, DV_D)),
        'od_w_out': nrm((N_ODD, MIX_ODD, D_MODEL), MIX_ODD ** -0.5),
        'od_norm_ffn': gain((N_ODD, D_MODEL)),
        'od_router': nrm((N_ODD, D_MODEL, N_EXPERTS), D_MODEL ** -0.5),
        'od_w_gate': nrm((N_ODD, N_EXPERTS, D_MODEL, D_FF_E), D_MODEL ** -0.5),
        'od_w_up': nrm((N_ODD, N_EXPERTS, D_MODEL, D_FF_E), D_MODEL ** -0.5),
        'od_w_down': nrm((N_ODD, N_EXPERTS, D_FF_E, D_MODEL), D_FF_E ** -0.5),
        'norm_final': gain((D_MODEL,)),
    }


def reference(x_prompt, x_sample, cache_kv, state_gla, state_delta, state_conv, page_table,
              ev_norm_mix, ev_w_in, ev_gmlp_ln_g, ev_gmlp_ln_b, ev_w_spatial, ev_b_spatial, ev_w_out,
              ev_norm_ffn, ev_w_gate, ev_w_up, ev_w_down,
              od_norm_mix, od_w_in, od_gla_w_gate2, od_gla_b_gate2, od_gla_norm, od_delta_conv,
              od_delta_a_log, od_delta_dt_bias, od_delta_norm, od_w_out, od_norm_ffn, od_router,
              od_w_gate, od_w_up, od_w_down, norm_final):
    Bp, Tp, _ = x_prompt.shape
    Bs, Ts, _ = x_sample.shape
    past_len = page_table.shape[1] * PAGE_SIZE
    pos_p = jnp.arange(Tp, dtype=jnp.int32)
    pos_s = past_len + jnp.arange(Ts, dtype=jnp.int32)
    xp, xs = x_prompt, x_sample
    kv_p, kv_s, gv_p, gv_s = [], [], [], []
    gla_p, gla_s, dl_p, dl_s, cv_p, cv_s = [], [], [], [], [], []
    for layer in range(DEPTH):
        i = layer // 2
        if layer % 2 == 0:
            w = (ev_norm_mix[i], ev_w_in[i], ev_gmlp_ln_g[i], ev_gmlp_ln_b[i], ev_w_spatial[i], ev_b_spatial[i],
                 ev_w_out[i], ev_norm_ffn[i], ev_w_gate[i], ev_w_up[i], ev_w_down[i])
            empty = jnp.zeros((Bp, 0, 2, H_A, D_A), cache_kv.dtype)
            past = cache_kv[i][page_table].reshape(Bs, past_len, 2, H_A, D_A)
            xp, r, g = even_layer(xp, pos_p, empty, *w)
            kv_p.append(r)
            gv_p.append(g)
            xs, r, g = even_layer(xs, pos_s, past, *w)
            kv_s.append(r)
            gv_s.append(g)
        else:
            w = (od_norm_mix[i], od_w_in[i], od_gla_w_gate2[i], od_gla_b_gate2[i], od_gla_norm[i], od_delta_conv[i],
                 od_delta_a_log[i], od_delta_dt_bias[i], od_delta_norm[i], od_w_out[i], od_norm_ffn[i], od_router[i],
                 od_w_gate[i], od_w_up[i], od_w_down[i])
            z_gla = jnp.zeros((Bp, H_C, DK_C, DV_C), state_gla.dtype)
            z_delta = jnp.zeros((Bp, H_D, DK_D, DV_D), state_delta.dtype)
            z_conv = jnp.zeros((Bp, CONV_W - 1, C_CONV), state_conv.dtype)
            xp, a, b, c = odd_layer(xp, z_gla, z_delta, z_conv, *w)
            gla_p.append(a)
            dl_p.append(b)
            cv_p.append(c)
            xs, a, b, c = odd_layer(xs, state_gla[i], state_delta[i], state_conv[i], *w)
            gla_s.append(a)
            dl_s.append(b)
            cv_s.append(c)
    y_prompt = rmsnorm(xp, norm_final)
    y_sample = rmsnorm(xs, norm_final)
    return (y_prompt, y_sample, jnp.stack(kv_p), jnp.stack(kv_s), jnp.stack(gv_p), jnp.stack(gv_s),
            jnp.stack(gla_p), jnp.stack(gla_s), jnp.stack(dl_p), jnp.stack(dl_s), jnp.stack(cv_p), jnp.stack(cv_s))
```

```python
import functools
import math

import jax
import jax.numpy as jnp
from jax import lax
from jax.experimental import pallas as pl
from jax.experimental.pallas import tpu as pltpu

F32 = jnp.float32
BF16 = jnp.bfloat16
HI = lax.Precision.HIGHEST
EPS = 1e-6
NEG = -1e30

D_MODEL = 1024
PAGE_SIZE = 128
H_A, D_A = 8, 64
ROT_DIM = D_A // 4
ROPE_THETA = 500000.0
MOBA_BLOCK = 256
MOBA_TOPK = 3
G_B, D_BG = 8, 64
D_B = G_B * D_BG
CHUNK_B = 128
H_C, DK_C, DV_C = 4, 64, 128
GATE_RANK = 16
GATE_NORM = 16.0
H_D, DK_D, DV_D = 4, 64, 128
CONV_W = 4
C_CONV = 2 * H_D * DK_D + H_D * DV_D
CHUNK_LIN = 64
N_EXPERTS = 8
TOP_K = 2
QKV_A = H_A * D_A
IN_EVEN = 3 * QKV_A + 2 * D_B
LANE = 128
VMEM_LIMIT = 56 * 1024 * 1024


def _cparams(*sem):
    return pltpu.CompilerParams(dimension_semantics=sem, vmem_limit_bytes=VMEM_LIMIT)


def _rms(x, g):
    return x * lax.rsqrt(jnp.mean(x * x, axis=-1, keepdims=True) + EPS) * g


def _gelu(x):
    return 0.5 * x * (1.0 + lax.erf(x * (2.0 ** -0.5)))


def _silu(x):
    return x * jax.nn.sigmoid(x)


def _softplus(x):
    return jnp.maximum(x, 0.0) + jnp.log1p(jnp.exp(-jnp.abs(x)))


def _dot(a, b):
    return jnp.dot(a, b, preferred_element_type=F32)


def _dot_nt(a, b):
    return lax.dot_general(a, b, (((1,), (1,)), ((), ())), preferred_element_type=F32)


def _dot_tn(a, b):
    return lax.dot_general(a, b, (((0,), (0,)), ((), ())), preferred_element_type=F32)


def _norm_matmul_kernel(x_ref, g_ref, w_ref, o_ref):
    h = _rms(x_ref[...], g_ref[...]).astype(BF16)
    o_ref[...] = _dot(h, w_ref[...])


def _norm_matmul(x, g, w, tm):
    m, d = x.shape
    n = w.shape[1]
    return pl.pallas_call(
        _norm_matmul_kernel,
        grid=(m // tm,),
        in_specs=[pl.BlockSpec((tm, d), lambda i: (i, 0)),
                  pl.BlockSpec((1, d), lambda i: (0, 0)),
                  pl.BlockSpec((d, n), lambda i: (0, 0))],
        out_specs=pl.BlockSpec((tm, n), lambda i: (i, 0)),
        out_shape=jax.ShapeDtypeStruct((m, n), F32),
        compiler_params=_cparams("parallel"),
        name="norm_matmul",
    )(x, g, w)


def _rope_tables(pos):
    half = ROT_DIM // 2
    inv = ROPE_THETA ** (-jnp.arange(half, dtype=F32) / half)
    ang = pos.astype(F32)[:, None] * inv[None, :]
    cos, sin = jnp.cos(ang), jnp.sin(ang)
    t = pos.shape[0]
    one = jnp.ones((t, D_A - ROT_DIM), F32)
    zero_h = jnp.zeros((t, half), F32)
    zero_r = jnp.zeros((t, D_A - ROT_DIM), F32)
    c = jnp.concatenate([cos, cos, one], axis=1)
    s_up = jnp.concatenate([-sin, zero_h, zero_r], axis=1)
    s_dn = jnp.concatenate([zero_h, sin, zero_r], axis=1)
    rep = LANE // D_A
    return jnp.tile(c, (1, rep)), jnp.tile(s_up, (1, rep)), jnp.tile(s_dn, (1, rep))


def _rope(x, c, s_up, s_dn):
    half = ROT_DIM // 2
    outs = []
    for j in range(x.shape[1] // LANE):
        xs = x[:, j * LANE:(j + 1) * LANE]
        up = pltpu.roll(xs, LANE - half, 1)
        dn = pltpu.roll(xs, half, 1)
        outs.append(xs * c + up * s_up + dn * s_dn)
    return jnp.concatenate(outs, axis=1)


def _layernorm(x, g, b):
    mu = jnp.mean(x, axis=-1, keepdims=True)
    xc = x - mu
    var = jnp.mean(xc * xc, axis=-1, keepdims=True)
    return xc * lax.rsqrt(var + EPS) * g + b


def _even_epilogue_kernel(z_ref, c_ref, su_ref, sd_ref, lng_ref, lnb_ref, wsp_ref, bspt_ref,
                          qt_ref, k_ref, vt_ref, kv_ref, kmean_ref, gv_ref, bout_ref):
    tm = z_ref.shape[0]
    c, su, sd = c_ref[...], su_ref[...], sd_ref[...]
    q = _rope(z_ref[:, 0:QKV_A], c, su, sd) * (D_A ** -0.5)
    k = _rope(z_ref[:, QKV_A:2 * QKV_A], c, su, sd)
    v = z_ref[:, 2 * QKV_A:3 * QKV_A]
    qt_ref[0] = q.T.astype(BF16)
    k_ref[...] = k.astype(BF16)
    vt_ref[0] = v.T.astype(BF16)
    kv_ref[:, 0:QKV_A] = k
    kv_ref[:, QKV_A:2 * QKV_A] = v
    for blk in range(tm // MOBA_BLOCK):
        kmean_ref[blk] = jnp.mean(k[blk * MOBA_BLOCK:(blk + 1) * MOBA_BLOCK], axis=0, keepdims=True)
    u = _gelu(z_ref[:, 3 * QKV_A:3 * QKV_A + D_B])
    gv = _layernorm(_gelu(z_ref[:, 3 * QKV_A + D_B:3 * QKV_A + 2 * D_B]), lng_ref[...], lnb_ref[...])
    gv_ref[...] = gv
    gvb = gv.astype(BF16)
    row = lax.broadcasted_iota(jnp.int32, (CHUNK_B, CHUNK_B), 0)
    col = lax.broadcasted_iota(jnp.int32, (CHUNK_B, CHUNK_B), 1)
    group = lax.broadcasted_iota(jnp.int32, (CHUNK_B, D_B), 1) // D_BG
    w = [jnp.where(row >= col, wsp_ref[g], 0.0).astype(BF16) for g in range(G_B)]
    for ch in range(tm // CHUNK_B):
        gvc = gvb[ch * CHUNK_B:(ch + 1) * CHUNK_B]
        mixed = jnp.zeros((CHUNK_B, D_B), F32)
        for g in range(G_B):
            mixed = jnp.where(group == g, _dot(w[g], gvc) + bspt_ref[:, g:g + 1], mixed)
        bout_ref[ch * CHUNK_B:(ch + 1) * CHUNK_B, :] = (u[ch * CHUNK_B:(ch + 1) * CHUNK_B] * mixed).astype(BF16)


def _even_epilogue(z, tables, ln_g, ln_b, w_sp, b_sp, batch, seq, tm):
    n = batch * seq
    nt = seq // tm
    nblk = tm // MOBA_BLOCK
    tab_spec = pl.BlockSpec((tm, LANE), lambda b, i: (i, 0))
    row_spec = lambda width: pl.BlockSpec((tm, width), lambda b, i: (b * nt + i, 0))
    t_spec = pl.BlockSpec((1, QKV_A, tm), lambda b, i: (b, 0, i))
    return pl.pallas_call(
        _even_epilogue_kernel,
        grid=(batch, nt),
        in_specs=[row_spec(IN_EVEN), tab_spec, tab_spec, tab_spec,
                  pl.BlockSpec((1, D_B), lambda b, i: (0, 0)),
                  pl.BlockSpec((1, D_B), lambda b, i: (0, 0)),
                  pl.BlockSpec((G_B, CHUNK_B, CHUNK_B), lambda b, i: (0, 0, 0)),
                  pl.BlockSpec((CHUNK_B, G_B), lambda b, i: (0, 0))],
        out_specs=[t_spec, row_spec(QKV_A), t_spec, row_spec(2 * QKV_A),
                   pl.BlockSpec((nblk, 1, QKV_A), lambda b, i: (b * nt + i, 0, 0)),
                   row_spec(D_B), row_spec(D_B)],
        out_shape=[jax.ShapeDtypeStruct((batch, QKV_A, seq), BF16),
                   jax.ShapeDtypeStruct((n, QKV_A), BF16),
                   jax.ShapeDtypeStruct((batch, QKV_A, seq), BF16),
                   jax.ShapeDtypeStruct((n, 2 * QKV_A), F32),
                   jax.ShapeDtypeStruct((n // MOBA_BLOCK, 1, QKV_A), F32),
                   jax.ShapeDtypeStruct((n, D_B), F32),
                   jax.ShapeDtypeStruct((n, D_B), BF16)],
        compiler_params=_cparams("parallel", "parallel"),
        name="even_epilogue",
    )(z, *tables, ln_g, ln_b, w_sp, b_sp.T)


def _moba_select(gate, n_own):
    nb = gate.shape[0]
    blk = lax.broadcasted_iota(jnp.int32, gate.shape, 0)
    elig = blk < n_own
    gm = jnp.where(elig, gate, NEG)
    rank = jnp.zeros(gate.shape, F32)
    for m in range(nb):
        gm_m = gm[m:m + 1, :]
        ahead = (gm_m > gm) | ((gm_m == gm) & (m < blk))
        rank = rank + ahead.astype(F32)
    return (elig & (rank < MOBA_TOPK)).astype(F32)


def _moba_prompt_kernel(qt_ref, k_ref, vt_ref, kmean_ref, o_ref, sel_ref):
    i = pl.program_id(2)
    tq = MOBA_BLOCK
    own = pl.multiple_of(i * tq, tq)
    key_i = lax.broadcasted_iota(jnp.int32, (tq, tq), 0)
    qry_i = lax.broadcasted_iota(jnp.int32, (tq, tq), 1)
    outs = []
    for hh in range(LANE // D_A):
        lanes = slice(hh * D_A, (hh + 1) * D_A)
        qt = qt_ref[0, lanes, :]
        gate = jnp.dot(kmean_ref[0, :, lanes], qt.astype(F32), precision=HI, preferred_element_type=F32)
        sel_ref[hh] = _moba_select(gate, i)
        s = _dot(k_ref[pl.ds(own, tq), lanes], qt)
        s = jnp.where(key_i <= qry_i, s, NEG)
        m = jnp.max(s, axis=0, keepdims=True)
        p = jnp.exp(s - m)
        l = jnp.sum(p, axis=0, keepdims=True)
        acc = _dot(vt_ref[0, lanes, pl.ds(own, tq)], p.astype(BF16))

        def body(j, carry, hh=hh, lanes=lanes, qt=qt):
            m, l, acc = carry
            start = pl.multiple_of(j * tq, tq)
            s = _dot(k_ref[pl.ds(start, tq), lanes], qt)
            s = jnp.where(sel_ref[hh, pl.ds(j, 1), :] > 0.0, s, NEG)
            m_new = jnp.maximum(m, jnp.max(s, axis=0, keepdims=True))
            alpha = jnp.exp(m - m_new)
            p = jnp.exp(s - m_new)
            l = l * alpha + jnp.sum(p, axis=0, keepdims=True)
            acc = acc * alpha + _dot(vt_ref[0, lanes, pl.ds(start, tq)], p.astype(BF16))
            return m_new, l, acc

        m, l, acc = lax.fori_loop(0, i, body, (m, l, acc))
        outs.append(acc / l)
    o_ref[...] = jnp.concatenate(outs, axis=0).T.astype(BF16)


def _moba_prompt(qt, k, vt, kmean, batch, seq):
    nq = seq // MOBA_BLOCK
    hp = QKV_A // LANE
    return pl.pallas_call(
        _moba_prompt_kernel,
        grid=(batch, hp, nq),
        in_specs=[pl.BlockSpec((1, LANE, MOBA_BLOCK), lambda b, h, i: (b, h, i)),
                  pl.BlockSpec((seq, LANE), lambda b, h, i: (b, h)),
                  pl.BlockSpec((1, LANE, seq), lambda b, h, i: (b, h, 0)),
                  pl.BlockSpec((1, nq, LANE), lambda b, h, i: (b, 0, h))],
        out_specs=pl.BlockSpec((MOBA_BLOCK, LANE), lambda b, h, i: (b * nq + i, h)),
        out_shape=jax.ShapeDtypeStruct((batch * seq, QKV_A), BF16),
        scratch_shapes=[pltpu.VMEM((LANE // D_A, nq, MOBA_BLOCK), F32)],
        compiler_params=_cparams("parallel", "parallel", "arbitrary"),
        name="moba_prompt",
    )(qt, k, vt, kmean)


def _proj_ffn_kernel(*refs, n_mix):
    mix_refs = refs[:n_mix]
    x_ref, wo_ref, g_ref, wg_ref, wu_ref, wd_ref, o_ref, hn_ref = refs[n_mix:]

    @pl.when(pl.program_id(1) == 0)
    def _():
        x1 = x_ref[...]
        off = 0
        for r in mix_refs:
            width = r.shape[1]
            x1 = x1 + _dot(r[...], wo_ref[off:off + width, :])
            off += width
        o_ref[...] = x1
        hn_ref[...] = _rms(x1, g_ref[...]).astype(BF16)

    h = hn_ref[...]
    act = (_silu(_dot(h, wg_ref[...])) * _dot(h, wu_ref[...])).astype(BF16)
    o_ref[...] += _dot(act, wd_ref[...])


def _proj_ffn(mixes, x, w_out, g, w_gate, w_up, w_down, tm, tf):
    m, d = x.shape
    ff = w_gate.shape[1]
    mix_specs = [pl.BlockSpec((tm, a.shape[1]), lambda i, f: (i, 0)) for a in mixes]
    return pl.pallas_call(
        functools.partial(_proj_ffn_kernel, n_mix=len(mixes)),
        grid=(m // tm, ff // tf),
        in_specs=mix_specs + [
            pl.BlockSpec((tm, d), lambda i, f: (i, 0)),
            pl.BlockSpec(w_out.shape, lambda i, f: (0, 0)),
            pl.BlockSpec((1, d), lambda i, f: (0, 0)),
            pl.BlockSpec((d, tf), lambda i, f: (0, f)),
            pl.BlockSpec((d, tf), lambda i, f: (0, f)),
            pl.BlockSpec((tf, d), lambda i, f: (f, 0))],
        out_specs=pl.BlockSpec((tm, d), lambda i, f: (i, 0)),
        out_shape=jax.ShapeDtypeStruct((m, d), F32),
        scratch_shapes=[pltpu.VMEM((tm, d), BF16)],
        compiler_params=_cparams("parallel", "arbitrary"),
        name="proj_ffn",
    )(*mixes, x, w_out, g, w_gate, w_up, w_down)


def _even_layer_prompt(x, p, batch, seq):
    z = _norm_matmul(x, p["norm_mix"], p["w_in"], 512)
    tables = _rope_tables(jnp.arange(seq, dtype=jnp.int32))
    qt, k, vt, kv, kmean, gv, b_out = _even_epilogue(
        z, tables, p["ln_g"], p["ln_b"], p["w_sp"], p["b_sp"], batch, seq, 256)
    a_out = _moba_prompt(qt, k, vt, kmean.reshape(batch, seq // MOBA_BLOCK, QKV_A), batch, seq)
    x = _proj_ffn([a_out, b_out], x, p["w_out"], p["norm_ffn"], p["w_gate"], p["w_up"], p["w_down"], 512, 256)
    n_open = seq - ((seq - 1) // CHUNK_B) * CHUNK_B
    gv_open = gv.reshape(batch, seq, D_B)[:, seq - n_open:]
    return x, kv.reshape(batch, seq, 2, H_A, D_A), gv_open


def _even_params(i, ev_norm_mix, ev_w_in, ev_gmlp_ln_g, ev_gmlp_ln_b, ev_w_spatial, ev_b_spatial, ev_w_out,
                 ev_norm_ffn, ev_w_gate, ev_w_up, ev_w_down):
    return dict(norm_mix=ev_norm_mix[i][None], w_in=ev_w_in[i].astype(BF16),
                ln_g=ev_gmlp_ln_g[i][None], ln_b=ev_gmlp_ln_b[i][None],
                w_sp=ev_w_spatial[i], b_sp=ev_b_spatial[i], w_out=ev_w_out[i].astype(BF16),
                norm_ffn=ev_norm_ffn[i][None], w_gate=ev_w_gate[i].astype(BF16),
                w_up=ev_w_up[i].astype(BF16), w_down=ev_w_down[i].astype(BF16))


O_CQ, O_CK, O_CV, O_CR, O_DQKV, O_DG, O_SM = 0, 256, 512, 1024, 1536, 2560, 3072
SM_DA, SM_DB = GATE_RANK, GATE_RANK + H_D
IN_ODD_PAD = O_SM + LANE
KEY_C = H_C * DK_C
PAIR = LANE // DK_C


def _log_sigmoid(x):
    return jnp.minimum(x, 0.0) - jnp.log1p(jnp.exp(-jnp.abs(x)))


def _split_bf16(a):
    hi = a.astype(BF16)
    return hi, (a - hi.astype(F32)).astype(BF16)


def _dot3(a, b):
    ah, al = _split_bf16(a)
    bh, bl = _split_bf16(b)
    return _dot(ah, bh) + _dot(ah, bl) + _dot(al, bh)


def _unit_lower_inverse(a):
    c = a.shape[0]
    eye = (lax.broadcasted_iota(jnp.int32, (c, c), 0) == lax.broadcasted_iota(jnp.int32, (c, c), 1)).astype(F32)
    x = -a
    p = eye + x
    steps = int(math.log2(c)) - 1
    for _ in range(steps):
        x = _dot3(x, x)
        p = p + _dot3(p, x)
    return p


def _group_sumsq(y, width):
    n = y.shape[1]
    same = (lax.broadcasted_iota(jnp.int32, (n, n), 0) // width
            == lax.broadcasted_iota(jnp.int32, (n, n), 1) // width).astype(F32)
    return jnp.dot(y * y, same, precision=HI, preferred_element_type=F32)


def _odd_mixer_prompt_kernel(z_ref, wg2_ref, bg2_ref, gnorm_ref, cw_ref, alog_ref, dtb_ref, dnorm_ref,
                             o_ref, sg_ref, sd_ref, tail_ref,
                             stg_ref, std_ref, prev_ref, la_ref, qkv_ref, dla_ref, beta_ref):
    i = pl.program_id(1)
    tc = z_ref.shape[0]
    c = CHUNK_LIN

    @pl.when(i == 0)
    def _():
        stg_ref[...] = jnp.zeros_like(stg_ref)
        std_ref[...] = jnp.zeros_like(std_ref)
        prev_ref[...] = jnp.zeros_like(prev_ref)

    small = z_ref[:, O_SM:O_SM + LANE]
    pre = jnp.dot(small, wg2_ref[...], precision=HI, preferred_element_type=F32) + bg2_ref[...]
    la_ref[...] = _log_sigmoid(pre) / GATE_NORM
    dla_ref[...] = -jnp.exp(alog_ref[...]) * _softplus(small + dtb_ref[...])
    beta_ref[...] = jax.nn.sigmoid(small)

    x = z_ref[:, O_DQKV:O_DQKV + C_CONV]
    x8 = x[0:8]
    p8 = prev_ref[...]
    row8 = lax.broadcasted_iota(jnp.int32, (8, C_CONV), 0)
    y = x * cw_ref[CONV_W - 1:CONV_W, :]
    y8 = x8 * cw_ref[CONV_W - 1:CONV_W, :]
    for s in range(1, CONV_W):
        wrow = cw_ref[CONV_W - 1 - s:CONV_W - s, :]
        y = y + pltpu.roll(x, s, 0) * wrow
        y8 = y8 + jnp.where(row8 < s, pltpu.roll(p8, s, 0), pltpu.roll(x8, s, 0)) * wrow
    prev_ref[...] = x[tc - 8:tc]
    y = _silu(y)
    y8 = _silu(y8)
    yqk = y[:, 0:2 * KEY_C]
    nrm = lax.rsqrt(_group_sumsq(yqk, DK_D) + EPS)
    qscale = jnp.where(lax.broadcasted_iota(jnp.int32, (1, 2 * KEY_C), 1) < KEY_C, DK_D ** -0.5, 1.0)
    qkv_ref[:, 0:2 * KEY_C] = yqk * nrm * qscale
    qkv_ref[:, 2 * KEY_C:] = y[:, 2 * KEY_C:]
    yqk8 = y8[:, 0:2 * KEY_C]
    qkv_ref[0:8, 0:2 * KEY_C] = yqk8 * lax.rsqrt(_group_sumsq(yqk8, DK_D) + EPS) * qscale
    qkv_ref[0:8, 2 * KEY_C:] = y8[:, 2 * KEY_C:]

    ri = lax.broadcasted_iota(jnp.int32, (c, c), 0)
    ci = lax.broadcasted_iota(jnp.int32, (c, c), 1)
    tril = ri >= ci
    strict = ri > ci
    lower = tril.astype(F32)
    upper = (ri <= ci).astype(F32)
    ones = jnp.ones((c, c), F32)
    lane_head = lax.broadcasted_iota(jnp.int32, (c, LANE), 1) // DK_C
    lane_head_row = lax.broadcasted_iota(jnp.int32, (1, LANE), 1) // DK_C

    def chunk(ch, carry):
        rows = pl.ds(pl.multiple_of(ch * c, c), c)
        for p in range(H_C // PAIR):
            lanes = slice(p * LANE, (p + 1) * LANE)
            bcum = jnp.dot(lower, la_ref[rows, lanes], precision=HI, preferred_element_type=F32)
            b_end = bcum[c - 1:c, :]
            q_in = z_ref[rows, O_CQ + p * LANE:O_CQ + (p + 1) * LANE] * (DK_C ** -0.5) * jnp.exp(bcum)
            k = z_ref[rows, O_CK + p * LANE:O_CK + (p + 1) * LANE]
            k_in = (k * jnp.exp(-bcum)).astype(BF16)
            k_end = k * jnp.exp(b_end - bcum)
            st = stg_ref[p]
            stb = st.astype(BF16)
            upd = jnp.zeros((DV_C, LANE), F32)
            for hh in range(PAIR):
                h = p * PAIR + hh
                mask = lane_head == hh
                qm = jnp.where(mask, q_in, 0.0).astype(BF16)
                sc = jnp.where(tril, _dot_nt(qm, k_in), 0.0)
                vb = z_ref[rows, O_CV + h * DV_C:O_CV + (h + 1) * DV_C].astype(BF16)
                o = _dot(sc.astype(BF16), vb) + _dot_nt(qm, stb)
                upd = upd + _dot_tn(vb, jnp.where(mask, k_end, 0.0).astype(BF16))
                gate = _silu(z_ref[rows, O_CR + h * DV_C:O_CR + (h + 1) * DV_C])
                o_ref[rows, h * DV_C:(h + 1) * DV_C] = (_rms(o, gnorm_ref[...]) * gate).astype(BF16)
            stg_ref[p] = st * jnp.exp(b_end) + upd
        for p in range(H_D // PAIR):
            q2 = qkv_ref[rows, p * LANE:(p + 1) * LANE]
            k2 = qkv_ref[rows, KEY_C + p * LANE:KEY_C + (p + 1) * LANE]
            st = std_ref[p]
            stb = st.astype(BF16)
            upd = jnp.zeros((DV_D, LANE), F32)
            dec_row = jnp.zeros((1, LANE), F32)
            for hh in range(PAIR):
                h = p * PAIR + hh
                mask = lane_head == hh
                la_b = jnp.broadcast_to(dla_ref[rows, SM_DA + h:SM_DA + h + 1], (c, LANE))
                beta_b = jnp.broadcast_to(beta_ref[rows, SM_DB + h:SM_DB + h + 1], (c, LANE))
                g_col = jnp.dot(lower, la_b, precision=HI, preferred_element_type=F32)
                g_row = jnp.dot(ones, la_b[:, 0:c] * upper, precision=HI, preferred_element_type=F32)
                decay = jnp.exp(jnp.where(tril, g_col[:, 0:c] - g_row, NEG))
                eg = jnp.exp(g_col)
                g_last = g_col[c - 1:c, :]
                km = jnp.where(mask, k2, 0.0)
                qm = jnp.where(mask, q2, 0.0)
                kmb = km.astype(BF16)
                kb = km * beta_b
                a_mat = jnp.where(strict, _dot_nt(kb.astype(BF16), kmb) * decay, 0.0)
                t_inv = _unit_lower_inverse(a_mat)
                v = qkv_ref[rows, 2 * KEY_C + h * DV_D:2 * KEY_C + (h + 1) * DV_D]
                u = _dot3(t_inv, v * beta_b)
                w = _dot3(t_inv, kb * eg)
                qk = jnp.where(tril, _dot_nt(qm.astype(BF16), kmb) * decay, 0.0)
                v_new = u - _dot_nt(w.astype(BF16), stb)
                vnb = v_new.astype(BF16)
                o = _dot_nt((qm * eg).astype(BF16), stb) + _dot(qk.astype(BF16), vnb)
                upd = upd + _dot_tn(vnb, (km * jnp.exp(g_last - g_col)).astype(BF16))
                dec_row = jnp.where(lane_head_row == hh, jnp.exp(g_last), dec_row)
                gate = _silu(z_ref[rows, O_DG + h * DV_D:O_DG + (h + 1) * DV_D])
                col = H_C * DV_C + h * DV_D
                o_ref[rows, col:col + DV_D] = (_rms(o, dnorm_ref[...]) * gate).astype(BF16)
            std_ref[p] = st * dec_row + upd
        return carry

    lax.fori_loop(0, tc // c, chunk, 0)

    @pl.when(i == pl.num_programs(1) - 1)
    def _():
        tail_ref[0] = prev_ref[...]
        for p in range(H_C // PAIR):
            tg = stg_ref[p].T
            td = std_ref[p].T
            for hh in range(PAIR):
                sg_ref[0, p * PAIR + hh] = tg[hh * DK_C:(hh + 1) * DK_C, :]
                sd_ref[0, p * PAIR + hh] = td[hh * DK_D:(hh + 1) * DK_D, :]


def _odd_mixer_prompt(z, p, batch, seq, tc):
    nt = seq // tc
    full = lambda a: pl.BlockSpec(a.shape, lambda b, i: (0,) * a.ndim)
    consts = [p["w_g2_pad"], p["b_g2"], p["gla_norm"], p["conv_w"], p["alog_slab"], p["dtb_slab"], p["delta_norm"]]
    st_spec = pl.BlockSpec((1, H_C, DK_C, DV_C), lambda b, i: (b, 0, 0, 0))
    return pl.pallas_call(
        _odd_mixer_prompt_kernel,
        grid=(batch, nt),
        in_specs=[pl.BlockSpec((tc, IN_ODD_PAD), lambda b, i: (b * nt + i, 0))] + [full(a) for a in consts],
        out_specs=[pl.BlockSpec((tc, D_MODEL), lambda b, i: (b * nt + i, 0)), st_spec, st_spec,
                   pl.BlockSpec((1, 8, C_CONV), lambda b, i: (b, 0, 0))],
        out_shape=[jax.ShapeDtypeStruct((batch * seq, D_MODEL), BF16),
                   jax.ShapeDtypeStruct((batch, H_C, DK_C, DV_C), F32),
                   jax.ShapeDtypeStruct((batch, H_D, DK_D, DV_D), F32),
                   jax.ShapeDtypeStruct((batch, 8, C_CONV), F32)],
        scratch_shapes=[pltpu.VMEM((H_C // PAIR, DV_C, LANE), F32),
                        pltpu.VMEM((H_D // PAIR, DV_D, LANE), F32),
                        pltpu.VMEM((8, C_CONV), F32),
                        pltpu.VMEM((tc, KEY_C), F32),
                        pltpu.VMEM((tc, C_CONV), F32),
                        pltpu.VMEM((tc, LANE), F32),
                        pltpu.VMEM((tc, LANE), F32)],
        compiler_params=_cparams("parallel", "arbitrary"),
        name="odd_mixer_prompt",
    )(z, *consts)


M_E1, M_E2, M_R1, M_R2 = N_EXPERTS, N_EXPERTS + 1, N_EXPERTS + 2, N_EXPERTS + 3


def _proj_router_kernel(mix_ref, x_ref, wo_ref, g_ref, wr_ref, x1_ref, hn_ref, meta_ref, cnt_ref, carry_ref):
    tm = x_ref.shape[0]

    @pl.when(pl.program_id(0) == 0)
    def _():
        carry_ref[...] = jnp.zeros_like(carry_ref)

    x1 = x_ref[...] + _dot(mix_ref[...], wo_ref[...])
    x1_ref[...] = x1
    hn = _rms(x1, g_ref[...])
    hn_ref[...] = hn.astype(BF16)
    lane = lax.broadcasted_iota(jnp.int32, (tm, LANE), 1)
    logits = jnp.dot(hn, wr_ref[...], precision=HI, preferred_element_type=F32)
    logits = jnp.where(lane < N_EXPERTS, logits, NEG)
    m1 = jnp.max(logits, axis=1, keepdims=True)
    e1 = jnp.min(jnp.where(logits == m1, lane, LANE), axis=1, keepdims=True)
    rest = jnp.where(lane == e1, NEG, logits)
    m2 = jnp.max(rest, axis=1, keepdims=True)
    e2 = jnp.min(jnp.where(rest == m2, lane, LANE), axis=1, keepdims=True)
    t = jnp.exp(m2 - m1)
    g1 = 1.0 / (1.0 + t)
    g2 = t / (1.0 + t)
    oh1 = lane == e1
    oh2 = lane == e2
    member = (oh1 | oh2).astype(F32)
    ri = lax.broadcasted_iota(jnp.int32, (tm, tm), 0)
    ci = lax.broadcasted_iota(jnp.int32, (tm, tm), 1)
    before = _dot((ri > ci).astype(BF16), member.astype(BF16)) + carry_ref[...]
    r1 = jnp.sum(jnp.where(oh1, before, 0.0), axis=1, keepdims=True)
    r2 = jnp.sum(jnp.where(oh2, before, 0.0), axis=1, keepdims=True)
    carry_ref[...] = carry_ref[...] + jnp.sum(member, axis=0, keepdims=True)
    cnt_ref[...] = carry_ref[...]
    meta = jnp.where(oh1, g1, 0.0) + jnp.where(oh2, g2, 0.0)
    meta = jnp.where(lane == M_E1, e1.astype(F32), meta)
    meta = jnp.where(lane == M_E2, e2.astype(F32), meta)
    meta = jnp.where(lane == M_R1, r1, meta)
    meta = jnp.where(lane == M_R2, r2, meta)
    meta_ref[...] = meta


def _proj_router(mix, x, w_out, g, w_router_pad, tm):
    m, d = x.shape
    row = lambda width: pl.BlockSpec((tm, width), lambda i: (i, 0))
    full = lambda a: pl.BlockSpec(a.shape, lambda i: (0,) * a.ndim)
    return pl.pallas_call(
        _proj_router_kernel,
        grid=(m // tm,),
        in_specs=[row(d), row(d), full(w_out), full(g), full(w_router_pad)],
        out_specs=[row(d), row(d), row(LANE), pl.BlockSpec((1, LANE), lambda i: (0, 0))],
        out_shape=[jax.ShapeDtypeStruct((m, d), F32), jax.ShapeDtypeStruct((m, d), BF16),
                   jax.ShapeDtypeStruct((m, LANE), F32), jax.ShapeDtypeStruct((1, LANE), F32)],
        scratch_shapes=[pltpu.VMEM((1, LANE), F32)],
        compiler_params=_cparams("arbitrary"),
        name="proj_router",
    )(mix, x, w_out, g, w_router_pad)


def _moe_dense_kernel(hn_ref, x1_ref, meta_ref, wg_ref, wu_ref, wd_ref, gf_ref, o_ref):
    e = pl.program_id(1)
    f = pl.program_id(2)

    @pl.when((e == 0) & (f == 0))
    def _():
        o_ref[...] = x1_ref[...]

    h = hn_ref[...]
    act = (_silu(_dot(h, wg_ref[...])) * _dot(h, wu_ref[...])).astype(BF16)
    lane = lax.broadcasted_iota(jnp.int32, meta_ref.shape, 1)
    gate = jnp.sum(jnp.where(lane == e, meta_ref[...], 0.0), axis=1, keepdims=True)
    o_ref[...] += gate * _dot(act, wd_ref[...])

    @pl.when((e == pl.num_programs(1) - 1) & (f == pl.num_programs(2) - 1))
    def _():
        o_ref[...] = _rms(o_ref[...], gf_ref[...])


def _moe_dense(hn, x1, meta, w_gate, w_up, w_down, g_final, tm, tf):
    m, d = x1.shape
    n_e, _, ff = w_gate.shape
    row = lambda width: pl.BlockSpec((tm, width), lambda i, e, f: (i, 0))
    return pl.pallas_call(
        _moe_dense_kernel,
        grid=(m // tm, n_e, ff // tf),
        in_specs=[row(d), row(d), row(LANE),
                  pl.BlockSpec((None, d, tf), lambda i, e, f: (e, 0, f)),
                  pl.BlockSpec((None, d, tf), lambda i, e, f: (e, 0, f)),
                  pl.BlockSpec((None, tf, d), lambda i, e, f: (e, f, 0)),
                  pl.BlockSpec((1, d), lambda i, e, f: (0, 0))],
        out_specs=row(d),
        out_shape=jax.ShapeDtypeStruct((m, d), F32),
        compiler_params=_cparams("parallel", "arbitrary", "arbitrary"),
        name="moe_dense",
    )(hn, x1, meta, w_gate, w_up, w_down, g_final)


def _odd_params(i, od_norm_mix, od_w_in, od_gla_w_gate2, od_gla_b_gate2, od_gla_norm, od_delta_conv,
                od_delta_a_log, od_delta_dt_bias, od_delta_norm, od_w_out, od_norm_ffn, od_router,
                od_w_gate, od_w_up, od_w_down):
    w = od_w_in[i]
    sizes = (KEY_C, KEY_C, H_C * DV_C, GATE_RANK, H_C * DV_C, C_CONV, H_D, H_D, H_D * DV_D)
    splits = tuple(sum(sizes[:j + 1]) for j in range(len(sizes) - 1))
    cq, ck, cv, c_lr, c_r, d_qkv, d_a, d_b, d_g = jnp.split(w, splits, axis=1)
    pad = jnp.zeros((w.shape[0], IN_ODD_PAD - w.shape[1]), w.dtype)
    w_in = jnp.concatenate([cq, ck, cv, c_r, d_qkv, d_g, c_lr, d_a, d_b, pad], axis=1).astype(BF16)
    slab = lambda v: jnp.zeros((1, LANE), F32).at[0, SM_DA:SM_DA + H_D].set(v)
    return dict(norm_mix=od_norm_mix[i][None], w_in=w_in,
                w_g2_pad=jnp.zeros((LANE, KEY_C), F32).at[:GATE_RANK].set(od_gla_w_gate2[i]),
                b_g2=od_gla_b_gate2[i][None], gla_norm=od_gla_norm[i][None], conv_w=od_delta_conv[i],
                alog_slab=slab(od_delta_a_log[i]), dtb_slab=slab(od_delta_dt_bias[i]),
                a_log=od_delta_a_log[i], dt_bias=od_delta_dt_bias[i],
                delta_norm=od_delta_norm[i][None], w_out=od_w_out[i].astype(BF16),
                norm_ffn=od_norm_ffn[i][None],
                router=jnp.zeros((D_MODEL, LANE), F32).at[:, :N_EXPERTS].set(od_router[i]),
                w_gate=od_w_gate[i].astype(BF16), w_up=od_w_up[i].astype(BF16), w_down=od_w_down[i].astype(BF16))


def _moe(mix, x, p, g_final, tm):
    x1, hn, meta, _ = _proj_router(mix, x, p["w_out"], p["norm_ffn"], p["router"], tm)
    return _moe_dense(hn, x1, meta, p["w_gate"], p["w_up"], p["w_down"], g_final, tm, 512)


def _odd_layer_prompt(x, p, g_final, batch, seq):
    z = _norm_matmul(x, p["norm_mix"], p["w_in"], 512)
    mix, s_gla, s_delta, tail = _odd_mixer_prompt(z, p, batch, seq, 256)
    y = _moe(mix, x, p, g_final, 512)
    return y, s_gla, s_delta, tail[:, 8 - (CONV_W - 1):]


def _even_epilogue_sample_kernel(z_ref, c_ref, su_ref, sd_ref, lng_ref, lnb_ref, w0_ref, b0_ref,
                                 q_ref, kv_ref, gv_ref, bout_ref):
    c, su, sd = c_ref[...], su_ref[...], sd_ref[...]
    q_ref[...] = _rope(z_ref[:, 0:QKV_A], c, su, sd) * (D_A ** -0.5)
    kv_ref[:, 0:QKV_A] = _rope(z_ref[:, QKV_A:2 * QKV_A], c, su, sd)
    kv_ref[:, QKV_A:2 * QKV_A] = z_ref[:, 2 * QKV_A:3 * QKV_A]
    u = _gelu(z_ref[:, 3 * QKV_A:3 * QKV_A + D_B])
    gv = _layernorm(_gelu(z_ref[:, 3 * QKV_A + D_B:3 * QKV_A + 2 * D_B]), lng_ref[...], lnb_ref[...])
    gv_ref[...] = gv
    bout_ref[...] = (u * (gv * w0_ref[...] + b0_ref[...])).astype(BF16)


def _even_epilogue_sample(z, tables, ln_g, ln_b, w_sp, b_sp):
    m = z.shape[0]
    w0 = jnp.repeat(w_sp[:, 0, 0], D_BG)[None]
    b0 = jnp.repeat(b_sp[:, 0], D_BG)[None]
    return pl.pallas_call(
        _even_epilogue_sample_kernel,
        out_shape=[jax.ShapeDtypeStruct((m, QKV_A), F32), jax.ShapeDtypeStruct((m, 2 * QKV_A), F32),
                   jax.ShapeDtypeStruct((m, D_B), F32), jax.ShapeDtypeStruct((m, D_B), BF16)],
        compiler_params=pltpu.CompilerParams(vmem_limit_bytes=VMEM_LIMIT),
        name="even_epilogue_sample",
    )(z, *tables, ln_g, ln_b, w0, b0)


def _moba_sample_kernel(pt_ref, q_ref, kvn_ref, cache_ref, o_ref, s_ref, v_ref):
    del pt_ref
    pg = pl.program_id(1)
    past = s_ref.shape[1]
    q = q_ref[0]
    head_lane = lax.broadcasted_iota(jnp.int32, (H_A, QKV_A), 1) // D_A
    head_row = lax.broadcasted_iota(jnp.int32, (H_A, QKV_A), 0)
    diag = head_lane == head_row
    qbd = jnp.where(diag, jnp.broadcast_to(q, (H_A, QKV_A)), 0.0)
    start = pl.multiple_of(pg * PAGE_SIZE, PAGE_SIZE)
    s_ref[:, pl.ds(start, PAGE_SIZE)] = lax.dot_general(
        qbd, cache_ref[0, :, 0:QKV_A], (((1,), (1,)), ((), ())), precision=HI, preferred_element_type=F32)
    v_ref[pl.ds(start, PAGE_SIZE), :] = cache_ref[0, :, QKV_A:2 * QKV_A].astype(BF16)

    @pl.when(pg == pl.num_programs(1) - 1)
    def _():
        nb = past // MOBA_BLOCK
        s = s_ref[...]
        blk_of_key = lax.broadcasted_iota(jnp.int32, (past, LANE), 0) // MOBA_BLOCK
        ind = (blk_of_key == lax.broadcasted_iota(jnp.int32, (past, LANE), 1)).astype(F32)
        gate = jnp.dot(s, ind, precision=HI, preferred_element_type=F32)
        lane = lax.broadcasted_iota(jnp.int32, (H_A, LANE), 1)
        elig = lane < nb
        gm = jnp.where(elig, gate, NEG)
        rank = jnp.zeros((H_A, LANE), F32)
        for m in range(nb):
            gm_m = gm[:, m:m + 1]
            rank = rank + ((gm_m > gm) | ((gm_m == gm) & (m < lane))).astype(F32)
        sel = (elig & (rank < MOBA_TOPK)).astype(BF16)
        blk_of_key_t = lax.broadcasted_iota(jnp.int32, (LANE, past), 1) // MOBA_BLOCK
        ind_t = (blk_of_key_t == lax.broadcasted_iota(jnp.int32, (LANE, past), 0)).astype(BF16)
        sel_keys = _dot(sel, ind_t)
        kvn = kvn_ref[0]
        s_own = jnp.sum(qbd * kvn[:, 0:QKV_A], axis=1, keepdims=True)
        sm = jnp.where(sel_keys > 0.0, s, NEG)
        mx = jnp.maximum(jnp.max(sm, axis=1, keepdims=True), s_own)
        p = jnp.exp(sm - mx)
        p_own = jnp.exp(s_own - mx)
        denom = jnp.sum(p, axis=1, keepdims=True) + p_own
        pv = _dot(p.astype(BF16), v_ref[...]) + p_own * kvn[:, QKV_A:2 * QKV_A]
        o_ref[0] = jnp.sum(jnp.where(diag, pv / denom, 0.0), axis=0, keepdims=True).astype(BF16)


def _moba_sample(q, kv_new, cache, page_table):
    bs, n_pages = page_table.shape
    past = n_pages * PAGE_SIZE
    assert past % MOBA_BLOCK == 0 and past // MOBA_BLOCK <= LANE
    pool = cache.shape[0]
    return pl.pallas_call(
        _moba_sample_kernel,
        grid_spec=pltpu.PrefetchScalarGridSpec(
            num_scalar_prefetch=1,
            grid=(bs, n_pages),
            in_specs=[pl.BlockSpec((1, 1, QKV_A), lambda b, p, pt: (b, 0, 0)),
                      pl.BlockSpec((1, 1, 2 * QKV_A), lambda b, p, pt: (b, 0, 0)),
                      pl.BlockSpec((1, PAGE_SIZE, 2 * QKV_A), lambda b, p, pt: (pt[b * n_pages + p], 0, 0))],
            out_specs=pl.BlockSpec((1, 1, QKV_A), lambda b, p, pt: (b, 0, 0)),
            scratch_shapes=[pltpu.VMEM((H_A, past), F32), pltpu.VMEM((past, QKV_A), BF16)]),
        out_shape=jax.ShapeDtypeStruct((bs, 1, QKV_A), BF16),
        compiler_params=_cparams("parallel", "arbitrary"),
        name="moba_sample",
    )(page_table.reshape(-1), q.reshape(bs, 1, QKV_A), kv_new.reshape(bs, 1, 2 * QKV_A),
      cache.reshape(pool, PAGE_SIZE, 2 * QKV_A)).reshape(bs, QKV_A)


def _odd_mixer_sample_kernel(z_ref, buf_ref, sg_ref, sd_ref, wg2_ref, bg2_ref, gnorm_ref, cw_ref, alog_ref, dtb_ref,
                             dnorm_ref, o_ref, sgo_ref, sdo_ref):
    bt = z_ref.shape[0]
    stride = H_C * DK_C
    small = z_ref[:, O_SM:O_SM + LANE]
    pre = jnp.dot(small, wg2_ref[...], precision=HI, preferred_element_type=F32) + bg2_ref[...]
    a_all = jnp.exp(_log_sigmoid(pre) / GATE_NORM)
    g_all = -jnp.exp(alog_ref[...]) * _softplus(small + dtb_ref[...])
    beta_all = jax.nn.sigmoid(small)
    x = z_ref[:, O_DQKV:O_DQKV + C_CONV]
    y = x * cw_ref[CONV_W - 1:CONV_W, :]
    for j in range(CONV_W - 1):
        y = y + buf_ref[:, j, :] * cw_ref[j:j + 1, :]
    y = _silu(y)
    for h in range(H_C):
        ks = slice(h * DK_C, (h + 1) * DK_C)
        a = a_all[:, ks]
        q = z_ref[:, O_CQ + h * DK_C:O_CQ + (h + 1) * DK_C] * (DK_C ** -0.5)
        k = z_ref[:, O_CK + h * DK_C:O_CK + (h + 1) * DK_C]
        v = z_ref[:, O_CV + h * DV_C:O_CV + (h + 1) * DV_C]
        qa = q * a
        acc = jnp.sum(q * k, axis=1, keepdims=True) * v
        for kk in range(DK_C):
            rows = pl.ds(h * DK_C + kk, bt, stride=stride)
            srow = sg_ref[rows, :]
            acc = acc + qa[:, kk:kk + 1] * srow
            sgo_ref[rows, :] = a[:, kk:kk + 1] * srow + k[:, kk:kk + 1] * v
        gate = _silu(z_ref[:, O_CR + h * DV_C:O_CR + (h + 1) * DV_C])
        o_ref[:, h * DV_C:(h + 1) * DV_C] = (_rms(acc, gnorm_ref[...]) * gate).astype(BF16)
        yq = y[:, h * DK_D:(h + 1) * DK_D]
        yk = y[:, KEY_C + h * DK_D:KEY_C + (h + 1) * DK_D]
        dv = y[:, 2 * KEY_C + h * DV_D:2 * KEY_C + (h + 1) * DV_D]
        dq = yq * lax.rsqrt(jnp.sum(yq * yq, axis=1, keepdims=True) + EPS) * (DK_D ** -0.5)
        dk = yk * lax.rsqrt(jnp.sum(yk * yk, axis=1, keepdims=True) + EPS)
        beta = beta_all[:, SM_DB + h:SM_DB + h + 1]
        eg = jnp.exp(g_all[:, SM_DA + h:SM_DA + h + 1])
        w = dk * (beta * eg)
        qd = dq * eg
        ws = jnp.zeros((bt, DV_D), F32)
        qs = jnp.zeros((bt, DV_D), F32)
        for kk in range(DK_D):
            srow = sd_ref[pl.ds(h * DK_D + kk, bt, stride=stride), :]
            ws = ws + w[:, kk:kk + 1] * srow
            qs = qs + qd[:, kk:kk + 1] * srow
        v_new = dv * beta - ws
        o = qs + jnp.sum(dq * dk, axis=1, keepdims=True) * v_new
        for kk in range(DK_D):
            rows = pl.ds(h * DK_D + kk, bt, stride=stride)
            sdo_ref[rows, :] = sd_ref[rows, :] * eg + dk[:, kk:kk + 1] * v_new
        gate = _silu(z_ref[:, O_DG + h * DV_D:O_DG + (h + 1) * DV_D])
        col = H_C * DV_C + h * DV_D
        o_ref[:, col:col + DV_D] = (_rms(o, dnorm_ref[...]) * gate).astype(BF16)


def _odd_mixer_sample(z, conv_buf, s_gla, s_delta, p, bt):
    bs = z.shape[0]
    rows = H_C * DK_C
    full = lambda a: pl.BlockSpec(a.shape, lambda i: (0,) * a.ndim)
    consts = [p["w_g2_pad"], p["b_g2"], p["gla_norm"], p["conv_w"], p["alog_slab"], p["dtb_slab"], p["delta_norm"]]
    st_spec = pl.BlockSpec((bt * rows, DV_C), lambda i: (i, 0))
    mix, sg, sd = pl.pallas_call(
        _odd_mixer_sample_kernel,
        grid=(bs // bt,),
        in_specs=[pl.BlockSpec((bt, IN_ODD_PAD), lambda i: (i, 0)),
                  pl.BlockSpec((bt, CONV_W - 1, C_CONV), lambda i: (i, 0, 0)),
                  st_spec, st_spec] + [full(a) for a in consts],
        out_specs=[pl.BlockSpec((bt, D_MODEL), lambda i: (i, 0)), st_spec, st_spec],
        out_shape=[jax.ShapeDtypeStruct((bs, D_MODEL), BF16),
                   jax.ShapeDtypeStruct((bs * rows, DV_C), F32),
                   jax.ShapeDtypeStruct((bs * rows, DV_D), F32)],
        compiler_params=_cparams("parallel"),
        name="odd_mixer_sample",
    )(z, conv_buf, s_gla.reshape(bs * rows, DV_C), s_delta.reshape(bs * rows, DV_D), *consts)
    return mix, sg.reshape(s_gla.shape), sd.reshape(s_delta.shape)


def _sample_step(x, cache, page_table, s_gla, s_delta, conv_buf, ev, od, g_final):
    bs = x.shape[0]
    past = page_table.shape[1] * PAGE_SIZE
    z = _norm_matmul(x, ev["norm_mix"], ev["w_in"], bs)
    tables = _rope_tables(jnp.full((1,), past, jnp.int32))
    q, kv, gv, b_out = _even_epilogue_sample(z, tables, ev["ln_g"], ev["ln_b"], ev["w_sp"], ev["b_sp"])
    a_out = _moba_sample(q, kv, cache, page_table)
    x = _proj_ffn([a_out, b_out], x, ev["w_out"], ev["norm_ffn"], ev["w_gate"], ev["w_up"], ev["w_down"], bs, 256)
    z = _norm_matmul(x, od["norm_mix"], od["w_in"], bs)
    mix, sg, sd = _odd_mixer_sample(z, conv_buf, s_gla, s_delta, od, 32)
    conv_new = jnp.concatenate([conv_buf[:, 1:], z[:, None, O_DQKV:O_DQKV + C_CONV]], axis=1)
    y = _moe(mix, x, od, g_final, bs)
    return y, kv, gv, sg, sd, conv_new


def kernel(x_prompt, x_sample, cache_kv, state_gla, state_delta, state_conv, page_table, ev_norm_mix, ev_w_in, ev_gmlp_ln_g, ev_gmlp_ln_b, ev_w_spatial, ev_b_spatial, ev_w_out, ev_norm_ffn, ev_w_gate, ev_w_up, ev_w_down, od_norm_mix, od_w_in, od_gla_w_gate2, od_gla_b_gate2, od_gla_norm, od_delta_conv, od_delta_a_log, od_delta_dt_bias, od_delta_norm, od_w_out, od_norm_ffn, od_router, od_w_gate, od_w_up, od_w_down, norm_final):
    bp, tp, d = x_prompt.shape
    ev = _even_params(0, ev_norm_mix, ev_w_in, ev_gmlp_ln_g, ev_gmlp_ln_b, ev_w_spatial, ev_b_spatial, ev_w_out,
                      ev_norm_ffn, ev_w_gate, ev_w_up, ev_w_down)
    od = _odd_params(0, od_norm_mix, od_w_in, od_gla_w_gate2, od_gla_b_gate2, od_gla_norm, od_delta_conv,
                     od_delta_a_log, od_delta_dt_bias, od_delta_norm, od_w_out, od_norm_ffn, od_router,
                     od_w_gate, od_w_up, od_w_down)
    bs, ts, _ = x_sample.shape
    assert ts == 1 and cache_kv.shape[0] == 1 and state_gla.shape[0] == 1
    xp, kv_p, gv_p = _even_layer_prompt(x_prompt.reshape(bp * tp, d), ev, bp, tp)
    yp, gla_p, dl_p, cv_p = _odd_layer_prompt(xp, od, norm_final[None], bp, tp)
    ys, kv_s, gv_s, gla_s, dl_s, cv_s = _sample_step(
        x_sample.reshape(bs, d), cache_kv[0], page_table, state_gla[0], state_delta[0], state_conv[0],
        ev, od, norm_final[None])
    return (yp.reshape(bp, tp, d), ys.reshape(bs, ts, d),
            kv_p[None], kv_s.reshape(1, bs, ts, 2, H_A, D_A),
            gv_p[None], gv_s.reshape(1, bs, ts, D_B),
            gla_p[None], gla_s[None], dl_p[None], dl_s[None], cv_p[None], cv_s[None])
```

```python
import functools
import math

import jax
import jax.numpy as jnp
from jax import lax
from jax.experimental import pallas as pl
from jax.experimental.pallas import tpu as pltpu

F32 = jnp.float32
BF16 = jnp.bfloat16
HI = lax.Precision.HIGHEST
EPS = 1e-6
NEG = -1e30

D_MODEL = 1024
PAGE_SIZE = 128
H_A, D_A = 8, 64
ROT_DIM = D_A // 4
ROPE_THETA = 500000.0
MOBA_BLOCK = 256
MOBA_TOPK = 3
G_B, D_BG = 8, 64
D_B = G_B * D_BG
CHUNK_B = 128
H_C, DK_C, DV_C = 4, 64, 128
GATE_RANK = 16
GATE_NORM = 16.0
H_D, DK_D, DV_D = 4, 64, 128
CONV_W = 4
C_CONV = 2 * H_D * DK_D + H_D * DV_D
CHUNK_LIN = 64
N_EXPERTS = 8
TOP_K = 2
QKV_A = H_A * D_A
IN_EVEN = 3 * QKV_A + 2 * D_B
LANE = 128
VMEM_LIMIT = 56 * 1024 * 1024


def _cparams(*sem):
    return pltpu.CompilerParams(dimension_semantics=sem, vmem_limit_bytes=VMEM_LIMIT)


def _rms(x, g):
    return x * lax.rsqrt(jnp.mean(x * x, axis=-1, keepdims=True) + EPS) * g


def _gelu(x):
    return 0.5 * x * (1.0 + lax.erf(x * (2.0 ** -0.5)))


def _silu(x):
    return x * jax.nn.sigmoid(x)


def _softplus(x):
    return jnp.maximum(x, 0.0) + jnp.log1p(jnp.exp(-jnp.abs(x)))


def _dot(a, b):
    return jnp.dot(a, b, preferred_element_type=F32)


def _dot_nt(a, b):
    return lax.dot_general(a, b, (((1,), (1,)), ((), ())), preferred_element_type=F32)


def _dot_tn(a, b):
    return lax.dot_general(a, b, (((0,), (0,)), ((), ())), preferred_element_type=F32)


def _norm_matmul_kernel(x_ref, g_ref, w_ref, o_ref):
    h = _rms(x_ref[...], g_ref[...]).astype(BF16)
    o_ref[...] = _dot(h, w_ref[...])


def _norm_matmul(x, g, w, tm):
    m, d = x.shape
    n = w.shape[1]
    return pl.pallas_call(
        _norm_matmul_kernel,
        grid=(m // tm,),
        in_specs=[pl.BlockSpec((tm, d), lambda i: (i, 0)),
                  pl.BlockSpec((1, d), lambda i: (0, 0)),
                  pl.BlockSpec((d, n), lambda i: (0, 0))],
        out_specs=pl.BlockSpec((tm, n), lambda i: (i, 0)),
        out_shape=jax.ShapeDtypeStruct((m, n), F32),
        compiler_params=_cparams("parallel"),
        name="norm_matmul",
    )(x, g, w)


def _rope_tables(pos):
    half = ROT_DIM // 2
    inv = ROPE_THETA ** (-jnp.arange(half, dtype=F32) / half)
    ang = pos.astype(F32)[:, None] * inv[None, :]
    cos, sin = jnp.cos(ang), jnp.sin(ang)
    t = pos.shape[0]
    one = jnp.ones((t, D_A - ROT_DIM), F32)
    zero_h = jnp.zeros((t, half), F32)
    zero_r = jnp.zeros((t, D_A - ROT_DIM), F32)
    c = jnp.concatenate([cos, cos, one], axis=1)
    s_up = jnp.concatenate([-sin, zero_h, zero_r], axis=1)
    s_dn = jnp.concatenate([zero_h, sin, zero_r], axis=1)
    rep = LANE // D_A
    return jnp.tile(c, (1, rep)), jnp.tile(s_up, (1, rep)), jnp.tile(s_dn, (1, rep))


def _rope(x, c, s_up, s_dn):
    half = ROT_DIM // 2
    outs = []
    for j in range(x.shape[1] // LANE):
        xs = x[:, j * LANE:(j + 1) * LANE]
        up = pltpu.roll(xs, LANE - half, 1)
        dn = pltpu.roll(xs, half, 1)
        outs.append(xs * c + up * s_up + dn * s_dn)
    return jnp.concatenate(outs, axis=1)


def _layernorm(x, g, b):
    mu = jnp.mean(x, axis=-1, keepdims=True)
    xc = x - mu
    var = jnp.mean(xc * xc, axis=-1, keepdims=True)
    return xc * lax.rsqrt(var + EPS) * g + b


def _even_epilogue_kernel(z_ref, c_ref, su_ref, sd_ref, lng_ref, lnb_ref, wsp_ref, bspt_ref,
                          qt_ref, k_ref, vt_ref, kv_ref, kmean_ref, gv_ref, bout_ref):
    tm = z_ref.shape[0]
    c, su, sd = c_ref[...], su_ref[...], sd_ref[...]
    q = _rope(z_ref[:, 0:QKV_A], c, su, sd) * (D_A ** -0.5)
    k = _rope(z_ref[:, QKV_A:2 * QKV_A], c, su, sd)
    v = z_ref[:, 2 * QKV_A:3 * QKV_A]
    qt_ref[0] = q.T.astype(BF16)
    k_ref[...] = k.astype(BF16)
    vt_ref[0] = v.T.astype(BF16)
    kv_ref[:, 0:QKV_A] = k
    kv_ref[:, QKV_A:2 * QKV_A] = v
    for blk in range(tm // MOBA_BLOCK):
        kmean_ref[blk] = jnp.mean(k[blk * MOBA_BLOCK:(blk + 1) * MOBA_BLOCK], axis=0, keepdims=True)
    u = _gelu(z_ref[:, 3 * QKV_A:3 * QKV_A + D_B])
    gv = _layernorm(_gelu(z_ref[:, 3 * QKV_A + D_B:3 * QKV_A + 2 * D_B]), lng_ref[...], lnb_ref[...])
    gv_ref[...] = gv
    gvb = gv.astype(BF16)
    row = lax.broadcasted_iota(jnp.int32, (CHUNK_B, CHUNK_B), 0)
    col = lax.broadcasted_iota(jnp.int32, (CHUNK_B, CHUNK_B), 1)
    group = lax.broadcasted_iota(jnp.int32, (CHUNK_B, D_B), 1) // D_BG
    w = [jnp.where(row >= col, wsp_ref[g], 0.0).astype(BF16) for g in range(G_B)]
    for ch in range(tm // CHUNK_B):
        gvc = gvb[ch * CHUNK_B:(ch + 1) * CHUNK_B]
        mixed = jnp.zeros((CHUNK_B, D_B), F32)
        for g in range(G_B):
            mixed = jnp.where(group == g, _dot(w[g], gvc) + bspt_ref[:, g:g + 1], mixed)
        bout_ref[ch * CHUNK_B:(ch + 1) * CHUNK_B, :] = (u[ch * CHUNK_B:(ch + 1) * CHUNK_B] * mixed).astype(BF16)


def _even_epilogue(z, tables, ln_g, ln_b, w_sp, b_sp, batch, seq, tm):
    n = batch * seq
    nt = seq // tm
    nblk = tm // MOBA_BLOCK
    tab_spec = pl.BlockSpec((tm, LANE), lambda b, i: (i, 0))
    row_spec = lambda width: pl.BlockSpec((tm, width), lambda b, i: (b * nt + i, 0))
    t_spec = pl.BlockSpec((1, QKV_A, tm), lambda b, i: (b, 0, i))
    return pl.pallas_call(
        _even_epilogue_kernel,
        grid=(batch, nt),
        in_specs=[row_spec(IN_EVEN), tab_spec, tab_spec, tab_spec,
                  pl.BlockSpec((1, D_B), lambda b, i: (0, 0)),
                  pl.BlockSpec((1, D_B), lambda b, i: (0, 0)),
                  pl.BlockSpec((G_B, CHUNK_B, CHUNK_B), lambda b, i: (0, 0, 0)),
                  pl.BlockSpec((CHUNK_B, G_B), lambda b, i: (0, 0))],
        out_specs=[t_spec, row_spec(QKV_A), t_spec, row_spec(2 * QKV_A),
                   pl.BlockSpec((nblk, 1, QKV_A), lambda b, i: (b * nt + i, 0, 0)),
                   row_spec(D_B), row_spec(D_B)],
        out_shape=[jax.ShapeDtypeStruct((batch, QKV_A, seq), BF16),
                   jax.ShapeDtypeStruct((n, QKV_A), BF16),
                   jax.ShapeDtypeStruct((batch, QKV_A, seq), BF16),
                   jax.ShapeDtypeStruct((n, 2 * QKV_A), F32),
                   jax.ShapeDtypeStruct((n // MOBA_BLOCK, 1, QKV_A), F32),
                   jax.ShapeDtypeStruct((n, D_B), F32),
                   jax.ShapeDtypeStruct((n, D_B), BF16)],
        compiler_params=_cparams("parallel", "parallel"),
        name="even_epilogue",
    )(z, *tables, ln_g, ln_b, w_sp, b_sp.T)


def _moba_select(gate, n_own):
    nb = gate.shape[0]
    blk = lax.broadcasted_iota(jnp.int32, gate.shape, 0)
    elig = blk < n_own
    gm = jnp.where(elig, gate, NEG)
    rank = jnp.zeros(gate.shape, F32)
    for m in range(nb):
        gm_m = gm[m:m + 1, :]
        ahead = (gm_m > gm) | ((gm_m == gm) & (m < blk))
        rank = rank + ahead.astype(F32)
    return (elig & (rank < MOBA_TOPK)).astype(F32)


def _moba_prompt_kernel(qt_ref, k_ref, vt_ref, kmean_ref, o_ref, sel_ref):
    i = pl.program_id(2)
    tq = MOBA_BLOCK
    own = pl.multiple_of(i * tq, tq)
    key_i = lax.broadcasted_iota(jnp.int32, (tq, tq), 0)
    qry_i = lax.broadcasted_iota(jnp.int32, (tq, tq), 1)
    outs = []
    for hh in range(LANE // D_A):
        lanes = slice(hh * D_A, (hh + 1) * D_A)
        qt = qt_ref[0, lanes, :]
        gate = jnp.dot(kmean_ref[0, :, lanes], qt.astype(F32), precision=HI, preferred_element_type=F32)
        sel_ref[hh] = _moba_select(gate, i)
        s = _dot(k_ref[pl.ds(own, tq), lanes], qt)
        s = jnp.where(key_i <= qry_i, s, NEG)
        m = jnp.max(s, axis=0, keepdims=True)
        p = jnp.exp(s - m)
        l = jnp.sum(p, axis=0, keepdims=True)
        acc = _dot(vt_ref[0, lanes, pl.ds(own, tq)], p.astype(BF16))

        def body(j, carry, hh=hh, lanes=lanes, qt=qt):
            m, l, acc = carry
            start = pl.multiple_of(j * tq, tq)
            s = _dot(k_ref[pl.ds(start, tq), lanes], qt)
            s = jnp.where(sel_ref[hh, pl.ds(j, 1), :] > 0.0, s, NEG)
            m_new = jnp.maximum(m, jnp.max(s, axis=0, keepdims=True))
            alpha = jnp.exp(m - m_new)
            p = jnp.exp(s - m_new)
            l = l * alpha + jnp.sum(p, axis=0, keepdims=True)
            acc = acc * alpha + _dot(vt_ref[0, lanes, pl.ds(start, tq)], p.astype(BF16))
            return m_new, l, acc

        m, l, acc = lax.fori_loop(0, i, body, (m, l, acc))
        outs.append(acc / l)
    o_ref[...] = jnp.concatenate(outs, axis=0).T.astype(BF16)


def _moba_prompt(qt, k, vt, kmean, batch, seq):
    nq = seq // MOBA_BLOCK
    hp = QKV_A // LANE
    return pl.pallas_call(
        _moba_prompt_kernel,
        grid=(batch, hp, nq),
        in_specs=[pl.BlockSpec((1, LANE, MOBA_BLOCK), lambda b, h, i: (b, h, i)),
                  pl.BlockSpec((seq, LANE), lambda b, h, i: (b, h)),
                  pl.BlockSpec((1, LANE, seq), lambda b, h, i: (b, h, 0)),
                  pl.BlockSpec((1, nq, LANE), lambda b, h, i: (b, 0, h))],
        out_specs=pl.BlockSpec((MOBA_BLOCK, LANE), lambda b, h, i: (b * nq + i, h)),
        out_shape=jax.ShapeDtypeStruct((batch * seq, QKV_A), BF16),
        scratch_shapes=[pltpu.VMEM((LANE // D_A, nq, MOBA_BLOCK), F32)],
        compiler_params=_cparams("parallel", "parallel", "arbitrary"),
        name="moba_prompt",
    )(qt, k, vt, kmean)


def _proj_ffn_kernel(*refs, n_mix):
    mix_refs = refs[:n_mix]
    x_ref, wo_ref, g_ref, wg_ref, wu_ref, wd_ref, o_ref, hn_ref = refs[n_mix:]

    @pl.when(pl.program_id(1) == 0)
    def _():
        x1 = x_ref[...]
        off = 0
        for r in mix_refs:
            width = r.shape[1]
            x1 = x1 + _dot(r[...], wo_ref[off:off + width, :])
            off += width
        o_ref[...] = x1
        hn_ref[...] = _rms(x1, g_ref[...]).astype(BF16)

    h = hn_ref[...]
    act = (_silu(_dot(h, wg_ref[...])) * _dot(h, wu_ref[...])).astype(BF16)
    o_ref[...] += _dot(act, wd_ref[...])


def _proj_ffn(mixes, x, w_out, g, w_gate, w_up, w_down, tm, tf):
    m, d = x.shape
    ff = w_gate.shape[1]
    mix_specs = [pl.BlockSpec((tm, a.shape[1]), lambda i, f: (i, 0)) for a in mixes]
    return pl.pallas_call(
        functools.partial(_proj_ffn_kernel, n_mix=len(mixes)),
        grid=(m // tm, ff // tf),
        in_specs=mix_specs + [
            pl.BlockSpec((tm, d), lambda i, f: (i, 0)),
            pl.BlockSpec(w_out.shape, lambda i, f: (0, 0)),
            pl.BlockSpec((1, d), lambda i, f: (0, 0)),
            pl.BlockSpec((d, tf), lambda i, f: (0, f)),
            pl.BlockSpec((d, tf), lambda i, f: (0, f)),
            pl.BlockSpec((tf, d), lambda i, f: (f, 0))],
        out_specs=pl.BlockSpec((tm, d), lambda i, f: (i, 0)),
        out_shape=jax.ShapeDtypeStruct((m, d), F32),
        scratch_shapes=[pltpu.VMEM((tm, d), BF16)],
        compiler_params=_cparams("parallel", "arbitrary"),
        name="proj_ffn",
    )(*mixes, x, w_out, g, w_gate, w_up, w_down)


def _even_layer_prompt(x, p, batch, seq):
    z = _norm_matmul(x, p["norm_mix"], p["w_in"], 512)
    tables = _rope_tables(jnp.arange(seq, dtype=jnp.int32))
    qt, k, vt, kv, kmean, gv, b_out = _even_epilogue(
        z, tables, p["ln_g"], p["ln_b"], p["w_sp"], p["b_sp"], batch, seq, 256)
    a_out = _moba_prompt(qt, k, vt, kmean.reshape(batch, seq // MOBA_BLOCK, QKV_A), batch, seq)
    x = _proj_ffn([a_out, b_out], x, p["w_out"], p["norm_ffn"], p["w_gate"], p["w_up"], p["w_down"], 512, 256)
    n_open = seq - ((seq - 1) // CHUNK_B) * CHUNK_B
    gv_open = gv.reshape(batch, seq, D_B)[:, seq - n_open:]
    return x, kv.reshape(batch, seq, 2, H_A, D_A), gv_open


def _even_params(i, ev_norm_mix, ev_w_in, ev_gmlp_ln_g, ev_gmlp_ln_b, ev_w_spatial, ev_b_spatial, ev_w_out,
                 ev_norm_ffn, ev_w_gate, ev_w_up, ev_w_down):
    return dict(norm_mix=ev_norm_mix[i][None], w_in=ev_w_in[i].astype(BF16),
                ln_g=ev_gmlp_ln_g[i][None], ln_b=ev_gmlp_ln_b[i][None],
                w_sp=ev_w_spatial[i], b_sp=ev_b_spatial[i], w_out=ev_w_out[i].astype(BF16),
                norm_ffn=ev_norm_ffn[i][None], w_gate=ev_w_gate[i].astype(BF16),
                w_up=ev_w_up[i].astype(BF16), w_down=ev_w_down[i].astype(BF16))


O_CQ, O_CK, O_CV, O_CR, O_DQKV, O_DG, O_SM = 0, 256, 512, 1024, 1536, 2560, 3072
SM_DA, SM_DB = GATE_RANK, GATE_RANK + H_D
IN_ODD_PAD = O_SM + LANE
KEY_C = H_C * DK_C
PAIR = LANE // DK_C


def _log_sigmoid(x):
    return jnp.minimum(x, 0.0) - jnp.log1p(jnp.exp(-jnp.abs(x)))


def _split_bf16(a):
    hi = a.astype(BF16)
    return hi, (a - hi.astype(F32)).astype(BF16)


def _dot3(a, b):
    ah, al = _split_bf16(a)
    bh, bl = _split_bf16(b)
    return _dot(ah, bh) + _dot(ah, bl) + _dot(al, bh)


def _unit_lower_inverse(a):
    c = a.shape[0]
    eye = (lax.broadcasted_iota(jnp.int32, (c, c), 0) == lax.broadcasted_iota(jnp.int32, (c, c), 1)).astype(F32)
    x = -a
    p = eye + x
    steps = int(math.log2(c)) - 1
    for _ in range(steps):
        x = _dot3(x, x)
        p = p + _dot3(p, x)
    return p


def _group_sumsq(y, width):
    n = y.shape[1]
    same = (lax.broadcasted_iota(jnp.int32, (n, n), 0) // width
            == lax.broadcasted_iota(jnp.int32, (n, n), 1) // width).astype(F32)
    return jnp.dot(y * y, same, precision=HI, preferred_element_type=F32)


def _odd_mixer_prompt_kernel(z_ref, wg2_ref, bg2_ref, gnorm_ref, cw_ref, alog_ref, dtb_ref, dnorm_ref,
                             o_ref, sg_ref, sd_ref, tail_ref,
                             stg_ref, std_ref, prev_ref, la_ref, qkv_ref, dla_ref, beta_ref):
    i = pl.program_id(1)
    tc = z_ref.shape[0]
    c = CHUNK_LIN

    @pl.when(i == 0)
    def _():
        stg_ref[...] = jnp.zeros_like(stg_ref)
        std_ref[...] = jnp.zeros_like(std_ref)
        prev_ref[...] = jnp.zeros_like(prev_ref)

    small = z_ref[:, O_SM:O_SM + LANE]
    pre = jnp.dot(small, wg2_ref[...], precision=HI, preferred_element_type=F32) + bg2_ref[...]
    la_ref[...] = _log_sigmoid(pre) / GATE_NORM
    dla_ref[...] = -jnp.exp(alog_ref[...]) * _softplus(small + dtb_ref[...])
    beta_ref[...] = jax.nn.sigmoid(small)

    x = z_ref[:, O_DQKV:O_DQKV + C_CONV]
    x8 = x[0:8]
    p8 = prev_ref[...]
    row8 = lax.broadcasted_iota(jnp.int32, (8, C_CONV), 0)
    y = x * cw_ref[CONV_W - 1:CONV_W, :]
    y8 = x8 * cw_ref[CONV_W - 1:CONV_W, :]
    for s in range(1, CONV_W):
        wrow = cw_ref[CONV_W - 1 - s:CONV_W - s, :]
        y = y + pltpu.roll(x, s, 0) * wrow
        y8 = y8 + jnp.where(row8 < s, pltpu.roll(p8, s, 0), pltpu.roll(x8, s, 0)) * wrow
    prev_ref[...] = x[tc - 8:tc]
    y = _silu(y)
    y8 = _silu(y8)
    yqk = y[:, 0:2 * KEY_C]
    nrm = lax.rsqrt(_group_sumsq(yqk, DK_D) + EPS)
    qscale = jnp.where(lax.broadcasted_iota(jnp.int32, (1, 2 * KEY_C), 1) < KEY_C, DK_D ** -0.5, 1.0)
    qkv_ref[:, 0:2 * KEY_C] = yqk * nrm * qscale
    qkv_ref[:, 2 * KEY_C:] = y[:, 2 * KEY_C:]
    yqk8 = y8[:, 0:2 * KEY_C]
    qkv_ref[0:8, 0:2 * KEY_C] = yqk8 * lax.rsqrt(_group_sumsq(yqk8, DK_D) + EPS) * qscale
    qkv_ref[0:8, 2 * KEY_C:] = y8[:, 2 * KEY_C:]

    ri = lax.broadcasted_iota(jnp.int32, (c, c), 0)
    ci = lax.broadcasted_iota(jnp.int32, (c, c), 1)
    tril = ri >= ci
    strict = ri > ci
    lower = tril.astype(F32)
    upper = (ri <= ci).astype(F32)
    ones = jnp.ones((c, c), F32)
    lane_head = lax.broadcasted_iota(jnp.int32, (c, LANE), 1) // DK_C
    lane_head_row = lax.broadcasted_iota(jnp.int32, (1, LANE), 1) // DK_C

    def chunk(ch, carry):
        rows = pl.ds(pl.multiple_of(ch * c, c), c)
        for p in range(H_C // PAIR):
            lanes = slice(p * LANE, (p + 1) * LANE)
            bcum = jnp.dot(lower, la_ref[rows, lanes], precision=HI, preferred_element_type=F32)
            b_end = bcum[c - 1:c, :]
            q_in = z_ref[rows, O_CQ + p * LANE:O_CQ + (p + 1) * LANE] * (DK_C ** -0.5) * jnp.exp(bcum)
            k = z_ref[rows, O_CK + p * LANE:O_CK + (p + 1) * LANE]
            k_in = (k * jnp.exp(-bcum)).astype(BF16)
            k_end = k * jnp.exp(b_end - bcum)
            st = stg_ref[p]
            stb = st.astype(BF16)
            upd = jnp.zeros((DV_C, LANE), F32)
            for hh in range(PAIR):
                h = p * PAIR + hh
                mask = lane_head == hh
                qm = jnp.where(mask, q_in, 0.0).astype(BF16)
                sc = jnp.where(tril, _dot_nt(qm, k_in), 0.0)
                vb = z_ref[rows, O_CV + h * DV_C:O_CV + (h + 1) * DV_C].astype(BF16)
                o = _dot(sc.astype(BF16), vb) + _dot_nt(qm, stb)
                upd = upd + _dot_tn(vb, jnp.where(mask, k_end, 0.0).astype(BF16))
                gate = _silu(z_ref[rows, O_CR + h * DV_C:O_CR + (h + 1) * DV_C])
                o_ref[rows, h * DV_C:(h + 1) * DV_C] = (_rms(o, gnorm_ref[...]) * gate).astype(BF16)
            stg_ref[p] = st * jnp.exp(b_end) + upd
        for p in range(H_D // PAIR):
            q2 = qkv_ref[rows, p * LANE:(p + 1) * LANE]
            k2 = qkv_ref[rows, KEY_C + p * LANE:KEY_C + (p + 1) * LANE]
            st = std_ref[p]
            stb = st.astype(BF16)
            upd = jnp.zeros((DV_D, LANE), F32)
            dec_row = jnp.zeros((1, LANE), F32)
            for hh in range(PAIR):
                h = p * PAIR + hh
                mask = lane_head == hh
                la_b = jnp.broadcast_to(dla_ref[rows, SM_DA + h:SM_DA + h + 1], (c, LANE))
                beta_b = jnp.broadcast_to(beta_ref[rows, SM_DB + h:SM_DB + h + 1], (c, LANE))
                g_col = jnp.dot(lower, la_b, precision=HI, preferred_element_type=F32)
                g_row = jnp.dot(ones, la_b[:, 0:c] * upper, precision=HI, preferred_element_type=F32)
                decay = jnp.exp(jnp.where(tril, g_col[:, 0:c] - g_row, NEG))
                eg = jnp.exp(g_col)
                g_last = g_col[c - 1:c, :]
                km = jnp.where(mask, k2, 0.0)
                qm = jnp.where(mask, q2, 0.0)
                kmb = km.astype(BF16)
                kb = km * beta_b
                a_mat = jnp.where(strict, _dot_nt(kb.astype(BF16), kmb) * decay, 0.0)
                t_inv = _unit_lower_inverse(a_mat)
                v = qkv_ref[rows, 2 * KEY_C + h * DV_D:2 * KEY_C + (h + 1) * DV_D]
                u = _dot3(t_inv, v * beta_b)
                w = _dot3(t_inv, kb * eg)
                qk = jnp.where(tril, _dot_nt(qm.astype(BF16), kmb) * decay, 0.0)
                v_new = u - _dot_nt(w.astype(BF16), stb)
                vnb = v_new.astype(BF16)
                o = _dot_nt((qm * eg).astype(BF16), stb) + _dot(qk.astype(BF16), vnb)
                upd = upd + _dot_tn(vnb, (km * jnp.exp(g_last - g_col)).astype(BF16))
                dec_row = jnp.where(lane_head_row == hh, jnp.exp(g_last), dec_row)
                gate = _silu(z_ref[rows, O_DG + h * DV_D:O_DG + (h + 1) * DV_D])
                col = H_C * DV_C + h * DV_D
                o_ref[rows, col:col + DV_D] = (_rms(o, dnorm_ref[...]) * gate).astype(BF16)
            std_ref[p] = st * dec_row + upd
        return carry

    lax.fori_loop(0, tc // c, chunk, 0)

    @pl.when(i == pl.num_programs(1) - 1)
    def _():
        tail_ref[0] = prev_ref[...]
        for p in range(H_C // PAIR):
            tg = stg_ref[p].T
            td = std_ref[p].T
            for hh in range(PAIR):
                sg_ref[0, p * PAIR + hh] = tg[hh * DK_C:(hh + 1) * DK_C, :]
                sd_ref[0, p * PAIR + hh] = td[hh * DK_D:(hh + 1) * DK_D, :]


def _odd_mixer_prompt(z, p, batch, seq, tc):
    nt = seq // tc
    full = lambda a: pl.BlockSpec(a.shape, lambda b, i: (0,) * a.ndim)
    consts = [p["w_g2_pad"], p["b_g2"], p["gla_norm"], p["conv_w"], p["alog_slab"], p["dtb_slab"], p["delta_norm"]]
    st_spec = pl.BlockSpec((1, H_C, DK_C, DV_C), lambda b, i: (b, 0, 0, 0))
    return pl.pallas_call(
        _odd_mixer_prompt_kernel,
        grid=(batch, nt),
        in_specs=[pl.BlockSpec((tc, IN_ODD_PAD), lambda b, i: (b * nt + i, 0))] + [full(a) for a in consts],
        out_specs=[pl.BlockSpec((tc, D_MODEL), lambda b, i: (b * nt + i, 0)), st_spec, st_spec,
                   pl.BlockSpec((1, 8, C_CONV), lambda b, i: (b, 0, 0))],
        out_shape=[jax.ShapeDtypeStruct((batch * seq, D_MODEL), BF16),
                   jax.ShapeDtypeStruct((batch, H_C, DK_C, DV_C), F32),
                   jax.ShapeDtypeStruct((batch, H_D, DK_D, DV_D), F32),
                   jax.ShapeDtypeStruct((batch, 8, C_CONV), F32)],
        scratch_shapes=[pltpu.VMEM((H_C // PAIR, DV_C, LANE), F32),
                        pltpu.VMEM((H_D // PAIR, DV_D, LANE), F32),
                        pltpu.VMEM((8, C_CONV), F32),
                        pltpu.VMEM((tc, KEY_C), F32),
                        pltpu.VMEM((tc, C_CONV), F32),
                        pltpu.VMEM((tc, LANE), F32),
                        pltpu.VMEM((tc, LANE), F32)],
        compiler_params=_cparams("parallel", "arbitrary"),
        name="odd_mixer_prompt",
    )(z, *consts)


M_E1, M_E2, M_R1, M_R2 = N_EXPERTS, N_EXPERTS + 1, N_EXPERTS + 2, N_EXPERTS + 3


def _proj_router_kernel(mix_ref, x_ref, wo_ref, g_ref, wr_ref, x1_ref, hn_ref, meta_ref, cnt_ref, carry_ref):
    tm = x_ref.shape[0]

    @pl.when(pl.program_id(0) == 0)
    def _():
        carry_ref[...] = jnp.zeros_like(carry_ref)

    x1 = x_ref[...] + _dot(mix_ref[...], wo_ref[...])
    x1_ref[...] = x1
    hn = _rms(x1, g_ref[...])
    hn_ref[...] = hn.astype(BF16)
    lane = lax.broadcasted_iota(jnp.int32, (tm, LANE), 1)
    logits = jnp.dot(hn, wr_ref[...], precision=HI, preferred_element_type=F32)
    logits = jnp.where(lane < N_EXPERTS, logits, NEG)
    m1 = jnp.max(logits, axis=1, keepdims=True)
    e1 = jnp.min(jnp.where(logits == m1, lane, LANE), axis=1, keepdims=True)
    rest = jnp.where(lane == e1, NEG, logits)
    m2 = jnp.max(rest, axis=1, keepdims=True)
    e2 = jnp.min(jnp.where(rest == m2, lane, LANE), axis=1, keepdims=True)
    t = jnp.exp(m2 - m1)
    g1 = 1.0 / (1.0 + t)
    g2 = t / (1.0 + t)
    oh1 = lane == e1
    oh2 = lane == e2
    member = (oh1 | oh2).astype(F32)
    ri = lax.broadcasted_iota(jnp.int32, (tm, tm), 0)
    ci = lax.broadcasted_iota(jnp.int32, (tm, tm), 1)
    before = _dot((ri > ci).astype(BF16), member.astype(BF16)) + carry_ref[...]
    r1 = jnp.sum(jnp.where(oh1, before, 0.0), axis=1, keepdims=True)
    r2 = jnp.sum(jnp.where(oh2, before, 0.0), axis=1, keepdims=True)
    carry_ref[...] = carry_ref[...] + jnp.sum(member, axis=0, keepdims=True)
    cnt_ref[...] = carry_ref[...]
    meta = jnp.where(oh1, g1, 0.0) + jnp.where(oh2, g2, 0.0)
    meta = jnp.where(lane == M_E1, e1.astype(F32), meta)
    meta = jnp.where(lane == M_E2, e2.astype(F32), meta)
    meta = jnp.where(lane == M_R1, r1, meta)
    meta = jnp.where(lane == M_R2, r2, meta)
    meta_ref[...] = meta


def _proj_router(mix, x, w_out, g, w_router_pad, tm):
    m, d = x.shape
    row = lambda width: pl.BlockSpec((tm, width), lambda i: (i, 0))
    full = lambda a: pl.BlockSpec(a.shape, lambda i: (0,) * a.ndim)
    return pl.pallas_call(
        _proj_router_kernel,
        grid=(m // tm,),
        in_specs=[row(d), row(d), full(w_out), full(g), full(w_router_pad)],
        out_specs=[row(d), row(d), row(LANE), pl.BlockSpec((1, LANE), lambda i: (0, 0))],
        out_shape=[jax.ShapeDtypeStruct((m, d), F32), jax.ShapeDtypeStruct((m, d), BF16),
                   jax.ShapeDtypeStruct((m, LANE), F32), jax.ShapeDtypeStruct((1, LANE), F32)],
        scratch_shapes=[pltpu.VMEM((1, LANE), F32)],
        compiler_params=_cparams("arbitrary"),
        name="proj_router",
    )(mix, x, w_out, g, w_router_pad)


def _moe_dense_kernel(hn_ref, x1_ref, meta_ref, wg_ref, wu_ref, wd_ref, gf_ref, o_ref):
    e = pl.program_id(1)
    f = pl.program_id(2)

    @pl.when((e == 0) & (f == 0))
    def _():
        o_ref[...] = x1_ref[...]

    h = hn_ref[...]
    act = (_silu(_dot(h, wg_ref[...])) * _dot(h, wu_ref[...])).astype(BF16)
    lane = lax.broadcasted_iota(jnp.int32, meta_ref.shape, 1)
    gate = jnp.sum(jnp.where(lane == e, meta_ref[...], 0.0), axis=1, keepdims=True)
    o_ref[...] += gate * _dot(act, wd_ref[...])

    @pl.when((e == pl.num_programs(1) - 1) & (f == pl.num_programs(2) - 1))
    def _():
        o_ref[...] = _rms(o_ref[...], gf_ref[...])


def _moe_dense(hn, x1, meta, w_gate, w_up, w_down, g_final, tm, tf):
    m, d = x1.shape
    n_e, _, ff = w_gate.shape
    row = lambda width: pl.BlockSpec((tm, width), lambda i, e, f: (i, 0))
    return pl.pallas_call(
        _moe_dense_kernel,
        grid=(m // tm, n_e, ff // tf),
        in_specs=[row(d), row(d), row(LANE),
                  pl.BlockSpec((None, d, tf), lambda i, e, f: (e, 0, f)),
                  pl.BlockSpec((None, d, tf), lambda i, e, f: (e, 0, f)),
                  pl.BlockSpec((None, tf, d), lambda i, e, f: (e, f, 0)),
                  pl.BlockSpec((1, d), lambda i, e, f: (0, 0))],
        out_specs=row(d),
        out_shape=jax.ShapeDtypeStruct((m, d), F32),
        compiler_params=_cparams("parallel", "arbitrary", "arbitrary"),
        name="moe_dense",
    )(hn, x1, meta, w_gate, w_up, w_down, g_final)


def _odd_params(i, od_norm_mix, od_w_in, od_gla_w_gate2, od_gla_b_gate2, od_gla_norm, od_delta_conv,
                od_delta_a_log, od_delta_dt_bias, od_delta_norm, od_w_out, od_norm_ffn, od_router,
                od_w_gate, od_w_up, od_w_down):
    w = od_w_in[i]
    sizes = (KEY_C, KEY_C, H_C * DV_C, GATE_RANK, H_C * DV_C, C_CONV, H_D, H_D, H_D * DV_D)
    splits = tuple(sum(sizes[:j + 1]) for j in range(len(sizes) - 1))
    cq, ck, cv, c_lr, c_r, d_qkv, d_a, d_b, d_g = jnp.split(w, splits, axis=1)
    pad = jnp.zeros((w.shape[0], IN_ODD_PAD - w.shape[1]), w.dtype)
    w_in = jnp.concatenate([cq, ck, cv, c_r, d_qkv, d_g, c_lr, d_a, d_b, pad], axis=1).astype(BF16)
    slab = lambda v: jnp.zeros((1, LANE), F32).at[0, SM_DA:SM_DA + H_D].set(v)
    return dict(norm_mix=od_norm_mix[i][None], w_in=w_in,
                w_g2_pad=jnp.zeros((LANE, KEY_C), F32).at[:GATE_RANK].set(od_gla_w_gate2[i]),
                b_g2=od_gla_b_gate2[i][None], gla_norm=od_gla_norm[i][None], conv_w=od_delta_conv[i],
                alog_slab=slab(od_delta_a_log[i]), dtb_slab=slab(od_delta_dt_bias[i]),
                a_log=od_delta_a_log[i], dt_bias=od_delta_dt_bias[i],
                delta_norm=od_delta_norm[i][None], w_out=od_w_out[i].astype(BF16),
                norm_ffn=od_norm_ffn[i][None],
                router=jnp.zeros((D_MODEL, LANE), F32).at[:, :N_EXPERTS].set(od_router[i]),
                w_gate=od_w_gate[i].astype(BF16), w_up=od_w_up[i].astype(BF16), w_down=od_w_down[i].astype(BF16))


def _moe(mix, x, p, g_final, tm):
    x1, hn, meta, _ = _proj_router(mix, x, p["w_out"], p["norm_ffn"], p["router"], tm)
    return _moe_dense(hn, x1, meta, p["w_gate"], p["w_up"], p["w_down"], g_final, tm, 512)


def _odd_layer_prompt(x, p, g_final, batch, seq):
    z = _norm_matmul(x, p["norm_mix"], p["w_in"], 512)
    mix, s_gla, s_delta, tail = _odd_mixer_prompt(z, p, batch, seq, 256)
    y = _moe(mix, x, p, g_final, 512)
    return y, s_gla, s_delta, tail[:, 8 - (CONV_W - 1):]


def _even_epilogue_sample_kernel(z_ref, c_ref, su_ref, sd_ref, lng_ref, lnb_ref, w0_ref, b0_ref,
                                 q_ref, kv_ref, gv_ref, bout_ref):
    c, su, sd = c_ref[...], su_ref[...], sd_ref[...]
    q_ref[...] = _rope(z_ref[:, 0:QKV_A], c, su, sd) * (D_A ** -0.5)
    kv_ref[:, 0:QKV_A] = _rope(z_ref[:, QKV_A:2 * QKV_A], c, su, sd)
    kv_ref[:, QKV_A:2 * QKV_A] = z_ref[:, 2 * QKV_A:3 * QKV_A]
    u = _gelu(z_ref[:, 3 * QKV_A:3 * QKV_A + D_B])
    gv = _layernorm(_gelu(z_ref[:, 3 * QKV_A + D_B:3 * QKV_A + 2 * D_B]), lng_ref[...], lnb_ref[...])
    gv_ref[...] = gv
    bout_ref[...] = (u * (gv * w0_ref[...] + b0_ref[...])).astype(BF16)


def _even_epilogue_sample(z, tables, ln_g, ln_b, w_sp, b_sp):
    m = z.shape[0]
    w0 = jnp.repeat(w_sp[:, 0, 0], D_BG)[None]
    b0 = jnp.repeat(b_sp[:, 0], D_BG)[None]
    return pl.pallas_call(
        _even_epilogue_sample_kernel,
        out_shape=[jax.ShapeDtypeStruct((m, QKV_A), F32), jax.ShapeDtypeStruct((m, 2 * QKV_A), F32),
                   jax.ShapeDtypeStruct((m, D_B), F32), jax.ShapeDtypeStruct((m, D_B), BF16)],
        compiler_params=pltpu.CompilerParams(vmem_limit_bytes=VMEM_LIMIT),
        name="even_epilogue_sample",
    )(z, *tables, ln_g, ln_b, w0, b0)


def _moba_sample_kernel(pt_ref, q_ref, kvn_ref, *refs):
    del pt_ref
    n_pages = len(refs) - 2
    page_refs, o_ref, s_ref = refs[:n_pages], refs[n_pages], refs[n_pages + 1]
    rows = PAGE_SIZE * H_A
    pages_per_block = MOBA_BLOCK // PAGE_SIZE
    nb = n_pages // pages_per_block
    q8 = q_ref[0]
    q8b = q8.astype(BF16)
    own_head = (lax.broadcasted_iota(jnp.int32, (H_A, rows), 1) % H_A
                == lax.broadcasted_iota(jnp.int32, (H_A, rows), 0))
    for j in range(n_pages):
        kp = page_refs[j][0, :, 0].reshape(rows, D_A).astype(BF16)
        s_all = _dot_nt(q8b, kp)
        s_ref[j:j + 1, :] = jnp.sum(jnp.where(own_head, s_all, 0.0), axis=0, keepdims=True)
    s = s_ref[...]
    ind = (lax.broadcasted_iota(jnp.int32, (rows, LANE), 0) % H_A
           == lax.broadcasted_iota(jnp.int32, (rows, LANE), 1)).astype(F32)
    ind_t = (lax.broadcasted_iota(jnp.int32, (LANE, rows), 1) % H_A
             == lax.broadcasted_iota(jnp.int32, (LANE, rows), 0)).astype(F32)
    pair = (lax.broadcasted_iota(jnp.int32, (nb, n_pages), 1) // pages_per_block
            == lax.broadcasted_iota(jnp.int32, (nb, n_pages), 0)).astype(F32)
    pair_t = (lax.broadcasted_iota(jnp.int32, (n_pages, nb), 0) // pages_per_block
              == lax.broadcasted_iota(jnp.int32, (n_pages, nb), 1)).astype(F32)
    hsum = lambda a: jnp.dot(a, ind, precision=HI, preferred_element_type=F32)
    spread = lambda a: jnp.dot(a, ind_t, precision=HI, preferred_element_type=F32)
    gate = jnp.dot(pair, hsum(s), precision=HI, preferred_element_type=F32)
    sel = _moba_select(gate, nb)
    sel_keys = spread(jnp.dot(pair_t, sel, precision=HI, preferred_element_type=F32))
    k_new = kvn_ref[0, 0]
    v_new = kvn_ref[0, 1]
    own_col = jnp.sum(q8 * k_new, axis=1, keepdims=True)
    eye = (lax.broadcasted_iota(jnp.int32, (H_A, LANE), 0) == lax.broadcasted_iota(jnp.int32, (H_A, LANE), 1))
    to_row = lambda col: jnp.sum(jnp.where(eye, col, 0.0), axis=0, keepdims=True)
    to_col = lambda row: jnp.sum(jnp.where(eye, row, 0.0), axis=1, keepdims=True)
    own_row = spread(to_row(own_col))
    sm = jnp.where(sel_keys > 0.0, s, NEG)
    mx = jnp.maximum(jnp.max(sm, axis=0, keepdims=True), own_row)
    shift = H_A
    while shift < rows:
        mx = jnp.maximum(mx, pltpu.roll(mx, shift, 1))
        shift *= 2
    p = jnp.exp(sm - mx)
    p_own = jnp.exp(own_row - mx)[:, 0:LANE]
    denom = to_col(hsum(jnp.sum(p, axis=0, keepdims=True)) + p_own)
    acc = to_col(p_own) * v_new
    for j in range(n_pages):
        vp = page_refs[j][0, :, 1].reshape(rows, D_A).astype(BF16)
        p_bd = jnp.where(own_head, jnp.broadcast_to(p[j:j + 1, :], (H_A, rows)), 0.0).astype(BF16)
        acc = acc + _dot(p_bd, vp)
    o_ref[0] = (acc / denom).astype(BF16)


def _moba_sample(q, kv_new, cache, page_table):
    bs, n_pages = page_table.shape
    assert (n_pages * PAGE_SIZE) % MOBA_BLOCK == 0
    page_spec = lambda j: pl.BlockSpec((1, PAGE_SIZE, 2, H_A, D_A),
                                       lambda b, pt: (pt[b * n_pages + j], 0, 0, 0, 0))
    return pl.pallas_call(
        _moba_sample_kernel,
        grid_spec=pltpu.PrefetchScalarGridSpec(
            num_scalar_prefetch=1,
            grid=(bs,),
            in_specs=[pl.BlockSpec((1, H_A, D_A), lambda b, pt: (b, 0, 0)),
                      pl.BlockSpec((1, 2, H_A, D_A), lambda b, pt: (b, 0, 0, 0))]
                     + [page_spec(j) for j in range(n_pages)],
            out_specs=pl.BlockSpec((1, H_A, D_A), lambda b, pt: (b, 0, 0)),
            scratch_shapes=[pltpu.VMEM((n_pages, PAGE_SIZE * H_A), F32)]),
        out_shape=jax.ShapeDtypeStruct((bs, H_A, D_A), BF16),
        compiler_params=_cparams("parallel"),
        name="moba_sample",
    )(page_table.reshape(-1), q.reshape(bs, H_A, D_A), kv_new.reshape(bs, 2, H_A, D_A),
      *([cache] * n_pages)).reshape(bs, QKV_A)


def _odd_mixer_sample_kernel(z_ref, buf_ref, sg_ref, sd_ref, wg2_ref, bg2_ref, gnorm_ref, cw_ref, alog_ref, dtb_ref,
                             dnorm_ref, o_ref, sgo_ref, sdo_ref):
    bt = z_ref.shape[0]
    stride = H_C * DK_C
    small = z_ref[:, O_SM:O_SM + LANE]
    pre = jnp.dot(small, wg2_ref[...], precision=HI, preferred_element_type=F32) + bg2_ref[...]
    a_all = jnp.exp(_log_sigmoid(pre) / GATE_NORM)
    g_all = -jnp.exp(alog_ref[...]) * _softplus(small + dtb_ref[...])
    beta_all = jax.nn.sigmoid(small)
    x = z_ref[:, O_DQKV:O_DQKV + C_CONV]
    y = x * cw_ref[CONV_W - 1:CONV_W, :]
    for j in range(CONV_W - 1):
        y = y + buf_ref[:, j, :] * cw_ref[j:j + 1, :]
    y = _silu(y)
    for h in range(H_C):
        ks = slice(h * DK_C, (h + 1) * DK_C)
        a = a_all[:, ks]
        q = z_ref[:, O_CQ + h * DK_C:O_CQ + (h + 1) * DK_C] * (DK_C ** -0.5)
        k = z_ref[:, O_CK + h * DK_C:O_CK + (h + 1) * DK_C]
        v = z_ref[:, O_CV + h * DV_C:O_CV + (h + 1) * DV_C]
        qa = q * a
        acc = jnp.sum(q * k, axis=1, keepdims=True) * v
        for kk in range(DK_C):
            rows = pl.ds(h * DK_C + kk, bt, stride=stride)
            srow = sg_ref[rows, :]
            acc = acc + qa[:, kk:kk + 1] * srow
            sgo_ref[rows, :] = a[:, kk:kk + 1] * srow + k[:, kk:kk + 1] * v
        gate = _silu(z_ref[:, O_CR + h * DV_C:O_CR + (h + 1) * DV_C])
        o_ref[:, h * DV_C:(h + 1) * DV_C] = (_rms(acc, gnorm_ref[...]) * gate).astype(BF16)
        yq = y[:, h * DK_D:(h + 1) * DK_D]
        yk = y[:, KEY_C + h * DK_D:KEY_C + (h + 1) * DK_D]
        dv = y[:, 2 * KEY_C + h * DV_D:2 * KEY_C + (h + 1) * DV_D]
        dq = yq * lax.rsqrt(jnp.sum(yq * yq, axis=1, keepdims=True) + EPS) * (DK_D ** -0.5)
        dk = yk * lax.rsqrt(jnp.sum(yk * yk, axis=1, keepdims=True) + EPS)
        beta = beta_all[:, SM_DB + h:SM_DB + h + 1]
        eg = jnp.exp(g_all[:, SM_DA + h:SM_DA + h + 1])
        w = dk * (beta * eg)
        qd = dq * eg
        ws = jnp.zeros((bt, DV_D), F32)
        qs = jnp.zeros((bt, DV_D), F32)
        for kk in range(DK_D):
            srow = sd_ref[pl.ds(h * DK_D + kk, bt, stride=stride), :]
            ws = ws + w[:, kk:kk + 1] * srow
            qs = qs + qd[:, kk:kk + 1] * srow
        v_new = dv * beta - ws
        o = qs + jnp.sum(dq * dk, axis=1, keepdims=True) * v_new
        for kk in range(DK_D):
            rows = pl.ds(h * DK_D + kk, bt, stride=stride)
            sdo_ref[rows, :] = sd_ref[rows, :] * eg + dk[:, kk:kk + 1] * v_new
        gate = _silu(z_ref[:, O_DG + h * DV_D:O_DG + (h + 1) * DV_D])
        col = H_C * DV_C + h * DV_D
        o_ref[:, col:col + DV_D] = (_rms(o, dnorm_ref[...]) * gate).astype(BF16)


def _odd_mixer_sample(z, conv_buf, s_gla, s_delta, p, bt):
    bs = z.shape[0]
    rows = H_C * DK_C
    full = lambda a: pl.BlockSpec(a.shape, lambda i: (0,) * a.ndim)
    consts = [p["w_g2_pad"], p["b_g2"], p["gla_norm"], p["conv_w"], p["alog_slab"], p["dtb_slab"], p["delta_norm"]]
    st_spec = pl.BlockSpec((bt * rows, DV_C), lambda i: (i, 0))
    mix, sg, sd = pl.pallas_call(
        _odd_mixer_sample_kernel,
        grid=(bs // bt,),
        in_specs=[pl.BlockSpec((bt, IN_ODD_PAD), lambda i: (i, 0)),
                  pl.BlockSpec((bt, CONV_W - 1, C_CONV), lambda i: (i, 0, 0)),
                  st_spec, st_spec] + [full(a) for a in consts],
        out_specs=[pl.BlockSpec((bt, D_MODEL), lambda i: (i, 0)), st_spec, st_spec],
        out_shape=[jax.ShapeDtypeStruct((bs, D_MODEL), BF16),
                   jax.ShapeDtypeStruct((bs * rows, DV_C), F32),
                   jax.ShapeDtypeStruct((bs * rows, DV_D), F32)],
        compiler_params=_cparams("parallel"),
        name="odd_mixer_sample",
    )(z, conv_buf, s_gla.reshape(bs * rows, DV_C), s_delta.reshape(bs * rows, DV_D), *consts)
    return mix, sg.reshape(s_gla.shape), sd.reshape(s_delta.shape)


def _sample_step(x, cache, page_table, s_gla, s_delta, conv_buf, ev, od, g_final):
    bs = x.shape[0]
    past = page_table.shape[1] * PAGE_SIZE
    z = _norm_matmul(x, ev["norm_mix"], ev["w_in"], bs)
    tables = _rope_tables(jnp.full((1,), past, jnp.int32))
    q, kv, gv, b_out = _even_epilogue_sample(z, tables, ev["ln_g"], ev["ln_b"], ev["w_sp"], ev["b_sp"])
    a_out = _moba_sample(q, kv, cache, page_table)
    x = _proj_ffn([a_out, b_out], x, ev["w_out"], ev["norm_ffn"], ev["w_gate"], ev["w_up"], ev["w_down"], bs, 256)
    z = _norm_matmul(x, od["norm_mix"], od["w_in"], bs)
    mix, sg, sd = _odd_mixer_sample(z, conv_buf, s_gla, s_delta, od, 32)
    conv_new = jnp.concatenate([conv_buf[:, 1:], z[:, None, O_DQKV:O_DQKV + C_CONV]], axis=1)
    y = _moe(mix, x, od, g_final, bs)
    return y, kv, gv, sg, sd, conv_new


def kernel(x_prompt, x_sample, cache_kv, state_gla, state_delta, state_conv, page_table, ev_norm_mix, ev_w_in, ev_gmlp_ln_g, ev_gmlp_ln_b, ev_w_spatial, ev_b_spatial, ev_w_out, ev_norm_ffn, ev_w_gate, ev_w_up, ev_w_down, od_norm_mix, od_w_in, od_gla_w_gate2, od_gla_b_gate2, od_gla_norm, od_delta_conv, od_delta_a_log, od_delta_dt_bias, od_delta_norm, od_w_out, od_norm_ffn, od_router, od_w_gate, od_w_up, od_w_down, norm_final):
    bp, tp, d = x_prompt.shape
    ev = _even_params(0, ev_norm_mix, ev_w_in, ev_gmlp_ln_g, ev_gmlp_ln_b, ev_w_spatial, ev_b_spatial, ev_w_out,
                      ev_norm_ffn, ev_w_gate, ev_w_up, ev_w_down)
    od = _odd_params(0, od_norm_mix, od_w_in, od_gla_w_gate2, od_gla_b_gate2, od_gla_norm, od_delta_conv,
                     od_delta_a_log, od_delta_dt_bias, od_delta_norm, od_w_out, od_norm_ffn, od_router,
                     od_w_gate, od_w_up, od_w_down)
    bs, ts, _ = x_sample.shape
    assert ts == 1 and cache_kv.shape[0] == 1 and state_gla.shape[0] == 1
    xp, kv_p, gv_p = _even_layer_prompt(x_prompt.reshape(bp * tp, d), ev, bp, tp)
    yp, gla_p, dl_p, cv_p = _odd_layer_prompt(xp, od, norm_final[None], bp, tp)
    ys, kv_s, gv_s, gla_s, dl_s, cv_s = _sample_step(
        x_sample.reshape(bs, d), cache_kv[0], page_table, state_gla[0], state_delta[0], state_conv[0],
        ev, od, norm_final[None])
    return (yp.reshape(bp, tp, d), ys.reshape(bs, ts, d),
            kv_p[None], kv_s.reshape(1, bs, ts, 2, H_A, D_A),
            gv_p[None], gv_s.reshape(1, bs, ts, D_B),
            gla_p[None], gla_s[None], dl_p[None], dl_s[None], cv_p[None], cv_s[None])
```

```python
import functools
import math

import jax
import jax.numpy as jnp
from jax import lax
from jax.experimental import pallas as pl
from jax.experimental.pallas import tpu as pltpu

F32 = jnp.float32
BF16 = jnp.bfloat16
HI = lax.Precision.HIGHEST
EPS = 1e-6
NEG = -1e30

D_MODEL = 1024
PAGE_SIZE = 128
H_A, D_A = 8, 64
ROT_DIM = D_A // 4
ROPE_THETA = 500000.0
MOBA_BLOCK = 256
MOBA_TOPK = 3
G_B, D_BG = 8, 64
D_B = G_B * D_BG
CHUNK_B = 128
H_C, DK_C, DV_C = 4, 64, 128
GATE_RANK = 16
GATE_NORM = 16.0
H_D, DK_D, DV_D = 4, 64, 128
CONV_W = 4
C_CONV = 2 * H_D * DK_D + H_D * DV_D
CHUNK_LIN = 64
N_EXPERTS = 8
TOP_K = 2
QKV_A = H_A * D_A
IN_EVEN = 3 * QKV_A + 2 * D_B
LANE = 128
VMEM_LIMIT = 56 * 1024 * 1024


def _cparams(*sem):
    return pltpu.CompilerParams(dimension_semantics=sem, vmem_limit_bytes=VMEM_LIMIT)


def _rms(x, g):
    return x * lax.rsqrt(jnp.mean(x * x, axis=-1, keepdims=True) + EPS) * g


def _gelu(x):
    return 0.5 * x * (1.0 + lax.erf(x * (2.0 ** -0.5)))


def _silu(x):
    return x * jax.nn.sigmoid(x)


def _softplus(x):
    return jnp.maximum(x, 0.0) + jnp.log1p(jnp.exp(-jnp.abs(x)))


def _dot(a, b):
    return jnp.dot(a, b, preferred_element_type=F32)


def _dot_nt(a, b):
    return lax.dot_general(a, b, (((1,), (1,)), ((), ())), preferred_element_type=F32)


def _dot_tn(a, b):
    return lax.dot_general(a, b, (((0,), (0,)), ((), ())), preferred_element_type=F32)


def _norm_matmul_kernel(x_ref, g_ref, w_ref, o_ref):
    h = _rms(x_ref[...], g_ref[...]).astype(BF16)
    o_ref[...] = _dot(h, w_ref[...])


def _norm_matmul(x, g, w, tm):
    m, d = x.shape
    n = w.shape[1]
    return pl.pallas_call(
        _norm_matmul_kernel,
        grid=(m // tm,),
        in_specs=[pl.BlockSpec((tm, d), lambda i: (i, 0)),
                  pl.BlockSpec((1, d), lambda i: (0, 0)),
                  pl.BlockSpec((d, n), lambda i: (0, 0))],
        out_specs=pl.BlockSpec((tm, n), lambda i: (i, 0)),
        out_shape=jax.ShapeDtypeStruct((m, n), F32),
        compiler_params=_cparams("parallel"),
        name="norm_matmul",
    )(x, g, w)


def _rope_tables(pos):
    half = ROT_DIM // 2
    inv = ROPE_THETA ** (-jnp.arange(half, dtype=F32) / half)
    ang = pos.astype(F32)[:, None] * inv[None, :]
    cos, sin = jnp.cos(ang), jnp.sin(ang)
    t = pos.shape[0]
    one = jnp.ones((t, D_A - ROT_DIM), F32)
    zero_h = jnp.zeros((t, half), F32)
    zero_r = jnp.zeros((t, D_A - ROT_DIM), F32)
    c = jnp.concatenate([cos, cos, one], axis=1)
    s_up = jnp.concatenate([-sin, zero_h, zero_r], axis=1)
    s_dn = jnp.concatenate([zero_h, sin, zero_r], axis=1)
    rep = LANE // D_A
    return jnp.tile(c, (1, rep)), jnp.tile(s_up, (1, rep)), jnp.tile(s_dn, (1, rep))


def _rope(x, c, s_up, s_dn):
    half = ROT_DIM // 2
    outs = []
    for j in range(x.shape[1] // LANE):
        xs = x[:, j * LANE:(j + 1) * LANE]
        up = pltpu.roll(xs, LANE - half, 1)
        dn = pltpu.roll(xs, half, 1)
        outs.append(xs * c + up * s_up + dn * s_dn)
    return jnp.concatenate(outs, axis=1)


def _layernorm(x, g, b):
    mu = jnp.mean(x, axis=-1, keepdims=True)
    xc = x - mu
    var = jnp.mean(xc * xc, axis=-1, keepdims=True)
    return xc * lax.rsqrt(var + EPS) * g + b


def _even_epilogue_kernel(z_ref, c_ref, su_ref, sd_ref, lng_ref, lnb_ref, wsp_ref, bspt_ref,
                          qt_ref, k_ref, vt_ref, kv_ref, kmean_ref, gv_ref, bout_ref):
    tm = z_ref.shape[0]
    c, su, sd = c_ref[...], su_ref[...], sd_ref[...]
    q = _rope(z_ref[:, 0:QKV_A], c, su, sd) * (D_A ** -0.5)
    k = _rope(z_ref[:, QKV_A:2 * QKV_A], c, su, sd)
    v = z_ref[:, 2 * QKV_A:3 * QKV_A]
    qt_ref[0] = q.T.astype(BF16)
    k_ref[...] = k.astype(BF16)
    vt = v.T
    vt_ref[0] = vt.astype(BF16)
    kv_ref[0, 0:QKV_A, :] = k.T
    kv_ref[0, QKV_A:2 * QKV_A, :] = vt
    for blk in range(tm // MOBA_BLOCK):
        kmean_ref[blk] = jnp.mean(k[blk * MOBA_BLOCK:(blk + 1) * MOBA_BLOCK], axis=0, keepdims=True)
    u = _gelu(z_ref[:, 3 * QKV_A:3 * QKV_A + D_B])
    gv = _layernorm(_gelu(z_ref[:, 3 * QKV_A + D_B:3 * QKV_A + 2 * D_B]), lng_ref[...], lnb_ref[...])
    gv_ref[...] = gv
    gvb = gv.astype(BF16)
    row = lax.broadcasted_iota(jnp.int32, (CHUNK_B, CHUNK_B), 0)
    col = lax.broadcasted_iota(jnp.int32, (CHUNK_B, CHUNK_B), 1)
    group = lax.broadcasted_iota(jnp.int32, (CHUNK_B, D_B), 1) // D_BG
    w = [jnp.where(row >= col, wsp_ref[g], 0.0).astype(BF16) for g in range(G_B)]
    for ch in range(tm // CHUNK_B):
        gvc = gvb[ch * CHUNK_B:(ch + 1) * CHUNK_B]
        mixed = jnp.zeros((CHUNK_B, D_B), F32)
        for g in range(G_B):
            mixed = jnp.where(group == g, _dot(w[g], gvc) + bspt_ref[:, g:g + 1], mixed)
        bout_ref[ch * CHUNK_B:(ch + 1) * CHUNK_B, :] = (u[ch * CHUNK_B:(ch + 1) * CHUNK_B] * mixed).astype(BF16)


def _even_epilogue(z, tables, ln_g, ln_b, w_sp, b_sp, batch, seq, tm):
    n = batch * seq
    nt = seq // tm
    nblk = tm // MOBA_BLOCK
    tab_spec = pl.BlockSpec((tm, LANE), lambda b, i: (i, 0))
    row_spec = lambda width: pl.BlockSpec((tm, width), lambda b, i: (b * nt + i, 0))
    t_spec = pl.BlockSpec((1, QKV_A, tm), lambda b, i: (b, 0, i))
    return pl.pallas_call(
        _even_epilogue_kernel,
        grid=(batch, nt),
        in_specs=[row_spec(IN_EVEN), tab_spec, tab_spec, tab_spec,
                  pl.BlockSpec((1, D_B), lambda b, i: (0, 0)),
                  pl.BlockSpec((1, D_B), lambda b, i: (0, 0)),
                  pl.BlockSpec((G_B, CHUNK_B, CHUNK_B), lambda b, i: (0, 0, 0)),
                  pl.BlockSpec((CHUNK_B, G_B), lambda b, i: (0, 0))],
        out_specs=[t_spec, row_spec(QKV_A), t_spec, pl.BlockSpec((1, 2 * QKV_A, tm), lambda b, i: (b, 0, i)),
                   pl.BlockSpec((nblk, 1, QKV_A), lambda b, i: (b * nt + i, 0, 0)),
                   row_spec(D_B), row_spec(D_B)],
        out_shape=[jax.ShapeDtypeStruct((batch, QKV_A, seq), BF16),
                   jax.ShapeDtypeStruct((n, QKV_A), BF16),
                   jax.ShapeDtypeStruct((batch, QKV_A, seq), BF16),
                   jax.ShapeDtypeStruct((batch, 2 * QKV_A, seq), F32),
                   jax.ShapeDtypeStruct((n // MOBA_BLOCK, 1, QKV_A), F32),
                   jax.ShapeDtypeStruct((n, D_B), F32),
                   jax.ShapeDtypeStruct((n, D_B), BF16)],
        compiler_params=_cparams("parallel", "parallel"),
        name="even_epilogue",
    )(z, *tables, ln_g, ln_b, w_sp, b_sp.T)


def _moba_select(gate, n_own):
    nb = gate.shape[0]
    blk = lax.broadcasted_iota(jnp.int32, gate.shape, 0)
    elig = blk < n_own
    gm = jnp.where(elig, gate, NEG)
    rank = jnp.zeros(gate.shape, F32)
    for m in range(nb):
        gm_m = gm[m:m + 1, :]
        ahead = (gm_m > gm) | ((gm_m == gm) & (m < blk))
        rank = rank + ahead.astype(F32)
    return (elig & (rank < MOBA_TOPK)).astype(F32)


def _moba_prompt_kernel(qt_ref, k_ref, vt_ref, kmean_ref, o_ref, sel_ref):
    i = pl.program_id(2)
    tq = MOBA_BLOCK
    own = pl.multiple_of(i * tq, tq)
    key_i = lax.broadcasted_iota(jnp.int32, (tq, tq), 0)
    qry_i = lax.broadcasted_iota(jnp.int32, (tq, tq), 1)
    outs = []
    for hh in range(LANE // D_A):
        lanes = slice(hh * D_A, (hh + 1) * D_A)
        qt = qt_ref[0, lanes, :]
        gate = jnp.dot(kmean_ref[0, :, lanes], qt.astype(F32), precision=HI, preferred_element_type=F32)
        sel_ref[hh] = _moba_select(gate, i)
        s = _dot(k_ref[pl.ds(own, tq), lanes], qt)
        s = jnp.where(key_i <= qry_i, s, NEG)
        m = jnp.max(s, axis=0, keepdims=True)
        p = jnp.exp(s - m)
        l = jnp.sum(p, axis=0, keepdims=True)
        acc = _dot(vt_ref[0, lanes, pl.ds(own, tq)], p.astype(BF16))

        def body(j, carry, hh=hh, lanes=lanes, qt=qt):
            m, l, acc = carry
            start = pl.multiple_of(j * tq, tq)
            s = _dot(k_ref[pl.ds(start, tq), lanes], qt)
            s = jnp.where(sel_ref[hh, pl.ds(j, 1), :] > 0.0, s, NEG)
            m_new = jnp.maximum(m, jnp.max(s, axis=0, keepdims=True))
            alpha = jnp.exp(m - m_new)
            p = jnp.exp(s - m_new)
            l = l * alpha + jnp.sum(p, axis=0, keepdims=True)
            acc = acc * alpha + _dot(vt_ref[0, lanes, pl.ds(start, tq)], p.astype(BF16))
            return m_new, l, acc

        m, l, acc = lax.fori_loop(0, i, body, (m, l, acc))
        outs.append(acc / l)
    o_ref[...] = jnp.concatenate(outs, axis=0).T.astype(BF16)


def _moba_prompt(qt, k, vt, kmean, batch, seq):
    nq = seq // MOBA_BLOCK
    hp = QKV_A // LANE
    return pl.pallas_call(
        _moba_prompt_kernel,
        grid=(batch, hp, nq),
        in_specs=[pl.BlockSpec((1, LANE, MOBA_BLOCK), lambda b, h, i: (b, h, i)),
                  pl.BlockSpec((seq, LANE), lambda b, h, i: (b, h)),
                  pl.BlockSpec((1, LANE, seq), lambda b, h, i: (b, h, 0)),
                  pl.BlockSpec((1, nq, LANE), lambda b, h, i: (b, 0, h))],
        out_specs=pl.BlockSpec((MOBA_BLOCK, LANE), lambda b, h, i: (b * nq + i, h)),
        out_shape=jax.ShapeDtypeStruct((batch * seq, QKV_A), BF16),
        scratch_shapes=[pltpu.VMEM((LANE // D_A, nq, MOBA_BLOCK), F32)],
        compiler_params=_cparams("parallel", "parallel", "arbitrary"),
        name="moba_prompt",
    )(qt, k, vt, kmean)


def _proj_ffn_kernel(*refs, n_mix):
    mix_refs = refs[:n_mix]
    x_ref, wo_ref, g_ref, wg_ref, wu_ref, wd_ref, o_ref, hn_ref = refs[n_mix:]

    @pl.when(pl.program_id(1) == 0)
    def _():
        x1 = x_ref[...]
        off = 0
        for r in mix_refs:
            width = r.shape[1]
            x1 = x1 + _dot(r[...], wo_ref[off:off + width, :])
            off += width
        o_ref[...] = x1
        hn_ref[...] = _rms(x1, g_ref[...]).astype(BF16)

    h = hn_ref[...]
    act = (_silu(_dot(h, wg_ref[...])) * _dot(h, wu_ref[...])).astype(BF16)
    o_ref[...] += _dot(act, wd_ref[...])


def _proj_ffn(mixes, x, w_out, g, w_gate, w_up, w_down, tm, tf):
    m, d = x.shape
    ff = w_gate.shape[1]
    mix_specs = [pl.BlockSpec((tm, a.shape[1]), lambda i, f: (i, 0)) for a in mixes]
    return pl.pallas_call(
        functools.partial(_proj_ffn_kernel, n_mix=len(mixes)),
        grid=(m // tm, ff // tf),
        in_specs=mix_specs + [
            pl.BlockSpec((tm, d), lambda i, f: (i, 0)),
            pl.BlockSpec(w_out.shape, lambda i, f: (0, 0)),
            pl.BlockSpec((1, d), lambda i, f: (0, 0)),
            pl.BlockSpec((d, tf), lambda i, f: (0, f)),
            pl.BlockSpec((d, tf), lambda i, f: (0, f)),
            pl.BlockSpec((tf, d), lambda i, f: (f, 0))],
        out_specs=pl.BlockSpec((tm, d), lambda i, f: (i, 0)),
        out_shape=jax.ShapeDtypeStruct((m, d), F32),
        scratch_shapes=[pltpu.VMEM((tm, d), BF16)],
        compiler_params=_cparams("parallel", "arbitrary"),
        name="proj_ffn",
    )(*mixes, x, w_out, g, w_gate, w_up, w_down)


def _even_layer_prompt(x, p, batch, seq):
    z = _norm_matmul(x, p["norm_mix"], p["w_in"], 512)
    tables = _rope_tables(jnp.arange(seq, dtype=jnp.int32))
    qt, k, vt, kv, kmean, gv, b_out = _even_epilogue(
        z, tables, p["ln_g"], p["ln_b"], p["w_sp"], p["b_sp"], batch, seq, 256)
    a_out = _moba_prompt(qt, k, vt, kmean.reshape(batch, seq // MOBA_BLOCK, QKV_A), batch, seq)
    x = _proj_ffn([a_out, b_out], x, p["w_out"], p["norm_ffn"], p["w_gate"], p["w_up"], p["w_down"], 512, 256)
    n_open = seq - ((seq - 1) // CHUNK_B) * CHUNK_B
    gv_open = gv.reshape(batch, seq, D_B)[:, seq - n_open:]
    kv = jnp.transpose(kv.reshape(batch, 2, H_A, D_A, seq), (0, 4, 1, 2, 3))
    return x, kv, gv_open


def _even_params(i, ev_norm_mix, ev_w_in, ev_gmlp_ln_g, ev_gmlp_ln_b, ev_w_spatial, ev_b_spatial, ev_w_out,
                 ev_norm_ffn, ev_w_gate, ev_w_up, ev_w_down):
    return dict(norm_mix=ev_norm_mix[i][None], w_in=ev_w_in[i].astype(BF16),
                ln_g=ev_gmlp_ln_g[i][None], ln_b=ev_gmlp_ln_b[i][None],
                w_sp=ev_w_spatial[i], b_sp=ev_b_spatial[i], w_out=ev_w_out[i].astype(BF16),
                norm_ffn=ev_norm_ffn[i][None], w_gate=ev_w_gate[i].astype(BF16),
                w_up=ev_w_up[i].astype(BF16), w_down=ev_w_down[i].astype(BF16))


O_CQ, O_CK, O_CV, O_CR, O_DQKV, O_DG, O_SM = 0, 256, 512, 1024, 1536, 2560, 3072
SM_DA, SM_DB = GATE_RANK, GATE_RANK + H_D
IN_ODD_PAD = O_SM + LANE
KEY_C = H_C * DK_C
PAIR = LANE // DK_C


def _log_sigmoid(x):
    return jnp.minimum(x, 0.0) - jnp.log1p(jnp.exp(-jnp.abs(x)))


def _split_bf16(a):
    hi = a.astype(BF16)
    return hi, (a - hi.astype(F32)).astype(BF16)


def _dot3(a, b):
    ah, al = _split_bf16(a)
    bh, bl = _split_bf16(b)
    return _dot(ah, bh) + _dot(ah, bl) + _dot(al, bh)


def _unit_lower_inverse(a):
    c = a.shape[0]
    eye = (lax.broadcasted_iota(jnp.int32, (c, c), 0) == lax.broadcasted_iota(jnp.int32, (c, c), 1)).astype(F32)
    x = -a
    p = eye + x
    steps = int(math.log2(c)) - 1
    for _ in range(steps):
        x = _dot3(x, x)
        p = p + _dot3(p, x)
    return p


def _group_sumsq(y, width):
    n = y.shape[1]
    same = (lax.broadcasted_iota(jnp.int32, (n, n), 0) // width
            == lax.broadcasted_iota(jnp.int32, (n, n), 1) // width).astype(F32)
    return jnp.dot(y * y, same, precision=HI, preferred_element_type=F32)


def _odd_mixer_prompt_kernel(z_ref, wg2_ref, bg2_ref, gnorm_ref, cw_ref, alog_ref, dtb_ref, dnorm_ref,
                             o_ref, sg_ref, sd_ref, tail_ref,
                             stg_ref, std_ref, prev_ref, la_ref, qkv_ref, dla_ref, beta_ref):
    i = pl.program_id(1)
    tc = z_ref.shape[0]
    c = CHUNK_LIN

    @pl.when(i == 0)
    def _():
        stg_ref[...] = jnp.zeros_like(stg_ref)
        std_ref[...] = jnp.zeros_like(std_ref)
        prev_ref[...] = jnp.zeros_like(prev_ref)

    small = z_ref[:, O_SM:O_SM + LANE]
    pre = jnp.dot(small, wg2_ref[...], precision=HI, preferred_element_type=F32) + bg2_ref[...]
    la_ref[...] = _log_sigmoid(pre) / GATE_NORM
    dla_ref[...] = -jnp.exp(alog_ref[...]) * _softplus(small + dtb_ref[...])
    beta_ref[...] = jax.nn.sigmoid(small)

    x = z_ref[:, O_DQKV:O_DQKV + C_CONV]
    x8 = x[0:8]
    p8 = prev_ref[...]
    row8 = lax.broadcasted_iota(jnp.int32, (8, C_CONV), 0)
    y = x * cw_ref[CONV_W - 1:CONV_W, :]
    y8 = x8 * cw_ref[CONV_W - 1:CONV_W, :]
    for s in range(1, CONV_W):
        wrow = cw_ref[CONV_W - 1 - s:CONV_W - s, :]
        y = y + pltpu.roll(x, s, 0) * wrow
        y8 = y8 + jnp.where(row8 < s, pltpu.roll(p8, s, 0), pltpu.roll(x8, s, 0)) * wrow
    prev_ref[...] = x[tc - 8:tc]
    y = _silu(y)
    y8 = _silu(y8)
    yqk = y[:, 0:2 * KEY_C]
    nrm = lax.rsqrt(_group_sumsq(yqk, DK_D) + EPS)
    qscale = jnp.where(lax.broadcasted_iota(jnp.int32, (1, 2 * KEY_C), 1) < KEY_C, DK_D ** -0.5, 1.0)
    qkv_ref[:, 0:2 * KEY_C] = yqk * nrm * qscale
    qkv_ref[:, 2 * KEY_C:] = y[:, 2 * KEY_C:]
    yqk8 = y8[:, 0:2 * KEY_C]
    qkv_ref[0:8, 0:2 * KEY_C] = yqk8 * lax.rsqrt(_group_sumsq(yqk8, DK_D) + EPS) * qscale
    qkv_ref[0:8, 2 * KEY_C:] = y8[:, 2 * KEY_C:]

    ri = lax.broadcasted_iota(jnp.int32, (c, c), 0)
    ci = lax.broadcasted_iota(jnp.int32, (c, c), 1)
    tril = ri >= ci
    strict = ri > ci
    lower = tril.astype(F32)
    upper = (ri <= ci).astype(F32)
    ones = jnp.ones((c, c), F32)
    lane_head = lax.broadcasted_iota(jnp.int32, (c, LANE), 1) // DK_C
    lane_head_row = lax.broadcasted_iota(jnp.int32, (1, LANE), 1) // DK_C

    def chunk(ch, carry):
        rows = pl.ds(pl.multiple_of(ch * c, c), c)
        for p in range(H_C // PAIR):
            lanes = slice(p * LANE, (p + 1) * LANE)
            bcum = jnp.dot(lower, la_ref[rows, lanes], precision=HI, preferred_element_type=F32)
            b_end = bcum[c - 1:c, :]
            q_in = z_ref[rows, O_CQ + p * LANE:O_CQ + (p + 1) * LANE] * (DK_C ** -0.5) * jnp.exp(bcum)
            k = z_ref[rows, O_CK + p * LANE:O_CK + (p + 1) * LANE]
            k_in = (k * jnp.exp(-bcum)).astype(BF16)
            k_end = k * jnp.exp(b_end - bcum)
            st = stg_ref[p]
            stb = st.astype(BF16)
            upd = jnp.zeros((DV_C, LANE), F32)
            for hh in range(PAIR):
                h = p * PAIR + hh
                mask = lane_head == hh
                qm = jnp.where(mask, q_in, 0.0).astype(BF16)
                sc = jnp.where(tril, _dot_nt(qm, k_in), 0.0)
                vb = z_ref[rows, O_CV + h * DV_C:O_CV + (h + 1) * DV_C].astype(BF16)
                o = _dot(sc.astype(BF16), vb) + _dot_nt(qm, stb)
                upd = upd + _dot_tn(vb, jnp.where(mask, k_end, 0.0).astype(BF16))
                gate = _silu(z_ref[rows, O_CR + h * DV_C:O_CR + (h + 1) * DV_C])
                o_ref[rows, h * DV_C:(h + 1) * DV_C] = (_rms(o, gnorm_ref[...]) * gate).astype(BF16)
            stg_ref[p] = st * jnp.exp(b_end) + upd
        for p in range(H_D // PAIR):
            q2 = qkv_ref[rows, p * LANE:(p + 1) * LANE]
            k2 = qkv_ref[rows, KEY_C + p * LANE:KEY_C + (p + 1) * LANE]
            st = std_ref[p]
            stb = st.astype(BF16)
            upd = jnp.zeros((DV_D, LANE), F32)
            dec_row = jnp.zeros((1, LANE), F32)
            for hh in range(PAIR):
                h = p * PAIR + hh
                mask = lane_head == hh
                la_b = jnp.broadcast_to(dla_ref[rows, SM_DA + h:SM_DA + h + 1], (c, LANE))
                beta_b = jnp.broadcast_to(beta_ref[rows, SM_DB + h:SM_DB + h + 1], (c, LANE))
                g_col = jnp.dot(lower, la_b, precision=HI, preferred_element_type=F32)
                g_row = jnp.dot(ones, la_b[:, 0:c] * upper, precision=HI, preferred_element_type=F32)
                decay = jnp.exp(jnp.where(tril, g_col[:, 0:c] - g_row, NEG))
                eg = jnp.exp(g_col)
                g_last = g_col[c - 1:c, :]
                km = jnp.where(mask, k2, 0.0)
                qm = jnp.where(mask, q2, 0.0)
                kmb = km.astype(BF16)
                kb = km * beta_b
                a_mat = jnp.where(strict, _dot_nt(kb.astype(BF16), kmb) * decay, 0.0)
                t_inv = _unit_lower_inverse(a_mat)
                v = qkv_ref[rows, 2 * KEY_C + h * DV_D:2 * KEY_C + (h + 1) * DV_D]
                u = _dot3(t_inv, v * beta_b)
                w = _dot3(t_inv, kb * eg)
                qk = jnp.where(tril, _dot_nt(qm.astype(BF16), kmb) * decay, 0.0)
                v_new = u - _dot_nt(w.astype(BF16), stb)
                vnb = v_new.astype(BF16)
                o = _dot_nt((qm * eg).astype(BF16), stb) + _dot(qk.astype(BF16), vnb)
                upd = upd + _dot_tn(vnb, (km * jnp.exp(g_last - g_col)).astype(BF16))
                dec_row = jnp.where(lane_head_row == hh, jnp.exp(g_last), dec_row)
                gate = _silu(z_ref[rows, O_DG + h * DV_D:O_DG + (h + 1) * DV_D])
                col = H_C * DV_C + h * DV_D
                o_ref[rows, col:col + DV_D] = (_rms(o, dnorm_ref[...]) * gate).astype(BF16)
            std_ref[p] = st * dec_row + upd
        return carry

    lax.fori_loop(0, tc // c, chunk, 0)

    @pl.when(i == pl.num_programs(1) - 1)
    def _():
        tail_ref[0] = prev_ref[...]
        for p in range(H_C // PAIR):
            tg = stg_ref[p].T
            td = std_ref[p].T
            for hh in range(PAIR):
                sg_ref[0, p * PAIR + hh] = tg[hh * DK_C:(hh + 1) * DK_C, :]
                sd_ref[0, p * PAIR + hh] = td[hh * DK_D:(hh + 1) * DK_D, :]


def _odd_mixer_prompt(z, p, batch, seq, tc):
    nt = seq // tc
    full = lambda a: pl.BlockSpec(a.shape, lambda b, i: (0,) * a.ndim)
    consts = [p["w_g2_pad"], p["b_g2"], p["gla_norm"], p["conv_w"], p["alog_slab"], p["dtb_slab"], p["delta_norm"]]
    st_spec = pl.BlockSpec((1, H_C, DK_C, DV_C), lambda b, i: (b, 0, 0, 0))
    return pl.pallas_call(
        _odd_mixer_prompt_kernel,
        grid=(batch, nt),
        in_specs=[pl.BlockSpec((tc, IN_ODD_PAD), lambda b, i: (b * nt + i, 0))] + [full(a) for a in consts],
        out_specs=[pl.BlockSpec((tc, D_MODEL), lambda b, i: (b * nt + i, 0)), st_spec, st_spec,
                   pl.BlockSpec((1, 8, C_CONV), lambda b, i: (b, 0, 0))],
        out_shape=[jax.ShapeDtypeStruct((batch * seq, D_MODEL), BF16),
                   jax.ShapeDtypeStruct((batch, H_C, DK_C, DV_C), F32),
                   jax.ShapeDtypeStruct((batch, H_D, DK_D, DV_D), F32),
                   jax.ShapeDtypeStruct((batch, 8, C_CONV), F32)],
        scratch_shapes=[pltpu.VMEM((H_C // PAIR, DV_C, LANE), F32),
                        pltpu.VMEM((H_D // PAIR, DV_D, LANE), F32),
                        pltpu.VMEM((8, C_CONV), F32),
                        pltpu.VMEM((tc, KEY_C), F32),
                        pltpu.VMEM((tc, C_CONV), F32),
                        pltpu.VMEM((tc, LANE), F32),
                        pltpu.VMEM((tc, LANE), F32)],
        compiler_params=_cparams("parallel", "arbitrary"),
        name="odd_mixer_prompt",
    )(z, *consts)


M_E1, M_E2, M_R1, M_R2 = N_EXPERTS, N_EXPERTS + 1, N_EXPERTS + 2, N_EXPERTS + 3


def _proj_router_kernel(mix_ref, x_ref, wo_ref, g_ref, wr_ref, x1_ref, hn_ref, meta_ref, cnt_ref, carry_ref):
    tm = x_ref.shape[0]

    @pl.when(pl.program_id(0) == 0)
    def _():
        carry_ref[...] = jnp.zeros_like(carry_ref)

    x1 = x_ref[...] + _dot(mix_ref[...], wo_ref[...])
    x1_ref[...] = x1
    hn = _rms(x1, g_ref[...])
    hn_ref[...] = hn.astype(BF16)
    lane = lax.broadcasted_iota(jnp.int32, (tm, LANE), 1)
    logits = jnp.dot(hn, wr_ref[...], precision=HI, preferred_element_type=F32)
    logits = jnp.where(lane < N_EXPERTS, logits, NEG)
    m1 = jnp.max(logits, axis=1, keepdims=True)
    e1 = jnp.min(jnp.where(logits == m1, lane, LANE), axis=1, keepdims=True)
    rest = jnp.where(lane == e1, NEG, logits)
    m2 = jnp.max(rest, axis=1, keepdims=True)
    e2 = jnp.min(jnp.where(rest == m2, lane, LANE), axis=1, keepdims=True)
    t = jnp.exp(m2 - m1)
    g1 = 1.0 / (1.0 + t)
    g2 = t / (1.0 + t)
    oh1 = lane == e1
    oh2 = lane == e2
    member = (oh1 | oh2).astype(F32)
    ri = lax.broadcasted_iota(jnp.int32, (tm, tm), 0)
    ci = lax.broadcasted_iota(jnp.int32, (tm, tm), 1)
    before = _dot((ri > ci).astype(BF16), member.astype(BF16)) + carry_ref[...]
    r1 = jnp.sum(jnp.where(oh1, before, 0.0), axis=1, keepdims=True)
    r2 = jnp.sum(jnp.where(oh2, before, 0.0), axis=1, keepdims=True)
    carry_ref[...] = carry_ref[...] + jnp.sum(member, axis=0, keepdims=True)
    cnt_ref[...] = carry_ref[...]
    meta = jnp.where(oh1, g1, 0.0) + jnp.where(oh2, g2, 0.0)
    meta = jnp.where(lane == M_E1, e1.astype(F32), meta)
    meta = jnp.where(lane == M_E2, e2.astype(F32), meta)
    meta = jnp.where(lane == M_R1, r1, meta)
    meta = jnp.where(lane == M_R2, r2, meta)
    meta_ref[...] = meta


def _proj_router(mix, x, w_out, g, w_router_pad, tm):
    m, d = x.shape
    row = lambda width: pl.BlockSpec((tm, width), lambda i: (i, 0))
    full = lambda a: pl.BlockSpec(a.shape, lambda i: (0,) * a.ndim)
    return pl.pallas_call(
        _proj_router_kernel,
        grid=(m // tm,),
        in_specs=[row(d), row(d), full(w_out), full(g), full(w_router_pad)],
        out_specs=[row(d), row(d), row(LANE), pl.BlockSpec((1, LANE), lambda i: (0, 0))],
        out_shape=[jax.ShapeDtypeStruct((m, d), F32), jax.ShapeDtypeStruct((m, d), BF16),
                   jax.ShapeDtypeStruct((m, LANE), F32), jax.ShapeDtypeStruct((1, LANE), F32)],
        scratch_shapes=[pltpu.VMEM((1, LANE), F32)],
        compiler_params=_cparams("arbitrary"),
        name="proj_router",
    )(mix, x, w_out, g, w_router_pad)


def _moe_dense_kernel(hn_ref, x1_ref, meta_ref, wg_ref, wu_ref, wd_ref, gf_ref, o_ref):
    e = pl.program_id(1)
    f = pl.program_id(2)

    @pl.when((e == 0) & (f == 0))
    def _():
        o_ref[...] = x1_ref[...]

    h = hn_ref[...]
    act = (_silu(_dot(h, wg_ref[...])) * _dot(h, wu_ref[...])).astype(BF16)
    lane = lax.broadcasted_iota(jnp.int32, meta_ref.shape, 1)
    gate = jnp.sum(jnp.where(lane == e, meta_ref[...], 0.0), axis=1, keepdims=True)
    o_ref[...] += gate * _dot(act, wd_ref[...])

    @pl.when((e == pl.num_programs(1) - 1) & (f == pl.num_programs(2) - 1))
    def _():
        o_ref[...] = _rms(o_ref[...], gf_ref[...])


def _moe_dense(hn, x1, meta, w_gate, w_up, w_down, g_final, tm, tf):
    m, d = x1.shape
    n_e, _, ff = w_gate.shape
    row = lambda width: pl.BlockSpec((tm, width), lambda i, e, f: (i, 0))
    return pl.pallas_call(
        _moe_dense_kernel,
        grid=(m // tm, n_e, ff // tf),
        in_specs=[row(d), row(d), row(LANE),
                  pl.BlockSpec((None, d, tf), lambda i, e, f: (e, 0, f)),
                  pl.BlockSpec((None, d, tf), lambda i, e, f: (e, 0, f)),
                  pl.BlockSpec((None, tf, d), lambda i, e, f: (e, f, 0)),
                  pl.BlockSpec((1, d), lambda i, e, f: (0, 0))],
        out_specs=row(d),
        out_shape=jax.ShapeDtypeStruct((m, d), F32),
        compiler_params=_cparams("parallel", "arbitrary", "arbitrary"),
        name="moe_dense",
    )(hn, x1, meta, w_gate, w_up, w_down, g_final)


def _odd_params(i, od_norm_mix, od_w_in, od_gla_w_gate2, od_gla_b_gate2, od_gla_norm, od_delta_conv,
                od_delta_a_log, od_delta_dt_bias, od_delta_norm, od_w_out, od_norm_ffn, od_router,
                od_w_gate, od_w_up, od_w_down):
    w = od_w_in[i]
    sizes = (KEY_C, KEY_C, H_C * DV_C, GATE_RANK, H_C * DV_C, C_CONV, H_D, H_D, H_D * DV_D)
    splits = tuple(sum(sizes[:j + 1]) for j in range(len(sizes) - 1))
    cq, ck, cv, c_lr, c_r, d_qkv, d_a, d_b, d_g = jnp.split(w, splits, axis=1)
    pad = jnp.zeros((w.shape[0], IN_ODD_PAD - w.shape[1]), w.dtype)
    w_in = jnp.concatenate([cq, ck, cv, c_r, d_qkv, d_g, c_lr, d_a, d_b, pad], axis=1).astype(BF16)
    slab = lambda v: jnp.zeros((1, LANE), F32).at[0, SM_DA:SM_DA + H_D].set(v)
    return dict(norm_mix=od_norm_mix[i][None], w_in=w_in,
                w_g2_pad=jnp.zeros((LANE, KEY_C), F32).at[:GATE_RANK].set(od_gla_w_gate2[i]),
                b_g2=od_gla_b_gate2[i][None], gla_norm=od_gla_norm[i][None], conv_w=od_delta_conv[i],
                alog_slab=slab(od_delta_a_log[i]), dtb_slab=slab(od_delta_dt_bias[i]),
                a_log=od_delta_a_log[i], dt_bias=od_delta_dt_bias[i],
                delta_norm=od_delta_norm[i][None], w_out=od_w_out[i].astype(BF16),
                norm_ffn=od_norm_ffn[i][None],
                router=jnp.zeros((D_MODEL, LANE), F32).at[:, :N_EXPERTS].set(od_router[i]),
                w_gate=od_w_gate[i].astype(BF16), w_up=od_w_up[i].astype(BF16), w_down=od_w_down[i].astype(BF16))


def _moe(mix, x, p, g_final, tm):
    x1, hn, meta, _ = _proj_router(mix, x, p["w_out"], p["norm_ffn"], p["router"], tm)
    return _moe_dense(hn, x1, meta, p["w_gate"], p["w_up"], p["w_down"], g_final, tm, 512)


def _odd_layer_prompt(x, p, g_final, batch, seq):
    z = _norm_matmul(x, p["norm_mix"], p["w_in"], 512)
    mix, s_gla, s_delta, tail = _odd_mixer_prompt(z, p, batch, seq, 256)
    y = _moe(mix, x, p, g_final, 512)
    return y, s_gla, s_delta, tail[:, 8 - (CONV_W - 1):]


def _even_epilogue_sample_kernel(z_ref, c_ref, su_ref, sd_ref, lng_ref, lnb_ref, w0_ref, b0_ref,
                                 q_ref, kv_ref, gv_ref, bout_ref):
    c, su, sd = c_ref[...], su_ref[...], sd_ref[...]
    q_ref[...] = _rope(z_ref[:, 0:QKV_A], c, su, sd) * (D_A ** -0.5)
    kv_ref[:, 0:QKV_A] = _rope(z_ref[:, QKV_A:2 * QKV_A], c, su, sd)
    kv_ref[:, QKV_A:2 * QKV_A] = z_ref[:, 2 * QKV_A:3 * QKV_A]
    u = _gelu(z_ref[:, 3 * QKV_A:3 * QKV_A + D_B])
    gv = _layernorm(_gelu(z_ref[:, 3 * QKV_A + D_B:3 * QKV_A + 2 * D_B]), lng_ref[...], lnb_ref[...])
    gv_ref[...] = gv
    bout_ref[...] = (u * (gv * w0_ref[...] + b0_ref[...])).astype(BF16)


def _even_epilogue_sample(z, tables, ln_g, ln_b, w_sp, b_sp):
    m = z.shape[0]
    w0 = jnp.repeat(w_sp[:, 0, 0], D_BG)[None]
    b0 = jnp.repeat(b_sp[:, 0], D_BG)[None]
    return pl.pallas_call(
        _even_epilogue_sample_kernel,
        out_shape=[jax.ShapeDtypeStruct((m, QKV_A), F32), jax.ShapeDtypeStruct((m, 2 * QKV_A), F32),
                   jax.ShapeDtypeStruct((m, D_B), F32), jax.ShapeDtypeStruct((m, D_B), BF16)],
        compiler_params=pltpu.CompilerParams(vmem_limit_bytes=VMEM_LIMIT),
        name="even_epilogue_sample",
    )(z, *tables, ln_g, ln_b, w0, b0)


def _moba_sample_kernel(pt_ref, qt_ref, knt_ref, vnt_ref, *refs):
    del pt_ref
    n_pages = len(refs) - 2
    page_refs, o_ref, s_ref = refs[:n_pages], refs[n_pages], refs[n_pages + 1]
    pages_per_block = MOBA_BLOCK // PAGE_SIZE
    nb = n_pages // pages_per_block
    qt = qt_ref[0]
    for h in range(H_A):
        qcol = jnp.broadcast_to(qt[:, h:h + 1], (D_A, PAGE_SIZE))
        for j in range(n_pages):
            s_ref[h, j:j + 1, :] = jnp.sum(page_refs[j][0, 0, h] * qcol, axis=0, keepdims=True)
    lane = lax.broadcasted_iota(jnp.int32, (n_pages, LANE), 1)
    page_sums = jnp.zeros((n_pages, LANE), F32)
    for h in range(H_A):
        page_sums = jnp.where(lane == h, jnp.sum(s_ref[h], axis=1, keepdims=True), page_sums)
    pair = (lax.broadcasted_iota(jnp.int32, (nb, n_pages), 1) // pages_per_block
            == lax.broadcasted_iota(jnp.int32, (nb, n_pages), 0)).astype(F32)
    pair_t = (lax.broadcasted_iota(jnp.int32, (n_pages, nb), 0) // pages_per_block
              == lax.broadcasted_iota(jnp.int32, (n_pages, nb), 1)).astype(F32)
    gate = jnp.dot(pair, page_sums, precision=HI, preferred_element_type=F32)
    sel = _moba_select(gate, nb)
    sel_pages = jnp.dot(pair_t, sel, precision=HI, preferred_element_type=F32)
    own = jnp.sum(qt * knt_ref[0], axis=0, keepdims=True)
    vnt = vnt_ref[0]
    out_lane = lax.broadcasted_iota(jnp.int32, (D_A, LANE), 1)
    out = jnp.zeros((D_A, LANE), F32)
    for h in range(H_A):
        sm = jnp.where(sel_pages[:, h:h + 1] > 0.0, s_ref[h], NEG)
        s_own = own[:, h:h + 1]
        mx = jnp.maximum(jnp.max(jnp.max(sm, axis=1, keepdims=True), axis=0, keepdims=True), s_own)
        p = jnp.exp(sm - mx)
        p_own = jnp.exp(s_own - mx)
        denom = jnp.sum(jnp.sum(p, axis=1, keepdims=True), axis=0, keepdims=True) + p_own
        acc = jnp.zeros((D_A, PAGE_SIZE), F32)
        for j in range(n_pages):
            acc = acc + page_refs[j][0, 1, h] * p[j:j + 1, :]
        o = (jnp.sum(acc, axis=1, keepdims=True) + p_own * vnt[:, h:h + 1]) / denom
        out = jnp.where(out_lane == h, o, out)
    o_ref[0] = out


def _moba_sample(q, kv_new, cache, page_table):
    bs, n_pages = page_table.shape
    assert (n_pages * PAGE_SIZE) % MOBA_BLOCK == 0
    cache_t = jnp.transpose(cache, (0, 2, 3, 4, 1))
    page_spec = lambda j: pl.BlockSpec((1, 2, H_A, D_A, PAGE_SIZE),
                                       lambda b, pt: (pt[b * n_pages + j], 0, 0, 0, 0))
    col_spec = pl.BlockSpec((1, D_A, H_A), lambda b, pt: (b, 0, 0))
    heads_t = lambda a: jnp.transpose(a.reshape(bs, H_A, D_A), (0, 2, 1))
    out = pl.pallas_call(
        _moba_sample_kernel,
        grid_spec=pltpu.PrefetchScalarGridSpec(
            num_scalar_prefetch=1,
            grid=(bs,),
            in_specs=[col_spec, col_spec, col_spec] + [page_spec(j) for j in range(n_pages)],
            out_specs=pl.BlockSpec((1, D_A, LANE), lambda b, pt: (b, 0, 0)),
            scratch_shapes=[pltpu.VMEM((H_A, n_pages, PAGE_SIZE), F32)]),
        out_shape=jax.ShapeDtypeStruct((bs, D_A, LANE), F32),
        compiler_params=_cparams("parallel"),
        name="moba_sample",
    )(page_table.reshape(-1), heads_t(q), heads_t(kv_new[:, 0:QKV_A]), heads_t(kv_new[:, QKV_A:]),
      *([cache_t] * n_pages))
    return jnp.transpose(out[:, :, 0:H_A], (0, 2, 1)).reshape(bs, QKV_A).astype(BF16)


def _odd_mixer_sample_kernel(z_ref, buf_ref, sg_ref, sd_ref, wg2_ref, bg2_ref, gnorm_ref, cw_ref, alog_ref, dtb_ref,
                             dnorm_ref, o_ref, sgo_ref, sdo_ref):
    bt = z_ref.shape[0]
    stride = H_C * DK_C
    small = z_ref[:, O_SM:O_SM + LANE]
    pre = jnp.dot(small, wg2_ref[...], precision=HI, preferred_element_type=F32) + bg2_ref[...]
    a_all = jnp.exp(_log_sigmoid(pre) / GATE_NORM)
    g_all = -jnp.exp(alog_ref[...]) * _softplus(small + dtb_ref[...])
    beta_all = jax.nn.sigmoid(small)
    x = z_ref[:, O_DQKV:O_DQKV + C_CONV]
    y = x * cw_ref[CONV_W - 1:CONV_W, :]
    for j in range(CONV_W - 1):
        y = y + buf_ref[:, j, :] * cw_ref[j:j + 1, :]
    y = _silu(y)
    for h in range(H_C):
        ks = slice(h * DK_C, (h + 1) * DK_C)
        a = a_all[:, ks]
        q = z_ref[:, O_CQ + h * DK_C:O_CQ + (h + 1) * DK_C] * (DK_C ** -0.5)
        k = z_ref[:, O_CK + h * DK_C:O_CK + (h + 1) * DK_C]
        v = z_ref[:, O_CV + h * DV_C:O_CV + (h + 1) * DV_C]
        qa = q * a
        acc = jnp.sum(q * k, axis=1, keepdims=True) * v
        for kk in range(DK_C):
            rows = pl.ds(h * DK_C + kk, bt, stride=stride)
            srow = sg_ref[rows, :]
            acc = acc + qa[:, kk:kk + 1] * srow
            sgo_ref[rows, :] = a[:, kk:kk + 1] * srow + k[:, kk:kk + 1] * v
        gate = _silu(z_ref[:, O_CR + h * DV_C:O_CR + (h + 1) * DV_C])
        o_ref[:, h * DV_C:(h + 1) * DV_C] = (_rms(acc, gnorm_ref[...]) * gate).astype(BF16)
        yq = y[:, h * DK_D:(h + 1) * DK_D]
        yk = y[:, KEY_C + h * DK_D:KEY_C + (h + 1) * DK_D]
        dv = y[:, 2 * KEY_C + h * DV_D:2 * KEY_C + (h + 1) * DV_D]
        dq = yq * lax.rsqrt(jnp.sum(yq * yq, axis=1, keepdims=True) + EPS) * (DK_D ** -0.5)
        dk = yk * lax.rsqrt(jnp.sum(yk * yk, axis=1, keepdims=True) + EPS)
        beta = beta_all[:, SM_DB + h:SM_DB + h + 1]
        eg = jnp.exp(g_all[:, SM_DA + h:SM_DA + h + 1])
        w = dk * (beta * eg)
        qd = dq * eg
        ws = jnp.zeros((bt, DV_D), F32)
        qs = jnp.zeros((bt, DV_D), F32)
        for kk in range(DK_D):
            srow = sd_ref[pl.ds(h * DK_D + kk, bt, stride=stride), :]
            ws = ws + w[:, kk:kk + 1] * srow
            qs = qs + qd[:, kk:kk + 1] * srow
        v_new = dv * beta - ws
        o = qs + jnp.sum(dq * dk, axis=1, keepdims=True) * v_new
        for kk in range(DK_D):
            rows = pl.ds(h * DK_D + kk, bt, stride=stride)
            sdo_ref[rows, :] = sd_ref[rows, :] * eg + dk[:, kk:kk + 1] * v_new
        gate = _silu(z_ref[:, O_DG + h * DV_D:O_DG + (h + 1) * DV_D])
        col = H_C * DV_C + h * DV_D
        o_ref[:, col:col + DV_D] = (_rms(o, dnorm_ref[...]) * gate).astype(BF16)


def _odd_mixer_sample(z, conv_buf, s_gla, s_delta, p, bt):
    bs = z.shape[0]
    rows = H_C * DK_C
    full = lambda a: pl.BlockSpec(a.shape, lambda i: (0,) * a.ndim)
    consts = [p["w_g2_pad"], p["b_g2"], p["gla_norm"], p["conv_w"], p["alog_slab"], p["dtb_slab"], p["delta_norm"]]
    st_spec = pl.BlockSpec((bt * rows, DV_C), lambda i: (i, 0))
    mix, sg, sd = pl.pallas_call(
        _odd_mixer_sample_kernel,
        grid=(bs // bt,),
        in_specs=[pl.BlockSpec((bt, IN_ODD_PAD), lambda i: (i, 0)),
                  pl.BlockSpec((bt, CONV_W - 1, C_CONV), lambda i: (i, 0, 0)),
                  st_spec, st_spec] + [full(a) for a in consts],
        out_specs=[pl.BlockSpec((bt, D_MODEL), lambda i: (i, 0)), st_spec, st_spec],
        out_shape=[jax.ShapeDtypeStruct((bs, D_MODEL), BF16),
                   jax.ShapeDtypeStruct((bs * rows, DV_C), F32),
                   jax.ShapeDtypeStruct((bs * rows, DV_D), F32)],
        compiler_params=_cparams("parallel"),
        name="odd_mixer_sample",
    )(z, conv_buf, s_gla.reshape(bs * rows, DV_C), s_delta.reshape(bs * rows, DV_D), *consts)
    return mix, sg.reshape(s_gla.shape), sd.reshape(s_delta.shape)


def _sample_step(x, cache, page_table, s_gla, s_delta, conv_buf, ev, od, g_final):
    bs = x.shape[0]
    past = page_table.shape[1] * PAGE_SIZE
    z = _norm_matmul(x, ev["norm_mix"], ev["w_in"], bs)
    tables = _rope_tables(jnp.full((1,), past, jnp.int32))
    q, kv, gv, b_out = _even_epilogue_sample(z, tables, ev["ln_g"], ev["ln_b"], ev["w_sp"], ev["b_sp"])
    a_out = _moba_sample(q, kv, cache, page_table)
    x = _proj_ffn([a_out, b_out], x, ev["w_out"], ev["norm_ffn"], ev["w_gate"], ev["w_up"], ev["w_down"], bs, 256)
    z = _norm_matmul(x, od["norm_mix"], od["w_in"], bs)
    mix, sg, sd = _odd_mixer_sample(z, conv_buf, s_gla, s_delta, od, 32)
    conv_new = jnp.concatenate([conv_buf[:, 1:], z[:, None, O_DQKV:O_DQKV + C_CONV]], axis=1)
    y = _moe(mix, x, od, g_final, bs)
    return y, kv, gv, sg, sd, conv_new


def kernel(x_prompt, x_sample, cache_kv, state_gla, state_delta, state_conv, page_table, ev_norm_mix, ev_w_in, ev_gmlp_ln_g, ev_gmlp_ln_b, ev_w_spatial, ev_b_spatial, ev_w_out, ev_norm_ffn, ev_w_gate, ev_w_up, ev_w_down, od_norm_mix, od_w_in, od_gla_w_gate2, od_gla_b_gate2, od_gla_norm, od_delta_conv, od_delta_a_log, od_delta_dt_bias, od_delta_norm, od_w_out, od_norm_ffn, od_router, od_w_gate, od_w_up, od_w_down, norm_final):
    bp, tp, d = x_prompt.shape
    ev = _even_params(0, ev_norm_mix, ev_w_in, ev_gmlp_ln_g, ev_gmlp_ln_b, ev_w_spatial, ev_b_spatial, ev_w_out,
                      ev_norm_ffn, ev_w_gate, ev_w_up, ev_w_down)
    od = _odd_params(0, od_norm_mix, od_w_in, od_gla_w_gate2, od_gla_b_gate2, od_gla_norm, od_delta_conv,
                     od_delta_a_log, od_delta_dt_bias, od_delta_norm, od_w_out, od_norm_ffn, od_router,
                     od_w_gate, od_w_up, od_w_down)
    bs, ts, _ = x_sample.shape
    assert ts == 1 and cache_kv.shape[0] == 1 and state_gla.shape[0] == 1
    xp, kv_p, gv_p = _even_layer_prompt(x_prompt.reshape(bp * tp, d), ev, bp, tp)
    yp, gla_p, dl_p, cv_p = _odd_layer_prompt(xp, od, norm_final[None], bp, tp)
    ys, kv_s, gv_s, gla_s, dl_s, cv_s = _sample_step(
        x_sample.reshape(bs, d), cache_kv[0], page_table, state_gla[0], state_delta[0], state_conv[0],
        ev, od, norm_final[None])
    return (yp.reshape(bp, tp, d), ys.reshape(bs, ts, d),
            kv_p[None], kv_s.reshape(1, bs, ts, 2, H_A, D_A),
            gv_p[None], gv_s.reshape(1, bs, ts, D_B),
            gla_p[None], gla_s[None], dl_p[None], dl_s[None], cv_p[None], cv_s[None])
```

```python
import functools
import math

import jax
import jax.numpy as jnp
from jax import lax
from jax.experimental import pallas as pl
from jax.experimental.pallas import tpu as pltpu

F32 = jnp.float32
BF16 = jnp.bfloat16
HI = lax.Precision.HIGHEST
EPS = 1e-6
NEG = -1e30

D_MODEL = 1024
PAGE_SIZE = 128
H_A, D_A = 8, 64
ROT_DIM = D_A // 4
ROPE_THETA = 500000.0
MOBA_BLOCK = 256
MOBA_TOPK = 3
G_B, D_BG = 8, 64
D_B = G_B * D_BG
CHUNK_B = 128
H_C, DK_C, DV_C = 4, 64, 128
GATE_RANK = 16
GATE_NORM = 16.0
H_D, DK_D, DV_D = 4, 64, 128
CONV_W = 4
C_CONV = 2 * H_D * DK_D + H_D * DV_D
CHUNK_LIN = 64
N_EXPERTS = 8
TOP_K = 2
QKV_A = H_A * D_A
IN_EVEN = 3 * QKV_A + 2 * D_B
LANE = 128
VMEM_LIMIT = 56 * 1024 * 1024


def _cparams(*sem):
    return pltpu.CompilerParams(dimension_semantics=sem, vmem_limit_bytes=VMEM_LIMIT)


def _rms(x, g):
    return x * lax.rsqrt(jnp.mean(x * x, axis=-1, keepdims=True) + EPS) * g


def _gelu(x):
    return 0.5 * x * (1.0 + lax.erf(x * (2.0 ** -0.5)))


def _silu(x):
    return x * jax.nn.sigmoid(x)


def _softplus(x):
    return jnp.maximum(x, 0.0) + jnp.log1p(jnp.exp(-jnp.abs(x)))


def _dot(a, b):
    return jnp.dot(a, b, preferred_element_type=F32)


def _dot_nt(a, b):
    return lax.dot_general(a, b, (((1,), (1,)), ((), ())), preferred_element_type=F32)


def _dot_tn(a, b):
    return lax.dot_general(a, b, (((0,), (0,)), ((), ())), preferred_element_type=F32)


def _norm_matmul_kernel(x_ref, g_ref, w_ref, o_ref):
    h = _rms(x_ref[...], g_ref[...]).astype(BF16)
    o_ref[...] = _dot(h, w_ref[...])


def _norm_matmul(x, g, w, tm):
    m, d = x.shape
    n = w.shape[1]
    return pl.pallas_call(
        _norm_matmul_kernel,
        grid=(m // tm,),
        in_specs=[pl.BlockSpec((tm, d), lambda i: (i, 0)),
                  pl.BlockSpec((1, d), lambda i: (0, 0)),
                  pl.BlockSpec((d, n), lambda i: (0, 0))],
        out_specs=pl.BlockSpec((tm, n), lambda i: (i, 0)),
        out_shape=jax.ShapeDtypeStruct((m, n), F32),
        compiler_params=_cparams("parallel"),
        name="norm_matmul",
    )(x, g, w)


def _rope_tables(pos):
    half = ROT_DIM // 2
    inv = ROPE_THETA ** (-jnp.arange(half, dtype=F32) / half)
    ang = pos.astype(F32)[:, None] * inv[None, :]
    cos, sin = jnp.cos(ang), jnp.sin(ang)
    t = pos.shape[0]
    one = jnp.ones((t, D_A - ROT_DIM), F32)
    zero_h = jnp.zeros((t, half), F32)
    zero_r = jnp.zeros((t, D_A - ROT_DIM), F32)
    c = jnp.concatenate([cos, cos, one], axis=1)
    s_up = jnp.concatenate([-sin, zero_h, zero_r], axis=1)
    s_dn = jnp.concatenate([zero_h, sin, zero_r], axis=1)
    rep = LANE // D_A
    return jnp.tile(c, (1, rep)), jnp.tile(s_up, (1, rep)), jnp.tile(s_dn, (1, rep))


def _rope(x, c, s_up, s_dn):
    half = ROT_DIM // 2
    outs = []
    for j in range(x.shape[1] // LANE):
        xs = x[:, j * LANE:(j + 1) * LANE]
        up = pltpu.roll(xs, LANE - half, 1)
        dn = pltpu.roll(xs, half, 1)
        outs.append(xs * c + up * s_up + dn * s_dn)
    return jnp.concatenate(outs, axis=1)


def _layernorm(x, g, b):
    mu = jnp.mean(x, axis=-1, keepdims=True)
    xc = x - mu
    var = jnp.mean(xc * xc, axis=-1, keepdims=True)
    return xc * lax.rsqrt(var + EPS) * g + b


def _even_epilogue_kernel(z_ref, c_ref, su_ref, sd_ref, lng_ref, lnb_ref, wsp_ref, bspt_ref,
                          qt_ref, k_ref, vt_ref, kv_ref, kmean_ref, gv_ref, bout_ref):
    tm = z_ref.shape[0]
    c, su, sd = c_ref[...], su_ref[...], sd_ref[...]
    q = _rope(z_ref[:, 0:QKV_A], c, su, sd) * (D_A ** -0.5)
    k = _rope(z_ref[:, QKV_A:2 * QKV_A], c, su, sd)
    v = z_ref[:, 2 * QKV_A:3 * QKV_A]
    qt_ref[0] = q.T.astype(BF16)
    k_ref[...] = k.astype(BF16)
    vt = v.T
    vt_ref[0] = vt.astype(BF16)
    kv_ref[0, 0:QKV_A, :] = k.T
    kv_ref[0, QKV_A:2 * QKV_A, :] = vt
    for blk in range(tm // MOBA_BLOCK):
        kmean_ref[blk] = jnp.mean(k[blk * MOBA_BLOCK:(blk + 1) * MOBA_BLOCK], axis=0, keepdims=True)
    u = _gelu(z_ref[:, 3 * QKV_A:3 * QKV_A + D_B])
    gv = _layernorm(_gelu(z_ref[:, 3 * QKV_A + D_B:3 * QKV_A + 2 * D_B]), lng_ref[...], lnb_ref[...])
    gv_ref[...] = gv
    gvb = gv.astype(BF16)
    row = lax.broadcasted_iota(jnp.int32, (CHUNK_B, CHUNK_B), 0)
    col = lax.broadcasted_iota(jnp.int32, (CHUNK_B, CHUNK_B), 1)
    group = lax.broadcasted_iota(jnp.int32, (CHUNK_B, D_B), 1) // D_BG
    w = [jnp.where(row >= col, wsp_ref[g], 0.0).astype(BF16) for g in range(G_B)]
    for ch in range(tm // CHUNK_B):
        gvc = gvb[ch * CHUNK_B:(ch + 1) * CHUNK_B]
        mixed = jnp.zeros((CHUNK_B, D_B), F32)
        for g in range(G_B):
            mixed = jnp.where(group == g, _dot(w[g], gvc) + bspt_ref[:, g:g + 1], mixed)
        bout_ref[ch * CHUNK_B:(ch + 1) * CHUNK_B, :] = (u[ch * CHUNK_B:(ch + 1) * CHUNK_B] * mixed).astype(BF16)


def _even_epilogue(z, tables, ln_g, ln_b, w_sp, b_sp, batch, seq, tm):
    n = batch * seq
    nt = seq // tm
    nblk = tm // MOBA_BLOCK
    tab_spec = pl.BlockSpec((tm, LANE), lambda b, i: (i, 0))
    row_spec = lambda width: pl.BlockSpec((tm, width), lambda b, i: (b * nt + i, 0))
    t_spec = pl.BlockSpec((1, QKV_A, tm), lambda b, i: (b, 0, i))
    return pl.pallas_call(
        _even_epilogue_kernel,
        grid=(batch, nt),
        in_specs=[row_spec(IN_EVEN), tab_spec, tab_spec, tab_spec,
                  pl.BlockSpec((1, D_B), lambda b, i: (0, 0)),
                  pl.BlockSpec((1, D_B), lambda b, i: (0, 0)),
                  pl.BlockSpec((G_B, CHUNK_B, CHUNK_B), lambda b, i: (0, 0, 0)),
                  pl.BlockSpec((CHUNK_B, G_B), lambda b, i: (0, 0))],
        out_specs=[t_spec, row_spec(QKV_A), t_spec, pl.BlockSpec((1, 2 * QKV_A, tm), lambda b, i: (b, 0, i)),
                   pl.BlockSpec((nblk, 1, QKV_A), lambda b, i: (b * nt + i, 0, 0)),
                   row_spec(D_B), row_spec(D_B)],
        out_shape=[jax.ShapeDtypeStruct((batch, QKV_A, seq), BF16),
                   jax.ShapeDtypeStruct((n, QKV_A), BF16),
                   jax.ShapeDtypeStruct((batch, QKV_A, seq), BF16),
                   jax.ShapeDtypeStruct((batch, 2 * QKV_A, seq), F32),
                   jax.ShapeDtypeStruct((n // MOBA_BLOCK, 1, QKV_A), F32),
                   jax.ShapeDtypeStruct((n, D_B), F32),
                   jax.ShapeDtypeStruct((n, D_B), BF16)],
        compiler_params=_cparams("parallel", "parallel"),
        name="even_epilogue",
    )(z, *tables, ln_g, ln_b, w_sp, b_sp.T)


def _moba_select(gate, n_own):
    nb = gate.shape[0]
    blk = lax.broadcasted_iota(jnp.int32, gate.shape, 0)
    elig = blk < n_own
    gm = jnp.where(elig, gate, NEG)
    rank = jnp.zeros(gate.shape, F32)
    for m in range(nb):
        gm_m = gm[m:m + 1, :]
        ahead = (gm_m > gm) | ((gm_m == gm) & (m < blk))
        rank = rank + ahead.astype(F32)
    return (elig & (rank < MOBA_TOPK)).astype(F32)


def _moba_prompt_kernel(qt_ref, k_ref, vt_ref, kmean_ref, o_ref, sel_ref):
    i = pl.program_id(2)
    tq = MOBA_BLOCK
    own = pl.multiple_of(i * tq, tq)
    key_i = lax.broadcasted_iota(jnp.int32, (tq, tq), 0)
    qry_i = lax.broadcasted_iota(jnp.int32, (tq, tq), 1)
    outs = []
    for hh in range(LANE // D_A):
        lanes = slice(hh * D_A, (hh + 1) * D_A)
        qt = qt_ref[0, lanes, :]
        gate = jnp.dot(kmean_ref[0, :, lanes], qt.astype(F32), precision=HI, preferred_element_type=F32)
        sel_ref[hh] = _moba_select(gate, i)
        s = _dot(k_ref[pl.ds(own, tq), lanes], qt)
        s = jnp.where(key_i <= qry_i, s, NEG)
        m = jnp.max(s, axis=0, keepdims=True)
        p = jnp.exp(s - m)
        l = jnp.sum(p, axis=0, keepdims=True)
        acc = _dot(vt_ref[0, lanes, pl.ds(own, tq)], p.astype(BF16))

        def body(j, carry, hh=hh, lanes=lanes, qt=qt):
            m, l, acc = carry
            start = pl.multiple_of(j * tq, tq)
            s = _dot(k_ref[pl.ds(start, tq), lanes], qt)
            s = jnp.where(sel_ref[hh, pl.ds(j, 1), :] > 0.0, s, NEG)
            m_new = jnp.maximum(m, jnp.max(s, axis=0, keepdims=True))
            alpha = jnp.exp(m - m_new)
            p = jnp.exp(s - m_new)
            l = l * alpha + jnp.sum(p, axis=0, keepdims=True)
            acc = acc * alpha + _dot(vt_ref[0, lanes, pl.ds(start, tq)], p.astype(BF16))
            return m_new, l, acc

        m, l, acc = lax.fori_loop(0, i, body, (m, l, acc))
        outs.append(acc / l)
    o_ref[...] = jnp.concatenate(outs, axis=0).T.astype(BF16)


def _moba_prompt(qt, k, vt, kmean, batch, seq):
    nq = seq // MOBA_BLOCK
    hp = QKV_A // LANE
    return pl.pallas_call(
        _moba_prompt_kernel,
        grid=(batch, hp, nq),
        in_specs=[pl.BlockSpec((1, LANE, MOBA_BLOCK), lambda b, h, i: (b, h, i)),
                  pl.BlockSpec((seq, LANE), lambda b, h, i: (b, h)),
                  pl.BlockSpec((1, LANE, seq), lambda b, h, i: (b, h, 0)),
                  pl.BlockSpec((1, nq, LANE), lambda b, h, i: (b, 0, h))],
        out_specs=pl.BlockSpec((MOBA_BLOCK, LANE), lambda b, h, i: (b * nq + i, h)),
        out_shape=jax.ShapeDtypeStruct((batch * seq, QKV_A), BF16),
        scratch_shapes=[pltpu.VMEM((LANE // D_A, nq, MOBA_BLOCK), F32)],
        compiler_params=_cparams("parallel", "parallel", "arbitrary"),
        name="moba_prompt",
    )(qt, k, vt, kmean)


def _proj_ffn_kernel(*refs, n_mix):
    mix_refs = refs[:n_mix]
    x_ref, wo_ref, g_ref, wg_ref, wu_ref, wd_ref, o_ref, hn_ref = refs[n_mix:]

    @pl.when(pl.program_id(1) == 0)
    def _():
        x1 = x_ref[...]
        off = 0
        for r in mix_refs:
            width = r.shape[1]
            x1 = x1 + _dot(r[...], wo_ref[off:off + width, :])
            off += width
        o_ref[...] = x1
        hn_ref[...] = _rms(x1, g_ref[...]).astype(BF16)

    h = hn_ref[...]
    act = (_silu(_dot(h, wg_ref[...])) * _dot(h, wu_ref[...])).astype(BF16)
    o_ref[...] += _dot(act, wd_ref[...])


def _proj_ffn(mixes, x, w_out, g, w_gate, w_up, w_down, tm, tf):
    m, d = x.shape
    ff = w_gate.shape[1]
    mix_specs = [pl.BlockSpec((tm, a.shape[1]), lambda i, f: (i, 0)) for a in mixes]
    return pl.pallas_call(
        functools.partial(_proj_ffn_kernel, n_mix=len(mixes)),
        grid=(m // tm, ff // tf),
        in_specs=mix_specs + [
            pl.BlockSpec((tm, d), lambda i, f: (i, 0)),
            pl.BlockSpec(w_out.shape, lambda i, f: (0, 0)),
            pl.BlockSpec((1, d), lambda i, f: (0, 0)),
            pl.BlockSpec((d, tf), lambda i, f: (0, f)),
            pl.BlockSpec((d, tf), lambda i, f: (0, f)),
            pl.BlockSpec((tf, d), lambda i, f: (f, 0))],
        out_specs=pl.BlockSpec((tm, d), lambda i, f: (i, 0)),
        out_shape=jax.ShapeDtypeStruct((m, d), F32),
        scratch_shapes=[pltpu.VMEM((tm, d), BF16)],
        compiler_params=_cparams("parallel", "arbitrary"),
        name="proj_ffn",
    )(*mixes, x, w_out, g, w_gate, w_up, w_down)


def _even_layer_prompt(x, p, batch, seq):
    z = _norm_matmul(x, p["norm_mix"], p["w_in"], 512)
    tables = _rope_tables(jnp.arange(seq, dtype=jnp.int32))
    qt, k, vt, kv, kmean, gv, b_out = _even_epilogue(
        z, tables, p["ln_g"], p["ln_b"], p["w_sp"], p["b_sp"], batch, seq, 256)
    a_out = _moba_prompt(qt, k, vt, kmean.reshape(batch, seq // MOBA_BLOCK, QKV_A), batch, seq)
    x = _proj_ffn([a_out, b_out], x, p["w_out"], p["norm_ffn"], p["w_gate"], p["w_up"], p["w_down"], 512, 256)
    n_open = seq - ((seq - 1) // CHUNK_B) * CHUNK_B
    gv_open = gv.reshape(batch, seq, D_B)[:, seq - n_open:]
    kv = jnp.transpose(kv.reshape(batch, 2, H_A, D_A, seq), (0, 4, 1, 2, 3))
    return x, kv, gv_open


def _even_params(i, ev_norm_mix, ev_w_in, ev_gmlp_ln_g, ev_gmlp_ln_b, ev_w_spatial, ev_b_spatial, ev_w_out,
                 ev_norm_ffn, ev_w_gate, ev_w_up, ev_w_down):
    return dict(norm_mix=ev_norm_mix[i][None], w_in=ev_w_in[i].astype(BF16),
                ln_g=ev_gmlp_ln_g[i][None], ln_b=ev_gmlp_ln_b[i][None],
                w_sp=ev_w_spatial[i], b_sp=ev_b_spatial[i], w_out=ev_w_out[i].astype(BF16),
                norm_ffn=ev_norm_ffn[i][None], w_gate=ev_w_gate[i].astype(BF16),
                w_up=ev_w_up[i].astype(BF16), w_down=ev_w_down[i].astype(BF16))


O_CQ, O_CK, O_CV, O_CR, O_DQKV, O_DG, O_SM = 0, 256, 512, 1024, 1536, 2560, 3072
SM_DA, SM_DB = GATE_RANK, GATE_RANK + H_D
IN_ODD_PAD = O_SM + LANE
KEY_C = H_C * DK_C
PAIR = LANE // DK_C


def _log_sigmoid(x):
    return jnp.minimum(x, 0.0) - jnp.log1p(jnp.exp(-jnp.abs(x)))


def _split_bf16(a):
    hi = a.astype(BF16)
    return hi, (a - hi.astype(F32)).astype(BF16)


def _dot3(a, b):
    ah, al = _split_bf16(a)
    bh, bl = _split_bf16(b)
    return _dot(ah, bh) + _dot(ah, bl) + _dot(al, bh)


def _unit_lower_inverse(a):
    c = a.shape[0]
    eye = (lax.broadcasted_iota(jnp.int32, (c, c), 0) == lax.broadcasted_iota(jnp.int32, (c, c), 1)).astype(F32)
    x = -a
    p = eye + x
    steps = int(math.log2(c)) - 1
    for _ in range(steps):
        x = _dot3(x, x)
        p = p + _dot3(p, x)
    return p


def _group_sumsq(y, width):
    n = y.shape[1]
    same = (lax.broadcasted_iota(jnp.int32, (n, n), 0) // width
            == lax.broadcasted_iota(jnp.int32, (n, n), 1) // width).astype(F32)
    return jnp.dot(y * y, same, precision=HI, preferred_element_type=F32)


def _odd_mixer_prompt_kernel(z_ref, wg2_ref, bg2_ref, gnorm_ref, cw_ref, alog_ref, dtb_ref, dnorm_ref,
                             o_ref, sg_ref, sd_ref, tail_ref,
                             stg_ref, std_ref, prev_ref, la_ref, qkv_ref, dla_ref, beta_ref):
    i = pl.program_id(1)
    tc = z_ref.shape[0]
    c = CHUNK_LIN

    @pl.when(i == 0)
    def _():
        stg_ref[...] = jnp.zeros_like(stg_ref)
        std_ref[...] = jnp.zeros_like(std_ref)
        prev_ref[...] = jnp.zeros_like(prev_ref)

    small = z_ref[:, O_SM:O_SM + LANE]
    pre = jnp.dot(small, wg2_ref[...], precision=HI, preferred_element_type=F32) + bg2_ref[...]
    la_ref[...] = _log_sigmoid(pre) / GATE_NORM
    dla_ref[...] = -jnp.exp(alog_ref[...]) * _softplus(small + dtb_ref[...])
    beta_ref[...] = jax.nn.sigmoid(small)

    x = z_ref[:, O_DQKV:O_DQKV + C_CONV]
    x8 = x[0:8]
    p8 = prev_ref[...]
    row8 = lax.broadcasted_iota(jnp.int32, (8, C_CONV), 0)
    y = x * cw_ref[CONV_W - 1:CONV_W, :]
    y8 = x8 * cw_ref[CONV_W - 1:CONV_W, :]
    for s in range(1, CONV_W):
        wrow = cw_ref[CONV_W - 1 - s:CONV_W - s, :]
        y = y + pltpu.roll(x, s, 0) * wrow
        y8 = y8 + jnp.where(row8 < s, pltpu.roll(p8, s, 0), pltpu.roll(x8, s, 0)) * wrow
    prev_ref[...] = x[tc - 8:tc]
    y = _silu(y)
    y8 = _silu(y8)
    yqk = y[:, 0:2 * KEY_C]
    nrm = lax.rsqrt(_group_sumsq(yqk, DK_D) + EPS)
    qscale = jnp.where(lax.broadcasted_iota(jnp.int32, (1, 2 * KEY_C), 1) < KEY_C, DK_D ** -0.5, 1.0)
    qkv_ref[:, 0:2 * KEY_C] = yqk * nrm * qscale
    qkv_ref[:, 2 * KEY_C:] = y[:, 2 * KEY_C:]
    yqk8 = y8[:, 0:2 * KEY_C]
    qkv_ref[0:8, 0:2 * KEY_C] = yqk8 * lax.rsqrt(_group_sumsq(yqk8, DK_D) + EPS) * qscale
    qkv_ref[0:8, 2 * KEY_C:] = y8[:, 2 * KEY_C:]

    ri = lax.broadcasted_iota(jnp.int32, (c, c), 0)
    ci = lax.broadcasted_iota(jnp.int32, (c, c), 1)
    tril = ri >= ci
    strict = ri > ci
    lower = tril.astype(F32)
    upper = (ri <= ci).astype(F32)
    ones = jnp.ones((c, c), F32)
    lane_head = lax.broadcasted_iota(jnp.int32, (c, LANE), 1) // DK_C
    lane_head_row = lax.broadcasted_iota(jnp.int32, (1, LANE), 1) // DK_C

    def chunk(ch, carry):
        rows = pl.ds(pl.multiple_of(ch * c, c), c)
        for p in range(H_C // PAIR):
            lanes = slice(p * LANE, (p + 1) * LANE)
            bcum = jnp.dot(lower, la_ref[rows, lanes], precision=HI, preferred_element_type=F32)
            b_end = bcum[c - 1:c, :]
            q_in = z_ref[rows, O_CQ + p * LANE:O_CQ + (p + 1) * LANE] * (DK_C ** -0.5) * jnp.exp(bcum)
            k = z_ref[rows, O_CK + p * LANE:O_CK + (p + 1) * LANE]
            k_in = (k * jnp.exp(-bcum)).astype(BF16)
            k_end = k * jnp.exp(b_end - bcum)
            st = stg_ref[p]
            stb = st.astype(BF16)
            upd = jnp.zeros((DV_C, LANE), F32)
            for hh in range(PAIR):
                h = p * PAIR + hh
                mask = lane_head == hh
                qm = jnp.where(mask, q_in, 0.0).astype(BF16)
                sc = jnp.where(tril, _dot_nt(qm, k_in), 0.0)
                vb = z_ref[rows, O_CV + h * DV_C:O_CV + (h + 1) * DV_C].astype(BF16)
                o = _dot(sc.astype(BF16), vb) + _dot_nt(qm, stb)
                upd = upd + _dot_tn(vb, jnp.where(mask, k_end, 0.0).astype(BF16))
                gate = _silu(z_ref[rows, O_CR + h * DV_C:O_CR + (h + 1) * DV_C])
                o_ref[rows, h * DV_C:(h + 1) * DV_C] = (_rms(o, gnorm_ref[...]) * gate).astype(BF16)
            stg_ref[p] = st * jnp.exp(b_end) + upd
        for p in range(H_D // PAIR):
            q2 = qkv_ref[rows, p * LANE:(p + 1) * LANE]
            k2 = qkv_ref[rows, KEY_C + p * LANE:KEY_C + (p + 1) * LANE]
            st = std_ref[p]
            stb = st.astype(BF16)
            upd = jnp.zeros((DV_D, LANE), F32)
            dec_row = jnp.zeros((1, LANE), F32)
            for hh in range(PAIR):
                h = p * PAIR + hh
                mask = lane_head == hh
                la_b = jnp.broadcast_to(dla_ref[rows, SM_DA + h:SM_DA + h + 1], (c, LANE))
                beta_b = jnp.broadcast_to(beta_ref[rows, SM_DB + h:SM_DB + h + 1], (c, LANE))
                g_col = jnp.dot(lower, la_b, precision=HI, preferred_element_type=F32)
                g_row = jnp.dot(ones, la_b[:, 0:c] * upper, precision=HI, preferred_element_type=F32)
                decay = jnp.exp(jnp.where(tril, g_col[:, 0:c] - g_row, NEG))
                eg = jnp.exp(g_col)
                g_last = g_col[c - 1:c, :]
                km = jnp.where(mask, k2, 0.0)
                qm = jnp.where(mask, q2, 0.0)
                kmb = km.astype(BF16)
                kb = km * beta_b
                a_mat = jnp.where(strict, _dot_nt(kb.astype(BF16), kmb) * decay, 0.0)
                t_inv = _unit_lower_inverse(a_mat)
                v = qkv_ref[rows, 2 * KEY_C + h * DV_D:2 * KEY_C + (h + 1) * DV_D]
                u = _dot3(t_inv, v * beta_b)
                w = _dot3(t_inv, kb * eg)
                qk = jnp.where(tril, _dot_nt(qm.astype(BF16), kmb) * decay, 0.0)
                v_new = u - _dot_nt(w.astype(BF16), stb)
                vnb = v_new.astype(BF16)
                o = _dot_nt((qm * eg).astype(BF16), stb) + _dot(qk.astype(BF16), vnb)
                upd = upd + _dot_tn(vnb, (km * jnp.exp(g_last - g_col)).astype(BF16))
                dec_row = jnp.where(lane_head_row == hh, jnp.exp(g_last), dec_row)
                gate = _silu(z_ref[rows, O_DG + h * DV_D:O_DG + (h + 1) * DV_D])
                col = H_C * DV_C + h * DV_D
                o_ref[rows, col:col + DV_D] = (_rms(o, dnorm_ref[...]) * gate).astype(BF16)
            std_ref[p] = st * dec_row + upd
        return carry

    lax.fori_loop(0, tc // c, chunk, 0)

    @pl.when(i == pl.num_programs(1) - 1)
    def _():
        tail_ref[0] = prev_ref[...]
        for p in range(H_C // PAIR):
            tg = stg_ref[p].T
            td = std_ref[p].T
            for hh in range(PAIR):
                sg_ref[0, p * PAIR + hh] = tg[hh * DK_C:(hh + 1) * DK_C, :]
                sd_ref[0, p * PAIR + hh] = td[hh * DK_D:(hh + 1) * DK_D, :]


def _odd_mixer_prompt(z, p, batch, seq, tc):
    nt = seq // tc
    full = lambda a: pl.BlockSpec(a.shape, lambda b, i: (0,) * a.ndim)
    consts = [p["w_g2_pad"], p["b_g2"], p["gla_norm"], p["conv_w"], p["alog_slab"], p["dtb_slab"], p["delta_norm"]]
    st_spec = pl.BlockSpec((1, H_C, DK_C, DV_C), lambda b, i: (b, 0, 0, 0))
    return pl.pallas_call(
        _odd_mixer_prompt_kernel,
        grid=(batch, nt),
        in_specs=[pl.BlockSpec((tc, IN_ODD_PAD), lambda b, i: (b * nt + i, 0))] + [full(a) for a in consts],
        out_specs=[pl.BlockSpec((tc, D_MODEL), lambda b, i: (b * nt + i, 0)), st_spec, st_spec,
                   pl.BlockSpec((1, 8, C_CONV), lambda b, i: (b, 0, 0))],
        out_shape=[jax.ShapeDtypeStruct((batch * seq, D_MODEL), BF16),
                   jax.ShapeDtypeStruct((batch, H_C, DK_C, DV_C), F32),
                   jax.ShapeDtypeStruct((batch, H_D, DK_D, DV_D), F32),
                   jax.ShapeDtypeStruct((batch, 8, C_CONV), F32)],
        scratch_shapes=[pltpu.VMEM((H_C // PAIR, DV_C, LANE), F32),
                        pltpu.VMEM((H_D // PAIR, DV_D, LANE), F32),
                        pltpu.VMEM((8, C_CONV), F32),
                        pltpu.VMEM((tc, KEY_C), F32),
                        pltpu.VMEM((tc, C_CONV), F32),
                        pltpu.VMEM((tc, LANE), F32),
                        pltpu.VMEM((tc, LANE), F32)],
        compiler_params=_cparams("parallel", "arbitrary"),
        name="odd_mixer_prompt",
    )(z, *consts)


M_E1, M_E2, M_R1, M_R2, M_G1, M_G2 = (N_EXPERTS + j for j in range(6))


def _proj_router_kernel(mix_ref, x_ref, wo_ref, g_ref, wr_ref, x1_ref, hn_ref, meta_ref, cnt_ref, carry_ref):
    tm = x_ref.shape[0]

    @pl.when(pl.program_id(0) == 0)
    def _():
        carry_ref[...] = jnp.zeros_like(carry_ref)

    x1 = x_ref[...] + _dot(mix_ref[...], wo_ref[...])
    x1_ref[...] = x1
    hn = _rms(x1, g_ref[...])
    hn_ref[...] = hn.astype(hn_ref.dtype)
    lane = lax.broadcasted_iota(jnp.int32, (tm, LANE), 1)
    logits = jnp.dot(hn, wr_ref[...], precision=HI, preferred_element_type=F32)
    logits = jnp.where(lane < N_EXPERTS, logits, NEG)
    m1 = jnp.max(logits, axis=1, keepdims=True)
    e1 = jnp.min(jnp.where(logits == m1, lane, LANE), axis=1, keepdims=True)
    rest = jnp.where(lane == e1, NEG, logits)
    m2 = jnp.max(rest, axis=1, keepdims=True)
    e2 = jnp.min(jnp.where(rest == m2, lane, LANE), axis=1, keepdims=True)
    t = jnp.exp(m2 - m1)
    g1 = 1.0 / (1.0 + t)
    g2 = t / (1.0 + t)
    oh1 = lane == e1
    oh2 = lane == e2
    member = (oh1 | oh2).astype(F32)
    ri = lax.broadcasted_iota(jnp.int32, (tm, tm), 0)
    ci = lax.broadcasted_iota(jnp.int32, (tm, tm), 1)
    before = _dot((ri > ci).astype(BF16), member.astype(BF16)) + carry_ref[...]
    r1 = jnp.sum(jnp.where(oh1, before, 0.0), axis=1, keepdims=True)
    r2 = jnp.sum(jnp.where(oh2, before, 0.0), axis=1, keepdims=True)
    carry_ref[...] = carry_ref[...] + jnp.sum(member, axis=0, keepdims=True)
    cnt_ref[...] = carry_ref[...]
    meta = jnp.where(oh1, g1, 0.0) + jnp.where(oh2, g2, 0.0)
    meta = jnp.where(lane == M_E1, e1.astype(F32), meta)
    meta = jnp.where(lane == M_E2, e2.astype(F32), meta)
    meta = jnp.where(lane == M_R1, r1, meta)
    meta = jnp.where(lane == M_R2, r2, meta)
    meta = jnp.where(lane == M_G1, g1, meta)
    meta = jnp.where(lane == M_G2, g2, meta)
    meta_ref[...] = meta


def _proj_router(mix, x, w_out, g, w_router_pad, tm, hn_dtype):
    m, d = x.shape
    row = lambda width: pl.BlockSpec((tm, width), lambda i: (i, 0))
    full = lambda a: pl.BlockSpec(a.shape, lambda i: (0,) * a.ndim)
    return pl.pallas_call(
        _proj_router_kernel,
        grid=(m // tm,),
        in_specs=[row(d), row(d), full(w_out), full(g), full(w_router_pad)],
        out_specs=[row(d), row(d), row(LANE), pl.BlockSpec((1, LANE), lambda i: (0, 0))],
        out_shape=[jax.ShapeDtypeStruct((m, d), F32), jax.ShapeDtypeStruct((m, d), hn_dtype),
                   jax.ShapeDtypeStruct((m, LANE), F32), jax.ShapeDtypeStruct((1, LANE), F32)],
        scratch_shapes=[pltpu.VMEM((1, LANE), F32)],
        compiler_params=_cparams("arbitrary"),
        name="proj_router",
    )(mix, x, w_out, g, w_router_pad)


def _moe_dense_kernel(hn_ref, x1_ref, meta_ref, wg_ref, wu_ref, wd_ref, gf_ref, o_ref):
    e = pl.program_id(1)
    f = pl.program_id(2)

    @pl.when((e == 0) & (f == 0))
    def _():
        o_ref[...] = x1_ref[...]

    h = hn_ref[...]
    act = (_silu(_dot(h, wg_ref[...])) * _dot(h, wu_ref[...])).astype(BF16)
    lane = lax.broadcasted_iota(jnp.int32, meta_ref.shape, 1)
    gate = jnp.sum(jnp.where(lane == e, meta_ref[...], 0.0), axis=1, keepdims=True)
    o_ref[...] += gate * _dot(act, wd_ref[...])

    @pl.when((e == pl.num_programs(1) - 1) & (f == pl.num_programs(2) - 1))
    def _():
        o_ref[...] = _rms(o_ref[...], gf_ref[...])


def _moe_dense(hn, x1, meta, w_gate, w_up, w_down, g_final, tm, tf):
    m, d = x1.shape
    n_e, _, ff = w_gate.shape
    row = lambda width: pl.BlockSpec((tm, width), lambda i, e, f: (i, 0))
    return pl.pallas_call(
        _moe_dense_kernel,
        grid=(m // tm, n_e, ff // tf),
        in_specs=[row(d), row(d), row(LANE),
                  pl.BlockSpec((None, d, tf), lambda i, e, f: (e, 0, f)),
                  pl.BlockSpec((None, d, tf), lambda i, e, f: (e, 0, f)),
                  pl.BlockSpec((None, tf, d), lambda i, e, f: (e, f, 0)),
                  pl.BlockSpec((1, d), lambda i, e, f: (0, 0))],
        out_specs=row(d),
        out_shape=jax.ShapeDtypeStruct((m, d), F32),
        compiler_params=_cparams("parallel", "arbitrary", "arbitrary"),
        name="moe_dense",
    )(hn, x1, meta, w_gate, w_up, w_down, g_final)


def _moe_dispatch_kernel(dest_ref, hn_ref, xs_in_ref, xs_ref, sem):
    del xs_in_ref
    tm = hn_ref.shape[0]
    base = pl.program_id(0) * (TOP_K * tm)

    def row_copy(t, k):
        return pltpu.make_async_copy(hn_ref.at[pl.ds(t, 1)], xs_ref.at[pl.ds(dest_ref[base + k * tm + t], 1)], sem)

    def issue(t, carry):
        for k in range(TOP_K):
            row_copy(t, k).start()
        return carry

    def drain(t, carry):
        for k in range(TOP_K):
            row_copy(t, k).wait()
        return carry

    lax.fori_loop(0, tm, issue, 0)
    lax.fori_loop(0, tm, drain, 0)


def _moe_dispatch(dest, hn, n_rows, tm):
    m, d = hn.shape
    return pl.pallas_call(
        _moe_dispatch_kernel,
        grid_spec=pltpu.PrefetchScalarGridSpec(
            num_scalar_prefetch=1,
            grid=(m // tm,),
            in_specs=[pl.BlockSpec((tm, d), lambda i, dest: (i, 0)), pl.BlockSpec(memory_space=pl.ANY)],
            out_specs=pl.BlockSpec(memory_space=pl.ANY),
            scratch_shapes=[pltpu.SemaphoreType.DMA(())]),
        out_shape=jax.ShapeDtypeStruct((n_rows, d), hn.dtype),
        input_output_aliases={2: 0},
        compiler_params=_cparams("arbitrary"),
        name="moe_dispatch",
    )(dest, hn, jnp.zeros((n_rows, d), hn.dtype))


def _moe_grouped_kernel(te_ref, nv_ref, xs_ref, wg_ref, wu_ref, wd_ref, o_ref, xb_ref):
    del te_ref
    f = pl.program_id(1)

    @pl.when(f == 0)
    def _():
        o_ref[...] = jnp.zeros_like(o_ref)
        xb_ref[...] = xs_ref[...].astype(BF16)

    @pl.when(pl.program_id(0) < nv_ref[0])
    def _():
        h = xb_ref[...]
        act = (_silu(_dot(h, wg_ref[...])) * _dot(h, wu_ref[...])).astype(BF16)
        o_ref[...] += _dot(act, wd_ref[...])


def _moe_grouped(tile_expert, n_valid, xs, w_gate, w_up, w_down, tg, tf):
    rows, d = xs.shape
    ff = w_gate.shape[2]
    nf = ff // tf
    fidx = lambda r, f, nv: jnp.where(r < nv[0], f, nf - 1)
    return pl.pallas_call(
        _moe_grouped_kernel,
        grid_spec=pltpu.PrefetchScalarGridSpec(
            num_scalar_prefetch=2,
            grid=(rows // tg, nf),
            in_specs=[pl.BlockSpec((tg, d), lambda r, f, te, nv: (r, 0)),
                      pl.BlockSpec((None, d, tf), lambda r, f, te, nv: (te[r], 0, fidx(r, f, nv))),
                      pl.BlockSpec((None, d, tf), lambda r, f, te, nv: (te[r], 0, fidx(r, f, nv))),
                      pl.BlockSpec((None, tf, d), lambda r, f, te, nv: (te[r], fidx(r, f, nv), 0))],
            out_specs=pl.BlockSpec((tg, d), lambda r, f, te, nv: (r, 0)),
            scratch_shapes=[pltpu.VMEM((tg, d), BF16)]),
        out_shape=jax.ShapeDtypeStruct((rows, d), F32),
        compiler_params=_cparams("parallel", "arbitrary"),
        name="moe_grouped",
    )(tile_expert, n_valid, xs, w_gate, w_up, w_down)


def _moe_combine_kernel(dest_ref, ys_ref, x1_ref, meta_ref, gf_ref, o_ref, buf_ref, sem):
    i = pl.program_id(0)
    tc = x1_ref.shape[0]
    rows = TOP_K * tc

    def row_copy(step, slot, j):
        return pltpu.make_async_copy(ys_ref.at[pl.ds(dest_ref[step * rows + j], 1)],
                                     buf_ref.at[slot, pl.ds(j, 1)], sem.at[slot])

    def issue(step, slot):
        def body(j, carry):
            row_copy(step, slot, j).start()
            return carry
        lax.fori_loop(0, rows, body, 0)

    def drain(step, slot):
        def body(j, carry):
            row_copy(step, slot, j).wait()
            return carry
        lax.fori_loop(0, rows, body, 0)

    slot = i % 2

    @pl.when(i == 0)
    def _():
        issue(0, 0)

    @pl.when(i + 1 < pl.num_programs(0))
    def _():
        issue(i + 1, 1 - slot)

    drain(i, slot)
    g1 = meta_ref[:, M_G1:M_G1 + 1]
    g2 = meta_ref[:, M_G2:M_G2 + 1]
    y = x1_ref[...] + g1 * buf_ref[slot, 0:tc] + g2 * buf_ref[slot, tc:rows]
    o_ref[...] = _rms(y, gf_ref[...])


def _moe_combine(dest, ys, x1, meta, g_final, tc):
    m, d = x1.shape
    row = lambda width: pl.BlockSpec((tc, width), lambda i, dest: (i, 0))
    return pl.pallas_call(
        _moe_combine_kernel,
        grid_spec=pltpu.PrefetchScalarGridSpec(
            num_scalar_prefetch=1,
            grid=(m // tc,),
            in_specs=[pl.BlockSpec(memory_space=pl.ANY), row(d), row(LANE),
                      pl.BlockSpec((1, d), lambda i, dest: (0, 0))],
            out_specs=row(d),
            scratch_shapes=[pltpu.VMEM((2, TOP_K * tc, d), F32), pltpu.SemaphoreType.DMA((2,))]),
        out_shape=jax.ShapeDtypeStruct((m, d), F32),
        compiler_params=_cparams("arbitrary"),
        name="moe_combine",
    )(dest, ys, x1, meta, g_final)


def _moe_routes(meta, counts, tile, tg):
    m = meta.shape[0]
    n_rows = TOP_K * m + N_EXPERTS * tg
    expert = meta[:, M_E1:M_E2 + 1].astype(jnp.int32)
    rank = meta[:, M_R1:M_R2 + 1].astype(jnp.int32)
    padded = (counts[0, :N_EXPERTS].astype(jnp.int32) + tg - 1) // tg * tg
    ends = jnp.cumsum(padded)
    dest = (ends - padded)[expert] + rank
    dest = jnp.transpose(dest.reshape(m // tile, tile, TOP_K), (0, 2, 1)).reshape(-1)
    n_valid = ends[-1] // tg
    tile_start = jnp.arange(n_rows // tg, dtype=jnp.int32) * tg
    tile_expert = jnp.searchsorted(ends, jnp.minimum(tile_start, ends[-1] - 1), side="right").astype(jnp.int32)
    return dest, tile_expert, n_valid.reshape(1).astype(jnp.int32), n_rows


def _odd_params(i, od_norm_mix, od_w_in, od_gla_w_gate2, od_gla_b_gate2, od_gla_norm, od_delta_conv,
                od_delta_a_log, od_delta_dt_bias, od_delta_norm, od_w_out, od_norm_ffn, od_router,
                od_w_gate, od_w_up, od_w_down):
    w = od_w_in[i]
    sizes = (KEY_C, KEY_C, H_C * DV_C, GATE_RANK, H_C * DV_C, C_CONV, H_D, H_D, H_D * DV_D)
    splits = tuple(sum(sizes[:j + 1]) for j in range(len(sizes) - 1))
    cq, ck, cv, c_lr, c_r, d_qkv, d_a, d_b, d_g = jnp.split(w, splits, axis=1)
    pad = jnp.zeros((w.shape[0], IN_ODD_PAD - w.shape[1]), w.dtype)
    w_in = jnp.concatenate([cq, ck, cv, c_r, d_qkv, d_g, c_lr, d_a, d_b, pad], axis=1).astype(BF16)
    slab = lambda v: jnp.zeros((1, LANE), F32).at[0, SM_DA:SM_DA + H_D].set(v)
    return dict(norm_mix=od_norm_mix[i][None], w_in=w_in,
                w_g2_pad=jnp.zeros((LANE, KEY_C), F32).at[:GATE_RANK].set(od_gla_w_gate2[i]),
                b_g2=od_gla_b_gate2[i][None], gla_norm=od_gla_norm[i][None], conv_w=od_delta_conv[i],
                alog_slab=slab(od_delta_a_log[i]), dtb_slab=slab(od_delta_dt_bias[i]),
                a_log=od_delta_a_log[i], dt_bias=od_delta_dt_bias[i],
                delta_norm=od_delta_norm[i][None], w_out=od_w_out[i].astype(BF16),
                norm_ffn=od_norm_ffn[i][None],
                router=jnp.zeros((D_MODEL, LANE), F32).at[:, :N_EXPERTS].set(od_router[i]),
                w_gate=od_w_gate[i].astype(BF16), w_up=od_w_up[i].astype(BF16), w_down=od_w_down[i].astype(BF16))


MOE_GROUP_TILE = 512
MOE_ROW_TILE = 256


def _moe(mix, x, p, g_final, tm):
    m = x.shape[0]
    if m < N_EXPERTS * MOE_GROUP_TILE:
        x1, hn, meta, _ = _proj_router(mix, x, p["w_out"], p["norm_ffn"], p["router"], tm, BF16)
        return _moe_dense(hn, x1, meta, p["w_gate"], p["w_up"], p["w_down"], g_final, tm, 512)
    x1, hn, meta, counts = _proj_router(mix, x, p["w_out"], p["norm_ffn"], p["router"], tm, F32)
    dest, tile_expert, n_valid, n_rows = _moe_routes(meta, counts, MOE_ROW_TILE, MOE_GROUP_TILE)
    xs = _moe_dispatch(dest, hn, n_rows, MOE_ROW_TILE)
    ys = _moe_grouped(tile_expert, n_valid, xs, p["w_gate"], p["w_up"], p["w_down"], MOE_GROUP_TILE, 512)
    return _moe_combine(dest, ys, x1, meta, g_final, MOE_ROW_TILE)


def _odd_layer_prompt(x, p, g_final, batch, seq):
    z = _norm_matmul(x, p["norm_mix"], p["w_in"], 512)
    mix, s_gla, s_delta, tail = _odd_mixer_prompt(z, p, batch, seq, 256)
    y = _moe(mix, x, p, g_final, 512)
    return y, s_gla, s_delta, tail[:, 8 - (CONV_W - 1):]


def _even_epilogue_sample_kernel(z_ref, c_ref, su_ref, sd_ref, lng_ref, lnb_ref, w0_ref, b0_ref,
                                 q_ref, kv_ref, gv_ref, bout_ref):
    c, su, sd = c_ref[...], su_ref[...], sd_ref[...]
    q_ref[...] = _rope(z_ref[:, 0:QKV_A], c, su, sd) * (D_A ** -0.5)
    kv_ref[:, 0:QKV_A] = _rope(z_ref[:, QKV_A:2 * QKV_A], c, su, sd)
    kv_ref[:, QKV_A:2 * QKV_A] = z_ref[:, 2 * QKV_A:3 * QKV_A]
    u = _gelu(z_ref[:, 3 * QKV_A:3 * QKV_A + D_B])
    gv = _layernorm(_gelu(z_ref[:, 3 * QKV_A + D_B:3 * QKV_A + 2 * D_B]), lng_ref[...], lnb_ref[...])
    gv_ref[...] = gv
    bout_ref[...] = (u * (gv * w0_ref[...] + b0_ref[...])).astype(BF16)


def _even_epilogue_sample(z, tables, ln_g, ln_b, w_sp, b_sp):
    m = z.shape[0]
    w0 = jnp.repeat(w_sp[:, 0, 0], D_BG)[None]
    b0 = jnp.repeat(b_sp[:, 0], D_BG)[None]
    return pl.pallas_call(
        _even_epilogue_sample_kernel,
        out_shape=[jax.ShapeDtypeStruct((m, QKV_A), F32), jax.ShapeDtypeStruct((m, 2 * QKV_A), F32),
                   jax.ShapeDtypeStruct((m, D_B), F32), jax.ShapeDtypeStruct((m, D_B), BF16)],
        compiler_params=pltpu.CompilerParams(vmem_limit_bytes=VMEM_LIMIT),
        name="even_epilogue_sample",
    )(z, *tables, ln_g, ln_b, w0, b0)


def _moba_sample_kernel(pt_ref, qt_ref, knt_ref, vnt_ref, *refs):
    del pt_ref
    n_pages = len(refs) - 2
    page_refs, o_ref, s_ref = refs[:n_pages], refs[n_pages], refs[n_pages + 1]
    pages_per_block = MOBA_BLOCK // PAGE_SIZE
    nb = n_pages // pages_per_block
    qt = qt_ref[0]
    for h in range(H_A):
        qcol = jnp.broadcast_to(qt[:, h:h + 1], (D_A, PAGE_SIZE))
        for j in range(n_pages):
            s_ref[h, j:j + 1, :] = jnp.sum(page_refs[j][0, 0, h] * qcol, axis=0, keepdims=True)
    lane = lax.broadcasted_iota(jnp.int32, (n_pages, LANE), 1)
    page_sums = jnp.zeros((n_pages, LANE), F32)
    for h in range(H_A):
        page_sums = jnp.where(lane == h, jnp.sum(s_ref[h], axis=1, keepdims=True), page_sums)
    pair = (lax.broadcasted_iota(jnp.int32, (nb, n_pages), 1) // pages_per_block
            == lax.broadcasted_iota(jnp.int32, (nb, n_pages), 0)).astype(F32)
    pair_t = (lax.broadcasted_iota(jnp.int32, (n_pages, nb), 0) // pages_per_block
              == lax.broadcasted_iota(jnp.int32, (n_pages, nb), 1)).astype(F32)
    gate = jnp.dot(pair, page_sums, precision=HI, preferred_element_type=F32)
    sel = _moba_select(gate, nb)
    sel_pages = jnp.dot(pair_t, sel, precision=HI, preferred_element_type=F32)
    own = jnp.sum(qt * knt_ref[0], axis=0, keepdims=True)
    vnt = vnt_ref[0]
    out_lane = lax.broadcasted_iota(jnp.int32, (D_A, LANE), 1)
    out = jnp.zeros((D_A, LANE), F32)
    for h in range(H_A):
        sm = jnp.where(sel_pages[:, h:h + 1] > 0.0, s_ref[h], NEG)
        s_own = own[:, h:h + 1]
        mx = jnp.maximum(jnp.max(jnp.max(sm, axis=1, keepdims=True), axis=0, keepdims=True), s_own)
        p = jnp.exp(sm - mx)
        p_own = jnp.exp(s_own - mx)
        denom = jnp.sum(jnp.sum(p, axis=1, keepdims=True), axis=0, keepdims=True) + p_own
        acc = jnp.zeros((D_A, PAGE_SIZE), F32)
        for j in range(n_pages):
            acc = acc + page_refs[j][0, 1, h] * p[j:j + 1, :]
        o = (jnp.sum(acc, axis=1, keepdims=True) + p_own * vnt[:, h:h + 1]) / denom
        out = jnp.where(out_lane == h, o, out)
    o_ref[0] = out


def _moba_sample(q, kv_new, cache, page_table):
    bs, n_pages = page_table.shape
    assert (n_pages * PAGE_SIZE) % MOBA_BLOCK == 0
    cache_t = jnp.transpose(cache, (0, 2, 3, 4, 1))
    page_spec = lambda j: pl.BlockSpec((1, 2, H_A, D_A, PAGE_SIZE),
                                       lambda b, pt: (pt[b * n_pages + j], 0, 0, 0, 0))
    col_spec = pl.BlockSpec((1, D_A, H_A), lambda b, pt: (b, 0, 0))
    heads_t = lambda a: jnp.transpose(a.reshape(bs, H_A, D_A), (0, 2, 1))
    out = pl.pallas_call(
        _moba_sample_kernel,
        grid_spec=pltpu.PrefetchScalarGridSpec(
            num_scalar_prefetch=1,
            grid=(bs,),
            in_specs=[col_spec, col_spec, col_spec] + [page_spec(j) for j in range(n_pages)],
            out_specs=pl.BlockSpec((1, D_A, LANE), lambda b, pt: (b, 0, 0)),
            scratch_shapes=[pltpu.VMEM((H_A, n_pages, PAGE_SIZE), F32)]),
        out_shape=jax.ShapeDtypeStruct((bs, D_A, LANE), F32),
        compiler_params=_cparams("parallel"),
        name="moba_sample",
    )(page_table.reshape(-1), heads_t(q), heads_t(kv_new[:, 0:QKV_A]), heads_t(kv_new[:, QKV_A:]),
      *([cache_t] * n_pages))
    return jnp.transpose(out[:, :, 0:H_A], (0, 2, 1)).reshape(bs, QKV_A).astype(BF16)


def _odd_mixer_sample_kernel(z_ref, buf_ref, sg_ref, sd_ref, wg2_ref, bg2_ref, gnorm_ref, cw_ref, alog_ref, dtb_ref,
                             dnorm_ref, o_ref, sgo_ref, sdo_ref):
    bt = z_ref.shape[0]
    stride = H_C * DK_C
    small = z_ref[:, O_SM:O_SM + LANE]
    pre = jnp.dot(small, wg2_ref[...], precision=HI, preferred_element_type=F32) + bg2_ref[...]
    a_all = jnp.exp(_log_sigmoid(pre) / GATE_NORM)
    g_all = -jnp.exp(alog_ref[...]) * _softplus(small + dtb_ref[...])
    beta_all = jax.nn.sigmoid(small)
    x = z_ref[:, O_DQKV:O_DQKV + C_CONV]
    y = x * cw_ref[CONV_W - 1:CONV_W, :]
    for j in range(CONV_W - 1):
        y = y + buf_ref[:, j, :] * cw_ref[j:j + 1, :]
    y = _silu(y)
    for h in range(H_C):
        ks = slice(h * DK_C, (h + 1) * DK_C)
        a = a_all[:, ks]
        q = z_ref[:, O_CQ + h * DK_C:O_CQ + (h + 1) * DK_C] * (DK_C ** -0.5)
        k = z_ref[:, O_CK + h * DK_C:O_CK + (h + 1) * DK_C]
        v = z_ref[:, O_CV + h * DV_C:O_CV + (h + 1) * DV_C]
        qa = q * a
        acc = jnp.sum(q * k, axis=1, keepdims=True) * v
        for kk in range(DK_C):
            rows = pl.ds(h * DK_C + kk, bt, stride=stride)
            srow = sg_ref[rows, :]
            acc = acc + qa[:, kk:kk + 1] * srow
            sgo_ref[rows, :] = a[:, kk:kk + 1] * srow + k[:, kk:kk + 1] * v
        gate = _silu(z_ref[:, O_CR + h * DV_C:O_CR + (h + 1) * DV_C])
        o_ref[:, h * DV_C:(h + 1) * DV_C] = (_rms(acc, gnorm_ref[...]) * gate).astype(BF16)
        yq = y[:, h * DK_D:(h + 1) * DK_D]
        yk = y[:, KEY_C + h * DK_D:KEY_C + (h + 1) * DK_D]
        dv = y[:, 2 * KEY_C + h * DV_D:2 * KEY_C + (h + 1) * DV_D]
        dq = yq * lax.rsqrt(jnp.sum(yq * yq, axis=1, keepdims=True) + EPS) * (DK_D ** -0.5)
        dk = yk * lax.rsqrt(jnp.sum(yk * yk, axis=1, keepdims=True) + EPS)
        beta = beta_all[:, SM_DB + h:SM_DB + h + 1]
        eg = jnp.exp(g_all[:, SM_DA + h:SM_DA + h + 1])
        w = dk * (beta * eg)
        qd = dq * eg
        ws = jnp.zeros((bt, DV_D), F32)
        qs = jnp.zeros((bt, DV_D), F32)
        for kk in range(DK_D):
            srow = sd_ref[pl.ds(h * DK_D + kk, bt, stride=stride), :]
            ws = ws + w[:, kk:kk + 1] * srow
            qs = qs + qd[:, kk:kk + 1] * srow
        v_new = dv * beta - ws
        o = qs + jnp.sum(dq * dk, axis=1, keepdims=True) * v_new
        for kk in range(DK_D):
            rows = pl.ds(h * DK_D + kk, bt, stride=stride)
            sdo_ref[rows, :] = sd_ref[rows, :] * eg + dk[:, kk:kk + 1] * v_new
        gate = _silu(z_ref[:, O_DG + h * DV_D:O_DG + (h + 1) * DV_D])
        col = H_C * DV_C + h * DV_D
        o_ref[:, col:col + DV_D] = (_rms(o, dnorm_ref[...]) * gate).astype(BF16)


def _odd_mixer_sample(z, conv_buf, s_gla, s_delta, p, bt):
    bs = z.shape[0]
    rows = H_C * DK_C
    full = lambda a: pl.BlockSpec(a.shape, lambda i: (0,) * a.ndim)
    consts = [p["w_g2_pad"], p["b_g2"], p["gla_norm"], p["conv_w"], p["alog_slab"], p["dtb_slab"], p["delta_norm"]]
    st_spec = pl.BlockSpec((bt * rows, DV_C), lambda i: (i, 0))
    mix, sg, sd = pl.pallas_call(
        _odd_mixer_sample_kernel,
        grid=(bs // bt,),
        in_specs=[pl.BlockSpec((bt, IN_ODD_PAD), lambda i: (i, 0)),
                  pl.BlockSpec((bt, CONV_W - 1, C_CONV), lambda i: (i, 0, 0)),
                  st_spec, st_spec] + [full(a) for a in consts],
        out_specs=[pl.BlockSpec((bt, D_MODEL), lambda i: (i, 0)), st_spec, st_spec],
        out_shape=[jax.ShapeDtypeStruct((bs, D_MODEL), BF16),
                   jax.ShapeDtypeStruct((bs * rows, DV_C), F32),
                   jax.ShapeDtypeStruct((bs * rows, DV_D), F32)],
        compiler_params=_cparams("parallel"),
        name="odd_mixer_sample",
    )(z, conv_buf, s_gla.reshape(bs * rows, DV_C), s_delta.reshape(bs * rows, DV_D), *consts)
    return mix, sg.reshape(s_gla.shape), sd.reshape(s_delta.shape)


def _sample_step(x, cache, page_table, s_gla, s_delta, conv_buf, ev, od, g_final):
    bs = x.shape[0]
    past = page_table.shape[1] * PAGE_SIZE
    z = _norm_matmul(x, ev["norm_mix"], ev["w_in"], bs)
    tables = _rope_tables(jnp.full((1,), past, jnp.int32))
    q, kv, gv, b_out = _even_epilogue_sample(z, tables, ev["ln_g"], ev["ln_b"], ev["w_sp"], ev["b_sp"])
    a_out = _moba_sample(q, kv, cache, page_table)
    x = _proj_ffn([a_out, b_out], x, ev["w_out"], ev["norm_ffn"], ev["w_gate"], ev["w_up"], ev["w_down"], bs, 256)
    z = _norm_matmul(x, od["norm_mix"], od["w_in"], bs)
    mix, sg, sd = _odd_mixer_sample(z, conv_buf, s_gla, s_delta, od, 32)
    conv_new = jnp.concatenate([conv_buf[:, 1:], z[:, None, O_DQKV:O_DQKV + C_CONV]], axis=1)
    y = _moe(mix, x, od, g_final, bs)
    return y, kv, gv, sg, sd, conv_new


def kernel(x_prompt, x_sample, cache_kv, state_gla, state_delta, state_conv, page_table, ev_norm_mix, ev_w_in, ev_gmlp_ln_g, ev_gmlp_ln_b, ev_w_spatial, ev_b_spatial, ev_w_out, ev_norm_ffn, ev_w_gate, ev_w_up, ev_w_down, od_norm_mix, od_w_in, od_gla_w_gate2, od_gla_b_gate2, od_gla_norm, od_delta_conv, od_delta_a_log, od_delta_dt_bias, od_delta_norm, od_w_out, od_norm_ffn, od_router, od_w_gate, od_w_up, od_w_down, norm_final):
    bp, tp, d = x_prompt.shape
    ev = _even_params(0, ev_norm_mix, ev_w_in, ev_gmlp_ln_g, ev_gmlp_ln_b, ev_w_spatial, ev_b_spatial, ev_w_out,
                      ev_norm_ffn, ev_w_gate, ev_w_up, ev_w_down)
    od = _odd_params(0, od_norm_mix, od_w_in, od_gla_w_gate2, od_gla_b_gate2, od_gla_norm, od_delta_conv,
                     od_delta_a_log, od_delta_dt_bias, od_delta_norm, od_w_out, od_norm_ffn, od_router,
                     od_w_gate, od_w_up, od_w_down)
    bs, ts, _ = x_sample.shape
    assert ts == 1 and cache_kv.shape[0] == 1 and state_gla.shape[0] == 1
    xp, kv_p, gv_p = _even_layer_prompt(x_prompt.reshape(bp * tp, d), ev, bp, tp)
    yp, gla_p, dl_p, cv_p = _odd_layer_prompt(xp, od, norm_final[None], bp, tp)
    ys, kv_s, gv_s, gla_s, dl_s, cv_s = _sample_step(
        x_sample.reshape(bs, d), cache_kv[0], page_table, state_gla[0], state_delta[0], state_conv[0],
        ev, od, norm_final[None])
    return (yp.reshape(bp, tp, d), ys.reshape(bs, ts, d),
            kv_p[None], kv_s.reshape(1, bs, ts, 2, H_A, D_A),
            gv_p[None], gv_s.reshape(1, bs, ts, D_B),
            gla_p[None], gla_s[None], dl_p[None], dl_s[None], cv_p[None], cv_s[None])
```

```python
import functools
import math

import jax
import jax.numpy as jnp
from jax import lax
from jax.experimental import pallas as pl
from jax.experimental.pallas import tpu as pltpu

F32 = jnp.float32
BF16 = jnp.bfloat16
HI = lax.Precision.HIGHEST
EPS = 1e-6
NEG = -1e30

D_MODEL = 1024
PAGE_SIZE = 128
H_A, D_A = 8, 64
ROT_DIM = D_A // 4
ROPE_THETA = 500000.0
MOBA_BLOCK = 256
MOBA_TOPK = 3
G_B, D_BG = 8, 64
D_B = G_B * D_BG
CHUNK_B = 128
H_C, DK_C, DV_C = 4, 64, 128
GATE_RANK = 16
GATE_NORM = 16.0
H_D, DK_D, DV_D = 4, 64, 128
CONV_W = 4
C_CONV = 2 * H_D * DK_D + H_D * DV_D
CHUNK_LIN = 64
N_EXPERTS = 8
TOP_K = 2
QKV_A = H_A * D_A
IN_EVEN = 3 * QKV_A + 2 * D_B
LANE = 128
VMEM_LIMIT = 56 * 1024 * 1024


def _cparams(*sem):
    return pltpu.CompilerParams(dimension_semantics=sem, vmem_limit_bytes=VMEM_LIMIT)


def _rms(x, g):
    return x * lax.rsqrt(jnp.mean(x * x, axis=-1, keepdims=True) + EPS) * g


def _gelu(x):
    return 0.5 * x * (1.0 + lax.erf(x * (2.0 ** -0.5)))


def _silu(x):
    return x * jax.nn.sigmoid(x)


def _softplus(x):
    return jnp.maximum(x, 0.0) + jnp.log1p(jnp.exp(-jnp.abs(x)))


def _dot(a, b):
    return jnp.dot(a, b, preferred_element_type=F32)


def _dot_nt(a, b):
    return lax.dot_general(a, b, (((1,), (1,)), ((), ())), preferred_element_type=F32)


def _dot_tn(a, b):
    return lax.dot_general(a, b, (((0,), (0,)), ((), ())), preferred_element_type=F32)


def _norm_matmul_kernel(x_ref, g_ref, w_ref, o_ref):
    h = _rms(x_ref[...], g_ref[...]).astype(BF16)
    o_ref[...] = _dot(h, w_ref[...])


def _norm_matmul(x, g, w, tm):
    m, d = x.shape
    n = w.shape[1]
    return pl.pallas_call(
        _norm_matmul_kernel,
        grid=(m // tm,),
        in_specs=[pl.BlockSpec((tm, d), lambda i: (i, 0)),
                  pl.BlockSpec((1, d), lambda i: (0, 0)),
                  pl.BlockSpec((d, n), lambda i: (0, 0))],
        out_specs=pl.BlockSpec((tm, n), lambda i: (i, 0)),
        out_shape=jax.ShapeDtypeStruct((m, n), F32),
        compiler_params=_cparams("parallel"),
        name="norm_matmul",
    )(x, g, w)


def _rope_tables(pos):
    half = ROT_DIM // 2
    inv = ROPE_THETA ** (-jnp.arange(half, dtype=F32) / half)
    ang = pos.astype(F32)[:, None] * inv[None, :]
    cos, sin = jnp.cos(ang), jnp.sin(ang)
    t = pos.shape[0]
    one = jnp.ones((t, D_A - ROT_DIM), F32)
    zero_h = jnp.zeros((t, half), F32)
    zero_r = jnp.zeros((t, D_A - ROT_DIM), F32)
    c = jnp.concatenate([cos, cos, one], axis=1)
    s_up = jnp.concatenate([-sin, zero_h, zero_r], axis=1)
    s_dn = jnp.concatenate([zero_h, sin, zero_r], axis=1)
    rep = LANE // D_A
    return jnp.tile(c, (1, rep)), jnp.tile(s_up, (1, rep)), jnp.tile(s_dn, (1, rep))


def _rope(x, c, s_up, s_dn):
    half = ROT_DIM // 2
    outs = []
    for j in range(x.shape[1] // LANE):
        xs = x[:, j * LANE:(j + 1) * LANE]
        up = pltpu.roll(xs, LANE - half, 1)
        dn = pltpu.roll(xs, half, 1)
        outs.append(xs * c + up * s_up + dn * s_dn)
    return jnp.concatenate(outs, axis=1)


def _layernorm(x, g, b):
    mu = jnp.mean(x, axis=-1, keepdims=True)
    xc = x - mu
    var = jnp.mean(xc * xc, axis=-1, keepdims=True)
    return xc * lax.rsqrt(var + EPS) * g + b


def _even_epilogue_kernel(z_ref, c_ref, su_ref, sd_ref, lng_ref, lnb_ref, wsp_ref, bspt_ref,
                          qt_ref, k_ref, vt_ref, kv_ref, kmean_ref, gv_ref, bout_ref):
    tm = z_ref.shape[0]
    c, su, sd = c_ref[...], su_ref[...], sd_ref[...]
    q = _rope(z_ref[:, 0:QKV_A], c, su, sd) * (D_A ** -0.5)
    k = _rope(z_ref[:, QKV_A:2 * QKV_A], c, su, sd)
    v = z_ref[:, 2 * QKV_A:3 * QKV_A]
    qt_ref[0] = q.T.astype(BF16)
    k_ref[...] = k.astype(BF16)
    vt = v.T
    vt_ref[0] = vt.astype(BF16)
    kv_ref[0, 0:QKV_A, :] = k.T
    kv_ref[0, QKV_A:2 * QKV_A, :] = vt
    for blk in range(tm // MOBA_BLOCK):
        kmean_ref[blk] = jnp.mean(k[blk * MOBA_BLOCK:(blk + 1) * MOBA_BLOCK], axis=0, keepdims=True)
    u = _gelu(z_ref[:, 3 * QKV_A:3 * QKV_A + D_B])
    gv = _layernorm(_gelu(z_ref[:, 3 * QKV_A + D_B:3 * QKV_A + 2 * D_B]), lng_ref[...], lnb_ref[...])
    gv_ref[...] = gv
    gvb = gv.astype(BF16)
    row = lax.broadcasted_iota(jnp.int32, (CHUNK_B, CHUNK_B), 0)
    col = lax.broadcasted_iota(jnp.int32, (CHUNK_B, CHUNK_B), 1)
    group = lax.broadcasted_iota(jnp.int32, (CHUNK_B, D_B), 1) // D_BG
    w = [jnp.where(row >= col, wsp_ref[g], 0.0).astype(BF16) for g in range(G_B)]
    for ch in range(tm // CHUNK_B):
        gvc = gvb[ch * CHUNK_B:(ch + 1) * CHUNK_B]
        mixed = jnp.zeros((CHUNK_B, D_B), F32)
        for g in range(G_B):
            mixed = jnp.where(group == g, _dot(w[g], gvc) + bspt_ref[:, g:g + 1], mixed)
        bout_ref[ch * CHUNK_B:(ch + 1) * CHUNK_B, :] = (u[ch * CHUNK_B:(ch + 1) * CHUNK_B] * mixed).astype(BF16)


def _even_epilogue(z, tables, ln_g, ln_b, w_sp, b_sp, batch, seq, tm):
    n = batch * seq
    nt = seq // tm
    nblk = tm // MOBA_BLOCK
    tab_spec = pl.BlockSpec((tm, LANE), lambda b, i: (i, 0))
    row_spec = lambda width: pl.BlockSpec((tm, width), lambda b, i: (b * nt + i, 0))
    t_spec = pl.BlockSpec((1, QKV_A, tm), lambda b, i: (b, 0, i))
    return pl.pallas_call(
        _even_epilogue_kernel,
        grid=(batch, nt),
        in_specs=[row_spec(IN_EVEN), tab_spec, tab_spec, tab_spec,
                  pl.BlockSpec((1, D_B), lambda b, i: (0, 0)),
                  pl.BlockSpec((1, D_B), lambda b, i: (0, 0)),
                  pl.BlockSpec((G_B, CHUNK_B, CHUNK_B), lambda b, i: (0, 0, 0)),
                  pl.BlockSpec((CHUNK_B, G_B), lambda b, i: (0, 0))],
        out_specs=[t_spec, row_spec(QKV_A), t_spec, pl.BlockSpec((1, 2 * QKV_A, tm), lambda b, i: (b, 0, i)),
                   pl.BlockSpec((nblk, 1, QKV_A), lambda b, i: (b * nt + i, 0, 0)),
                   row_spec(D_B), row_spec(D_B)],
        out_shape=[jax.ShapeDtypeStruct((batch, QKV_A, seq), BF16),
                   jax.ShapeDtypeStruct((n, QKV_A), BF16),
                   jax.ShapeDtypeStruct((batch, QKV_A, seq), BF16),
                   jax.ShapeDtypeStruct((batch, 2 * QKV_A, seq), F32),
                   jax.ShapeDtypeStruct((n // MOBA_BLOCK, 1, QKV_A), F32),
                   jax.ShapeDtypeStruct((n, D_B), F32),
                   jax.ShapeDtypeStruct((n, D_B), BF16)],
        compiler_params=_cparams("parallel", "parallel"),
        name="even_epilogue",
    )(z, *tables, ln_g, ln_b, w_sp, b_sp.T)


def _moba_select(gate, n_own):
    nb = gate.shape[0]
    blk = lax.broadcasted_iota(jnp.int32, gate.shape, 0)
    elig = blk < n_own
    gm = jnp.where(elig, gate, NEG)
    rank = jnp.zeros(gate.shape, F32)
    for m in range(nb):
        gm_m = gm[m:m + 1, :]
        ahead = (gm_m > gm) | ((gm_m == gm) & (m < blk))
        rank = rank + ahead.astype(F32)
    return (elig & (rank < MOBA_TOPK)).astype(F32)


def _moba_prompt_kernel(qt_ref, k_ref, vt_ref, kmean_ref, o_ref, bias_ref):
    i = pl.program_id(2)
    tq = MOBA_BLOCK
    n_head = LANE // D_A
    n_split = tq // LANE
    own = pl.multiple_of(i * tq, tq)
    key_i = lax.broadcasted_iota(jnp.int32, (tq, LANE), 0)
    qry_i = lax.broadcasted_iota(jnp.int32, (tq, LANE), 1)
    head_lanes = [slice(hh * D_A, (hh + 1) * D_A) for hh in range(n_head)]
    streams = [(hh, qh) for hh in range(n_head) for qh in range(n_split)]
    qts = [qt_ref[0, head_lanes[hh], :] for hh in range(n_head)]
    queries = [qts[hh][:, qh * LANE:(qh + 1) * LANE] for hh, qh in streams]
    k_own = [k_ref[pl.ds(own, tq), head_lanes[hh]] for hh in range(n_head)]
    own_scores = [_dot(k_own[hh], queries[idx]) for idx, (hh, qh) in enumerate(streams)]
    gates = [jnp.dot(kmean_ref[0, :, head_lanes[hh]], qts[hh].astype(F32), precision=HI, preferred_element_type=F32)
             for hh in range(n_head)]
    for hh in range(n_head):
        bias = jnp.where(_moba_select(gates[hh], i) > 0.0, 0.0, NEG)
        for qh in range(n_split):
            bias_ref[hh, qh] = bias[:, qh * LANE:(qh + 1) * LANE]
    own_probs, own_stats = [], []
    for idx, (hh, qh) in enumerate(streams):
        s = jnp.where(key_i <= qry_i + qh * LANE, own_scores[idx], NEG)
        m = jnp.max(s, axis=0, keepdims=True)
        p = jnp.exp(s - m)
        own_probs.append(p.astype(BF16))
        own_stats.append((m, jnp.sum(p, axis=0, keepdims=True)))
    vt_own = [vt_ref[0, head_lanes[hh], pl.ds(own, tq)] for hh in range(n_head)]
    state = []
    for idx, (hh, qh) in enumerate(streams):
        state += [own_stats[idx][0], own_stats[idx][1], _dot(vt_own[hh], own_probs[idx])]

    def body(j, carry):
        start = pl.multiple_of(j * tq, tq)
        k_j = [k_ref[pl.ds(start, tq), head_lanes[hh]] for hh in range(n_head)]
        scores = [_dot(k_j[hh], queries[hh * n_split + qh]) for hh, qh in streams]
        probs, stats = [], []
        for idx, (hh, qh) in enumerate(streams):
            m, l = carry[3 * idx], carry[3 * idx + 1]
            s = scores[idx] + bias_ref[hh, qh, pl.ds(j, 1), :]
            m_new = jnp.maximum(m, jnp.max(s, axis=0, keepdims=True))
            alpha = jnp.exp(m - m_new)
            p = jnp.exp(s - m_new)
            probs.append(p.astype(BF16))
            stats.append((m_new, l * alpha + jnp.sum(p, axis=0, keepdims=True), alpha))
        vt_j = [vt_ref[0, head_lanes[hh], pl.ds(start, tq)] for hh in range(n_head)]
        pv = [_dot(vt_j[hh], probs[idx]) for idx, (hh, qh) in enumerate(streams)]
        out = []
        for idx in range(len(streams)):
            m_new, l_new, alpha = stats[idx]
            out += [m_new, l_new, carry[3 * idx + 2] * alpha + pv[idx]]
        return tuple(out)

    state = lax.fori_loop(0, i, body, tuple(state))
    outs = []
    for hh in range(n_head):
        halves = [state[3 * (hh * n_split + qh) + 2] / state[3 * (hh * n_split + qh) + 1] for qh in range(n_split)]
        outs.append(jnp.concatenate(halves, axis=1))
    o_ref[...] = jnp.concatenate(outs, axis=0).T.astype(BF16)


def _moba_prompt(qt, k, vt, kmean, batch, seq):
    nq = seq // MOBA_BLOCK
    hp = QKV_A // LANE
    return pl.pallas_call(
        _moba_prompt_kernel,
        grid=(batch, hp, nq),
        in_specs=[pl.BlockSpec((1, LANE, MOBA_BLOCK), lambda b, h, i: (b, h, i)),
                  pl.BlockSpec((seq, LANE), lambda b, h, i: (b, h)),
                  pl.BlockSpec((1, LANE, seq), lambda b, h, i: (b, h, 0)),
                  pl.BlockSpec((1, nq, LANE), lambda b, h, i: (b, 0, h))],
        out_specs=pl.BlockSpec((MOBA_BLOCK, LANE), lambda b, h, i: (b * nq + i, h)),
        out_shape=jax.ShapeDtypeStruct((batch * seq, QKV_A), BF16),
        scratch_shapes=[pltpu.VMEM((LANE // D_A, MOBA_BLOCK // LANE, nq, LANE), F32)],
        compiler_params=_cparams("parallel", "parallel", "arbitrary"),
        name="moba_prompt",
    )(qt, k, vt, kmean)


def _proj_ffn_kernel(*refs, n_mix):
    mix_refs = refs[:n_mix]
    x_ref, wo_ref, g_ref, wg_ref, wu_ref, wd_ref, o_ref, hn_ref = refs[n_mix:]

    @pl.when(pl.program_id(1) == 0)
    def _():
        x1 = x_ref[...]
        off = 0
        for r in mix_refs:
            width = r.shape[1]
            x1 = x1 + _dot(r[...], wo_ref[off:off + width, :])
            off += width
        o_ref[...] = x1
        hn_ref[...] = _rms(x1, g_ref[...]).astype(BF16)

    h = hn_ref[...]
    act = (_silu(_dot(h, wg_ref[...])) * _dot(h, wu_ref[...])).astype(BF16)
    o_ref[...] += _dot(act, wd_ref[...])


def _proj_ffn(mixes, x, w_out, g, w_gate, w_up, w_down, tm, tf):
    m, d = x.shape
    ff = w_gate.shape[1]
    mix_specs = [pl.BlockSpec((tm, a.shape[1]), lambda i, f: (i, 0)) for a in mixes]
    return pl.pallas_call(
        functools.partial(_proj_ffn_kernel, n_mix=len(mixes)),
        grid=(m // tm, ff // tf),
        in_specs=mix_specs + [
            pl.BlockSpec((tm, d), lambda i, f: (i, 0)),
            pl.BlockSpec(w_out.shape, lambda i, f: (0, 0)),
            pl.BlockSpec((1, d), lambda i, f: (0, 0)),
            pl.BlockSpec((d, tf), lambda i, f: (0, f)),
            pl.BlockSpec((d, tf), lambda i, f: (0, f)),
            pl.BlockSpec((tf, d), lambda i, f: (f, 0))],
        out_specs=pl.BlockSpec((tm, d), lambda i, f: (i, 0)),
        out_shape=jax.ShapeDtypeStruct((m, d), F32),
        scratch_shapes=[pltpu.VMEM((tm, d), BF16)],
        compiler_params=_cparams("parallel", "arbitrary"),
        name="proj_ffn",
    )(*mixes, x, w_out, g, w_gate, w_up, w_down)


def _even_layer_prompt(x, p, batch, seq):
    z = _norm_matmul(x, p["norm_mix"], p["w_in"], 512)
    tables = _rope_tables(jnp.arange(seq, dtype=jnp.int32))
    qt, k, vt, kv, kmean, gv, b_out = _even_epilogue(
        z, tables, p["ln_g"], p["ln_b"], p["w_sp"], p["b_sp"], batch, seq, 256)
    a_out = _moba_prompt(qt, k, vt, kmean.reshape(batch, seq // MOBA_BLOCK, QKV_A), batch, seq)
    x = _proj_ffn([a_out, b_out], x, p["w_out"], p["norm_ffn"], p["w_gate"], p["w_up"], p["w_down"], 512, 256)
    n_open = seq - ((seq - 1) // CHUNK_B) * CHUNK_B
    gv_open = gv.reshape(batch, seq, D_B)[:, seq - n_open:]
    kv = jnp.transpose(kv.reshape(batch, 2, H_A, D_A, seq), (0, 4, 1, 2, 3))
    return x, kv, gv_open


def _even_params(i, ev_norm_mix, ev_w_in, ev_gmlp_ln_g, ev_gmlp_ln_b, ev_w_spatial, ev_b_spatial, ev_w_out,
                 ev_norm_ffn, ev_w_gate, ev_w_up, ev_w_down):
    return dict(norm_mix=ev_norm_mix[i][None], w_in=ev_w_in[i].astype(BF16),
                ln_g=ev_gmlp_ln_g[i][None], ln_b=ev_gmlp_ln_b[i][None],
                w_sp=ev_w_spatial[i], b_sp=ev_b_spatial[i], w_out=ev_w_out[i].astype(BF16),
                norm_ffn=ev_norm_ffn[i][None], w_gate=ev_w_gate[i].astype(BF16),
                w_up=ev_w_up[i].astype(BF16), w_down=ev_w_down[i].astype(BF16))


O_CQ, O_CK, O_CV, O_CR, O_DQKV, O_DG, O_SM = 0, 256, 512, 1024, 1536, 2560, 3072
SM_DA, SM_DB = GATE_RANK, GATE_RANK + H_D
IN_ODD_PAD = O_SM + LANE
KEY_C = H_C * DK_C
PAIR = LANE // DK_C


def _log_sigmoid(x):
    return jnp.minimum(x, 0.0) - jnp.log1p(jnp.exp(-jnp.abs(x)))


def _split_bf16(a, terms):
    parts = []
    for _ in range(terms - 1):
        hi = a.astype(BF16)
        parts.append(hi)
        a = a - hi.astype(F32)
    parts.append(a.astype(BF16))
    return parts


def _dot_select(a, sel_bf16, terms):
    parts = _split_bf16(a, terms)
    out = _dot(parts[0], sel_bf16)
    for part in parts[1:]:
        out = out + _dot(part, sel_bf16)
    return out


def _select_dot(sel_bf16, a, terms):
    parts = _split_bf16(a, terms)
    out = _dot(sel_bf16, parts[0])
    for part in parts[1:]:
        out = out + _dot(sel_bf16, part)
    return out


def _unit_lower_inverses(mats):
    c = mats[0].shape[0]
    eye = (lax.broadcasted_iota(jnp.int32, (c, c), 0) == lax.broadcasted_iota(jnp.int32, (c, c), 1)).astype(F32)
    xs = [(-a).astype(BF16) for a in mats]
    ps = [eye - a for a in mats]
    for _ in range(int(math.log2(c)) - 1):
        xs = [_dot(x, x).astype(BF16) for x in xs]
        ps = [p + _dot(p.astype(BF16), x) for p, x in zip(ps, xs)]
    return ps


def _group_sumsq(y, width):
    n = y.shape[1]
    same = (lax.broadcasted_iota(jnp.int32, (n, n), 0) // width
            == lax.broadcasted_iota(jnp.int32, (n, n), 1) // width).astype(BF16)
    return _dot_select(y * y, same, 2)


def _odd_mixer_prompt_kernel(z_ref, wg2_ref, bg2_ref, gnorm_ref, cw_ref, alog_ref, dtb_ref, dnorm_ref,
                             o_ref, sg_ref, sd_ref, tail_ref,
                             stg_ref, std_ref, prev_ref, la_ref, qkv_ref, dla_ref, beta_ref):
    i = pl.program_id(1)
    tc = z_ref.shape[0]
    c = CHUNK_LIN

    @pl.when(i == 0)
    def _():
        stg_ref[...] = jnp.zeros_like(stg_ref)
        std_ref[...] = jnp.zeros_like(std_ref)
        prev_ref[...] = jnp.zeros_like(prev_ref)

    small = z_ref[:, O_SM:O_SM + LANE]
    pre = jnp.dot(small, wg2_ref[...], precision=HI, preferred_element_type=F32) + bg2_ref[...]
    la_ref[...] = _log_sigmoid(pre) / GATE_NORM
    dla_ref[...] = -jnp.exp(alog_ref[...]) * _softplus(small + dtb_ref[...])
    beta_ref[...] = jax.nn.sigmoid(small)

    x = z_ref[:, O_DQKV:O_DQKV + C_CONV]
    x8 = x[0:8]
    p8 = prev_ref[...]
    row8 = lax.broadcasted_iota(jnp.int32, (8, C_CONV), 0)
    y = x * cw_ref[CONV_W - 1:CONV_W, :]
    y8 = x8 * cw_ref[CONV_W - 1:CONV_W, :]
    for s in range(1, CONV_W):
        wrow = cw_ref[CONV_W - 1 - s:CONV_W - s, :]
        y = y + pltpu.roll(x, s, 0) * wrow
        y8 = y8 + jnp.where(row8 < s, pltpu.roll(p8, s, 0), pltpu.roll(x8, s, 0)) * wrow
    prev_ref[...] = x[tc - 8:tc]
    y = _silu(y)
    y8 = _silu(y8)
    yqk = y[:, 0:2 * KEY_C]
    nrm = lax.rsqrt(_group_sumsq(yqk, DK_D) + EPS)
    qscale = jnp.where(lax.broadcasted_iota(jnp.int32, (1, 2 * KEY_C), 1) < KEY_C, DK_D ** -0.5, 1.0)
    qkv_ref[:, 0:2 * KEY_C] = yqk * nrm * qscale
    qkv_ref[:, 2 * KEY_C:] = y[:, 2 * KEY_C:]
    yqk8 = y8[:, 0:2 * KEY_C]
    qkv_ref[0:8, 0:2 * KEY_C] = yqk8 * lax.rsqrt(_group_sumsq(yqk8, DK_D) + EPS) * qscale
    qkv_ref[0:8, 2 * KEY_C:] = y8[:, 2 * KEY_C:]

    ri = lax.broadcasted_iota(jnp.int32, (c, c), 0)
    ci = lax.broadcasted_iota(jnp.int32, (c, c), 1)
    tril = ri >= ci
    strict = ri > ci
    lower = tril.astype(BF16)
    upper = (ri <= ci).astype(BF16)
    lane_head = lax.broadcasted_iota(jnp.int32, (c, LANE), 1) // DK_C
    lane_head_row = lax.broadcasted_iota(jnp.int32, (1, LANE), 1) // DK_C

    def chunk(ch, carry):
        rows = pl.ds(pl.multiple_of(ch * c, c), c)
        bcum_all = _select_dot(lower, la_ref[rows, :], 3)
        for p in range(H_C // PAIR):
            bcum = bcum_all[:, p * LANE:(p + 1) * LANE]
            b_end = bcum[c - 1:c, :]
            q_in = z_ref[rows, O_CQ + p * LANE:O_CQ + (p + 1) * LANE] * (DK_C ** -0.5) * jnp.exp(bcum)
            k = z_ref[rows, O_CK + p * LANE:O_CK + (p + 1) * LANE]
            k_in = (k * jnp.exp(-bcum)).astype(BF16)
            k_end = k * jnp.exp(b_end - bcum)
            st = stg_ref[p]
            stb = st.astype(BF16)
            upd = jnp.zeros((DV_C, LANE), F32)
            for hh in range(PAIR):
                h = p * PAIR + hh
                mask = lane_head == hh
                qm = jnp.where(mask, q_in, 0.0).astype(BF16)
                sc = jnp.where(tril, _dot_nt(qm, k_in), 0.0)
                vb = z_ref[rows, O_CV + h * DV_C:O_CV + (h + 1) * DV_C].astype(BF16)
                o = _dot(sc.astype(BF16), vb) + _dot_nt(qm, stb)
                upd = upd + _dot_tn(vb, jnp.where(mask, k_end, 0.0).astype(BF16))
                gate = _silu(z_ref[rows, O_CR + h * DV_C:O_CR + (h + 1) * DV_C])
                o_ref[rows, h * DV_C:(h + 1) * DV_C] = (_rms(o, gnorm_ref[...]) * gate).astype(BF16)
            stg_ref[p] = st * jnp.exp(b_end) + upd
        dla_parts = _split_bf16(dla_ref[rows, :], 3)
        g_cols = _dot(lower, dla_parts[0]) + _dot(lower, dla_parts[1]) + _dot(lower, dla_parts[2])
        g_rows = _dot_tn(dla_parts[0], upper) + _dot_tn(dla_parts[1], upper) + _dot_tn(dla_parts[2], upper)
        heads = []
        for h in range(H_D):
            p, hh = divmod(h, PAIR)
            mask = lane_head == hh
            g_col = jnp.broadcast_to(g_cols[:, SM_DA + h:SM_DA + h + 1], (c, LANE))
            g_row = jnp.broadcast_to(g_rows[SM_DA + h:SM_DA + h + 1, :], (c, c))
            beta_b = jnp.broadcast_to(beta_ref[rows, SM_DB + h:SM_DB + h + 1], (c, LANE))
            decay = jnp.exp(jnp.where(tril, g_col[:, 0:c] - g_row, NEG))
            km = jnp.where(mask, qkv_ref[rows, KEY_C + p * LANE:KEY_C + (p + 1) * LANE], 0.0)
            qm = jnp.where(mask, qkv_ref[rows, p * LANE:(p + 1) * LANE], 0.0)
            kmb = km.astype(BF16)
            kb = km * beta_b
            heads.append(dict(
                g_col=g_col, decay=decay, km=km, qm=qm, kmb=kmb, kb=kb, beta_b=beta_b,
                a_mat=jnp.where(strict, _dot_nt(kb.astype(BF16), kmb) * decay, 0.0)))
        t_invs = _unit_lower_inverses([hd["a_mat"] for hd in heads])
        for p in range(H_D // PAIR):
            st = std_ref[p]
            stb = st.astype(BF16)
            upd = jnp.zeros((DV_D, LANE), F32)
            dec_row = jnp.zeros((1, LANE), F32)
            for hh in range(PAIR):
                h = p * PAIR + hh
                hd = heads[h]
                g_col, qm, km = hd["g_col"], hd["qm"], hd["km"]
                eg = jnp.exp(g_col)
                g_last = g_col[c - 1:c, :]
                v = qkv_ref[rows, 2 * KEY_C + h * DV_D:2 * KEY_C + (h + 1) * DV_D]
                rhs = jnp.concatenate([v * hd["beta_b"], hd["kb"] * eg], axis=1).astype(BF16)
                sol = _dot(t_invs[h].astype(BF16), rhs)
                u = sol[:, 0:DV_D]
                w = sol[:, DV_D:]
                qk = jnp.where(tril, _dot_nt(qm.astype(BF16), hd["kmb"]) * hd["decay"], 0.0)
                v_new = u - _dot_nt(w.astype(BF16), stb)
                vnb = v_new.astype(BF16)
                o = _dot_nt((qm * eg).astype(BF16), stb) + _dot(qk.astype(BF16), vnb)
                upd = upd + _dot_tn(vnb, (km * jnp.exp(g_last - g_col)).astype(BF16))
                dec_row = jnp.where(lane_head_row == hh, jnp.exp(g_last), dec_row)
                gate = _silu(z_ref[rows, O_DG + h * DV_D:O_DG + (h + 1) * DV_D])
                col = H_C * DV_C + h * DV_D
                o_ref[rows, col:col + DV_D] = (_rms(o, dnorm_ref[...]) * gate).astype(BF16)
            std_ref[p] = st * dec_row + upd
        return carry

    lax.fori_loop(0, tc // c, chunk, 0)

    @pl.when(i == pl.num_programs(1) - 1)
    def _():
        tail_ref[0] = prev_ref[...]
        for p in range(H_C // PAIR):
            tg = stg_ref[p].T
            td = std_ref[p].T
            for hh in range(PAIR):
                sg_ref[0, p * PAIR + hh] = tg[hh * DK_C:(hh + 1) * DK_C, :]
                sd_ref[0, p * PAIR + hh] = td[hh * DK_D:(hh + 1) * DK_D, :]


def _odd_mixer_prompt(z, p, batch, seq, tc):
    nt = seq // tc
    full = lambda a: pl.BlockSpec(a.shape, lambda b, i: (0,) * a.ndim)
    consts = [p["w_g2_pad"], p["b_g2"], p["gla_norm"], p["conv_w"], p["alog_slab"], p["dtb_slab"], p["delta_norm"]]
    st_spec = pl.BlockSpec((1, H_C, DK_C, DV_C), lambda b, i: (b, 0, 0, 0))
    return pl.pallas_call(
        _odd_mixer_prompt_kernel,
        grid=(batch, nt),
        in_specs=[pl.BlockSpec((tc, IN_ODD_PAD), lambda b, i: (b * nt + i, 0))] + [full(a) for a in consts],
        out_specs=[pl.BlockSpec((tc, D_MODEL), lambda b, i: (b * nt + i, 0)), st_spec, st_spec,
                   pl.BlockSpec((1, 8, C_CONV), lambda b, i: (b, 0, 0))],
        out_shape=[jax.ShapeDtypeStruct((batch * seq, D_MODEL), BF16),
                   jax.ShapeDtypeStruct((batch, H_C, DK_C, DV_C), F32),
                   jax.ShapeDtypeStruct((batch, H_D, DK_D, DV_D), F32),
                   jax.ShapeDtypeStruct((batch, 8, C_CONV), F32)],
        scratch_shapes=[pltpu.VMEM((H_C // PAIR, DV_C, LANE), F32),
                        pltpu.VMEM((H_D // PAIR, DV_D, LANE), F32),
                        pltpu.VMEM((8, C_CONV), F32),
                        pltpu.VMEM((tc, KEY_C), F32),
                        pltpu.VMEM((tc, C_CONV), F32),
                        pltpu.VMEM((tc, LANE), F32),
                        pltpu.VMEM((tc, LANE), F32)],
        compiler_params=_cparams("parallel", "arbitrary"),
        name="odd_mixer_prompt",
    )(z, *consts)


M_E1, M_E2, M_R1, M_R2, M_G1, M_G2 = (N_EXPERTS + j for j in range(6))


def _proj_router_kernel(mix_ref, x_ref, wo_ref, g_ref, wr_ref, x1_ref, hn_ref, meta_ref, cnt_ref, carry_ref):
    tm = x_ref.shape[0]

    @pl.when(pl.program_id(0) == 0)
    def _():
        carry_ref[...] = jnp.zeros_like(carry_ref)

    x1 = x_ref[...] + _dot(mix_ref[...], wo_ref[...])
    x1_ref[...] = x1
    hn = _rms(x1, g_ref[...])
    hn_ref[...] = hn.astype(hn_ref.dtype)
    lane = lax.broadcasted_iota(jnp.int32, (tm, LANE), 1)
    logits = jnp.dot(hn, wr_ref[...], precision=HI, preferred_element_type=F32)
    logits = jnp.where(lane < N_EXPERTS, logits, NEG)
    m1 = jnp.max(logits, axis=1, keepdims=True)
    e1 = jnp.min(jnp.where(logits == m1, lane, LANE), axis=1, keepdims=True)
    rest = jnp.where(lane == e1, NEG, logits)
    m2 = jnp.max(rest, axis=1, keepdims=True)
    e2 = jnp.min(jnp.where(rest == m2, lane, LANE), axis=1, keepdims=True)
    t = jnp.exp(m2 - m1)
    g1 = 1.0 / (1.0 + t)
    g2 = t / (1.0 + t)
    oh1 = lane == e1
    oh2 = lane == e2
    member = (oh1 | oh2).astype(F32)
    ri = lax.broadcasted_iota(jnp.int32, (tm, tm), 0)
    ci = lax.broadcasted_iota(jnp.int32, (tm, tm), 1)
    before = _dot((ri > ci).astype(BF16), member.astype(BF16)) + carry_ref[...]
    r1 = jnp.sum(jnp.where(oh1, before, 0.0), axis=1, keepdims=True)
    r2 = jnp.sum(jnp.where(oh2, before, 0.0), axis=1, keepdims=True)
    carry_ref[...] = carry_ref[...] + jnp.sum(member, axis=0, keepdims=True)
    cnt_ref[...] = carry_ref[...]
    meta = jnp.where(oh1, g1, 0.0) + jnp.where(oh2, g2, 0.0)
    meta = jnp.where(lane == M_E1, e1.astype(F32), meta)
    meta = jnp.where(lane == M_E2, e2.astype(F32), meta)
    meta = jnp.where(lane == M_R1, r1, meta)
    meta = jnp.where(lane == M_R2, r2, meta)
    meta = jnp.where(lane == M_G1, g1, meta)
    meta = jnp.where(lane == M_G2, g2, meta)
    meta_ref[...] = meta


def _proj_router(mix, x, w_out, g, w_router_pad, tm, hn_dtype):
    m, d = x.shape
    row = lambda width: pl.BlockSpec((tm, width), lambda i: (i, 0))
    full = lambda a: pl.BlockSpec(a.shape, lambda i: (0,) * a.ndim)
    return pl.pallas_call(
        _proj_router_kernel,
        grid=(m // tm,),
        in_specs=[row(d), row(d), full(w_out), full(g), full(w_router_pad)],
        out_specs=[row(d), row(d), row(LANE), pl.BlockSpec((1, LANE), lambda i: (0, 0))],
        out_shape=[jax.ShapeDtypeStruct((m, d), F32), jax.ShapeDtypeStruct((m, d), hn_dtype),
                   jax.ShapeDtypeStruct((m, LANE), F32), jax.ShapeDtypeStruct((1, LANE), F32)],
        scratch_shapes=[pltpu.VMEM((1, LANE), F32)],
        compiler_params=_cparams("arbitrary"),
        name="proj_router",
    )(mix, x, w_out, g, w_router_pad)


def _moe_dense_kernel(hn_ref, x1_ref, meta_ref, wg_ref, wu_ref, wd_ref, gf_ref, o_ref):
    e = pl.program_id(1)
    f = pl.program_id(2)

    @pl.when((e == 0) & (f == 0))
    def _():
        o_ref[...] = x1_ref[...]

    h = hn_ref[...]
    act = (_silu(_dot(h, wg_ref[...])) * _dot(h, wu_ref[...])).astype(BF16)
    lane = lax.broadcasted_iota(jnp.int32, meta_ref.shape, 1)
    gate = jnp.sum(jnp.where(lane == e, meta_ref[...], 0.0), axis=1, keepdims=True)
    o_ref[...] += gate * _dot(act, wd_ref[...])

    @pl.when((e == pl.num_programs(1) - 1) & (f == pl.num_programs(2) - 1))
    def _():
        o_ref[...] = _rms(o_ref[...], gf_ref[...])


def _moe_dense(hn, x1, meta, w_gate, w_up, w_down, g_final, tm, tf):
    m, d = x1.shape
    n_e, _, ff = w_gate.shape
    row = lambda width: pl.BlockSpec((tm, width), lambda i, e, f: (i, 0))
    return pl.pallas_call(
        _moe_dense_kernel,
        grid=(m // tm, n_e, ff // tf),
        in_specs=[row(d), row(d), row(LANE),
                  pl.BlockSpec((None, d, tf), lambda i, e, f: (e, 0, f)),
                  pl.BlockSpec((None, d, tf), lambda i, e, f: (e, 0, f)),
                  pl.BlockSpec((None, tf, d), lambda i, e, f: (e, f, 0)),
                  pl.BlockSpec((1, d), lambda i, e, f: (0, 0))],
        out_specs=row(d),
        out_shape=jax.ShapeDtypeStruct((m, d), F32),
        compiler_params=_cparams("parallel", "arbitrary", "arbitrary"),
        name="moe_dense",
    )(hn, x1, meta, w_gate, w_up, w_down, g_final)


def _moe_dispatch_kernel(dest_ref, hn_ref, xs_in_ref, xs_ref, sem):
    del xs_in_ref
    tm = hn_ref.shape[0]
    base = pl.program_id(0) * (TOP_K * tm)

    def row_copy(t, k):
        return pltpu.make_async_copy(hn_ref.at[pl.ds(t, 1)], xs_ref.at[pl.ds(dest_ref[base + k * tm + t], 1)], sem)

    def issue(t, carry):
        for k in range(TOP_K):
            row_copy(t, k).start()
        return carry

    def drain(t, carry):
        for k in range(TOP_K):
            row_copy(t, k).wait()
        return carry

    lax.fori_loop(0, tm, issue, 0)
    lax.fori_loop(0, tm, drain, 0)


def _moe_dispatch(dest, hn, n_rows, tm):
    m, d = hn.shape
    return pl.pallas_call(
        _moe_dispatch_kernel,
        grid_spec=pltpu.PrefetchScalarGridSpec(
            num_scalar_prefetch=1,
            grid=(m // tm,),
            in_specs=[pl.BlockSpec((tm, d), lambda i, dest: (i, 0)), pl.BlockSpec(memory_space=pl.ANY)],
            out_specs=pl.BlockSpec(memory_space=pl.ANY),
            scratch_shapes=[pltpu.SemaphoreType.DMA(())]),
        out_shape=jax.ShapeDtypeStruct((n_rows, d), hn.dtype),
        input_output_aliases={2: 0},
        compiler_params=_cparams("arbitrary"),
        name="moe_dispatch",
    )(dest, hn, jnp.zeros((n_rows, d), hn.dtype))


def _moe_grouped_kernel(te_ref, nv_ref, xs_ref, wg_ref, wu_ref, wd_ref, o_ref, xb_ref):
    del te_ref
    f = pl.program_id(1)

    @pl.when(f == 0)
    def _():
        o_ref[...] = jnp.zeros_like(o_ref)
        xb_ref[...] = xs_ref[...].astype(BF16)

    @pl.when(pl.program_id(0) < nv_ref[0])
    def _():
        h = xb_ref[...]
        act = (_silu(_dot(h, wg_ref[...])) * _dot(h, wu_ref[...])).astype(BF16)
        o_ref[...] += _dot(act, wd_ref[...])


def _moe_grouped(tile_expert, n_valid, xs, w_gate, w_up, w_down, tg, tf):
    rows, d = xs.shape
    ff = w_gate.shape[2]
    nf = ff // tf
    fidx = lambda r, f, nv: jnp.where(r < nv[0], f, nf - 1)
    return pl.pallas_call(
        _moe_grouped_kernel,
        grid_spec=pltpu.PrefetchScalarGridSpec(
            num_scalar_prefetch=2,
            grid=(rows // tg, nf),
            in_specs=[pl.BlockSpec((tg, d), lambda r, f, te, nv: (r, 0)),
                      pl.BlockSpec((None, d, tf), lambda r, f, te, nv: (te[r], 0, fidx(r, f, nv))),
                      pl.BlockSpec((None, d, tf), lambda r, f, te, nv: (te[r], 0, fidx(r, f, nv))),
                      pl.BlockSpec((None, tf, d), lambda r, f, te, nv: (te[r], fidx(r, f, nv), 0))],
            out_specs=pl.BlockSpec((tg, d), lambda r, f, te, nv: (r, 0)),
            scratch_shapes=[pltpu.VMEM((tg, d), BF16)]),
        out_shape=jax.ShapeDtypeStruct((rows, d), F32),
        compiler_params=_cparams("parallel", "arbitrary"),
        name="moe_grouped",
    )(tile_expert, n_valid, xs, w_gate, w_up, w_down)


def _moe_combine_kernel(dest_ref, ys_ref, x1_ref, meta_ref, gf_ref, o_ref, buf_ref, sem):
    i = pl.program_id(0)
    tc = x1_ref.shape[0]
    rows = TOP_K * tc

    def row_copy(step, slot, j):
        return pltpu.make_async_copy(ys_ref.at[pl.ds(dest_ref[step * rows + j], 1)],
                                     buf_ref.at[slot, pl.ds(j, 1)], sem.at[slot])

    def issue(step, slot):
        def body(j, carry):
            row_copy(step, slot, j).start()
            return carry
        lax.fori_loop(0, rows, body, 0)

    def drain(step, slot):
        def body(j, carry):
            row_copy(step, slot, j).wait()
            return carry
        lax.fori_loop(0, rows, body, 0)

    slot = i % 2

    @pl.when(i == 0)
    def _():
        issue(0, 0)

    @pl.when(i + 1 < pl.num_programs(0))
    def _():
        issue(i + 1, 1 - slot)

    drain(i, slot)
    g1 = meta_ref[:, M_G1:M_G1 + 1]
    g2 = meta_ref[:, M_G2:M_G2 + 1]
    y = x1_ref[...] + g1 * buf_ref[slot, 0:tc] + g2 * buf_ref[slot, tc:rows]
    o_ref[...] = _rms(y, gf_ref[...])


def _moe_combine(dest, ys, x1, meta, g_final, tc):
    m, d = x1.shape
    row = lambda width: pl.BlockSpec((tc, width), lambda i, dest: (i, 0))
    return pl.pallas_call(
        _moe_combine_kernel,
        grid_spec=pltpu.PrefetchScalarGridSpec(
            num_scalar_prefetch=1,
            grid=(m // tc,),
            in_specs=[pl.BlockSpec(memory_space=pl.ANY), row(d), row(LANE),
                      pl.BlockSpec((1, d), lambda i, dest: (0, 0))],
            out_specs=row(d),
            scratch_shapes=[pltpu.VMEM((2, TOP_K * tc, d), F32), pltpu.SemaphoreType.DMA((2,))]),
        out_shape=jax.ShapeDtypeStruct((m, d), F32),
        compiler_params=_cparams("arbitrary"),
        name="moe_combine",
    )(dest, ys, x1, meta, g_final)


def _moe_routes(meta, counts, tile, tg):
    m = meta.shape[0]
    n_rows = TOP_K * m + N_EXPERTS * tg
    expert = meta[:, M_E1:M_E2 + 1].astype(jnp.int32)
    rank = meta[:, M_R1:M_R2 + 1].astype(jnp.int32)
    padded = (counts[0, :N_EXPERTS].astype(jnp.int32) + tg - 1) // tg * tg
    ends = jnp.cumsum(padded)
    dest = (ends - padded)[expert] + rank
    dest = jnp.transpose(dest.reshape(m // tile, tile, TOP_K), (0, 2, 1)).reshape(-1)
    n_valid = ends[-1] // tg
    tile_start = jnp.arange(n_rows // tg, dtype=jnp.int32) * tg
    tile_expert = jnp.searchsorted(ends, jnp.minimum(tile_start, ends[-1] - 1), side="right").astype(jnp.int32)
    return dest, tile_expert, n_valid.reshape(1).astype(jnp.int32), n_rows


def _odd_params(i, od_norm_mix, od_w_in, od_gla_w_gate2, od_gla_b_gate2, od_gla_norm, od_delta_conv,
                od_delta_a_log, od_delta_dt_bias, od_delta_norm, od_w_out, od_norm_ffn, od_router,
                od_w_gate, od_w_up, od_w_down):
    w = od_w_in[i]
    sizes = (KEY_C, KEY_C, H_C * DV_C, GATE_RANK, H_C * DV_C, C_CONV, H_D, H_D, H_D * DV_D)
    splits = tuple(sum(sizes[:j + 1]) for j in range(len(sizes) - 1))
    cq, ck, cv, c_lr, c_r, d_qkv, d_a, d_b, d_g = jnp.split(w, splits, axis=1)
    pad = jnp.zeros((w.shape[0], IN_ODD_PAD - w.shape[1]), w.dtype)
    w_in = jnp.concatenate([cq, ck, cv, c_r, d_qkv, d_g, c_lr, d_a, d_b, pad], axis=1).astype(BF16)
    slab = lambda v: jnp.zeros((1, LANE), F32).at[0, SM_DA:SM_DA + H_D].set(v)
    return dict(norm_mix=od_norm_mix[i][None], w_in=w_in,
                w_g2_pad=jnp.zeros((LANE, KEY_C), F32).at[:GATE_RANK].set(od_gla_w_gate2[i]),
                b_g2=od_gla_b_gate2[i][None], gla_norm=od_gla_norm[i][None], conv_w=od_delta_conv[i],
                alog_slab=slab(od_delta_a_log[i]), dtb_slab=slab(od_delta_dt_bias[i]),
                a_log=od_delta_a_log[i], dt_bias=od_delta_dt_bias[i],
                delta_norm=od_delta_norm[i][None], w_out=od_w_out[i].astype(BF16),
                norm_ffn=od_norm_ffn[i][None],
                router=jnp.zeros((D_MODEL, LANE), F32).at[:, :N_EXPERTS].set(od_router[i]),
                w_gate=od_w_gate[i].astype(BF16), w_up=od_w_up[i].astype(BF16), w_down=od_w_down[i].astype(BF16))


MOE_GROUP_TILE = 512
MOE_ROW_TILE = 256


def _moe(mix, x, p, g_final, tm):
    m = x.shape[0]
    if m < N_EXPERTS * MOE_GROUP_TILE:
        x1, hn, meta, _ = _proj_router(mix, x, p["w_out"], p["norm_ffn"], p["router"], tm, BF16)
        return _moe_dense(hn, x1, meta, p["w_gate"], p["w_up"], p["w_down"], g_final, tm, 512)
    x1, hn, meta, counts = _proj_router(mix, x, p["w_out"], p["norm_ffn"], p["router"], tm, F32)
    dest, tile_expert, n_valid, n_rows = _moe_routes(meta, counts, MOE_ROW_TILE, MOE_GROUP_TILE)
    xs = _moe_dispatch(dest, hn, n_rows, MOE_ROW_TILE)
    ys = _moe_grouped(tile_expert, n_valid, xs, p["w_gate"], p["w_up"], p["w_down"], MOE_GROUP_TILE, 512)
    return _moe_combine(dest, ys, x1, meta, g_final, MOE_ROW_TILE)


def _odd_layer_prompt(x, p, g_final, batch, seq):
    z = _norm_matmul(x, p["norm_mix"], p["w_in"], 512)
    mix, s_gla, s_delta, tail = _odd_mixer_prompt(z, p, batch, seq, 256)
    y = _moe(mix, x, p, g_final, 512)
    return y, s_gla, s_delta, tail[:, 8 - (CONV_W - 1):]


def _even_epilogue_sample_kernel(z_ref, c_ref, su_ref, sd_ref, lng_ref, lnb_ref, w0_ref, b0_ref,
                                 q_ref, kv_ref, gv_ref, bout_ref):
    c, su, sd = c_ref[...], su_ref[...], sd_ref[...]
    q_ref[...] = _rope(z_ref[:, 0:QKV_A], c, su, sd) * (D_A ** -0.5)
    kv_ref[:, 0:QKV_A] = _rope(z_ref[:, QKV_A:2 * QKV_A], c, su, sd)
    kv_ref[:, QKV_A:2 * QKV_A] = z_ref[:, 2 * QKV_A:3 * QKV_A]
    u = _gelu(z_ref[:, 3 * QKV_A:3 * QKV_A + D_B])
    gv = _layernorm(_gelu(z_ref[:, 3 * QKV_A + D_B:3 * QKV_A + 2 * D_B]), lng_ref[...], lnb_ref[...])
    gv_ref[...] = gv
    bout_ref[...] = (u * (gv * w0_ref[...] + b0_ref[...])).astype(BF16)


def _even_epilogue_sample(z, tables, ln_g, ln_b, w_sp, b_sp):
    m = z.shape[0]
    w0 = jnp.repeat(w_sp[:, 0, 0], D_BG)[None]
    b0 = jnp.repeat(b_sp[:, 0], D_BG)[None]
    return pl.pallas_call(
        _even_epilogue_sample_kernel,
        out_shape=[jax.ShapeDtypeStruct((m, QKV_A), F32), jax.ShapeDtypeStruct((m, 2 * QKV_A), F32),
                   jax.ShapeDtypeStruct((m, D_B), F32), jax.ShapeDtypeStruct((m, D_B), BF16)],
        compiler_params=pltpu.CompilerParams(vmem_limit_bytes=VMEM_LIMIT),
        name="even_epilogue_sample",
    )(z, *tables, ln_g, ln_b, w0, b0)


def _moba_sample_kernel(pt_ref, qt_ref, knt_ref, vnt_ref, *refs):
    del pt_ref
    n_pages = len(refs) - 2
    page_refs, o_ref, s_ref = refs[:n_pages], refs[n_pages], refs[n_pages + 1]
    pages_per_block = MOBA_BLOCK // PAGE_SIZE
    nb = n_pages // pages_per_block
    qt = qt_ref[0]
    for h in range(H_A):
        qcol = jnp.broadcast_to(qt[:, h:h + 1], (D_A, PAGE_SIZE))
        for j in range(n_pages):
            s_ref[h, j:j + 1, :] = jnp.sum(page_refs[j][0, 0, h] * qcol, axis=0, keepdims=True)
    lane = lax.broadcasted_iota(jnp.int32, (n_pages, LANE), 1)
    page_sums = jnp.zeros((n_pages, LANE), F32)
    for h in range(H_A):
        page_sums = jnp.where(lane == h, jnp.sum(s_ref[h], axis=1, keepdims=True), page_sums)
    pair = (lax.broadcasted_iota(jnp.int32, (nb, n_pages), 1) // pages_per_block
            == lax.broadcasted_iota(jnp.int32, (nb, n_pages), 0)).astype(F32)
    pair_t = (lax.broadcasted_iota(jnp.int32, (n_pages, nb), 0) // pages_per_block
              == lax.broadcasted_iota(jnp.int32, (n_pages, nb), 1)).astype(F32)
    gate = jnp.dot(pair, page_sums, precision=HI, preferred_element_type=F32)
    sel = _moba_select(gate, nb)
    sel_pages = jnp.dot(pair_t, sel, precision=HI, preferred_element_type=F32)
    own = jnp.sum(qt * knt_ref[0], axis=0, keepdims=True)
    vnt = vnt_ref[0]
    out_lane = lax.broadcasted_iota(jnp.int32, (D_A, LANE), 1)
    out = jnp.zeros((D_A, LANE), F32)
    for h in range(H_A):
        sm = jnp.where(sel_pages[:, h:h + 1] > 0.0, s_ref[h], NEG)
        s_own = own[:, h:h + 1]
        mx = jnp.maximum(jnp.max(jnp.max(sm, axis=1, keepdims=True), axis=0, keepdims=True), s_own)
        p = jnp.exp(sm - mx)
        p_own = jnp.exp(s_own - mx)
        denom = jnp.sum(jnp.sum(p, axis=1, keepdims=True), axis=0, keepdims=True) + p_own
        acc = jnp.zeros((D_A, PAGE_SIZE), F32)
        for j in range(n_pages):
            acc = acc + page_refs[j][0, 1, h] * p[j:j + 1, :]
        o = (jnp.sum(acc, axis=1, keepdims=True) + p_own * vnt[:, h:h + 1]) / denom
        out = jnp.where(out_lane == h, o, out)
    o_ref[0] = out


def _moba_sample(q, kv_new, cache, page_table):
    bs, n_pages = page_table.shape
    assert (n_pages * PAGE_SIZE) % MOBA_BLOCK == 0
    cache_t = jnp.transpose(cache, (0, 2, 3, 4, 1))
    page_spec = lambda j: pl.BlockSpec((1, 2, H_A, D_A, PAGE_SIZE),
                                       lambda b, pt: (pt[b * n_pages + j], 0, 0, 0, 0))
    col_spec = pl.BlockSpec((1, D_A, H_A), lambda b, pt: (b, 0, 0))
    heads_t = lambda a: jnp.transpose(a.reshape(bs, H_A, D_A), (0, 2, 1))
    out = pl.pallas_call(
        _moba_sample_kernel,
        grid_spec=pltpu.PrefetchScalarGridSpec(
            num_scalar_prefetch=1,
            grid=(bs,),
            in_specs=[col_spec, col_spec, col_spec] + [page_spec(j) for j in range(n_pages)],
            out_specs=pl.BlockSpec((1, D_A, LANE), lambda b, pt: (b, 0, 0)),
            scratch_shapes=[pltpu.VMEM((H_A, n_pages, PAGE_SIZE), F32)]),
        out_shape=jax.ShapeDtypeStruct((bs, D_A, LANE), F32),
        compiler_params=_cparams("parallel"),
        name="moba_sample",
    )(page_table.reshape(-1), heads_t(q), heads_t(kv_new[:, 0:QKV_A]), heads_t(kv_new[:, QKV_A:]),
      *([cache_t] * n_pages))
    return jnp.transpose(out[:, :, 0:H_A], (0, 2, 1)).reshape(bs, QKV_A).astype(BF16)


def _odd_mixer_sample_kernel(z_ref, buf_ref, sg_ref, sd_ref, wg2_ref, bg2_ref, gnorm_ref, cw_ref, alog_ref, dtb_ref,
                             dnorm_ref, o_ref, sgo_ref, sdo_ref):
    bt = z_ref.shape[0]
    stride = H_C * DK_C
    small = z_ref[:, O_SM:O_SM + LANE]
    pre = jnp.dot(small, wg2_ref[...], precision=HI, preferred_element_type=F32) + bg2_ref[...]
    a_all = jnp.exp(_log_sigmoid(pre) / GATE_NORM)
    g_all = -jnp.exp(alog_ref[...]) * _softplus(small + dtb_ref[...])
    beta_all = jax.nn.sigmoid(small)
    x = z_ref[:, O_DQKV:O_DQKV + C_CONV]
    y = x * cw_ref[CONV_W - 1:CONV_W, :]
    for j in range(CONV_W - 1):
        y = y + buf_ref[:, j, :] * cw_ref[j:j + 1, :]
    y = _silu(y)
    for h in range(H_C):
        ks = slice(h * DK_C, (h + 1) * DK_C)
        a = a_all[:, ks]
        q = z_ref[:, O_CQ + h * DK_C:O_CQ + (h + 1) * DK_C] * (DK_C ** -0.5)
        k = z_ref[:, O_CK + h * DK_C:O_CK + (h + 1) * DK_C]
        v = z_ref[:, O_CV + h * DV_C:O_CV + (h + 1) * DV_C]
        qa = q * a
        acc = jnp.sum(q * k, axis=1, keepdims=True) * v
        for kk in range(DK_C):
            rows = pl.ds(h * DK_C + kk, bt, stride=stride)
            srow = sg_ref[rows, :]
            acc = acc + qa[:, kk:kk + 1] * srow
            sgo_ref[rows, :] = a[:, kk:kk + 1] * srow + k[:, kk:kk + 1] * v
        gate = _silu(z_ref[:, O_CR + h * DV_C:O_CR + (h + 1) * DV_C])
        o_ref[:, h * DV_C:(h + 1) * DV_C] = (_rms(acc, gnorm_ref[...]) * gate).astype(BF16)
        yq = y[:, h * DK_D:(h + 1) * DK_D]
        yk = y[:, KEY_C + h * DK_D:KEY_C + (h + 1) * DK_D]
        dv = y[:, 2 * KEY_C + h * DV_D:2 * KEY_C + (h + 1) * DV_D]
        dq = yq * lax.rsqrt(jnp.sum(yq * yq, axis=1, keepdims=True) + EPS) * (DK_D ** -0.5)
        dk = yk * lax.rsqrt(jnp.sum(yk * yk, axis=1, keepdims=True) + EPS)
        beta = beta_all[:, SM_DB + h:SM_DB + h + 1]
        eg = jnp.exp(g_all[:, SM_DA + h:SM_DA + h + 1])
        w = dk * (beta * eg)
        qd = dq * eg
        ws = jnp.zeros((bt, DV_D), F32)
        qs = jnp.zeros((bt, DV_D), F32)
        for kk in range(DK_D):
            srow = sd_ref[pl.ds(h * DK_D + kk, bt, stride=stride), :]
            ws = ws + w[:, kk:kk + 1] * srow
            qs = qs + qd[:, kk:kk + 1] * srow
        v_new = dv * beta - ws
        o = qs + jnp.sum(dq * dk, axis=1, keepdims=True) * v_new
        for kk in range(DK_D):
            rows = pl.ds(h * DK_D + kk, bt, stride=stride)
            sdo_ref[rows, :] = sd_ref[rows, :] * eg + dk[:, kk:kk + 1] * v_new
        gate = _silu(z_ref[:, O_DG + h * DV_D:O_DG + (h + 1) * DV_D])
        col = H_C * DV_C + h * DV_D
        o_ref[:, col:col + DV_D] = (_rms(o, dnorm_ref[...]) * gate).astype(BF16)


def _odd_mixer_sample(z, conv_buf, s_gla, s_delta, p, bt):
    bs = z.shape[0]
    rows = H_C * DK_C
    full = lambda a: pl.BlockSpec(a.shape, lambda i: (0,) * a.ndim)
    consts = [p["w_g2_pad"], p["b_g2"], p["gla_norm"], p["conv_w"], p["alog_slab"], p["dtb_slab"], p["delta_norm"]]
    st_spec = pl.BlockSpec((bt * rows, DV_C), lambda i: (i, 0))
    mix, sg, sd = pl.pallas_call(
        _odd_mixer_sample_kernel,
        grid=(bs // bt,),
        in_specs=[pl.BlockSpec((bt, IN_ODD_PAD), lambda i: (i, 0)),
                  pl.BlockSpec((bt, CONV_W - 1, C_CONV), lambda i: (i, 0, 0)),
                  st_spec, st_spec] + [full(a) for a in consts],
        out_specs=[pl.BlockSpec((bt, D_MODEL), lambda i: (i, 0)), st_spec, st_spec],
        out_shape=[jax.ShapeDtypeStruct((bs, D_MODEL), BF16),
                   jax.ShapeDtypeStruct((bs * rows, DV_C), F32),
                   jax.ShapeDtypeStruct((bs * rows, DV_D), F32)],
        compiler_params=_cparams("parallel"),
        name="odd_mixer_sample",
    )(z, conv_buf, s_gla.reshape(bs * rows, DV_C), s_delta.reshape(bs * rows, DV_D), *consts)
    return mix, sg.reshape(s_gla.shape), sd.reshape(s_delta.shape)


def _sample_step(x, cache, page_table, s_gla, s_delta, conv_buf, ev, od, g_final):
    bs = x.shape[0]
    past = page_table.shape[1] * PAGE_SIZE
    z = _norm_matmul(x, ev["norm_mix"], ev["w_in"], bs)
    tables = _rope_tables(jnp.full((1,), past, jnp.int32))
    q, kv, gv, b_out = _even_epilogue_sample(z, tables, ev["ln_g"], ev["ln_b"], ev["w_sp"], ev["b_sp"])
    a_out = _moba_sample(q, kv, cache, page_table)
    x = _proj_ffn([a_out, b_out], x, ev["w_out"], ev["norm_ffn"], ev["w_gate"], ev["w_up"], ev["w_down"], bs, 256)
    z = _norm_matmul(x, od["norm_mix"], od["w_in"], bs)
    mix, sg, sd = _odd_mixer_sample(z, conv_buf, s_gla, s_delta, od, 32)
    conv_new = jnp.concatenate([conv_buf[:, 1:], z[:, None, O_DQKV:O_DQKV + C_CONV]], axis=1)
    y = _moe(mix, x, od, g_final, bs)
    return y, kv, gv, sg, sd, conv_new


def kernel(x_prompt, x_sample, cache_kv, state_gla, state_delta, state_conv, page_table, ev_norm_mix, ev_w_in, ev_gmlp_ln_g, ev_gmlp_ln_b, ev_w_spatial, ev_b_spatial, ev_w_out, ev_norm_ffn, ev_w_gate, ev_w_up, ev_w_down, od_norm_mix, od_w_in, od_gla_w_gate2, od_gla_b_gate2, od_gla_norm, od_delta_conv, od_delta_a_log, od_delta_dt_bias, od_delta_norm, od_w_out, od_norm_ffn, od_router, od_w_gate, od_w_up, od_w_down, norm_final):
    bp, tp, d = x_prompt.shape
    ev = _even_params(0, ev_norm_mix, ev_w_in, ev_gmlp_ln_g, ev_gmlp_ln_b, ev_w_spatial, ev_b_spatial, ev_w_out,
                      ev_norm_ffn, ev_w_gate, ev_w_up, ev_w_down)
    od = _odd_params(0, od_norm_mix, od_w_in, od_gla_w_gate2, od_gla_b_gate2, od_gla_norm, od_delta_conv,
                     od_delta_a_log, od_delta_dt_bias, od_delta_norm, od_w_out, od_norm_ffn, od_router,
                     od_w_gate, od_w_up, od_w_down)
    bs, ts, _ = x_sample.shape
    assert ts == 1 and cache_kv.shape[0] == 1 and state_gla.shape[0] == 1
    xp, kv_p, gv_p = _even_layer_prompt(x_prompt.reshape(bp * tp, d), ev, bp, tp)
    yp, gla_p, dl_p, cv_p = _odd_layer_prompt(xp, od, norm_final[None], bp, tp)
    ys, kv_s, gv_s, gla_s, dl_s, cv_s = _sample_step(
        x_sample.reshape(bs, d), cache_kv[0], page_table, state_gla[0], state_delta[0], state_conv[0],
        ev, od, norm_final[None])
    return (yp.reshape(bp, tp, d), ys.reshape(bs, ts, d),
            kv_p[None], kv_s.reshape(1, bs, ts, 2, H_A, D_A),
            gv_p[None], gv_s.reshape(1, bs, ts, D_B),
            gla_p[None], gla_s[None], dl_p[None], dl_s[None], cv_p[None], cv_s[None])
```

```python
import functools
import math

import jax
import jax.numpy as jnp
from jax import lax
from jax.experimental import pallas as pl
from jax.experimental.pallas import tpu as pltpu

F32 = jnp.float32
BF16 = jnp.bfloat16
HI = lax.Precision.HIGHEST
EPS = 1e-6
NEG = -1e30

D_MODEL = 1024
PAGE_SIZE = 128
H_A, D_A = 8, 64
ROT_DIM = D_A // 4
ROPE_THETA = 500000.0
MOBA_BLOCK = 256
MOBA_TOPK = 3
G_B, D_BG = 8, 64
D_B = G_B * D_BG
CHUNK_B = 128
H_C, DK_C, DV_C = 4, 64, 128
GATE_RANK = 16
GATE_NORM = 16.0
H_D, DK_D, DV_D = 4, 64, 128
CONV_W = 4
C_CONV = 2 * H_D * DK_D + H_D * DV_D
CHUNK_LIN = 64
N_EXPERTS = 8
TOP_K = 2
QKV_A = H_A * D_A
IN_EVEN = 3 * QKV_A + 2 * D_B
FFN_TF = 1408
MOE_TF = 896
MOBA_LOOP_BLOCKS = 4
DMA_UNROLL = 8
LANE = 128
VMEM_LIMIT = 56 * 1024 * 1024


def _cparams(*sem):
    return pltpu.CompilerParams(dimension_semantics=sem, vmem_limit_bytes=VMEM_LIMIT)


def _rms(x, g):
    return x * lax.rsqrt(jnp.mean(x * x, axis=-1, keepdims=True) + EPS) * g


def _gelu(x):
    return 0.5 * x * (1.0 + lax.erf(x * (2.0 ** -0.5)))


def _silu(x):
    return x * jax.nn.sigmoid(x)


def _softplus(x):
    return jnp.maximum(x, 0.0) + jnp.log1p(jnp.exp(-jnp.abs(x)))


def _dot(a, b):
    return jnp.dot(a, b, preferred_element_type=F32)


def _dot_nt(a, b):
    return lax.dot_general(a, b, (((1,), (1,)), ((), ())), preferred_element_type=F32)


def _dot_tn(a, b):
    return lax.dot_general(a, b, (((0,), (0,)), ((), ())), preferred_element_type=F32)


def _norm_matmul_kernel(x_ref, g_ref, w_ref, o_ref):
    h = _rms(x_ref[...], g_ref[...]).astype(BF16)
    o_ref[...] = _dot(h, w_ref[...])


def _norm_matmul(x, g, w, tm):
    m, d = x.shape
    n = w.shape[1]
    return pl.pallas_call(
        _norm_matmul_kernel,
        grid=(m // tm,),
        in_specs=[pl.BlockSpec((tm, d), lambda i: (i, 0)),
                  pl.BlockSpec((1, d), lambda i: (0, 0)),
                  pl.BlockSpec((d, n), lambda i: (0, 0))],
        out_specs=pl.BlockSpec((tm, n), lambda i: (i, 0)),
        out_shape=jax.ShapeDtypeStruct((m, n), F32),
        compiler_params=_cparams("parallel"),
        name="norm_matmul",
    )(x, g, w)


def _rope_tables(pos):
    half = ROT_DIM // 2
    inv = ROPE_THETA ** (-jnp.arange(half, dtype=F32) / half)
    ang = pos.astype(F32)[:, None] * inv[None, :]
    cos, sin = jnp.cos(ang), jnp.sin(ang)
    t = pos.shape[0]
    one = jnp.ones((t, D_A - ROT_DIM), F32)
    zero_h = jnp.zeros((t, half), F32)
    zero_r = jnp.zeros((t, D_A - ROT_DIM), F32)
    c = jnp.concatenate([cos, cos, one], axis=1)
    s_up = jnp.concatenate([-sin, zero_h, zero_r], axis=1)
    s_dn = jnp.concatenate([zero_h, sin, zero_r], axis=1)
    rep = LANE // D_A
    return jnp.tile(c, (1, rep)), jnp.tile(s_up, (1, rep)), jnp.tile(s_dn, (1, rep))


def _rope(x, c, s_up, s_dn):
    half = ROT_DIM // 2
    outs = []
    for j in range(x.shape[1] // LANE):
        xs = x[:, j * LANE:(j + 1) * LANE]
        up = pltpu.roll(xs, LANE - half, 1)
        dn = pltpu.roll(xs, half, 1)
        outs.append(xs * c + up * s_up + dn * s_dn)
    return jnp.concatenate(outs, axis=1)


def _layernorm(x, g, b):
    mu = jnp.mean(x, axis=-1, keepdims=True)
    xc = x - mu
    var = jnp.mean(xc * xc, axis=-1, keepdims=True)
    return xc * lax.rsqrt(var + EPS) * g + b


def _even_epilogue_kernel(z_ref, c_ref, su_ref, sd_ref, lng_ref, lnb_ref, wsp_ref, bspt_ref,
                          qt_ref, k_ref, vt_ref, kv_ref, kmean_ref, gv_ref, bout_ref):
    tm = z_ref.shape[0]
    c, su, sd = c_ref[...], su_ref[...], sd_ref[...]
    q = _rope(z_ref[:, 0:QKV_A], c, su, sd) * (D_A ** -0.5)
    k = _rope(z_ref[:, QKV_A:2 * QKV_A], c, su, sd)
    v = z_ref[:, 2 * QKV_A:3 * QKV_A]
    qt_ref[0] = q.T.astype(BF16)
    k_ref[...] = k.astype(BF16)
    vt = v.T
    vt_ref[0] = vt.astype(BF16)
    kv_ref[0, 0:QKV_A, :] = k.T
    kv_ref[0, QKV_A:2 * QKV_A, :] = vt
    for blk in range(tm // MOBA_BLOCK):
        kmean_ref[blk] = jnp.mean(k[blk * MOBA_BLOCK:(blk + 1) * MOBA_BLOCK], axis=0, keepdims=True)
    u = _gelu(z_ref[:, 3 * QKV_A:3 * QKV_A + D_B])
    gv = _layernorm(_gelu(z_ref[:, 3 * QKV_A + D_B:3 * QKV_A + 2 * D_B]), lng_ref[...], lnb_ref[...])
    gv_ref[...] = gv
    gvb = gv.astype(BF16)
    row = lax.broadcasted_iota(jnp.int32, (CHUNK_B, CHUNK_B), 0)
    col = lax.broadcasted_iota(jnp.int32, (CHUNK_B, CHUNK_B), 1)
    group = lax.broadcasted_iota(jnp.int32, (CHUNK_B, D_B), 1) // D_BG
    w = [jnp.where(row >= col, wsp_ref[g], 0.0).astype(BF16) for g in range(G_B)]
    for ch in range(tm // CHUNK_B):
        gvc = gvb[ch * CHUNK_B:(ch + 1) * CHUNK_B]
        mixed = jnp.zeros((CHUNK_B, D_B), F32)
        for g in range(G_B):
            mixed = jnp.where(group == g, _dot(w[g], gvc) + bspt_ref[:, g:g + 1], mixed)
        bout_ref[ch * CHUNK_B:(ch + 1) * CHUNK_B, :] = (u[ch * CHUNK_B:(ch + 1) * CHUNK_B] * mixed).astype(BF16)


def _even_epilogue(z, tables, ln_g, ln_b, w_sp, b_sp, batch, seq, tm):
    n = batch * seq
    nt = seq // tm
    nblk = tm // MOBA_BLOCK
    tab_spec = pl.BlockSpec((tm, LANE), lambda b, i: (i, 0))
    row_spec = lambda width: pl.BlockSpec((tm, width), lambda b, i: (b * nt + i, 0))
    t_spec = pl.BlockSpec((1, QKV_A, tm), lambda b, i: (b, 0, i))
    return pl.pallas_call(
        _even_epilogue_kernel,
        grid=(batch, nt),
        in_specs=[row_spec(IN_EVEN), tab_spec, tab_spec, tab_spec,
                  pl.BlockSpec((1, D_B), lambda b, i: (0, 0)),
                  pl.BlockSpec((1, D_B), lambda b, i: (0, 0)),
                  pl.BlockSpec((G_B, CHUNK_B, CHUNK_B), lambda b, i: (0, 0, 0)),
                  pl.BlockSpec((CHUNK_B, G_B), lambda b, i: (0, 0))],
        out_specs=[t_spec, row_spec(QKV_A), t_spec, pl.BlockSpec((1, 2 * QKV_A, tm), lambda b, i: (b, 0, i)),
                   pl.BlockSpec((nblk, 1, QKV_A), lambda b, i: (b * nt + i, 0, 0)),
                   row_spec(D_B), row_spec(D_B)],
        out_shape=[jax.ShapeDtypeStruct((batch, QKV_A, seq), BF16),
                   jax.ShapeDtypeStruct((n, QKV_A), BF16),
                   jax.ShapeDtypeStruct((batch, QKV_A, seq), BF16),
                   jax.ShapeDtypeStruct((batch, 2 * QKV_A, seq), F32),
                   jax.ShapeDtypeStruct((n // MOBA_BLOCK, 1, QKV_A), F32),
                   jax.ShapeDtypeStruct((n, D_B), F32),
                   jax.ShapeDtypeStruct((n, D_B), BF16)],
        compiler_params=_cparams("parallel", "parallel"),
        name="even_epilogue",
    )(z, *tables, ln_g, ln_b, w_sp, b_sp.T)


def _moba_select(gate, n_own):
    nb = gate.shape[0]
    blk = lax.broadcasted_iota(jnp.int32, gate.shape, 0)
    elig = blk < n_own
    gm = jnp.where(elig, gate, NEG)
    rank = jnp.zeros(gate.shape, F32)
    for m in range(nb):
        gm_m = gm[m:m + 1, :]
        ahead = (gm_m > gm) | ((gm_m == gm) & (m < blk))
        rank = rank + ahead.astype(F32)
    return (elig & (rank < MOBA_TOPK)).astype(F32)


def _moba_prompt_kernel(qt_ref, k_ref, vt_ref, kmean_ref, o_ref, bias_ref):
    i = pl.program_id(2)
    tq = MOBA_BLOCK
    n_head = LANE // D_A
    n_split = tq // LANE
    own = pl.multiple_of(i * tq, tq)
    key_i = lax.broadcasted_iota(jnp.int32, (tq, LANE), 0)
    qry_i = lax.broadcasted_iota(jnp.int32, (tq, LANE), 1)
    head_lanes = [slice(hh * D_A, (hh + 1) * D_A) for hh in range(n_head)]
    streams = [(hh, qh) for hh in range(n_head) for qh in range(n_split)]
    qts = [qt_ref[0, head_lanes[hh], :] for hh in range(n_head)]
    queries = [qts[hh][:, qh * LANE:(qh + 1) * LANE] for hh, qh in streams]
    k_own = [k_ref[pl.ds(own, tq), head_lanes[hh]] for hh in range(n_head)]
    own_scores = [_dot(k_own[hh], queries[idx]) for idx, (hh, qh) in enumerate(streams)]
    gates = [jnp.dot(kmean_ref[0, :, head_lanes[hh]], qts[hh].astype(F32), precision=HI, preferred_element_type=F32)
             for hh in range(n_head)]
    for hh in range(n_head):
        bias = jnp.where(_moba_select(gates[hh], i) > 0.0, 0.0, NEG)
        for qh in range(n_split):
            bias_ref[hh, qh] = bias[:, qh * LANE:(qh + 1) * LANE]
    own_probs, own_stats = [], []
    for idx, (hh, qh) in enumerate(streams):
        s = jnp.where(key_i <= qry_i + qh * LANE, own_scores[idx], NEG)
        m = jnp.max(s, axis=0, keepdims=True)
        p = jnp.exp(s - m)
        own_probs.append(p.astype(BF16))
        own_stats.append((m, jnp.sum(p, axis=0, keepdims=True)))
    vt_own = [vt_ref[0, head_lanes[hh], pl.ds(own, tq)] for hh in range(n_head)]
    state = []
    for idx, (hh, qh) in enumerate(streams):
        state += [own_stats[idx][0], own_stats[idx][1], _dot(vt_own[hh], own_probs[idx])]

    def body(j0, carry, nk):
        tk = nk * tq
        start = pl.multiple_of(j0 * tq, tq)
        k_j = [k_ref[pl.ds(start, tk), head_lanes[hh]] for hh in range(n_head)]
        scores = [_dot(k_j[hh], queries[hh * n_split + qh]) for hh, qh in streams]
        probs, stats = [], []
        for idx, (hh, qh) in enumerate(streams):
            m, l = carry[3 * idx], carry[3 * idx + 1]
            s = [scores[idx][b * tq:(b + 1) * tq] + bias_ref[hh, qh, pl.ds(j0 + b, 1), :] for b in range(nk)]
            m_new = m
            for sb in s:
                m_new = jnp.maximum(m_new, jnp.max(sb, axis=0, keepdims=True))
            alpha = jnp.exp(m - m_new)
            p = [jnp.exp(sb - m_new) for sb in s]
            l_new = l * alpha
            for pb in p:
                l_new = l_new + jnp.sum(pb, axis=0, keepdims=True)
            probs.append(jnp.concatenate([pb.astype(BF16) for pb in p], axis=0))
            stats.append((m_new, l_new, alpha))
        vt_j = [vt_ref[0, head_lanes[hh], pl.ds(start, tk)] for hh in range(n_head)]
        pv = [_dot(vt_j[hh], probs[idx]) for idx, (hh, qh) in enumerate(streams)]
        out = []
        for idx in range(len(streams)):
            m_new, l_new, alpha = stats[idx]
            out += [m_new, l_new, carry[3 * idx + 2] * alpha + pv[idx]]
        return tuple(out)

    n_group = i // MOBA_LOOP_BLOCKS
    state = lax.fori_loop(0, n_group, lambda t, c: body(t * MOBA_LOOP_BLOCKS, c, MOBA_LOOP_BLOCKS), tuple(state))
    state = lax.fori_loop(n_group * MOBA_LOOP_BLOCKS, i, lambda j, c: body(j, c, 1), state)
    outs = []
    for hh in range(n_head):
        halves = [state[3 * (hh * n_split + qh) + 2] / state[3 * (hh * n_split + qh) + 1] for qh in range(n_split)]
        outs.append(jnp.concatenate(halves, axis=1))
    o_ref[...] = jnp.concatenate(outs, axis=0).T.astype(BF16)


def _moba_prompt(qt, k, vt, kmean, batch, seq):
    nq = seq // MOBA_BLOCK
    hp = QKV_A // LANE
    return pl.pallas_call(
        _moba_prompt_kernel,
        grid=(batch, hp, nq),
        in_specs=[pl.BlockSpec((1, LANE, MOBA_BLOCK), lambda b, h, i: (b, h, i)),
                  pl.BlockSpec((seq, LANE), lambda b, h, i: (b, h)),
                  pl.BlockSpec((1, LANE, seq), lambda b, h, i: (b, h, 0)),
                  pl.BlockSpec((1, nq, LANE), lambda b, h, i: (b, 0, h))],
        out_specs=pl.BlockSpec((MOBA_BLOCK, LANE), lambda b, h, i: (b * nq + i, h)),
        out_shape=jax.ShapeDtypeStruct((batch * seq, QKV_A), BF16),
        scratch_shapes=[pltpu.VMEM((LANE // D_A, MOBA_BLOCK // LANE, nq, LANE), F32)],
        compiler_params=_cparams("parallel", "parallel", "arbitrary"),
        name="moba_prompt",
    )(qt, k, vt, kmean)


def _proj_ffn_kernel(*refs, n_mix):
    mix_refs = refs[:n_mix]
    x_ref, wo_ref, g_ref, wg_ref, wu_ref, wd_ref, o_ref, hn_ref = refs[n_mix:]

    @pl.when(pl.program_id(1) == 0)
    def _():
        x1 = x_ref[...]
        off = 0
        for r in mix_refs:
            width = r.shape[1]
            x1 = x1 + _dot(r[...], wo_ref[off:off + width, :])
            off += width
        o_ref[...] = x1
        hn_ref[...] = _rms(x1, g_ref[...]).astype(BF16)

    h = hn_ref[...]
    act = (_silu(_dot(h, wg_ref[...])) * _dot(h, wu_ref[...])).astype(BF16)
    o_ref[...] += _dot(act, wd_ref[...])


def _col_blocks(w, tf):
    *lead, d, ff = w.shape
    n = len(lead)
    return jnp.transpose(w.reshape(*lead, d, ff // tf, tf), (*range(n), n + 1, n, n + 2))


def _proj_ffn(mixes, x, w_out, g, w_gate, w_up, w_down, tm):
    m, d = x.shape
    nf, _, tf = w_gate.shape
    mix_specs = [pl.BlockSpec((tm, a.shape[1]), lambda i, f: (i, 0)) for a in mixes]
    return pl.pallas_call(
        functools.partial(_proj_ffn_kernel, n_mix=len(mixes)),
        grid=(m // tm, nf),
        in_specs=mix_specs + [
            pl.BlockSpec((tm, d), lambda i, f: (i, 0)),
            pl.BlockSpec(w_out.shape, lambda i, f: (0, 0)),
            pl.BlockSpec((1, d), lambda i, f: (0, 0)),
            pl.BlockSpec((None, d, tf), lambda i, f: (f, 0, 0)),
            pl.BlockSpec((None, d, tf), lambda i, f: (f, 0, 0)),
            pl.BlockSpec((tf, d), lambda i, f: (f, 0))],
        out_specs=pl.BlockSpec((tm, d), lambda i, f: (i, 0)),
        out_shape=jax.ShapeDtypeStruct((m, d), F32),
        scratch_shapes=[pltpu.VMEM((tm, d), BF16)],
        compiler_params=_cparams("parallel", "arbitrary"),
        name="proj_ffn",
    )(*mixes, x, w_out, g, w_gate, w_up, w_down)


def _even_layer_prompt(x, p, batch, seq):
    z = _norm_matmul(x, p["norm_mix"], p["w_in"], 512)
    tables = _rope_tables(jnp.arange(seq, dtype=jnp.int32))
    qt, k, vt, kv, kmean, gv, b_out = _even_epilogue(
        z, tables, p["ln_g"], p["ln_b"], p["w_sp"], p["b_sp"], batch, seq, 256)
    a_out = _moba_prompt(qt, k, vt, kmean.reshape(batch, seq // MOBA_BLOCK, QKV_A), batch, seq)
    x = _proj_ffn([a_out, b_out], x, p["w_out"], p["norm_ffn"], p["w_gate"], p["w_up"], p["w_down"], 512)
    n_open = seq - ((seq - 1) // CHUNK_B) * CHUNK_B
    gv_open = gv.reshape(batch, seq, D_B)[:, seq - n_open:]
    kv = jnp.transpose(kv.reshape(batch, 2, H_A, D_A, seq), (0, 4, 1, 2, 3))
    return x, kv, gv_open


def _even_params(i, ev_norm_mix, ev_w_in, ev_gmlp_ln_g, ev_gmlp_ln_b, ev_w_spatial, ev_b_spatial, ev_w_out,
                 ev_norm_ffn, ev_w_gate, ev_w_up, ev_w_down):
    return dict(norm_mix=ev_norm_mix[i][None], w_in=ev_w_in[i].astype(BF16),
                ln_g=ev_gmlp_ln_g[i][None], ln_b=ev_gmlp_ln_b[i][None],
                w_sp=ev_w_spatial[i], b_sp=ev_b_spatial[i], w_out=ev_w_out[i].astype(BF16),
                norm_ffn=ev_norm_ffn[i][None], w_gate=_col_blocks(ev_w_gate[i].astype(BF16), FFN_TF),
                w_up=_col_blocks(ev_w_up[i].astype(BF16), FFN_TF), w_down=ev_w_down[i].astype(BF16))


O_CQ, O_CK, O_CV, O_CR, O_DQKV, O_DG, O_SM = 0, 256, 512, 1024, 1536, 2560, 3072
SM_DA, SM_DB = GATE_RANK, GATE_RANK + H_D
IN_ODD_PAD = O_SM + LANE
KEY_C = H_C * DK_C
PAIR = LANE // DK_C


def _log_sigmoid(x):
    return jnp.minimum(x, 0.0) - jnp.log1p(jnp.exp(-jnp.abs(x)))


def _split_bf16(a, terms):
    parts = []
    for _ in range(terms - 1):
        hi = a.astype(BF16)
        parts.append(hi)
        a = a - hi.astype(F32)
    parts.append(a.astype(BF16))
    return parts


def _dot_select(a, sel_bf16, terms):
    parts = _split_bf16(a, terms)
    out = _dot(parts[0], sel_bf16)
    for part in parts[1:]:
        out = out + _dot(part, sel_bf16)
    return out


def _select_dot(sel_bf16, a, terms):
    parts = _split_bf16(a, terms)
    out = _dot(sel_bf16, parts[0])
    for part in parts[1:]:
        out = out + _dot(sel_bf16, part)
    return out


def _unit_lower_inverses(mats):
    c = mats[0].shape[0]
    eye = (lax.broadcasted_iota(jnp.int32, (c, c), 0) == lax.broadcasted_iota(jnp.int32, (c, c), 1)).astype(F32)
    xs = [(-a).astype(BF16) for a in mats]
    ps = [eye - a for a in mats]
    for _ in range(int(math.log2(c)) - 1):
        xs = [_dot(x, x).astype(BF16) for x in xs]
        ps = [p + _dot(p.astype(BF16), x) for p, x in zip(ps, xs)]
    return ps


def _group_sumsq(y, width):
    n = y.shape[1]
    same = (lax.broadcasted_iota(jnp.int32, (n, n), 0) // width
            == lax.broadcasted_iota(jnp.int32, (n, n), 1) // width).astype(BF16)
    return _dot_select(y * y, same, 2)


def _odd_mixer_prompt_kernel(z_ref, wg2_ref, bg2_ref, gnorm_ref, cw_ref, alog_ref, dtb_ref, dnorm_ref,
                             o_ref, sg_ref, sd_ref, tail_ref,
                             stg_ref, std_ref, prev_ref, la_ref, qkv_ref, dla_ref, beta_ref):
    i = pl.program_id(1)
    tc = z_ref.shape[0]
    c = CHUNK_LIN

    @pl.when(i == 0)
    def _():
        stg_ref[...] = jnp.zeros_like(stg_ref)
        std_ref[...] = jnp.zeros_like(std_ref)
        prev_ref[...] = jnp.zeros_like(prev_ref)

    small = z_ref[:, O_SM:O_SM + LANE]
    pre = jnp.dot(small, wg2_ref[...], precision=HI, preferred_element_type=F32) + bg2_ref[...]
    la_ref[...] = _log_sigmoid(pre) / GATE_NORM
    dla_ref[...] = -jnp.exp(alog_ref[...]) * _softplus(small + dtb_ref[...])
    beta_ref[...] = jax.nn.sigmoid(small)

    x = z_ref[:, O_DQKV:O_DQKV + C_CONV]
    x8 = x[0:8]
    p8 = prev_ref[...]
    row8 = lax.broadcasted_iota(jnp.int32, (8, C_CONV), 0)
    y = x * cw_ref[CONV_W - 1:CONV_W, :]
    y8 = x8 * cw_ref[CONV_W - 1:CONV_W, :]
    for s in range(1, CONV_W):
        wrow = cw_ref[CONV_W - 1 - s:CONV_W - s, :]
        y = y + pltpu.roll(x, s, 0) * wrow
        y8 = y8 + jnp.where(row8 < s, pltpu.roll(p8, s, 0), pltpu.roll(x8, s, 0)) * wrow
    prev_ref[...] = x[tc - 8:tc]
    y = _silu(y)
    y8 = _silu(y8)
    yqk = y[:, 0:2 * KEY_C]
    nrm = lax.rsqrt(_group_sumsq(yqk, DK_D) + EPS)
    qscale = jnp.where(lax.broadcasted_iota(jnp.int32, (1, 2 * KEY_C), 1) < KEY_C, DK_D ** -0.5, 1.0)
    qkv_ref[:, 0:2 * KEY_C] = yqk * nrm * qscale
    qkv_ref[:, 2 * KEY_C:] = y[:, 2 * KEY_C:]
    yqk8 = y8[:, 0:2 * KEY_C]
    qkv_ref[0:8, 0:2 * KEY_C] = yqk8 * lax.rsqrt(_group_sumsq(yqk8, DK_D) + EPS) * qscale
    qkv_ref[0:8, 2 * KEY_C:] = y8[:, 2 * KEY_C:]

    ri = lax.broadcasted_iota(jnp.int32, (c, c), 0)
    ci = lax.broadcasted_iota(jnp.int32, (c, c), 1)
    tril = ri >= ci
    strict = ri > ci
    lower = tril.astype(BF16)
    upper = (ri <= ci).astype(BF16)
    lane_head = lax.broadcasted_iota(jnp.int32, (c, LANE), 1) // DK_C
    lane_head_row = lax.broadcasted_iota(jnp.int32, (1, LANE), 1) // DK_C

    def chunk(ch, carry):
        rows = pl.ds(pl.multiple_of(ch * c, c), c)
        bcum_all = _select_dot(lower, la_ref[rows, :], 3)
        for p in range(H_C // PAIR):
            bcum = bcum_all[:, p * LANE:(p + 1) * LANE]
            b_end = bcum[c - 1:c, :]
            q_in = z_ref[rows, O_CQ + p * LANE:O_CQ + (p + 1) * LANE] * (DK_C ** -0.5) * jnp.exp(bcum)
            k = z_ref[rows, O_CK + p * LANE:O_CK + (p + 1) * LANE]
            k_in = (k * jnp.exp(-bcum)).astype(BF16)
            k_end = k * jnp.exp(b_end - bcum)
            st = stg_ref[p]
            stb = st.astype(BF16)
            upd = jnp.zeros((DV_C, LANE), F32)
            for hh in range(PAIR):
                h = p * PAIR + hh
                mask = lane_head == hh
                qm = jnp.where(mask, q_in, 0.0).astype(BF16)
                sc = jnp.where(tril, _dot_nt(qm, k_in), 0.0)
                vb = z_ref[rows, O_CV + h * DV_C:O_CV + (h + 1) * DV_C].astype(BF16)
                o = _dot(sc.astype(BF16), vb) + _dot_nt(qm, stb)
                upd = upd + _dot_tn(vb, jnp.where(mask, k_end, 0.0).astype(BF16))
                gate = _silu(z_ref[rows, O_CR + h * DV_C:O_CR + (h + 1) * DV_C])
                o_ref[rows, h * DV_C:(h + 1) * DV_C] = (_rms(o, gnorm_ref[...]) * gate).astype(BF16)
            stg_ref[p] = st * jnp.exp(b_end) + upd
        dla_parts = _split_bf16(dla_ref[rows, :], 3)
        g_cols = _dot(lower, dla_parts[0]) + _dot(lower, dla_parts[1]) + _dot(lower, dla_parts[2])
        g_rows = _dot_tn(dla_parts[0], upper) + _dot_tn(dla_parts[1], upper) + _dot_tn(dla_parts[2], upper)
        heads = []
        for h in range(H_D):
            p, hh = divmod(h, PAIR)
            mask = lane_head == hh
            g_col = jnp.broadcast_to(g_cols[:, SM_DA + h:SM_DA + h + 1], (c, LANE))
            g_row = jnp.broadcast_to(g_rows[SM_DA + h:SM_DA + h + 1, :], (c, c))
            beta_b = jnp.broadcast_to(beta_ref[rows, SM_DB + h:SM_DB + h + 1], (c, LANE))
            decay = jnp.exp(jnp.where(tril, g_col[:, 0:c] - g_row, NEG))
            km = jnp.where(mask, qkv_ref[rows, KEY_C + p * LANE:KEY_C + (p + 1) * LANE], 0.0)
            qm = jnp.where(mask, qkv_ref[rows, p * LANE:(p + 1) * LANE], 0.0)
            kmb = km.astype(BF16)
            kb = km * beta_b
            heads.append(dict(
                g_col=g_col, decay=decay, km=km, qm=qm, kmb=kmb, kb=kb, beta_b=beta_b,
                a_mat=jnp.where(strict, _dot_nt(kb.astype(BF16), kmb) * decay, 0.0)))
        t_invs = _unit_lower_inverses([hd["a_mat"] for hd in heads])
        for p in range(H_D // PAIR):
            st = std_ref[p]
            stb = st.astype(BF16)
            upd = jnp.zeros((DV_D, LANE), F32)
            dec_row = jnp.zeros((1, LANE), F32)
            for hh in range(PAIR):
                h = p * PAIR + hh
                hd = heads[h]
                g_col, qm, km = hd["g_col"], hd["qm"], hd["km"]
                eg = jnp.exp(g_col)
                g_last = g_col[c - 1:c, :]
                v = qkv_ref[rows, 2 * KEY_C + h * DV_D:2 * KEY_C + (h + 1) * DV_D]
                rhs = jnp.concatenate([v * hd["beta_b"], hd["kb"] * eg], axis=1).astype(BF16)
                sol = _dot(t_invs[h].astype(BF16), rhs)
                u = sol[:, 0:DV_D]
                w = sol[:, DV_D:]
                qk = jnp.where(tril, _dot_nt(qm.astype(BF16), hd["kmb"]) * hd["decay"], 0.0)
                v_new = u - _dot_nt(w.astype(BF16), stb)
                vnb = v_new.astype(BF16)
                o = _dot_nt((qm * eg).astype(BF16), stb) + _dot(qk.astype(BF16), vnb)
                upd = upd + _dot_tn(vnb, (km * jnp.exp(g_last - g_col)).astype(BF16))
                dec_row = jnp.where(lane_head_row == hh, jnp.exp(g_last), dec_row)
                gate = _silu(z_ref[rows, O_DG + h * DV_D:O_DG + (h + 1) * DV_D])
                col = H_C * DV_C + h * DV_D
                o_ref[rows, col:col + DV_D] = (_rms(o, dnorm_ref[...]) * gate).astype(BF16)
            std_ref[p] = st * dec_row + upd
        return carry

    lax.fori_loop(0, tc // c, chunk, 0)

    @pl.when(i == pl.num_programs(1) - 1)
    def _():
        tail_ref[0] = prev_ref[...]
        for p in range(H_C // PAIR):
            tg = stg_ref[p].T
            td = std_ref[p].T
            for hh in range(PAIR):
                sg_ref[0, p * PAIR + hh] = tg[hh * DK_C:(hh + 1) * DK_C, :]
                sd_ref[0, p * PAIR + hh] = td[hh * DK_D:(hh + 1) * DK_D, :]


def _odd_mixer_prompt(z, p, batch, seq, tc):
    nt = seq // tc
    full = lambda a: pl.BlockSpec(a.shape, lambda b, i: (0,) * a.ndim)
    consts = [p["w_g2_pad"], p["b_g2"], p["gla_norm"], p["conv_w"], p["alog_slab"], p["dtb_slab"], p["delta_norm"]]
    st_spec = pl.BlockSpec((1, H_C, DK_C, DV_C), lambda b, i: (b, 0, 0, 0))
    return pl.pallas_call(
        _odd_mixer_prompt_kernel,
        grid=(batch, nt),
        in_specs=[pl.BlockSpec((tc, IN_ODD_PAD), lambda b, i: (b * nt + i, 0))] + [full(a) for a in consts],
        out_specs=[pl.BlockSpec((tc, D_MODEL), lambda b, i: (b * nt + i, 0)), st_spec, st_spec,
                   pl.BlockSpec((1, 8, C_CONV), lambda b, i: (b, 0, 0))],
        out_shape=[jax.ShapeDtypeStruct((batch * seq, D_MODEL), BF16),
                   jax.ShapeDtypeStruct((batch, H_C, DK_C, DV_C), F32),
                   jax.ShapeDtypeStruct((batch, H_D, DK_D, DV_D), F32),
                   jax.ShapeDtypeStruct((batch, 8, C_CONV), F32)],
        scratch_shapes=[pltpu.VMEM((H_C // PAIR, DV_C, LANE), F32),
                        pltpu.VMEM((H_D // PAIR, DV_D, LANE), F32),
                        pltpu.VMEM((8, C_CONV), F32),
                        pltpu.VMEM((tc, KEY_C), F32),
                        pltpu.VMEM((tc, C_CONV), F32),
                        pltpu.VMEM((tc, LANE), F32),
                        pltpu.VMEM((tc, LANE), F32)],
        compiler_params=_cparams("parallel", "arbitrary"),
        name="odd_mixer_prompt",
    )(z, *consts)


M_E1, M_E2, M_R1, M_R2, M_G1, M_G2 = (N_EXPERTS + j for j in range(6))


def _proj_router_kernel(mix_ref, x_ref, wo_ref, g_ref, wr_ref, x1_ref, hn_ref, meta_ref, cnt_ref, carry_ref):
    tm = x_ref.shape[0]

    @pl.when(pl.program_id(0) == 0)
    def _():
        carry_ref[...] = jnp.zeros_like(carry_ref)

    x1 = x_ref[...] + _dot(mix_ref[...], wo_ref[...])
    x1_ref[...] = x1
    hn = _rms(x1, g_ref[...])
    hn_ref[...] = hn.astype(hn_ref.dtype)
    lane = lax.broadcasted_iota(jnp.int32, (tm, LANE), 1)
    hn_hi, hn_lo = _split_bf16(hn, 2)
    wr_hi, wr_lo = _split_bf16(wr_ref[...], 2)
    logits = _dot(hn_hi, wr_hi) + _dot(hn_hi, wr_lo) + _dot(hn_lo, wr_hi)
    logits = jnp.where(lane < N_EXPERTS, logits, NEG)
    m1 = jnp.max(logits, axis=1, keepdims=True)
    e1 = jnp.min(jnp.where(logits == m1, lane, LANE), axis=1, keepdims=True)
    rest = jnp.where(lane == e1, NEG, logits)
    m2 = jnp.max(rest, axis=1, keepdims=True)
    e2 = jnp.min(jnp.where(rest == m2, lane, LANE), axis=1, keepdims=True)
    t = jnp.exp(m2 - m1)
    g1 = 1.0 / (1.0 + t)
    g2 = t / (1.0 + t)
    oh1 = lane == e1
    oh2 = lane == e2
    member = (oh1 | oh2).astype(F32)
    ri = lax.broadcasted_iota(jnp.int32, (tm, tm), 0)
    ci = lax.broadcasted_iota(jnp.int32, (tm, tm), 1)
    before = _dot((ri > ci).astype(BF16), member.astype(BF16)) + carry_ref[...]
    r1 = jnp.sum(jnp.where(oh1, before, 0.0), axis=1, keepdims=True)
    r2 = jnp.sum(jnp.where(oh2, before, 0.0), axis=1, keepdims=True)
    carry_ref[...] = carry_ref[...] + jnp.sum(member, axis=0, keepdims=True)
    cnt_ref[...] = carry_ref[...]
    meta = jnp.where(oh1, g1, 0.0) + jnp.where(oh2, g2, 0.0)
    meta = jnp.where(lane == M_E1, e1.astype(F32), meta)
    meta = jnp.where(lane == M_E2, e2.astype(F32), meta)
    meta = jnp.where(lane == M_R1, r1, meta)
    meta = jnp.where(lane == M_R2, r2, meta)
    meta = jnp.where(lane == M_G1, g1, meta)
    meta = jnp.where(lane == M_G2, g2, meta)
    meta_ref[...] = meta


def _proj_router(mix, x, w_out, g, w_router_pad, tm, hn_dtype):
    m, d = x.shape
    row = lambda width: pl.BlockSpec((tm, width), lambda i: (i, 0))
    full = lambda a: pl.BlockSpec(a.shape, lambda i: (0,) * a.ndim)
    return pl.pallas_call(
        _proj_router_kernel,
        grid=(m // tm,),
        in_specs=[row(d), row(d), full(w_out), full(g), full(w_router_pad)],
        out_specs=[row(d), row(d), row(LANE), pl.BlockSpec((1, LANE), lambda i: (0, 0))],
        out_shape=[jax.ShapeDtypeStruct((m, d), F32), jax.ShapeDtypeStruct((m, d), hn_dtype),
                   jax.ShapeDtypeStruct((m, LANE), F32), jax.ShapeDtypeStruct((1, LANE), F32)],
        scratch_shapes=[pltpu.VMEM((1, LANE), F32)],
        compiler_params=_cparams("arbitrary"),
        name="proj_router",
    )(mix, x, w_out, g, w_router_pad)


def _moe_dense_kernel(hn_ref, x1_ref, meta_ref, wg_ref, wu_ref, wd_ref, gf_ref, o_ref):
    e = pl.program_id(1)
    f = pl.program_id(2)

    @pl.when((e == 0) & (f == 0))
    def _():
        o_ref[...] = x1_ref[...]

    h = hn_ref[...]
    act = (_silu(_dot(h, wg_ref[...])) * _dot(h, wu_ref[...])).astype(BF16)
    lane = lax.broadcasted_iota(jnp.int32, meta_ref.shape, 1)
    gate = jnp.sum(jnp.where(lane == e, meta_ref[...], 0.0), axis=1, keepdims=True)
    o_ref[...] += gate * _dot(act, wd_ref[...])

    @pl.when((e == pl.num_programs(1) - 1) & (f == pl.num_programs(2) - 1))
    def _():
        o_ref[...] = _rms(o_ref[...], gf_ref[...])


def _moe_dense(hn, x1, meta, w_gate, w_up, w_down, g_final, tm):
    m, d = x1.shape
    n_e, nf, _, tf = w_gate.shape
    row = lambda width: pl.BlockSpec((tm, width), lambda i, e, f: (i, 0))
    return pl.pallas_call(
        _moe_dense_kernel,
        grid=(m // tm, n_e, nf),
        in_specs=[row(d), row(d), row(LANE),
                  pl.BlockSpec((None, None, d, tf), lambda i, e, f: (e, f, 0, 0)),
                  pl.BlockSpec((None, None, d, tf), lambda i, e, f: (e, f, 0, 0)),
                  pl.BlockSpec((None, tf, d), lambda i, e, f: (e, f, 0)),
                  pl.BlockSpec((1, d), lambda i, e, f: (0, 0))],
        out_specs=row(d),
        out_shape=jax.ShapeDtypeStruct((m, d), F32),
        compiler_params=_cparams("parallel", "arbitrary", "arbitrary"),
        name="moe_dense",
    )(hn, x1, meta, w_gate, w_up, w_down, g_final)


def _moe_dispatch_kernel(dest_ref, hn_ref, xs_in_ref, xs_ref, sem):
    del xs_in_ref
    tm = hn_ref.shape[0]
    base = pl.program_id(0) * (TOP_K * tm)

    def row_copy(t, k):
        return pltpu.make_async_copy(hn_ref.at[pl.ds(t, 1)], xs_ref.at[pl.ds(dest_ref[base + k * tm + t], 1)], sem)

    def issue(t, carry):
        for k in range(TOP_K):
            row_copy(t, k).start()
        return carry

    def drain(t, carry):
        for k in range(TOP_K):
            row_copy(t, k).wait()
        return carry

    lax.fori_loop(0, tm, issue, 0, unroll=DMA_UNROLL)
    lax.fori_loop(0, tm, drain, 0, unroll=DMA_UNROLL)


def _moe_dispatch(dest, hn, n_rows, tm):
    m, d = hn.shape
    return pl.pallas_call(
        _moe_dispatch_kernel,
        grid_spec=pltpu.PrefetchScalarGridSpec(
            num_scalar_prefetch=1,
            grid=(m // tm,),
            in_specs=[pl.BlockSpec((tm, d), lambda i, dest: (i, 0)), pl.BlockSpec(memory_space=pl.ANY)],
            out_specs=pl.BlockSpec(memory_space=pl.ANY),
            scratch_shapes=[pltpu.SemaphoreType.DMA(())]),
        out_shape=jax.ShapeDtypeStruct((n_rows, d), hn.dtype),
        input_output_aliases={2: 0},
        compiler_params=_cparams("arbitrary"),
        name="moe_dispatch",
    )(dest, hn, jnp.zeros((n_rows, d), hn.dtype))


def _moe_grouped_kernel(te_ref, nv_ref, xs_ref, wg_ref, wu_ref, wd_ref, o_ref, xb_ref):
    del te_ref
    f = pl.program_id(1)

    @pl.when(f == 0)
    def _():
        o_ref[...] = jnp.zeros_like(o_ref)
        xb_ref[...] = xs_ref[...].astype(BF16)

    @pl.when(pl.program_id(0) < nv_ref[0])
    def _():
        h = xb_ref[...]
        act = (_silu(_dot(h, wg_ref[...])) * _dot(h, wu_ref[...])).astype(BF16)
        o_ref[...] += _dot(act, wd_ref[...])


def _moe_grouped(tile_expert, n_valid, xs, w_gate, w_up, w_down, tg):
    rows, d = xs.shape
    _, nf, _, tf = w_gate.shape
    fidx = lambda r, f, nv: jnp.where(r < nv[0], f, nf - 1)
    return pl.pallas_call(
        _moe_grouped_kernel,
        grid_spec=pltpu.PrefetchScalarGridSpec(
            num_scalar_prefetch=2,
            grid=(rows // tg, nf),
            in_specs=[pl.BlockSpec((tg, d), lambda r, f, te, nv: (r, 0)),
                      pl.BlockSpec((None, None, d, tf), lambda r, f, te, nv: (te[r], fidx(r, f, nv), 0, 0)),
                      pl.BlockSpec((None, None, d, tf), lambda r, f, te, nv: (te[r], fidx(r, f, nv), 0, 0)),
                      pl.BlockSpec((None, tf, d), lambda r, f, te, nv: (te[r], fidx(r, f, nv), 0))],
            out_specs=pl.BlockSpec((tg, d), lambda r, f, te, nv: (r, 0)),
            scratch_shapes=[pltpu.VMEM((tg, d), BF16)]),
        out_shape=jax.ShapeDtypeStruct((rows, d), F32),
        compiler_params=_cparams("parallel", "arbitrary"),
        name="moe_grouped",
    )(tile_expert, n_valid, xs, w_gate, w_up, w_down)


def _moe_combine_kernel(dest_ref, ys_ref, x1_ref, meta_ref, gf_ref, o_ref, buf_ref, sem):
    i = pl.program_id(0)
    tc = x1_ref.shape[0]
    rows = TOP_K * tc

    def row_copy(step, slot, j):
        return pltpu.make_async_copy(ys_ref.at[pl.ds(dest_ref[step * rows + j], 1)],
                                     buf_ref.at[slot, pl.ds(j, 1)], sem.at[slot])

    def issue(step, slot):
        def body(j, carry):
            row_copy(step, slot, j).start()
            return carry
        lax.fori_loop(0, rows, body, 0, unroll=DMA_UNROLL)

    def drain(step, slot):
        def body(j, carry):
            row_copy(step, slot, j).wait()
            return carry
        lax.fori_loop(0, rows, body, 0, unroll=DMA_UNROLL)

    slot = i % 2

    @pl.when(i == 0)
    def _():
        issue(0, 0)

    @pl.when(i + 1 < pl.num_programs(0))
    def _():
        issue(i + 1, 1 - slot)

    drain(i, slot)
    g1 = meta_ref[:, M_G1:M_G1 + 1]
    g2 = meta_ref[:, M_G2:M_G2 + 1]
    y = x1_ref[...] + g1 * buf_ref[slot, 0:tc] + g2 * buf_ref[slot, tc:rows]
    o_ref[...] = _rms(y, gf_ref[...])


def _moe_combine(dest, ys, x1, meta, g_final, tc):
    m, d = x1.shape
    row = lambda width: pl.BlockSpec((tc, width), lambda i, dest: (i, 0))
    return pl.pallas_call(
        _moe_combine_kernel,
        grid_spec=pltpu.PrefetchScalarGridSpec(
            num_scalar_prefetch=1,
            grid=(m // tc,),
            in_specs=[pl.BlockSpec(memory_space=pl.ANY), row(d), row(LANE),
                      pl.BlockSpec((1, d), lambda i, dest: (0, 0))],
            out_specs=row(d),
            scratch_shapes=[pltpu.VMEM((2, TOP_K * tc, d), F32), pltpu.SemaphoreType.DMA((2,))]),
        out_shape=jax.ShapeDtypeStruct((m, d), F32),
        compiler_params=_cparams("arbitrary"),
        name="moe_combine",
    )(dest, ys, x1, meta, g_final)


def _moe_routes(meta, counts, tile, tg):
    m = meta.shape[0]
    n_rows = TOP_K * m + N_EXPERTS * tg
    expert = meta[:, M_E1:M_E2 + 1].astype(jnp.int32)
    rank = meta[:, M_R1:M_R2 + 1].astype(jnp.int32)
    padded = (counts[0, :N_EXPERTS].astype(jnp.int32) + tg - 1) // tg * tg
    ends = jnp.cumsum(padded)
    dest = (ends - padded)[expert] + rank
    dest = jnp.transpose(dest.reshape(m // tile, tile, TOP_K), (0, 2, 1)).reshape(-1)
    n_valid = ends[-1] // tg
    tile_start = jnp.arange(n_rows // tg, dtype=jnp.int32) * tg
    tile_expert = jnp.searchsorted(ends, jnp.minimum(tile_start, ends[-1] - 1), side="right").astype(jnp.int32)
    return dest, tile_expert, n_valid.reshape(1).astype(jnp.int32), n_rows


def _odd_params(i, od_norm_mix, od_w_in, od_gla_w_gate2, od_gla_b_gate2, od_gla_norm, od_delta_conv,
                od_delta_a_log, od_delta_dt_bias, od_delta_norm, od_w_out, od_norm_ffn, od_router,
                od_w_gate, od_w_up, od_w_down):
    w = od_w_in[i]
    sizes = (KEY_C, KEY_C, H_C * DV_C, GATE_RANK, H_C * DV_C, C_CONV, H_D, H_D, H_D * DV_D)
    splits = tuple(sum(sizes[:j + 1]) for j in range(len(sizes) - 1))
    cq, ck, cv, c_lr, c_r, d_qkv, d_a, d_b, d_g = jnp.split(w, splits, axis=1)
    pad = jnp.zeros((w.shape[0], IN_ODD_PAD - w.shape[1]), w.dtype)
    w_in = jnp.concatenate([cq, ck, cv, c_r, d_qkv, d_g, c_lr, d_a, d_b, pad], axis=1).astype(BF16)
    slab = lambda v: jnp.zeros((1, LANE), F32).at[0, SM_DA:SM_DA + H_D].set(v)
    return dict(norm_mix=od_norm_mix[i][None], w_in=w_in,
                w_g2_pad=jnp.zeros((LANE, KEY_C), F32).at[:GATE_RANK].set(od_gla_w_gate2[i]),
                b_g2=od_gla_b_gate2[i][None], gla_norm=od_gla_norm[i][None], conv_w=od_delta_conv[i],
                alog_slab=slab(od_delta_a_log[i]), dtb_slab=slab(od_delta_dt_bias[i]),
                a_log=od_delta_a_log[i], dt_bias=od_delta_dt_bias[i],
                delta_norm=od_delta_norm[i][None], w_out=od_w_out[i].astype(BF16),
                norm_ffn=od_norm_ffn[i][None],
                router=jnp.zeros((D_MODEL, LANE), F32).at[:, :N_EXPERTS].set(od_router[i]),
                w_gate=_col_blocks(od_w_gate[i].astype(BF16), MOE_TF),
                w_up=_col_blocks(od_w_up[i].astype(BF16), MOE_TF), w_down=od_w_down[i].astype(BF16))


MOE_GROUP_TILE = 512
MOE_ROW_TILE = 256


def _moe(mix, x, p, g_final, tm):
    m = x.shape[0]
    if m < N_EXPERTS * MOE_GROUP_TILE:
        x1, hn, meta, _ = _proj_router(mix, x, p["w_out"], p["norm_ffn"], p["router"], tm, BF16)
        return _moe_dense(hn, x1, meta, p["w_gate"], p["w_up"], p["w_down"], g_final, tm)
    x1, hn, meta, counts = _proj_router(mix, x, p["w_out"], p["norm_ffn"], p["router"], tm, F32)
    dest, tile_expert, n_valid, n_rows = _moe_routes(meta, counts, MOE_ROW_TILE, MOE_GROUP_TILE)
    xs = _moe_dispatch(dest, hn, n_rows, MOE_ROW_TILE)
    ys = _moe_grouped(tile_expert, n_valid, xs, p["w_gate"], p["w_up"], p["w_down"], MOE_GROUP_TILE)
    return _moe_combine(dest, ys, x1, meta, g_final, MOE_ROW_TILE)


def _odd_layer_prompt(x, p, g_final, batch, seq):
    z = _norm_matmul(x, p["norm_mix"], p["w_in"], 512)
    mix, s_gla, s_delta, tail = _odd_mixer_prompt(z, p, batch, seq, 256)
    y = _moe(mix, x, p, g_final, 512)
    return y, s_gla, s_delta, tail[:, 8 - (CONV_W - 1):]


def _even_epilogue_sample_kernel(z_ref, c_ref, su_ref, sd_ref, lng_ref, lnb_ref, w0_ref, b0_ref,
                                 q_ref, kv_ref, gv_ref, bout_ref):
    c, su, sd = c_ref[...], su_ref[...], sd_ref[...]
    q_ref[...] = _rope(z_ref[:, 0:QKV_A], c, su, sd) * (D_A ** -0.5)
    kv_ref[:, 0:QKV_A] = _rope(z_ref[:, QKV_A:2 * QKV_A], c, su, sd)
    kv_ref[:, QKV_A:2 * QKV_A] = z_ref[:, 2 * QKV_A:3 * QKV_A]
    u = _gelu(z_ref[:, 3 * QKV_A:3 * QKV_A + D_B])
    gv = _layernorm(_gelu(z_ref[:, 3 * QKV_A + D_B:3 * QKV_A + 2 * D_B]), lng_ref[...], lnb_ref[...])
    gv_ref[...] = gv
    bout_ref[...] = (u * (gv * w0_ref[...] + b0_ref[...])).astype(BF16)


def _even_epilogue_sample(z, tables, ln_g, ln_b, w_sp, b_sp):
    m = z.shape[0]
    w0 = jnp.repeat(w_sp[:, 0, 0], D_BG)[None]
    b0 = jnp.repeat(b_sp[:, 0], D_BG)[None]
    return pl.pallas_call(
        _even_epilogue_sample_kernel,
        out_shape=[jax.ShapeDtypeStruct((m, QKV_A), F32), jax.ShapeDtypeStruct((m, 2 * QKV_A), F32),
                   jax.ShapeDtypeStruct((m, D_B), F32), jax.ShapeDtypeStruct((m, D_B), BF16)],
        compiler_params=pltpu.CompilerParams(vmem_limit_bytes=VMEM_LIMIT),
        name="even_epilogue_sample",
    )(z, *tables, ln_g, ln_b, w0, b0)


def _moba_sample_kernel(pt_ref, qt_ref, knt_ref, vnt_ref, *refs):
    del pt_ref
    n_pages = len(refs) - 2
    page_refs, o_ref, s_ref = refs[:n_pages], refs[n_pages], refs[n_pages + 1]
    pages_per_block = MOBA_BLOCK // PAGE_SIZE
    nb = n_pages // pages_per_block
    qt = qt_ref[0]
    for h in range(H_A):
        qcol = jnp.broadcast_to(qt[:, h:h + 1], (D_A, PAGE_SIZE))
        for j in range(n_pages):
            s_ref[h, j:j + 1, :] = jnp.sum(page_refs[j][0, 0, h] * qcol, axis=0, keepdims=True)
    lane = lax.broadcasted_iota(jnp.int32, (n_pages, LANE), 1)
    page_sums = jnp.zeros((n_pages, LANE), F32)
    for h in range(H_A):
        page_sums = jnp.where(lane == h, jnp.sum(s_ref[h], axis=1, keepdims=True), page_sums)
    pair = (lax.broadcasted_iota(jnp.int32, (nb, n_pages), 1) // pages_per_block
            == lax.broadcasted_iota(jnp.int32, (nb, n_pages), 0)).astype(F32)
    pair_t = (lax.broadcasted_iota(jnp.int32, (n_pages, nb), 0) // pages_per_block
              == lax.broadcasted_iota(jnp.int32, (n_pages, nb), 1)).astype(F32)
    gate = jnp.dot(pair, page_sums, precision=HI, preferred_element_type=F32)
    sel = _moba_select(gate, nb)
    sel_pages = jnp.dot(pair_t, sel, precision=HI, preferred_element_type=F32)
    own = jnp.sum(qt * knt_ref[0], axis=0, keepdims=True)
    vnt = vnt_ref[0]
    out_lane = lax.broadcasted_iota(jnp.int32, (D_A, LANE), 1)
    out = jnp.zeros((D_A, LANE), F32)
    for h in range(H_A):
        sm = jnp.where(sel_pages[:, h:h + 1] > 0.0, s_ref[h], NEG)
        s_own = own[:, h:h + 1]
        mx = jnp.maximum(jnp.max(jnp.max(sm, axis=1, keepdims=True), axis=0, keepdims=True), s_own)
        p = jnp.exp(sm - mx)
        p_own = jnp.exp(s_own - mx)
        denom = jnp.sum(jnp.sum(p, axis=1, keepdims=True), axis=0, keepdims=True) + p_own
        acc = jnp.zeros((D_A, PAGE_SIZE), F32)
        for j in range(n_pages):
            acc = acc + page_refs[j][0, 1, h] * p[j:j + 1, :]
        o = (jnp.sum(acc, axis=1, keepdims=True) + p_own * vnt[:, h:h + 1]) / denom
        out = jnp.where(out_lane == h, o, out)
    o_ref[0] = out


def _moba_sample(q, kv_new, cache, page_table):
    bs, n_pages = page_table.shape
    assert (n_pages * PAGE_SIZE) % MOBA_BLOCK == 0
    cache_t = jnp.transpose(cache, (0, 2, 3, 4, 1))
    page_spec = lambda j: pl.BlockSpec((1, 2, H_A, D_A, PAGE_SIZE),
                                       lambda b, pt: (pt[b * n_pages + j], 0, 0, 0, 0))
    col_spec = pl.BlockSpec((1, D_A, H_A), lambda b, pt: (b, 0, 0))
    heads_t = lambda a: jnp.transpose(a.reshape(bs, H_A, D_A), (0, 2, 1))
    out = pl.pallas_call(
        _moba_sample_kernel,
        grid_spec=pltpu.PrefetchScalarGridSpec(
            num_scalar_prefetch=1,
            grid=(bs,),
            in_specs=[col_spec, col_spec, col_spec] + [page_spec(j) for j in range(n_pages)],
            out_specs=pl.BlockSpec((1, D_A, LANE), lambda b, pt: (b, 0, 0)),
            scratch_shapes=[pltpu.VMEM((H_A, n_pages, PAGE_SIZE), F32)]),
        out_shape=jax.ShapeDtypeStruct((bs, D_A, LANE), F32),
        compiler_params=_cparams("parallel"),
        name="moba_sample",
    )(page_table.reshape(-1), heads_t(q), heads_t(kv_new[:, 0:QKV_A]), heads_t(kv_new[:, QKV_A:]),
      *([cache_t] * n_pages))
    return jnp.transpose(out[:, :, 0:H_A], (0, 2, 1)).reshape(bs, QKV_A).astype(BF16)


def _odd_mixer_sample_kernel(z_ref, buf_ref, sg_ref, sd_ref, wg2_ref, bg2_ref, gnorm_ref, cw_ref, alog_ref, dtb_ref,
                             dnorm_ref, o_ref, sgo_ref, sdo_ref):
    bt = z_ref.shape[0]
    stride = H_C * DK_C
    small = z_ref[:, O_SM:O_SM + LANE]
    pre = jnp.dot(small, wg2_ref[...], precision=HI, preferred_element_type=F32) + bg2_ref[...]
    a_all = jnp.exp(_log_sigmoid(pre) / GATE_NORM)
    g_all = -jnp.exp(alog_ref[...]) * _softplus(small + dtb_ref[...])
    beta_all = jax.nn.sigmoid(small)
    x = z_ref[:, O_DQKV:O_DQKV + C_CONV]
    y = x * cw_ref[CONV_W - 1:CONV_W, :]
    for j in range(CONV_W - 1):
        y = y + buf_ref[:, j, :] * cw_ref[j:j + 1, :]
    y = _silu(y)
    for h in range(H_C):
        ks = slice(h * DK_C, (h + 1) * DK_C)
        a = a_all[:, ks]
        q = z_ref[:, O_CQ + h * DK_C:O_CQ + (h + 1) * DK_C] * (DK_C ** -0.5)
        k = z_ref[:, O_CK + h * DK_C:O_CK + (h + 1) * DK_C]
        v = z_ref[:, O_CV + h * DV_C:O_CV + (h + 1) * DV_C]
        qa = q * a
        acc = jnp.sum(q * k, axis=1, keepdims=True) * v
        for kk in range(DK_C):
            rows = pl.ds(h * DK_C + kk, bt, stride=stride)
            srow = sg_ref[rows, :]
            acc = acc + qa[:, kk:kk + 1] * srow
            sgo_ref[rows, :] = a[:, kk:kk + 1] * srow + k[:, kk:kk + 1] * v
        gate = _silu(z_ref[:, O_CR + h * DV_C:O_CR + (h + 1) * DV_C])
        o_ref[:, h * DV_C:(h + 1) * DV_C] = (_rms(acc, gnorm_ref[...]) * gate).astype(BF16)
        yq = y[:, h * DK_D:(h + 1) * DK_D]
        yk = y[:, KEY_C + h * DK_D:KEY_C + (h + 1) * DK_D]
        dv = y[:, 2 * KEY_C + h * DV_D:2 * KEY_C + (h + 1) * DV_D]
        dq = yq * lax.rsqrt(jnp.sum(yq * yq, axis=1, keepdims=True) + EPS) * (DK_D ** -0.5)
        dk = yk * lax.rsqrt(jnp.sum(yk * yk, axis=1, keepdims=True) + EPS)
        beta = beta_all[:, SM_DB + h:SM_DB + h + 1]
        eg = jnp.exp(g_all[:, SM_DA + h:SM_DA + h + 1])
        w = dk * (beta * eg)
        qd = dq * eg
        ws = jnp.zeros((bt, DV_D), F32)
        qs = jnp.zeros((bt, DV_D), F32)
        for kk in range(DK_D):
            srow = sd_ref[pl.ds(h * DK_D + kk, bt, stride=stride), :]
            ws = ws + w[:, kk:kk + 1] * srow
            qs = qs + qd[:, kk:kk + 1] * srow
        v_new = dv * beta - ws
        o = qs + jnp.sum(dq * dk, axis=1, keepdims=True) * v_new
        for kk in range(DK_D):
            rows = pl.ds(h * DK_D + kk, bt, stride=stride)
            sdo_ref[rows, :] = sd_ref[rows, :] * eg + dk[:, kk:kk + 1] * v_new
        gate = _silu(z_ref[:, O_DG + h * DV_D:O_DG + (h + 1) * DV_D])
        col = H_C * DV_C + h * DV_D
        o_ref[:, col:col + DV_D] = (_rms(o, dnorm_ref[...]) * gate).astype(BF16)


def _odd_mixer_sample(z, conv_buf, s_gla, s_delta, p, bt):
    bs = z.shape[0]
    rows = H_C * DK_C
    full = lambda a: pl.BlockSpec(a.shape, lambda i: (0,) * a.ndim)
    consts = [p["w_g2_pad"], p["b_g2"], p["gla_norm"], p["conv_w"], p["alog_slab"], p["dtb_slab"], p["delta_norm"]]
    st_spec = pl.BlockSpec((bt * rows, DV_C), lambda i: (i, 0))
    mix, sg, sd = pl.pallas_call(
        _odd_mixer_sample_kernel,
        grid=(bs // bt,),
        in_specs=[pl.BlockSpec((bt, IN_ODD_PAD), lambda i: (i, 0)),
                  pl.BlockSpec((bt, CONV_W - 1, C_CONV), lambda i: (i, 0, 0)),
                  st_spec, st_spec] + [full(a) for a in consts],
        out_specs=[pl.BlockSpec((bt, D_MODEL), lambda i: (i, 0)), st_spec, st_spec],
        out_shape=[jax.ShapeDtypeStruct((bs, D_MODEL), BF16),
                   jax.ShapeDtypeStruct((bs * rows, DV_C), F32),
                   jax.ShapeDtypeStruct((bs * rows, DV_D), F32)],
        compiler_params=_cparams("parallel"),
        name="odd_mixer_sample",
    )(z, conv_buf, s_gla.reshape(bs * rows, DV_C), s_delta.reshape(bs * rows, DV_D), *consts)
    return mix, sg.reshape(s_gla.shape), sd.reshape(s_delta.shape)


def _sample_step(x, cache, page_table, s_gla, s_delta, conv_buf, ev, od, g_final):
    bs = x.shape[0]
    past = page_table.shape[1] * PAGE_SIZE
    z = _norm_matmul(x, ev["norm_mix"], ev["w_in"], bs)
    tables = _rope_tables(jnp.full((1,), past, jnp.int32))
    q, kv, gv, b_out = _even_epilogue_sample(z, tables, ev["ln_g"], ev["ln_b"], ev["w_sp"], ev["b_sp"])
    a_out = _moba_sample(q, kv, cache, page_table)
    x = _proj_ffn([a_out, b_out], x, ev["w_out"], ev["norm_ffn"], ev["w_gate"], ev["w_up"], ev["w_down"], bs)
    z = _norm_matmul(x, od["norm_mix"], od["w_in"], bs)
    mix, sg, sd = _odd_mixer_sample(z, conv_buf, s_gla, s_delta, od, 32)
    conv_new = jnp.concatenate([conv_buf[:, 1:], z[:, None, O_DQKV:O_DQKV + C_CONV]], axis=1)
    y = _moe(mix, x, od, g_final, bs)
    return y, kv, gv, sg, sd, conv_new


def kernel(x_prompt, x_sample, cache_kv, state_gla, state_delta, state_conv, page_table, ev_norm_mix, ev_w_in, ev_gmlp_ln_g, ev_gmlp_ln_b, ev_w_spatial, ev_b_spatial, ev_w_out, ev_norm_ffn, ev_w_gate, ev_w_up, ev_w_down, od_norm_mix, od_w_in, od_gla_w_gate2, od_gla_b_gate2, od_gla_norm, od_delta_conv, od_delta_a_log, od_delta_dt_bias, od_delta_norm, od_w_out, od_norm_ffn, od_router, od_w_gate, od_w_up, od_w_down, norm_final):
    bp, tp, d = x_prompt.shape
    ev = _even_params(0, ev_norm_mix, ev_w_in, ev_gmlp_ln_g, ev_gmlp_ln_b, ev_w_spatial, ev_b_spatial, ev_w_out,
                      ev_norm_ffn, ev_w_gate, ev_w_up, ev_w_down)
    od = _odd_params(0, od_norm_mix, od_w_in, od_gla_w_gate2, od_gla_b_gate2, od_gla_norm, od_delta_conv,
                     od_delta_a_log, od_delta_dt_bias, od_delta_norm, od_w_out, od_norm_ffn, od_router,
                     od_w_gate, od_w_up, od_w_down)
    bs, ts, _ = x_sample.shape
    assert ts == 1 and cache_kv.shape[0] == 1 and state_gla.shape[0] == 1
    xp, kv_p, gv_p = _even_layer_prompt(x_prompt.reshape(bp * tp, d), ev, bp, tp)
    yp, gla_p, dl_p, cv_p = _odd_layer_prompt(xp, od, norm_final[None], bp, tp)
    ys, kv_s, gv_s, gla_s, dl_s, cv_s = _sample_step(
        x_sample.reshape(bs, d), cache_kv[0], page_table, state_gla[0], state_delta[0], state_conv[0],
        ev, od, norm_final[None])
    return (yp.reshape(bp, tp, d), ys.reshape(bs, ts, d),
            kv_p[None], kv_s.reshape(1, bs, ts, 2, H_A, D_A),
            gv_p[None], gv_s.reshape(1, bs, ts, D_B),
            gla_p[None], gla_s[None], dl_p[None], dl_s[None], cv_p[None], cv_s[None])
```

```python
import functools
import math

import jax
import jax.numpy as jnp
from jax import lax
from jax.experimental import pallas as pl
from jax.experimental.pallas import tpu as pltpu

F32 = jnp.float32
BF16 = jnp.bfloat16
HI = lax.Precision.HIGHEST
EPS = 1e-6
NEG = -1e30

D_MODEL = 1024
PAGE_SIZE = 128
H_A, D_A = 8, 64
ROT_DIM = D_A // 4
ROPE_THETA = 500000.0
MOBA_BLOCK = 256
MOBA_TOPK = 3
G_B, D_BG = 8, 64
D_B = G_B * D_BG
CHUNK_B = 128
H_C, DK_C, DV_C = 4, 64, 128
GATE_RANK = 16
GATE_NORM = 16.0
H_D, DK_D, DV_D = 4, 64, 128
CONV_W = 4
C_CONV = 2 * H_D * DK_D + H_D * DV_D
CHUNK_LIN = 64
N_EXPERTS = 8
TOP_K = 2
QKV_A = H_A * D_A
IN_EVEN = 3 * QKV_A + 2 * D_B
FFN_TF = 1408
MOE_TF = 896
MOBA_LOOP_BLOCKS = 4
DMA_UNROLL = 8
LANE = 128
VMEM_LIMIT = 56 * 1024 * 1024


def _cparams(*sem):
    return pltpu.CompilerParams(dimension_semantics=sem, vmem_limit_bytes=VMEM_LIMIT)


def _rms(x, g):
    return x * lax.rsqrt(jnp.mean(x * x, axis=-1, keepdims=True) + EPS) * g


def _gelu(x):
    return 0.5 * x * (1.0 + lax.erf(x * (2.0 ** -0.5)))


def _silu(x):
    return x * jax.nn.sigmoid(x)


def _softplus(x):
    return jnp.maximum(x, 0.0) + jnp.log1p(jnp.exp(-jnp.abs(x)))


def _dot(a, b):
    return jnp.dot(a, b, preferred_element_type=F32)


def _dot_nt(a, b):
    return lax.dot_general(a, b, (((1,), (1,)), ((), ())), preferred_element_type=F32)


def _dot_tn(a, b):
    return lax.dot_general(a, b, (((0,), (0,)), ((), ())), preferred_element_type=F32)


def _norm_matmul_kernel(x_ref, g_ref, w_ref, o_ref):
    h = _rms(x_ref[...], g_ref[...]).astype(BF16)
    o_ref[...] = _dot(h, w_ref[...])


def _norm_matmul(x, g, w, tm):
    m, d = x.shape
    n = w.shape[1]
    return pl.pallas_call(
        _norm_matmul_kernel,
        grid=(m // tm,),
        in_specs=[pl.BlockSpec((tm, d), lambda i: (i, 0)),
                  pl.BlockSpec((1, d), lambda i: (0, 0)),
                  pl.BlockSpec((d, n), lambda i: (0, 0))],
        out_specs=pl.BlockSpec((tm, n), lambda i: (i, 0)),
        out_shape=jax.ShapeDtypeStruct((m, n), F32),
        compiler_params=_cparams("parallel"),
        name="norm_matmul",
    )(x, g, w)


def _rope_tables(pos):
    half = ROT_DIM // 2
    inv = ROPE_THETA ** (-jnp.arange(half, dtype=F32) / half)
    ang = pos.astype(F32)[:, None] * inv[None, :]
    cos, sin = jnp.cos(ang), jnp.sin(ang)
    t = pos.shape[0]
    one = jnp.ones((t, D_A - ROT_DIM), F32)
    zero_h = jnp.zeros((t, half), F32)
    zero_r = jnp.zeros((t, D_A - ROT_DIM), F32)
    c = jnp.concatenate([cos, cos, one], axis=1)
    s_up = jnp.concatenate([-sin, zero_h, zero_r], axis=1)
    s_dn = jnp.concatenate([zero_h, sin, zero_r], axis=1)
    rep = LANE // D_A
    return jnp.tile(c, (1, rep)), jnp.tile(s_up, (1, rep)), jnp.tile(s_dn, (1, rep))


def _rope(x, c, s_up, s_dn):
    half = ROT_DIM // 2
    outs = []
    for j in range(x.shape[1] // LANE):
        xs = x[:, j * LANE:(j + 1) * LANE]
        up = pltpu.roll(xs, LANE - half, 1)
        dn = pltpu.roll(xs, half, 1)
        outs.append(xs * c + up * s_up + dn * s_dn)
    return jnp.concatenate(outs, axis=1)


def _layernorm(x, g, b):
    mu = jnp.mean(x, axis=-1, keepdims=True)
    xc = x - mu
    var = jnp.mean(xc * xc, axis=-1, keepdims=True)
    return xc * lax.rsqrt(var + EPS) * g + b


def _even_epilogue_kernel(z_ref, c_ref, su_ref, sd_ref, lng_ref, lnb_ref, wsp_ref, bspt_ref,
                          qt_ref, k_ref, vt_ref, kv_ref, kmean_ref, gv_ref, bout_ref):
    tm = z_ref.shape[0]
    c, su, sd = c_ref[...], su_ref[...], sd_ref[...]
    q = _rope(z_ref[:, 0:QKV_A], c, su, sd) * (D_A ** -0.5)
    k = _rope(z_ref[:, QKV_A:2 * QKV_A], c, su, sd)
    v = z_ref[:, 2 * QKV_A:3 * QKV_A]
    qt_ref[0] = q.T.astype(BF16)
    k_ref[...] = k.astype(BF16)
    vt = v.T
    vt_ref[0] = vt.astype(BF16)
    kv_ref[0, 0:QKV_A, :] = k.T
    kv_ref[0, QKV_A:2 * QKV_A, :] = vt
    for blk in range(tm // MOBA_BLOCK):
        kmean_ref[blk] = jnp.mean(k[blk * MOBA_BLOCK:(blk + 1) * MOBA_BLOCK], axis=0, keepdims=True)
    u = _gelu(z_ref[:, 3 * QKV_A:3 * QKV_A + D_B])
    gv = _layernorm(_gelu(z_ref[:, 3 * QKV_A + D_B:3 * QKV_A + 2 * D_B]), lng_ref[...], lnb_ref[...])
    gv_ref[...] = gv
    gvb = gv.astype(BF16)
    row = lax.broadcasted_iota(jnp.int32, (CHUNK_B, CHUNK_B), 0)
    col = lax.broadcasted_iota(jnp.int32, (CHUNK_B, CHUNK_B), 1)
    group = lax.broadcasted_iota(jnp.int32, (CHUNK_B, D_B), 1) // D_BG
    w = [jnp.where(row >= col, wsp_ref[g], 0.0).astype(BF16) for g in range(G_B)]
    for ch in range(tm // CHUNK_B):
        gvc = gvb[ch * CHUNK_B:(ch + 1) * CHUNK_B]
        mixed = jnp.zeros((CHUNK_B, D_B), F32)
        for g in range(G_B):
            mixed = jnp.where(group == g, _dot(w[g], gvc) + bspt_ref[:, g:g + 1], mixed)
        bout_ref[ch * CHUNK_B:(ch + 1) * CHUNK_B, :] = (u[ch * CHUNK_B:(ch + 1) * CHUNK_B] * mixed).astype(BF16)


def _even_epilogue(z, tables, ln_g, ln_b, w_sp, b_sp, batch, seq, tm):
    n = batch * seq
    nt = seq // tm
    nblk = tm // MOBA_BLOCK
    tab_spec = pl.BlockSpec((tm, LANE), lambda b, i: (i, 0))
    row_spec = lambda width: pl.BlockSpec((tm, width), lambda b, i: (b * nt + i, 0))
    t_spec = pl.BlockSpec((1, QKV_A, tm), lambda b, i: (b, 0, i))
    return pl.pallas_call(
        _even_epilogue_kernel,
        grid=(batch, nt),
        in_specs=[row_spec(IN_EVEN), tab_spec, tab_spec, tab_spec,
                  pl.BlockSpec((1, D_B), lambda b, i: (0, 0)),
                  pl.BlockSpec((1, D_B), lambda b, i: (0, 0)),
                  pl.BlockSpec((G_B, CHUNK_B, CHUNK_B), lambda b, i: (0, 0, 0)),
                  pl.BlockSpec((CHUNK_B, G_B), lambda b, i: (0, 0))],
        out_specs=[t_spec, row_spec(QKV_A), t_spec, pl.BlockSpec((1, 2 * QKV_A, tm), lambda b, i: (b, 0, i)),
                   pl.BlockSpec((nblk, 1, QKV_A), lambda b, i: (b * nt + i, 0, 0)),
                   row_spec(D_B), row_spec(D_B)],
        out_shape=[jax.ShapeDtypeStruct((batch, QKV_A, seq), BF16),
                   jax.ShapeDtypeStruct((n, QKV_A), BF16),
                   jax.ShapeDtypeStruct((batch, QKV_A, seq), BF16),
                   jax.ShapeDtypeStruct((batch, 2 * QKV_A, seq), F32),
                   jax.ShapeDtypeStruct((n // MOBA_BLOCK, 1, QKV_A), F32),
                   jax.ShapeDtypeStruct((n, D_B), F32),
                   jax.ShapeDtypeStruct((n, D_B), BF16)],
        compiler_params=_cparams("parallel", "parallel"),
        name="even_epilogue",
    )(z, *tables, ln_g, ln_b, w_sp, b_sp.T)


def _moba_select(gate, n_own):
    nb = gate.shape[0]
    blk = lax.broadcasted_iota(jnp.int32, gate.shape, 0)
    elig = blk < n_own
    gm = jnp.where(elig, gate, NEG)
    rank = jnp.zeros(gate.shape, F32)
    for m in range(nb):
        gm_m = gm[m:m + 1, :]
        ahead = (gm_m > gm) | ((gm_m == gm) & (m < blk))
        rank = rank + ahead.astype(F32)
    return (elig & (rank < MOBA_TOPK)).astype(F32)


def _moba_prompt_kernel(qt_ref, k_ref, vt_ref, kmean_ref, o_ref, bias_ref):
    i = pl.program_id(2)
    tq = MOBA_BLOCK
    n_head = LANE // D_A
    n_split = tq // LANE
    own = pl.multiple_of(i * tq, tq)
    key_i = lax.broadcasted_iota(jnp.int32, (tq, LANE), 0)
    qry_i = lax.broadcasted_iota(jnp.int32, (tq, LANE), 1)
    head_lanes = [slice(hh * D_A, (hh + 1) * D_A) for hh in range(n_head)]
    streams = [(hh, qh) for hh in range(n_head) for qh in range(n_split)]
    qts = [qt_ref[0, head_lanes[hh], :] for hh in range(n_head)]
    queries = [qts[hh][:, qh * LANE:(qh + 1) * LANE] for hh, qh in streams]
    k_own = [k_ref[pl.ds(own, tq), head_lanes[hh]] for hh in range(n_head)]
    own_scores = [_dot(k_own[hh], queries[idx]) for idx, (hh, qh) in enumerate(streams)]
    gates = [jnp.dot(kmean_ref[0, :, head_lanes[hh]], qts[hh].astype(F32), precision=HI, preferred_element_type=F32)
             for hh in range(n_head)]
    for hh in range(n_head):
        bias = jnp.where(_moba_select(gates[hh], i) > 0.0, 0.0, NEG)
        for qh in range(n_split):
            bias_ref[hh, qh] = bias[:, qh * LANE:(qh + 1) * LANE]
    own_probs, own_stats = [], []
    for idx, (hh, qh) in enumerate(streams):
        s = jnp.where(key_i <= qry_i + qh * LANE, own_scores[idx], NEG)
        m = jnp.max(s, axis=0, keepdims=True)
        p = jnp.exp(s - m)
        own_probs.append(p.astype(BF16))
        own_stats.append((m, jnp.sum(p, axis=0, keepdims=True)))
    vt_own = [vt_ref[0, head_lanes[hh], pl.ds(own, tq)] for hh in range(n_head)]
    state = []
    for idx, (hh, qh) in enumerate(streams):
        state += [own_stats[idx][0], own_stats[idx][1], _dot(vt_own[hh], own_probs[idx])]

    def body(j0, carry, nk):
        tk = nk * tq
        start = pl.multiple_of(j0 * tq, tq)
        k_j = [k_ref[pl.ds(start, tk), head_lanes[hh]] for hh in range(n_head)]
        scores = [_dot(k_j[hh], queries[hh * n_split + qh]) for hh, qh in streams]
        probs, stats = [], []
        for idx, (hh, qh) in enumerate(streams):
            m, l = carry[3 * idx], carry[3 * idx + 1]
            s = [scores[idx][b * tq:(b + 1) * tq] + bias_ref[hh, qh, pl.ds(j0 + b, 1), :] for b in range(nk)]
            m_new = m
            for sb in s:
                m_new = jnp.maximum(m_new, jnp.max(sb, axis=0, keepdims=True))
            alpha = jnp.exp(m - m_new)
            p = [jnp.exp(sb - m_new) for sb in s]
            l_new = l * alpha
            for pb in p:
                l_new = l_new + jnp.sum(pb, axis=0, keepdims=True)
            probs.append(jnp.concatenate([pb.astype(BF16) for pb in p], axis=0))
            stats.append((m_new, l_new, alpha))
        vt_j = [vt_ref[0, head_lanes[hh], pl.ds(start, tk)] for hh in range(n_head)]
        pv = [_dot(vt_j[hh], probs[idx]) for idx, (hh, qh) in enumerate(streams)]
        out = []
        for idx in range(len(streams)):
            m_new, l_new, alpha = stats[idx]
            out += [m_new, l_new, carry[3 * idx + 2] * alpha + pv[idx]]
        return tuple(out)

    n_group = i // MOBA_LOOP_BLOCKS
    state = lax.fori_loop(0, n_group, lambda t, c: body(t * MOBA_LOOP_BLOCKS, c, MOBA_LOOP_BLOCKS), tuple(state))
    state = lax.fori_loop(n_group * MOBA_LOOP_BLOCKS, i, lambda j, c: body(j, c, 1), state)
    outs = []
    for hh in range(n_head):
        halves = [state[3 * (hh * n_split + qh) + 2] / state[3 * (hh * n_split + qh) + 1] for qh in range(n_split)]
        outs.append(jnp.concatenate(halves, axis=1))
    o_ref[...] = jnp.concatenate(outs, axis=0).T.astype(BF16)


def _moba_prompt(qt, k, vt, kmean, batch, seq):
    nq = seq // MOBA_BLOCK
    hp = QKV_A // LANE
    return pl.pallas_call(
        _moba_prompt_kernel,
        grid=(batch, hp, nq),
        in_specs=[pl.BlockSpec((1, LANE, MOBA_BLOCK), lambda b, h, i: (b, h, i)),
                  pl.BlockSpec((seq, LANE), lambda b, h, i: (b, h)),
                  pl.BlockSpec((1, LANE, seq), lambda b, h, i: (b, h, 0)),
                  pl.BlockSpec((1, nq, LANE), lambda b, h, i: (b, 0, h))],
        out_specs=pl.BlockSpec((MOBA_BLOCK, LANE), lambda b, h, i: (b * nq + i, h)),
        out_shape=jax.ShapeDtypeStruct((batch * seq, QKV_A), BF16),
        scratch_shapes=[pltpu.VMEM((LANE // D_A, MOBA_BLOCK // LANE, nq, LANE), F32)],
        compiler_params=_cparams("parallel", "parallel", "arbitrary"),
        name="moba_prompt",
    )(qt, k, vt, kmean)


def _proj_ffn_kernel(*refs, n_mix):
    mix_refs = refs[:n_mix]
    x_ref, wo_ref, g_ref, wg_ref, wu_ref, wd_ref, o_ref, hn_ref = refs[n_mix:]

    @pl.when(pl.program_id(1) == 0)
    def _():
        x1 = x_ref[...]
        off = 0
        for r in mix_refs:
            width = r.shape[1]
            x1 = x1 + _dot(r[...], wo_ref[off:off + width, :])
            off += width
        o_ref[...] = x1
        hn_ref[...] = _rms(x1, g_ref[...]).astype(BF16)

    h = hn_ref[...]
    act = (_silu(_dot(h, wg_ref[...])) * _dot(h, wu_ref[...])).astype(BF16)
    o_ref[...] += _dot(act, wd_ref[...])


def _col_blocks(w, tf):
    *lead, d, ff = w.shape
    n = len(lead)
    return jnp.transpose(w.reshape(*lead, d, ff // tf, tf), (*range(n), n + 1, n, n + 2))


def _proj_ffn(mixes, x, w_out, g, w_gate, w_up, w_down, tm):
    m, d = x.shape
    nf, _, tf = w_gate.shape
    mix_specs = [pl.BlockSpec((tm, a.shape[1]), lambda i, f: (i, 0)) for a in mixes]
    return pl.pallas_call(
        functools.partial(_proj_ffn_kernel, n_mix=len(mixes)),
        grid=(m // tm, nf),
        in_specs=mix_specs + [
            pl.BlockSpec((tm, d), lambda i, f: (i, 0)),
            pl.BlockSpec(w_out.shape, lambda i, f: (0, 0)),
            pl.BlockSpec((1, d), lambda i, f: (0, 0)),
            pl.BlockSpec((None, d, tf), lambda i, f: (f, 0, 0)),
            pl.BlockSpec((None, d, tf), lambda i, f: (f, 0, 0)),
            pl.BlockSpec((tf, d), lambda i, f: (f, 0))],
        out_specs=pl.BlockSpec((tm, d), lambda i, f: (i, 0)),
        out_shape=jax.ShapeDtypeStruct((m, d), F32),
        scratch_shapes=[pltpu.VMEM((tm, d), BF16)],
        compiler_params=_cparams("parallel", "arbitrary"),
        name="proj_ffn",
    )(*mixes, x, w_out, g, w_gate, w_up, w_down)


def _even_layer_prompt(x, p, batch, seq):
    z = _norm_matmul(x, p["norm_mix"], p["w_in"], 512)
    tables = _rope_tables(jnp.arange(seq, dtype=jnp.int32))
    qt, k, vt, kv, kmean, gv, b_out = _even_epilogue(
        z, tables, p["ln_g"], p["ln_b"], p["w_sp"], p["b_sp"], batch, seq, 256)
    a_out = _moba_prompt(qt, k, vt, kmean.reshape(batch, seq // MOBA_BLOCK, QKV_A), batch, seq)
    x = _proj_ffn([a_out, b_out], x, p["w_out"], p["norm_ffn"], p["w_gate"], p["w_up"], p["w_down"], 512)
    n_open = seq - ((seq - 1) // CHUNK_B) * CHUNK_B
    gv_open = gv.reshape(batch, seq, D_B)[:, seq - n_open:]
    kv = jnp.transpose(kv.reshape(batch, 2, H_A, D_A, seq), (0, 4, 1, 2, 3))
    return x, kv, gv_open


def _even_params(i, ev_norm_mix, ev_w_in, ev_gmlp_ln_g, ev_gmlp_ln_b, ev_w_spatial, ev_b_spatial, ev_w_out,
                 ev_norm_ffn, ev_w_gate, ev_w_up, ev_w_down):
    return dict(norm_mix=ev_norm_mix[i][None], w_in=ev_w_in[i].astype(BF16),
                ln_g=ev_gmlp_ln_g[i][None], ln_b=ev_gmlp_ln_b[i][None],
                w_sp=ev_w_spatial[i], b_sp=ev_b_spatial[i], w_out=ev_w_out[i].astype(BF16),
                norm_ffn=ev_norm_ffn[i][None], w_gate=_col_blocks(ev_w_gate[i].astype(BF16), FFN_TF),
                w_up=_col_blocks(ev_w_up[i].astype(BF16), FFN_TF), w_down=ev_w_down[i].astype(BF16))


O_CQ, O_CK, O_CV, O_CR, O_DQKV, O_DG, O_SM = 0, 256, 512, 1024, 1536, 2560, 3072
SM_DA, SM_DB = GATE_RANK, GATE_RANK + H_D
IN_ODD_PAD = O_SM + LANE
KEY_C = H_C * DK_C
PAIR = LANE // DK_C


def _log_sigmoid(x):
    return jnp.minimum(x, 0.0) - jnp.log1p(jnp.exp(-jnp.abs(x)))


def _split_bf16(a, terms):
    parts = []
    for _ in range(terms - 1):
        hi = a.astype(BF16)
        parts.append(hi)
        a = a - hi.astype(F32)
    parts.append(a.astype(BF16))
    return parts


def _dot_select(a, sel_bf16, terms):
    parts = _split_bf16(a, terms)
    out = _dot(parts[0], sel_bf16)
    for part in parts[1:]:
        out = out + _dot(part, sel_bf16)
    return out


def _select_dot(sel_bf16, a, terms):
    parts = _split_bf16(a, terms)
    out = _dot(sel_bf16, parts[0])
    for part in parts[1:]:
        out = out + _dot(sel_bf16, part)
    return out


def _unit_lower_inverses(mats):
    c = mats[0].shape[0]
    assert c == 64
    eye = (lax.broadcasted_iota(jnp.int32, (c, c), 0) == lax.broadcasted_iota(jnp.int32, (c, c), 1)).astype(F32)
    b16 = lambda ms: [m.astype(BF16) for m in ms]
    mul = lambda aa, bb: [_dot(a, b) for a, b in zip(aa, bb)]
    x1 = [-a for a in mats]
    x1b = b16(x1)
    x2 = mul(x1b, x1b)
    x2b = b16(x2)
    x4 = mul(x2b, x2b)
    x3 = mul(x1b, x2b)
    x4b = b16(x4)
    x8 = mul(x4b, x4b)
    f01 = [eye + a + b + d for a, b, d in zip(x1, x2, x3)]
    x8b = b16(x8)
    x16 = mul(x8b, x8b)
    x12 = mul(x4b, x8b)
    g23 = [a + b + d for a, b, d in zip(x4, x8, x12)]
    x16b = b16(x16)
    x32 = mul(x16b, x16b)
    f0123 = [f + d for f, d in zip(f01, mul(b16(f01), b16(g23)))]
    x48 = mul(x16b, b16(x32))
    g45 = [a + b + d for a, b, d in zip(x16, x32, x48)]
    return [f + d for f, d in zip(f0123, mul(b16(f0123), b16(g45)))]


def _group_sumsq(y, width):
    n = y.shape[1]
    same = (lax.broadcasted_iota(jnp.int32, (n, n), 0) // width
            == lax.broadcasted_iota(jnp.int32, (n, n), 1) // width).astype(BF16)
    return _dot_select(y * y, same, 2)


def _odd_mixer_prompt_kernel(z_ref, wg2_ref, bg2_ref, gnorm_ref, cw_ref, alog_ref, dtb_ref, dnorm_ref,
                             o_ref, sg_ref, sd_ref, tail_ref,
                             stg_ref, std_ref, prev_ref, la_ref, qkv_ref, dla_ref, beta_ref):
    i = pl.program_id(1)
    tc = z_ref.shape[0]
    c = CHUNK_LIN

    @pl.when(i == 0)
    def _():
        stg_ref[...] = jnp.zeros_like(stg_ref)
        std_ref[...] = jnp.zeros_like(std_ref)
        prev_ref[...] = jnp.zeros_like(prev_ref)

    small = z_ref[:, O_SM:O_SM + LANE]
    pre = jnp.dot(small, wg2_ref[...], precision=HI, preferred_element_type=F32) + bg2_ref[...]
    la_ref[...] = _log_sigmoid(pre) / GATE_NORM
    dla_ref[...] = -jnp.exp(alog_ref[...]) * _softplus(small + dtb_ref[...])
    beta_ref[...] = jax.nn.sigmoid(small)

    x = z_ref[:, O_DQKV:O_DQKV + C_CONV]
    x8 = x[0:8]
    p8 = prev_ref[...]
    row8 = lax.broadcasted_iota(jnp.int32, (8, C_CONV), 0)
    y = x * cw_ref[CONV_W - 1:CONV_W, :]
    y8 = x8 * cw_ref[CONV_W - 1:CONV_W, :]
    for s in range(1, CONV_W):
        wrow = cw_ref[CONV_W - 1 - s:CONV_W - s, :]
        y = y + pltpu.roll(x, s, 0) * wrow
        y8 = y8 + jnp.where(row8 < s, pltpu.roll(p8, s, 0), pltpu.roll(x8, s, 0)) * wrow
    prev_ref[...] = x[tc - 8:tc]
    y = _silu(y)
    y8 = _silu(y8)
    yqk = y[:, 0:2 * KEY_C]
    nrm = lax.rsqrt(_group_sumsq(yqk, DK_D) + EPS)
    qscale = jnp.where(lax.broadcasted_iota(jnp.int32, (1, 2 * KEY_C), 1) < KEY_C, DK_D ** -0.5, 1.0)
    qkv_ref[:, 0:2 * KEY_C] = yqk * nrm * qscale
    qkv_ref[:, 2 * KEY_C:] = y[:, 2 * KEY_C:]
    yqk8 = y8[:, 0:2 * KEY_C]
    qkv_ref[0:8, 0:2 * KEY_C] = yqk8 * lax.rsqrt(_group_sumsq(yqk8, DK_D) + EPS) * qscale
    qkv_ref[0:8, 2 * KEY_C:] = y8[:, 2 * KEY_C:]

    ri = lax.broadcasted_iota(jnp.int32, (c, c), 0)
    ci = lax.broadcasted_iota(jnp.int32, (c, c), 1)
    tril = ri >= ci
    strict = ri > ci
    lower = tril.astype(BF16)
    upper = (ri <= ci).astype(BF16)
    lane_head = lax.broadcasted_iota(jnp.int32, (c, LANE), 1) // DK_C
    lane_head_row = lax.broadcasted_iota(jnp.int32, (1, LANE), 1) // DK_C

    def chunk(ch, carry):
        rows = pl.ds(pl.multiple_of(ch * c, c), c)
        n_pair = H_C // PAIR
        pair_of = [divmod(h, PAIR) for h in range(H_C)]
        masks = [lane_head == hh for hh in range(PAIR)]
        bcum_all = _select_dot(lower, la_ref[rows, :], 3)
        dla_parts = _split_bf16(dla_ref[rows, :], 3)
        g_cols = _dot(lower, dla_parts[0]) + _dot(lower, dla_parts[1]) + _dot(lower, dla_parts[2])
        g_rows = _dot_tn(dla_parts[0], upper) + _dot_tn(dla_parts[1], upper) + _dot_tn(dla_parts[2], upper)
        st_d = [std_ref[p] for p in range(n_pair)]
        st_db = [s.astype(BF16) for s in st_d]
        dn = []
        for h, (p, hh) in enumerate(pair_of):
            g_col = jnp.broadcast_to(g_cols[:, SM_DA + h:SM_DA + h + 1], (c, LANE))
            g_row = jnp.broadcast_to(g_rows[SM_DA + h:SM_DA + h + 1, :], (c, c))
            beta_b = jnp.broadcast_to(beta_ref[rows, SM_DB + h:SM_DB + h + 1], (c, LANE))
            km = jnp.where(masks[hh], qkv_ref[rows, KEY_C + p * LANE:KEY_C + (p + 1) * LANE], 0.0)
            qm = jnp.where(masks[hh], qkv_ref[rows, p * LANE:(p + 1) * LANE], 0.0)
            dn.append(dict(g_col=g_col, g_row=g_row, beta_b=beta_b, km=km, qm=qm, kmb=km.astype(BF16),
                           kb=km * beta_b, eg=jnp.exp(g_col), g_last=g_col[c - 1:c, :]))
        kk = [_dot_nt(d["kb"].astype(BF16), d["kmb"]) for d in dn]
        qk = [_dot_nt(d["qm"].astype(BF16), d["kmb"]) for d in dn]
        d_inter = [_dot_nt((d["qm"] * d["eg"]).astype(BF16), st_db[p]) for d, (p, hh) in zip(dn, pair_of)]
        decay = [jnp.exp(jnp.where(tril, d["g_col"][:, 0:c] - d["g_row"], NEG)) for d in dn]
        a_mats = [jnp.where(strict, m * dc, 0.0) for m, dc in zip(kk, decay)]
        st_g = [stg_ref[p] for p in range(n_pair)]
        st_gb = [s.astype(BF16) for s in st_g]
        gl_pair = []
        for p in range(n_pair):
            bcum = bcum_all[:, p * LANE:(p + 1) * LANE]
            b_end = bcum[c - 1:c, :]
            k = z_ref[rows, O_CK + p * LANE:O_CK + (p + 1) * LANE]
            gl_pair.append(dict(
                q_in=z_ref[rows, O_CQ + p * LANE:O_CQ + (p + 1) * LANE] * (DK_C ** -0.5) * jnp.exp(bcum),
                k_in=(k * jnp.exp(-bcum)).astype(BF16), k_end=k * jnp.exp(b_end - bcum), b_end=b_end))
        g_qm = [jnp.where(masks[hh], gl_pair[p]["q_in"], 0.0).astype(BF16) for p, hh in pair_of]
        g_ke = [jnp.where(masks[hh], gl_pair[p]["k_end"], 0.0).astype(BF16) for p, hh in pair_of]
        g_vb = [z_ref[rows, O_CV + h * DV_C:O_CV + (h + 1) * DV_C].astype(BF16) for h in range(H_C)]
        g_sc = [_dot_nt(g_qm[h], gl_pair[p]["k_in"]) for h, (p, hh) in enumerate(pair_of)]
        g_inter = [_dot_nt(g_qm[h], st_gb[p]) for h, (p, hh) in enumerate(pair_of)]
        g_upd = [_dot_tn(g_vb[h], g_ke[h]) for h in range(H_C)]
        g_intra = [_dot(jnp.where(tril, g_sc[h], 0.0).astype(BF16), g_vb[h]) for h in range(H_C)]
        t_invs = _unit_lower_inverses(a_mats)
        for h in range(H_C):
            gate = _silu(z_ref[rows, O_CR + h * DV_C:O_CR + (h + 1) * DV_C])
            o_ref[rows, h * DV_C:(h + 1) * DV_C] = (_rms(g_intra[h] + g_inter[h], gnorm_ref[...]) * gate).astype(BF16)
        for p in range(n_pair):
            stg_ref[p] = st_g[p] * jnp.exp(gl_pair[p]["b_end"]) + g_upd[p * PAIR] + g_upd[p * PAIR + 1]
        rhs = [jnp.concatenate([qkv_ref[rows, 2 * KEY_C + h * DV_D:2 * KEY_C + (h + 1) * DV_D] * d["beta_b"],
                                d["kb"] * d["eg"]], axis=1).astype(BF16) for h, d in enumerate(dn)]
        sol = [_dot(t.astype(BF16), r) for t, r in zip(t_invs, rhs)]
        w_st = [_dot_nt(s[:, DV_D:].astype(BF16), st_db[p]) for s, (p, hh) in zip(sol, pair_of)]
        v_new = [(s[:, 0:DV_D] - ws).astype(BF16) for s, ws in zip(sol, w_st)]
        d_intra = [_dot(jnp.where(tril, q * dc, 0.0).astype(BF16), vn) for q, dc, vn in zip(qk, decay, v_new)]
        d_upd = [_dot_tn(vn, (d["km"] * jnp.exp(d["g_last"] - d["g_col"])).astype(BF16)) for vn, d in zip(v_new, dn)]
        for h in range(H_D):
            gate = _silu(z_ref[rows, O_DG + h * DV_D:O_DG + (h + 1) * DV_D])
            col = H_C * DV_C + h * DV_D
            o_ref[rows, col:col + DV_D] = (_rms(d_inter[h] + d_intra[h], dnorm_ref[...]) * gate).astype(BF16)
        for p in range(n_pair):
            dec_row = jnp.where(lane_head_row == 0, jnp.exp(dn[p * PAIR]["g_last"]), jnp.exp(dn[p * PAIR + 1]["g_last"]))
            std_ref[p] = st_d[p] * dec_row + d_upd[p * PAIR] + d_upd[p * PAIR + 1]
        return carry

    lax.fori_loop(0, tc // c, chunk, 0)

    @pl.when(i == pl.num_programs(1) - 1)
    def _():
        tail_ref[0] = prev_ref[...]
        for p in range(H_C // PAIR):
            tg = stg_ref[p].T
            td = std_ref[p].T
            for hh in range(PAIR):
                sg_ref[0, p * PAIR + hh] = tg[hh * DK_C:(hh + 1) * DK_C, :]
                sd_ref[0, p * PAIR + hh] = td[hh * DK_D:(hh + 1) * DK_D, :]


def _odd_mixer_prompt(z, p, batch, seq, tc):
    nt = seq // tc
    full = lambda a: pl.BlockSpec(a.shape, lambda b, i: (0,) * a.ndim)
    consts = [p["w_g2_pad"], p["b_g2"], p["gla_norm"], p["conv_w"], p["alog_slab"], p["dtb_slab"], p["delta_norm"]]
    st_spec = pl.BlockSpec((1, H_C, DK_C, DV_C), lambda b, i: (b, 0, 0, 0))
    return pl.pallas_call(
        _odd_mixer_prompt_kernel,
        grid=(batch, nt),
        in_specs=[pl.BlockSpec((tc, IN_ODD_PAD), lambda b, i: (b * nt + i, 0))] + [full(a) for a in consts],
        out_specs=[pl.BlockSpec((tc, D_MODEL), lambda b, i: (b * nt + i, 0)), st_spec, st_spec,
                   pl.BlockSpec((1, 8, C_CONV), lambda b, i: (b, 0, 0))],
        out_shape=[jax.ShapeDtypeStruct((batch * seq, D_MODEL), BF16),
                   jax.ShapeDtypeStruct((batch, H_C, DK_C, DV_C), F32),
                   jax.ShapeDtypeStruct((batch, H_D, DK_D, DV_D), F32),
                   jax.ShapeDtypeStruct((batch, 8, C_CONV), F32)],
        scratch_shapes=[pltpu.VMEM((H_C // PAIR, DV_C, LANE), F32),
                        pltpu.VMEM((H_D // PAIR, DV_D, LANE), F32),
                        pltpu.VMEM((8, C_CONV), F32),
                        pltpu.VMEM((tc, KEY_C), F32),
                        pltpu.VMEM((tc, C_CONV), F32),
                        pltpu.VMEM((tc, LANE), F32),
                        pltpu.VMEM((tc, LANE), F32)],
        compiler_params=_cparams("parallel", "arbitrary"),
        name="odd_mixer_prompt",
    )(z, *consts)


M_E1, M_E2, M_R1, M_R2, M_G1, M_G2 = (N_EXPERTS + j for j in range(6))


def _proj_router_kernel(mix_ref, x_ref, wo_ref, g_ref, wr_ref, x1_ref, hn_ref, meta_ref, cnt_ref, carry_ref):
    tm = x_ref.shape[0]

    @pl.when(pl.program_id(0) == 0)
    def _():
        carry_ref[...] = jnp.zeros_like(carry_ref)

    x1 = x_ref[...] + _dot(mix_ref[...], wo_ref[...])
    x1_ref[...] = x1
    hn = _rms(x1, g_ref[...])
    hn_ref[...] = hn.astype(hn_ref.dtype)
    lane = lax.broadcasted_iota(jnp.int32, (tm, LANE), 1)
    hn_hi, hn_lo = _split_bf16(hn, 2)
    wr_hi, wr_lo = _split_bf16(wr_ref[...], 2)
    logits = _dot(hn_hi, wr_hi) + _dot(hn_hi, wr_lo) + _dot(hn_lo, wr_hi)
    logits = jnp.where(lane < N_EXPERTS, logits, NEG)
    m1 = jnp.max(logits, axis=1, keepdims=True)
    e1 = jnp.min(jnp.where(logits == m1, lane, LANE), axis=1, keepdims=True)
    rest = jnp.where(lane == e1, NEG, logits)
    m2 = jnp.max(rest, axis=1, keepdims=True)
    e2 = jnp.min(jnp.where(rest == m2, lane, LANE), axis=1, keepdims=True)
    t = jnp.exp(m2 - m1)
    g1 = 1.0 / (1.0 + t)
    g2 = t / (1.0 + t)
    oh1 = lane == e1
    oh2 = lane == e2
    member = (oh1 | oh2).astype(F32)
    ri = lax.broadcasted_iota(jnp.int32, (tm, tm), 0)
    ci = lax.broadcasted_iota(jnp.int32, (tm, tm), 1)
    before = _dot((ri > ci).astype(BF16), member.astype(BF16)) + carry_ref[...]
    r1 = jnp.sum(jnp.where(oh1, before, 0.0), axis=1, keepdims=True)
    r2 = jnp.sum(jnp.where(oh2, before, 0.0), axis=1, keepdims=True)
    carry_ref[...] = carry_ref[...] + jnp.sum(member, axis=0, keepdims=True)
    cnt_ref[...] = carry_ref[...]
    meta = jnp.where(oh1, g1, 0.0) + jnp.where(oh2, g2, 0.0)
    meta = jnp.where(lane == M_E1, e1.astype(F32), meta)
    meta = jnp.where(lane == M_E2, e2.astype(F32), meta)
    meta = jnp.where(lane == M_R1, r1, meta)
    meta = jnp.where(lane == M_R2, r2, meta)
    meta = jnp.where(lane == M_G1, g1, meta)
    meta = jnp.where(lane == M_G2, g2, meta)
    meta_ref[...] = meta


def _proj_router(mix, x, w_out, g, w_router_pad, tm, hn_dtype):
    m, d = x.shape
    row = lambda width: pl.BlockSpec((tm, width), lambda i: (i, 0))
    full = lambda a: pl.BlockSpec(a.shape, lambda i: (0,) * a.ndim)
    return pl.pallas_call(
        _proj_router_kernel,
        grid=(m // tm,),
        in_specs=[row(d), row(d), full(w_out), full(g), full(w_router_pad)],
        out_specs=[row(d), row(d), row(LANE), pl.BlockSpec((1, LANE), lambda i: (0, 0))],
        out_shape=[jax.ShapeDtypeStruct((m, d), F32), jax.ShapeDtypeStruct((m, d), hn_dtype),
                   jax.ShapeDtypeStruct((m, LANE), F32), jax.ShapeDtypeStruct((1, LANE), F32)],
        scratch_shapes=[pltpu.VMEM((1, LANE), F32)],
        compiler_params=_cparams("arbitrary"),
        name="proj_router",
    )(mix, x, w_out, g, w_router_pad)


def _moe_dense_kernel(hn_ref, x1_ref, meta_ref, wg_ref, wu_ref, wd_ref, gf_ref, o_ref):
    e = pl.program_id(1)
    f = pl.program_id(2)

    @pl.when((e == 0) & (f == 0))
    def _():
        o_ref[...] = x1_ref[...]

    h = hn_ref[...]
    act = (_silu(_dot(h, wg_ref[...])) * _dot(h, wu_ref[...])).astype(BF16)
    lane = lax.broadcasted_iota(jnp.int32, meta_ref.shape, 1)
    gate = jnp.sum(jnp.where(lane == e, meta_ref[...], 0.0), axis=1, keepdims=True)
    o_ref[...] += gate * _dot(act, wd_ref[...])

    @pl.when((e == pl.num_programs(1) - 1) & (f == pl.num_programs(2) - 1))
    def _():
        o_ref[...] = _rms(o_ref[...], gf_ref[...])


def _moe_dense(hn, x1, meta, w_gate, w_up, w_down, g_final, tm):
    m, d = x1.shape
    n_e, _, ff = w_gate.shape
    tf = MOE_TF
    row = lambda width: pl.BlockSpec((tm, width), lambda i, e, f: (i, 0))
    return pl.pallas_call(
        _moe_dense_kernel,
        grid=(m // tm, n_e, ff // tf),
        in_specs=[row(d), row(d), row(LANE),
                  pl.BlockSpec((None, d, tf), lambda i, e, f: (e, 0, f)),
                  pl.BlockSpec((None, d, tf), lambda i, e, f: (e, 0, f)),
                  pl.BlockSpec((None, tf, d), lambda i, e, f: (e, f, 0)),
                  pl.BlockSpec((1, d), lambda i, e, f: (0, 0))],
        out_specs=row(d),
        out_shape=jax.ShapeDtypeStruct((m, d), F32),
        compiler_params=_cparams("parallel", "arbitrary", "arbitrary"),
        name="moe_dense",
    )(hn, x1, meta, w_gate, w_up, w_down, g_final)


def _moe_dispatch_kernel(dest_ref, hn_ref, xs_in_ref, xs_ref, sem):
    del xs_in_ref
    tm = hn_ref.shape[0]
    base = pl.program_id(0) * (TOP_K * tm)

    def row_copy(t, k):
        return pltpu.make_async_copy(hn_ref.at[pl.ds(t, 1)], xs_ref.at[pl.ds(dest_ref[base + k * tm + t], 1)], sem)

    def issue(t, carry):
        for k in range(TOP_K):
            row_copy(t, k).start()
        return carry

    def drain(t, carry):
        for k in range(TOP_K):
            row_copy(t, k).wait()
        return carry

    lax.fori_loop(0, tm, issue, 0, unroll=DMA_UNROLL)
    lax.fori_loop(0, tm, drain, 0, unroll=DMA_UNROLL)


def _moe_dispatch(dest, hn, n_rows, tm):
    m, d = hn.shape
    return pl.pallas_call(
        _moe_dispatch_kernel,
        grid_spec=pltpu.PrefetchScalarGridSpec(
            num_scalar_prefetch=1,
            grid=(m // tm,),
            in_specs=[pl.BlockSpec((tm, d), lambda i, dest: (i, 0)), pl.BlockSpec(memory_space=pl.ANY)],
            out_specs=pl.BlockSpec(memory_space=pl.ANY),
            scratch_shapes=[pltpu.SemaphoreType.DMA(())]),
        out_shape=jax.ShapeDtypeStruct((n_rows, d), hn.dtype),
        input_output_aliases={2: 0},
        compiler_params=_cparams("arbitrary"),
        name="moe_dispatch",
    )(dest, hn, jnp.zeros((n_rows, d), hn.dtype))


def _moe_grouped_kernel(te_ref, nv_ref, xs_ref, wg_ref, wu_ref, wd_ref, o_ref, xb_ref):
    del te_ref
    f = pl.program_id(1)

    @pl.when(f == 0)
    def _():
        o_ref[...] = jnp.zeros_like(o_ref)
        xb_ref[...] = xs_ref[...].astype(BF16)

    @pl.when(pl.program_id(0) < nv_ref[0])
    def _():
        h = xb_ref[...]
        act = (_silu(_dot(h, wg_ref[...])) * _dot(h, wu_ref[...])).astype(BF16)
        o_ref[...] += _dot(act, wd_ref[...])


def _moe_grouped(tile_expert, n_valid, xs, w_gate, w_up, w_down, tg):
    rows, d = xs.shape
    tf = MOE_TF
    nf = w_gate.shape[2] // tf
    fidx = lambda r, f, nv: jnp.where(r < nv[0], f, nf - 1)
    return pl.pallas_call(
        _moe_grouped_kernel,
        grid_spec=pltpu.PrefetchScalarGridSpec(
            num_scalar_prefetch=2,
            grid=(rows // tg, nf),
            in_specs=[pl.BlockSpec((tg, d), lambda r, f, te, nv: (r, 0)),
                      pl.BlockSpec((None, d, tf), lambda r, f, te, nv: (te[r], 0, fidx(r, f, nv))),
                      pl.BlockSpec((None, d, tf), lambda r, f, te, nv: (te[r], 0, fidx(r, f, nv))),
                      pl.BlockSpec((None, tf, d), lambda r, f, te, nv: (te[r], fidx(r, f, nv), 0))],
            out_specs=pl.BlockSpec((tg, d), lambda r, f, te, nv: (r, 0)),
            scratch_shapes=[pltpu.VMEM((tg, d), BF16)]),
        out_shape=jax.ShapeDtypeStruct((rows, d), F32),
        compiler_params=_cparams("parallel", "arbitrary"),
        name="moe_grouped",
    )(tile_expert, n_valid, xs, w_gate, w_up, w_down)


def _moe_combine_kernel(dest_ref, ys_ref, x1_ref, meta_ref, gf_ref, o_ref, buf_ref, sem):
    i = pl.program_id(0)
    tc = x1_ref.shape[0]
    rows = TOP_K * tc

    def row_copy(step, slot, j):
        return pltpu.make_async_copy(ys_ref.at[pl.ds(dest_ref[step * rows + j], 1)],
                                     buf_ref.at[slot, pl.ds(j, 1)], sem.at[slot])

    def issue(step, slot):
        def body(j, carry):
            row_copy(step, slot, j).start()
            return carry
        lax.fori_loop(0, rows, body, 0, unroll=DMA_UNROLL)

    def drain(step, slot):
        def body(j, carry):
            row_copy(step, slot, j).wait()
            return carry
        lax.fori_loop(0, rows, body, 0, unroll=DMA_UNROLL)

    slot = i % 2

    @pl.when(i == 0)
    def _():
        issue(0, 0)

    @pl.when(i + 1 < pl.num_programs(0))
    def _():
        issue(i + 1, 1 - slot)

    drain(i, slot)
    g1 = meta_ref[:, M_G1:M_G1 + 1]
    g2 = meta_ref[:, M_G2:M_G2 + 1]
    y = x1_ref[...] + g1 * buf_ref[slot, 0:tc] + g2 * buf_ref[slot, tc:rows]
    o_ref[...] = _rms(y, gf_ref[...])


def _moe_combine(dest, ys, x1, meta, g_final, tc):
    m, d = x1.shape
    row = lambda width: pl.BlockSpec((tc, width), lambda i, dest: (i, 0))
    return pl.pallas_call(
        _moe_combine_kernel,
        grid_spec=pltpu.PrefetchScalarGridSpec(
            num_scalar_prefetch=1,
            grid=(m // tc,),
            in_specs=[pl.BlockSpec(memory_space=pl.ANY), row(d), row(LANE),
                      pl.BlockSpec((1, d), lambda i, dest: (0, 0))],
            out_specs=row(d),
            scratch_shapes=[pltpu.VMEM((2, TOP_K * tc, d), F32), pltpu.SemaphoreType.DMA((2,))]),
        out_shape=jax.ShapeDtypeStruct((m, d), F32),
        compiler_params=_cparams("arbitrary"),
        name="moe_combine",
    )(dest, ys, x1, meta, g_final)


def _moe_routes(meta, counts, tile, tg):
    m = meta.shape[0]
    n_rows = TOP_K * m + N_EXPERTS * tg
    expert = meta[:, M_E1:M_E2 + 1].astype(jnp.int32)
    rank = meta[:, M_R1:M_R2 + 1].astype(jnp.int32)
    padded = (counts[0, :N_EXPERTS].astype(jnp.int32) + tg - 1) // tg * tg
    ends = jnp.cumsum(padded)
    dest = (ends - padded)[expert] + rank
    dest = jnp.transpose(dest.reshape(m // tile, tile, TOP_K), (0, 2, 1)).reshape(-1)
    n_valid = ends[-1] // tg
    tile_start = jnp.arange(n_rows // tg, dtype=jnp.int32) * tg
    tile_expert = jnp.searchsorted(ends, jnp.minimum(tile_start, ends[-1] - 1), side="right").astype(jnp.int32)
    return dest, tile_expert, n_valid.reshape(1).astype(jnp.int32), n_rows


def _odd_params(i, od_norm_mix, od_w_in, od_gla_w_gate2, od_gla_b_gate2, od_gla_norm, od_delta_conv,
                od_delta_a_log, od_delta_dt_bias, od_delta_norm, od_w_out, od_norm_ffn, od_router,
                od_w_gate, od_w_up, od_w_down):
    w = od_w_in[i]
    sizes = (KEY_C, KEY_C, H_C * DV_C, GATE_RANK, H_C * DV_C, C_CONV, H_D, H_D, H_D * DV_D)
    splits = tuple(sum(sizes[:j + 1]) for j in range(len(sizes) - 1))
    cq, ck, cv, c_lr, c_r, d_qkv, d_a, d_b, d_g = jnp.split(w, splits, axis=1)
    pad = jnp.zeros((w.shape[0], IN_ODD_PAD - w.shape[1]), w.dtype)
    w_in = jnp.concatenate([cq, ck, cv, c_r, d_qkv, d_g, c_lr, d_a, d_b, pad], axis=1).astype(BF16)
    slab = lambda v: jnp.zeros((1, LANE), F32).at[0, SM_DA:SM_DA + H_D].set(v)
    return dict(norm_mix=od_norm_mix[i][None], w_in=w_in,
                w_g2_pad=jnp.zeros((LANE, KEY_C), F32).at[:GATE_RANK].set(od_gla_w_gate2[i]),
                b_g2=od_gla_b_gate2[i][None], gla_norm=od_gla_norm[i][None], conv_w=od_delta_conv[i],
                alog_slab=slab(od_delta_a_log[i]), dtb_slab=slab(od_delta_dt_bias[i]),
                a_log=od_delta_a_log[i], dt_bias=od_delta_dt_bias[i],
                delta_norm=od_delta_norm[i][None], w_out=od_w_out[i].astype(BF16),
                norm_ffn=od_norm_ffn[i][None],
                router=jnp.zeros((D_MODEL, LANE), F32).at[:, :N_EXPERTS].set(od_router[i]),
                w_gate=od_w_gate[i].astype(BF16), w_up=od_w_up[i].astype(BF16), w_down=od_w_down[i].astype(BF16))


MOE_GROUP_TILE = 512
MOE_ROW_TILE = 256


def _moe(mix, x, p, g_final, tm):
    m = x.shape[0]
    if m < N_EXPERTS * MOE_GROUP_TILE:
        x1, hn, meta, _ = _proj_router(mix, x, p["w_out"], p["norm_ffn"], p["router"], tm, BF16)
        return _moe_dense(hn, x1, meta, p["w_gate"], p["w_up"], p["w_down"], g_final, tm)
    x1, hn, meta, counts = _proj_router(mix, x, p["w_out"], p["norm_ffn"], p["router"], tm, F32)
    dest, tile_expert, n_valid, n_rows = _moe_routes(meta, counts, MOE_ROW_TILE, MOE_GROUP_TILE)
    xs = _moe_dispatch(dest, hn, n_rows, MOE_ROW_TILE)
    ys = _moe_grouped(tile_expert, n_valid, xs, p["w_gate"], p["w_up"], p["w_down"], MOE_GROUP_TILE)
    return _moe_combine(dest, ys, x1, meta, g_final, MOE_ROW_TILE)


def _odd_layer_prompt(x, p, g_final, batch, seq):
    z = _norm_matmul(x, p["norm_mix"], p["w_in"], 512)
    mix, s_gla, s_delta, tail = _odd_mixer_prompt(z, p, batch, seq, 256)
    y = _moe(mix, x, p, g_final, 512)
    return y, s_gla, s_delta, tail[:, 8 - (CONV_W - 1):]


def _even_epilogue_sample_kernel(z_ref, c_ref, su_ref, sd_ref, lng_ref, lnb_ref, w0_ref, b0_ref,
                                 q_ref, kv_ref, gv_ref, bout_ref):
    c, su, sd = c_ref[...], su_ref[...], sd_ref[...]
    q_ref[...] = _rope(z_ref[:, 0:QKV_A], c, su, sd) * (D_A ** -0.5)
    kv_ref[:, 0:QKV_A] = _rope(z_ref[:, QKV_A:2 * QKV_A], c, su, sd)
    kv_ref[:, QKV_A:2 * QKV_A] = z_ref[:, 2 * QKV_A:3 * QKV_A]
    u = _gelu(z_ref[:, 3 * QKV_A:3 * QKV_A + D_B])
    gv = _layernorm(_gelu(z_ref[:, 3 * QKV_A + D_B:3 * QKV_A + 2 * D_B]), lng_ref[...], lnb_ref[...])
    gv_ref[...] = gv
    bout_ref[...] = (u * (gv * w0_ref[...] + b0_ref[...])).astype(BF16)


def _even_epilogue_sample(z, tables, ln_g, ln_b, w_sp, b_sp):
    m = z.shape[0]
    w0 = jnp.repeat(w_sp[:, 0, 0], D_BG)[None]
    b0 = jnp.repeat(b_sp[:, 0], D_BG)[None]
    return pl.pallas_call(
        _even_epilogue_sample_kernel,
        out_shape=[jax.ShapeDtypeStruct((m, QKV_A), F32), jax.ShapeDtypeStruct((m, 2 * QKV_A), F32),
                   jax.ShapeDtypeStruct((m, D_B), F32), jax.ShapeDtypeStruct((m, D_B), BF16)],
        compiler_params=pltpu.CompilerParams(vmem_limit_bytes=VMEM_LIMIT),
        name="even_epilogue_sample",
    )(z, *tables, ln_g, ln_b, w0, b0)


def _moba_sample_kernel(pt_ref, qt_ref, knt_ref, vnt_ref, *refs):
    del pt_ref
    n_pages = len(refs) - 2
    page_refs, o_ref, s_ref = refs[:n_pages], refs[n_pages], refs[n_pages + 1]
    pages_per_block = MOBA_BLOCK // PAGE_SIZE
    nb = n_pages // pages_per_block
    qt = qt_ref[0]
    for h in range(H_A):
        qcol = jnp.broadcast_to(qt[:, h:h + 1], (D_A, PAGE_SIZE))
        for j in range(n_pages):
            s_ref[h, j:j + 1, :] = jnp.sum(page_refs[j][0, 0, h] * qcol, axis=0, keepdims=True)
    lane = lax.broadcasted_iota(jnp.int32, (n_pages, LANE), 1)
    page_sums = jnp.zeros((n_pages, LANE), F32)
    for h in range(H_A):
        page_sums = jnp.where(lane == h, jnp.sum(s_ref[h], axis=1, keepdims=True), page_sums)
    pair = (lax.broadcasted_iota(jnp.int32, (nb, n_pages), 1) // pages_per_block
            == lax.broadcasted_iota(jnp.int32, (nb, n_pages), 0)).astype(F32)
    pair_t = (lax.broadcasted_iota(jnp.int32, (n_pages, nb), 0) // pages_per_block
              == lax.broadcasted_iota(jnp.int32, (n_pages, nb), 1)).astype(F32)
    gate = jnp.dot(pair, page_sums, precision=HI, preferred_element_type=F32)
    sel = _moba_select(gate, nb)
    sel_pages = jnp.dot(pair_t, sel, precision=HI, preferred_element_type=F32)
    own = jnp.sum(qt * knt_ref[0], axis=0, keepdims=True)
    vnt = vnt_ref[0]
    out_lane = lax.broadcasted_iota(jnp.int32, (D_A, LANE), 1)
    out = jnp.zeros((D_A, LANE), F32)
    for h in range(H_A):
        sm = jnp.where(sel_pages[:, h:h + 1] > 0.0, s_ref[h], NEG)
        s_own = own[:, h:h + 1]
        mx = jnp.maximum(jnp.max(jnp.max(sm, axis=1, keepdims=True), axis=0, keepdims=True), s_own)
        p = jnp.exp(sm - mx)
        p_own = jnp.exp(s_own - mx)
        denom = jnp.sum(jnp.sum(p, axis=1, keepdims=True), axis=0, keepdims=True) + p_own
        acc = jnp.zeros((D_A, PAGE_SIZE), F32)
        for j in range(n_pages):
            acc = acc + page_refs[j][0, 1, h] * p[j:j + 1, :]
        o = (jnp.sum(acc, axis=1, keepdims=True) + p_own * vnt[:, h:h + 1]) / denom
        out = jnp.where(out_lane == h, o, out)
    o_ref[0] = out


def _moba_sample(q, kv_new, cache, page_table):
    bs, n_pages = page_table.shape
    assert (n_pages * PAGE_SIZE) % MOBA_BLOCK == 0
    cache_t = jnp.transpose(cache, (0, 2, 3, 4, 1))
    page_spec = lambda j: pl.BlockSpec((1, 2, H_A, D_A, PAGE_SIZE),
                                       lambda b, pt: (pt[b * n_pages + j], 0, 0, 0, 0))
    col_spec = pl.BlockSpec((1, D_A, H_A), lambda b, pt: (b, 0, 0))
    heads_t = lambda a: jnp.transpose(a.reshape(bs, H_A, D_A), (0, 2, 1))
    out = pl.pallas_call(
        _moba_sample_kernel,
        grid_spec=pltpu.PrefetchScalarGridSpec(
            num_scalar_prefetch=1,
            grid=(bs,),
            in_specs=[col_spec, col_spec, col_spec] + [page_spec(j) for j in range(n_pages)],
            out_specs=pl.BlockSpec((1, D_A, LANE), lambda b, pt: (b, 0, 0)),
            scratch_shapes=[pltpu.VMEM((H_A, n_pages, PAGE_SIZE), F32)]),
        out_shape=jax.ShapeDtypeStruct((bs, D_A, LANE), F32),
        compiler_params=_cparams("parallel"),
        name="moba_sample",
    )(page_table.reshape(-1), heads_t(q), heads_t(kv_new[:, 0:QKV_A]), heads_t(kv_new[:, QKV_A:]),
      *([cache_t] * n_pages))
    return jnp.transpose(out[:, :, 0:H_A], (0, 2, 1)).reshape(bs, QKV_A).astype(BF16)


def _odd_mixer_sample_kernel(z_ref, buf_ref, sg_ref, sd_ref, wg2_ref, bg2_ref, gnorm_ref, cw_ref, alog_ref, dtb_ref,
                             dnorm_ref, o_ref, sgo_ref, sdo_ref):
    bt = z_ref.shape[0]
    stride = H_C * DK_C
    small = z_ref[:, O_SM:O_SM + LANE]
    pre = jnp.dot(small, wg2_ref[...], precision=HI, preferred_element_type=F32) + bg2_ref[...]
    a_all = jnp.exp(_log_sigmoid(pre) / GATE_NORM)
    g_all = -jnp.exp(alog_ref[...]) * _softplus(small + dtb_ref[...])
    beta_all = jax.nn.sigmoid(small)
    x = z_ref[:, O_DQKV:O_DQKV + C_CONV]
    y = x * cw_ref[CONV_W - 1:CONV_W, :]
    for j in range(CONV_W - 1):
        y = y + buf_ref[:, j, :] * cw_ref[j:j + 1, :]
    y = _silu(y)
    for h in range(H_C):
        ks = slice(h * DK_C, (h + 1) * DK_C)
        a = a_all[:, ks]
        q = z_ref[:, O_CQ + h * DK_C:O_CQ + (h + 1) * DK_C] * (DK_C ** -0.5)
        k = z_ref[:, O_CK + h * DK_C:O_CK + (h + 1) * DK_C]
        v = z_ref[:, O_CV + h * DV_C:O_CV + (h + 1) * DV_C]
        qa = q * a
        acc = jnp.sum(q * k, axis=1, keepdims=True) * v
        for kk in range(DK_C):
            rows = pl.ds(h * DK_C + kk, bt, stride=stride)
            srow = sg_ref[rows, :]
            acc = acc + qa[:, kk:kk + 1] * srow
            sgo_ref[rows, :] = a[:, kk:kk + 1] * srow + k[:, kk:kk + 1] * v
        gate = _silu(z_ref[:, O_CR + h * DV_C:O_CR + (h + 1) * DV_C])
        o_ref[:, h * DV_C:(h + 1) * DV_C] = (_rms(acc, gnorm_ref[...]) * gate).astype(BF16)
        yq = y[:, h * DK_D:(h + 1) * DK_D]
        yk = y[:, KEY_C + h * DK_D:KEY_C + (h + 1) * DK_D]
        dv = y[:, 2 * KEY_C + h * DV_D:2 * KEY_C + (h + 1) * DV_D]
        dq = yq * lax.rsqrt(jnp.sum(yq * yq, axis=1, keepdims=True) + EPS) * (DK_D ** -0.5)
        dk = yk * lax.rsqrt(jnp.sum(yk * yk, axis=1, keepdims=True) + EPS)
        beta = beta_all[:, SM_DB + h:SM_DB + h + 1]
        eg = jnp.exp(g_all[:, SM_DA + h:SM_DA + h + 1])
        w = dk * (beta * eg)
        qd = dq * eg
        ws = jnp.zeros((bt, DV_D), F32)
        qs = jnp.zeros((bt, DV_D), F32)
        for kk in range(DK_D):
            srow = sd_ref[pl.ds(h * DK_D + kk, bt, stride=stride), :]
            ws = ws + w[:, kk:kk + 1] * srow
            qs = qs + qd[:, kk:kk + 1] * srow
        v_new = dv * beta - ws
        o = qs + jnp.sum(dq * dk, axis=1, keepdims=True) * v_new
        for kk in range(DK_D):
            rows = pl.ds(h * DK_D + kk, bt, stride=stride)
            sdo_ref[rows, :] = sd_ref[rows, :] * eg + dk[:, kk:kk + 1] * v_new
        gate = _silu(z_ref[:, O_DG + h * DV_D:O_DG + (h + 1) * DV_D])
        col = H_C * DV_C + h * DV_D
        o_ref[:, col:col + DV_D] = (_rms(o, dnorm_ref[...]) * gate).astype(BF16)


def _odd_mixer_sample(z, conv_buf, s_gla, s_delta, p, bt):
    bs = z.shape[0]
    rows = H_C * DK_C
    full = lambda a: pl.BlockSpec(a.shape, lambda i: (0,) * a.ndim)
    consts = [p["w_g2_pad"], p["b_g2"], p["gla_norm"], p["conv_w"], p["alog_slab"], p["dtb_slab"], p["delta_norm"]]
    st_spec = pl.BlockSpec((bt * rows, DV_C), lambda i: (i, 0))
    mix, sg, sd = pl.pallas_call(
        _odd_mixer_sample_kernel,
        grid=(bs // bt,),
        in_specs=[pl.BlockSpec((bt, IN_ODD_PAD), lambda i: (i, 0)),
                  pl.BlockSpec((bt, CONV_W - 1, C_CONV), lambda i: (i, 0, 0)),
                  st_spec, st_spec] + [full(a) for a in consts],
        out_specs=[pl.BlockSpec((bt, D_MODEL), lambda i: (i, 0)), st_spec, st_spec],
        out_shape=[jax.ShapeDtypeStruct((bs, D_MODEL), BF16),
                   jax.ShapeDtypeStruct((bs * rows, DV_C), F32),
                   jax.ShapeDtypeStruct((bs * rows, DV_D), F32)],
        compiler_params=_cparams("parallel"),
        name="odd_mixer_sample",
    )(z, conv_buf, s_gla.reshape(bs * rows, DV_C), s_delta.reshape(bs * rows, DV_D), *consts)
    return mix, sg.reshape(s_gla.shape), sd.reshape(s_delta.shape)


def _sample_step(x, cache, page_table, s_gla, s_delta, conv_buf, ev, od, g_final):
    bs = x.shape[0]
    past = page_table.shape[1] * PAGE_SIZE
    z = _norm_matmul(x, ev["norm_mix"], ev["w_in"], bs)
    tables = _rope_tables(jnp.full((1,), past, jnp.int32))
    q, kv, gv, b_out = _even_epilogue_sample(z, tables, ev["ln_g"], ev["ln_b"], ev["w_sp"], ev["b_sp"])
    a_out = _moba_sample(q, kv, cache, page_table)
    x = _proj_ffn([a_out, b_out], x, ev["w_out"], ev["norm_ffn"], ev["w_gate"], ev["w_up"], ev["w_down"], bs)
    z = _norm_matmul(x, od["norm_mix"], od["w_in"], bs)
    mix, sg, sd = _odd_mixer_sample(z, conv_buf, s_gla, s_delta, od, 32)
    conv_new = jnp.concatenate([conv_buf[:, 1:], z[:, None, O_DQKV:O_DQKV + C_CONV]], axis=1)
    y = _moe(mix, x, od, g_final, bs)
    return y, kv, gv, sg, sd, conv_new


def kernel(x_prompt, x_sample, cache_kv, state_gla, state_delta, state_conv, page_table, ev_norm_mix, ev_w_in, ev_gmlp_ln_g, ev_gmlp_ln_b, ev_w_spatial, ev_b_spatial, ev_w_out, ev_norm_ffn, ev_w_gate, ev_w_up, ev_w_down, od_norm_mix, od_w_in, od_gla_w_gate2, od_gla_b_gate2, od_gla_norm, od_delta_conv, od_delta_a_log, od_delta_dt_bias, od_delta_norm, od_w_out, od_norm_ffn, od_router, od_w_gate, od_w_up, od_w_down, norm_final):
    bp, tp, d = x_prompt.shape
    ev = _even_params(0, ev_norm_mix, ev_w_in, ev_gmlp_ln_g, ev_gmlp_ln_b, ev_w_spatial, ev_b_spatial, ev_w_out,
                      ev_norm_ffn, ev_w_gate, ev_w_up, ev_w_down)
    od = _odd_params(0, od_norm_mix, od_w_in, od_gla_w_gate2, od_gla_b_gate2, od_gla_norm, od_delta_conv,
                     od_delta_a_log, od_delta_dt_bias, od_delta_norm, od_w_out, od_norm_ffn, od_router,
                     od_w_gate, od_w_up, od_w_down)
    bs, ts, _ = x_sample.shape
    assert ts == 1 and cache_kv.shape[0] == 1 and state_gla.shape[0] == 1
    xp, kv_p, gv_p = _even_layer_prompt(x_prompt.reshape(bp * tp, d), ev, bp, tp)
    yp, gla_p, dl_p, cv_p = _odd_layer_prompt(xp, od, norm_final[None], bp, tp)
    ys, kv_s, gv_s, gla_s, dl_s, cv_s = _sample_step(
        x_sample.reshape(bs, d), cache_kv[0], page_table, state_gla[0], state_delta[0], state_conv[0],
        ev, od, norm_final[None])
    return (yp.reshape(bp, tp, d), ys.reshape(bs, ts, d),
            kv_p[None], kv_s.reshape(1, bs, ts, 2, H_A, D_A),
            gv_p[None], gv_s.reshape(1, bs, ts, D_B),
            gla_p[None], gla_s[None], dl_p[None], dl_s[None], cv_p[None], cv_s[None])
```

```python
import functools
import math

import jax
import jax.numpy as jnp
from jax import lax
from jax.experimental import pallas as pl
from jax.experimental.pallas import tpu as pltpu

F32 = jnp.float32
BF16 = jnp.bfloat16
HI = lax.Precision.HIGHEST
EPS = 1e-6
NEG = -1e30

D_MODEL = 1024
PAGE_SIZE = 128
H_A, D_A = 8, 64
ROT_DIM = D_A // 4
ROPE_THETA = 500000.0
MOBA_BLOCK = 256
MOBA_TOPK = 3
G_B, D_BG = 8, 64
D_B = G_B * D_BG
CHUNK_B = 128
H_C, DK_C, DV_C = 4, 64, 128
GATE_RANK = 16
GATE_NORM = 16.0
H_D, DK_D, DV_D = 4, 64, 128
CONV_W = 4
C_CONV = 2 * H_D * DK_D + H_D * DV_D
CHUNK_LIN = 64
N_EXPERTS = 8
TOP_K = 2
QKV_A = H_A * D_A
IN_EVEN = 3 * QKV_A + 2 * D_B
FFN_TF = 1408
MOE_TF = 896
MOBA_LOOP_BLOCKS = 4
DMA_UNROLL = 8
LANE = 128
VMEM_LIMIT = 56 * 1024 * 1024


def _cparams(*sem):
    return pltpu.CompilerParams(dimension_semantics=sem, vmem_limit_bytes=VMEM_LIMIT)


def _rms(x, g):
    return x * lax.rsqrt(jnp.mean(x * x, axis=-1, keepdims=True) + EPS) * g


def _gelu(x):
    return 0.5 * x * (1.0 + lax.erf(x * (2.0 ** -0.5)))


def _silu(x):
    return x * jax.nn.sigmoid(x)


def _softplus(x):
    return jnp.maximum(x, 0.0) + jnp.log1p(jnp.exp(-jnp.abs(x)))


def _dot(a, b):
    return jnp.dot(a, b, preferred_element_type=F32)


def _dot_nt(a, b):
    return lax.dot_general(a, b, (((1,), (1,)), ((), ())), preferred_element_type=F32)


def _dot_tn(a, b):
    return lax.dot_general(a, b, (((0,), (0,)), ((), ())), preferred_element_type=F32)


def _norm_matmul_kernel(x_ref, g_ref, w_ref, o_ref):
    h = _rms(x_ref[...], g_ref[...]).astype(BF16)
    o_ref[...] = _dot(h, w_ref[...])


def _norm_matmul(x, g, w, tm):
    m, d = x.shape
    n = w.shape[1]
    return pl.pallas_call(
        _norm_matmul_kernel,
        grid=(m // tm,),
        in_specs=[pl.BlockSpec((tm, d), lambda i: (i, 0)),
                  pl.BlockSpec((1, d), lambda i: (0, 0)),
                  pl.BlockSpec((d, n), lambda i: (0, 0))],
        out_specs=pl.BlockSpec((tm, n), lambda i: (i, 0)),
        out_shape=jax.ShapeDtypeStruct((m, n), F32),
        compiler_params=_cparams("parallel"),
        name="norm_matmul",
    )(x, g, w)


def _rope_tables(pos):
    half = ROT_DIM // 2
    inv = ROPE_THETA ** (-jnp.arange(half, dtype=F32) / half)
    ang = pos.astype(F32)[:, None] * inv[None, :]
    cos, sin = jnp.cos(ang), jnp.sin(ang)
    t = pos.shape[0]
    one = jnp.ones((t, D_A - ROT_DIM), F32)
    zero_h = jnp.zeros((t, half), F32)
    zero_r = jnp.zeros((t, D_A - ROT_DIM), F32)
    c = jnp.concatenate([cos, cos, one], axis=1)
    s_up = jnp.concatenate([-sin, zero_h, zero_r], axis=1)
    s_dn = jnp.concatenate([zero_h, sin, zero_r], axis=1)
    rep = LANE // D_A
    return jnp.tile(c, (1, rep)), jnp.tile(s_up, (1, rep)), jnp.tile(s_dn, (1, rep))


def _rope(x, c, s_up, s_dn):
    half = ROT_DIM // 2
    outs = []
    for j in range(x.shape[1] // LANE):
        xs = x[:, j * LANE:(j + 1) * LANE]
        up = pltpu.roll(xs, LANE - half, 1)
        dn = pltpu.roll(xs, half, 1)
        outs.append(xs * c + up * s_up + dn * s_dn)
    return jnp.concatenate(outs, axis=1)


def _layernorm(x, g, b):
    mu = jnp.mean(x, axis=-1, keepdims=True)
    xc = x - mu
    var = jnp.mean(xc * xc, axis=-1, keepdims=True)
    return xc * lax.rsqrt(var + EPS) * g + b


def _even_epilogue_kernel(z_ref, c_ref, su_ref, sd_ref, lng_ref, lnb_ref, wsp_ref, bspt_ref,
                          qt_ref, k_ref, vt_ref, kv_ref, kmean_ref, gv_ref, bout_ref):
    tm = z_ref.shape[0]
    c, su, sd = c_ref[...], su_ref[...], sd_ref[...]
    q = _rope(z_ref[:, 0:QKV_A], c, su, sd) * (D_A ** -0.5)
    k = _rope(z_ref[:, QKV_A:2 * QKV_A], c, su, sd)
    v = z_ref[:, 2 * QKV_A:3 * QKV_A]
    qt_ref[0] = q.T.astype(BF16)
    k_ref[...] = k.astype(BF16)
    vt = v.T
    vt_ref[0] = vt.astype(BF16)
    kv_ref[0, 0:QKV_A, :] = k.T
    kv_ref[0, QKV_A:2 * QKV_A, :] = vt
    for blk in range(tm // MOBA_BLOCK):
        kmean_ref[blk] = jnp.mean(k[blk * MOBA_BLOCK:(blk + 1) * MOBA_BLOCK], axis=0, keepdims=True)
    u = _gelu(z_ref[:, 3 * QKV_A:3 * QKV_A + D_B])
    gv = _layernorm(_gelu(z_ref[:, 3 * QKV_A + D_B:3 * QKV_A + 2 * D_B]), lng_ref[...], lnb_ref[...])
    gv_ref[...] = gv
    gvb = gv.astype(BF16)
    row = lax.broadcasted_iota(jnp.int32, (CHUNK_B, CHUNK_B), 0)
    col = lax.broadcasted_iota(jnp.int32, (CHUNK_B, CHUNK_B), 1)
    group = lax.broadcasted_iota(jnp.int32, (CHUNK_B, D_B), 1) // D_BG
    w = [jnp.where(row >= col, wsp_ref[g], 0.0).astype(BF16) for g in range(G_B)]
    for ch in range(tm // CHUNK_B):
        gvc = gvb[ch * CHUNK_B:(ch + 1) * CHUNK_B]
        mixed = jnp.zeros((CHUNK_B, D_B), F32)
        for g in range(G_B):
            mixed = jnp.where(group == g, _dot(w[g], gvc) + bspt_ref[:, g:g + 1], mixed)
        bout_ref[ch * CHUNK_B:(ch + 1) * CHUNK_B, :] = (u[ch * CHUNK_B:(ch + 1) * CHUNK_B] * mixed).astype(BF16)


def _even_epilogue(z, tables, ln_g, ln_b, w_sp, b_sp, batch, seq, tm):
    n = batch * seq
    nt = seq // tm
    nblk = tm // MOBA_BLOCK
    tab_spec = pl.BlockSpec((tm, LANE), lambda b, i: (i, 0))
    row_spec = lambda width: pl.BlockSpec((tm, width), lambda b, i: (b * nt + i, 0))
    t_spec = pl.BlockSpec((1, QKV_A, tm), lambda b, i: (b, 0, i))
    return pl.pallas_call(
        _even_epilogue_kernel,
        grid=(batch, nt),
        in_specs=[row_spec(IN_EVEN), tab_spec, tab_spec, tab_spec,
                  pl.BlockSpec((1, D_B), lambda b, i: (0, 0)),
                  pl.BlockSpec((1, D_B), lambda b, i: (0, 0)),
                  pl.BlockSpec((G_B, CHUNK_B, CHUNK_B), lambda b, i: (0, 0, 0)),
                  pl.BlockSpec((CHUNK_B, G_B), lambda b, i: (0, 0))],
        out_specs=[t_spec, row_spec(QKV_A), t_spec, pl.BlockSpec((1, 2 * QKV_A, tm), lambda b, i: (b, 0, i)),
                   pl.BlockSpec((nblk, 1, QKV_A), lambda b, i: (b * nt + i, 0, 0)),
                   row_spec(D_B), row_spec(D_B)],
        out_shape=[jax.ShapeDtypeStruct((batch, QKV_A, seq), BF16),
                   jax.ShapeDtypeStruct((n, QKV_A), BF16),
                   jax.ShapeDtypeStruct((batch, QKV_A, seq), BF16),
                   jax.ShapeDtypeStruct((batch, 2 * QKV_A, seq), F32),
                   jax.ShapeDtypeStruct((n // MOBA_BLOCK, 1, QKV_A), F32),
                   jax.ShapeDtypeStruct((n, D_B), F32),
                   jax.ShapeDtypeStruct((n, D_B), BF16)],
        compiler_params=_cparams("parallel", "parallel"),
        name="even_epilogue",
    )(z, *tables, ln_g, ln_b, w_sp, b_sp.T)


def _moba_select(gate, n_own):
    nb = gate.shape[0]
    blk = lax.broadcasted_iota(jnp.int32, gate.shape, 0)
    elig = blk < n_own
    gm = jnp.where(elig, gate, NEG)
    rank = jnp.zeros(gate.shape, F32)
    for m in range(nb):
        gm_m = gm[m:m + 1, :]
        ahead = (gm_m > gm) | ((gm_m == gm) & (m < blk))
        rank = rank + ahead.astype(F32)
    return (elig & (rank < MOBA_TOPK)).astype(F32)


def _moba_prompt_kernel(qt_ref, k_ref, vt_ref, kmean_ref, o_ref, bias_ref):
    i = pl.program_id(2)
    tq = MOBA_BLOCK
    n_head = LANE // D_A
    n_split = tq // LANE
    own = pl.multiple_of(i * tq, tq)
    key_i = lax.broadcasted_iota(jnp.int32, (tq, LANE), 0)
    qry_i = lax.broadcasted_iota(jnp.int32, (tq, LANE), 1)
    head_lanes = [slice(hh * D_A, (hh + 1) * D_A) for hh in range(n_head)]
    streams = [(hh, qh) for hh in range(n_head) for qh in range(n_split)]
    qts = [qt_ref[0, head_lanes[hh], :] for hh in range(n_head)]
    queries = [qts[hh][:, qh * LANE:(qh + 1) * LANE] for hh, qh in streams]
    k_own = [k_ref[pl.ds(own, tq), head_lanes[hh]] for hh in range(n_head)]
    own_scores = [_dot(k_own[hh], queries[idx]) for idx, (hh, qh) in enumerate(streams)]
    gates = [jnp.dot(kmean_ref[0, :, head_lanes[hh]], qts[hh].astype(F32), precision=HI, preferred_element_type=F32)
             for hh in range(n_head)]
    for hh in range(n_head):
        bias = jnp.where(_moba_select(gates[hh], i) > 0.0, 0.0, NEG)
        for qh in range(n_split):
            bias_ref[hh, qh] = bias[:, qh * LANE:(qh + 1) * LANE]
    own_probs, own_stats = [], []
    for idx, (hh, qh) in enumerate(streams):
        s = jnp.where(key_i <= qry_i + qh * LANE, own_scores[idx], NEG)
        m = jnp.max(s, axis=0, keepdims=True)
        p = jnp.exp(s - m)
        own_probs.append(p.astype(BF16))
        own_stats.append((m, jnp.sum(p, axis=0, keepdims=True)))
    vt_own = [vt_ref[0, head_lanes[hh], pl.ds(own, tq)] for hh in range(n_head)]
    state = []
    for idx, (hh, qh) in enumerate(streams):
        state += [own_stats[idx][0], own_stats[idx][1], _dot(vt_own[hh], own_probs[idx])]

    def body(j0, carry, nk):
        tk = nk * tq
        start = pl.multiple_of(j0 * tq, tq)
        k_j = [k_ref[pl.ds(start, tk), head_lanes[hh]] for hh in range(n_head)]
        scores = [_dot(k_j[hh], queries[hh * n_split + qh]) for hh, qh in streams]
        probs, stats = [], []
        for idx, (hh, qh) in enumerate(streams):
            m, l = carry[3 * idx], carry[3 * idx + 1]
            s = [scores[idx][b * tq:(b + 1) * tq] + bias_ref[hh, qh, pl.ds(j0 + b, 1), :] for b in range(nk)]
            m_new = m
            for sb in s:
                m_new = jnp.maximum(m_new, jnp.max(sb, axis=0, keepdims=True))
            alpha = jnp.exp(m - m_new)
            p = [jnp.exp(sb - m_new) for sb in s]
            l_new = l * alpha
            for pb in p:
                l_new = l_new + jnp.sum(pb, axis=0, keepdims=True)
            probs.append(jnp.concatenate([pb.astype(BF16) for pb in p], axis=0))
            stats.append((m_new, l_new, alpha))
        vt_j = [vt_ref[0, head_lanes[hh], pl.ds(start, tk)] for hh in range(n_head)]
        pv = [_dot(vt_j[hh], probs[idx]) for idx, (hh, qh) in enumerate(streams)]
        out = []
        for idx in range(len(streams)):
            m_new, l_new, alpha = stats[idx]
            out += [m_new, l_new, carry[3 * idx + 2] * alpha + pv[idx]]
        return tuple(out)

    n_group = i // MOBA_LOOP_BLOCKS
    state = lax.fori_loop(0, n_group, lambda t, c: body(t * MOBA_LOOP_BLOCKS, c, MOBA_LOOP_BLOCKS), tuple(state))
    state = lax.fori_loop(n_group * MOBA_LOOP_BLOCKS, i, lambda j, c: body(j, c, 1), state)
    outs = []
    for hh in range(n_head):
        halves = [state[3 * (hh * n_split + qh) + 2] / state[3 * (hh * n_split + qh) + 1] for qh in range(n_split)]
        outs.append(jnp.concatenate(halves, axis=1))
    o_ref[...] = jnp.concatenate(outs, axis=0).T.astype(BF16)


def _moba_prompt(qt, k, vt, kmean, batch, seq):
    nq = seq // MOBA_BLOCK
    hp = QKV_A // LANE
    return pl.pallas_call(
        _moba_prompt_kernel,
        grid=(batch, hp, nq),
        in_specs=[pl.BlockSpec((1, LANE, MOBA_BLOCK), lambda b, h, i: (b, h, i)),
                  pl.BlockSpec((seq, LANE), lambda b, h, i: (b, h)),
                  pl.BlockSpec((1, LANE, seq), lambda b, h, i: (b, h, 0)),
                  pl.BlockSpec((1, nq, LANE), lambda b, h, i: (b, 0, h))],
        out_specs=pl.BlockSpec((MOBA_BLOCK, LANE), lambda b, h, i: (b * nq + i, h)),
        out_shape=jax.ShapeDtypeStruct((batch * seq, QKV_A), BF16),
        scratch_shapes=[pltpu.VMEM((LANE // D_A, MOBA_BLOCK // LANE, nq, LANE), F32)],
        compiler_params=_cparams("parallel", "parallel", "arbitrary"),
        name="moba_prompt",
    )(qt, k, vt, kmean)


def _proj_ffn_kernel(*refs, n_mix):
    mix_refs = refs[:n_mix]
    x_ref, wo_ref, g_ref, wg_ref, wu_ref, wd_ref, o_ref, hn_ref = refs[n_mix:]

    @pl.when(pl.program_id(1) == 0)
    def _():
        x1 = x_ref[...]
        off = 0
        for r in mix_refs:
            width = r.shape[1]
            x1 = x1 + _dot(r[...], wo_ref[off:off + width, :])
            off += width
        o_ref[...] = x1
        hn_ref[...] = _rms(x1, g_ref[...]).astype(BF16)

    h = hn_ref[...]
    act = (_silu(_dot(h, wg_ref[...])) * _dot(h, wu_ref[...])).astype(BF16)
    o_ref[...] += _dot(act, wd_ref[...])


def _col_blocks(w, tf):
    *lead, d, ff = w.shape
    n = len(lead)
    return jnp.transpose(w.reshape(*lead, d, ff // tf, tf), (*range(n), n + 1, n, n + 2))


def _proj_ffn(mixes, x, w_out, g, w_gate, w_up, w_down, tm):
    m, d = x.shape
    nf, _, tf = w_gate.shape
    mix_specs = [pl.BlockSpec((tm, a.shape[1]), lambda i, f: (i, 0)) for a in mixes]
    return pl.pallas_call(
        functools.partial(_proj_ffn_kernel, n_mix=len(mixes)),
        grid=(m // tm, nf),
        in_specs=mix_specs + [
            pl.BlockSpec((tm, d), lambda i, f: (i, 0)),
            pl.BlockSpec(w_out.shape, lambda i, f: (0, 0)),
            pl.BlockSpec((1, d), lambda i, f: (0, 0)),
            pl.BlockSpec((None, d, tf), lambda i, f: (f, 0, 0)),
            pl.BlockSpec((None, d, tf), lambda i, f: (f, 0, 0)),
            pl.BlockSpec((tf, d), lambda i, f: (f, 0))],
        out_specs=pl.BlockSpec((tm, d), lambda i, f: (i, 0)),
        out_shape=jax.ShapeDtypeStruct((m, d), F32),
        scratch_shapes=[pltpu.VMEM((tm, d), BF16)],
        compiler_params=_cparams("parallel", "arbitrary"),
        name="proj_ffn",
    )(*mixes, x, w_out, g, w_gate, w_up, w_down)


def _even_layer_prompt(x, p, batch, seq):
    z = _norm_matmul(x, p["norm_mix"], p["w_in"], 512)
    tables = _rope_tables(jnp.arange(seq, dtype=jnp.int32))
    qt, k, vt, kv, kmean, gv, b_out = _even_epilogue(
        z, tables, p["ln_g"], p["ln_b"], p["w_sp"], p["b_sp"], batch, seq, 256)
    a_out = _moba_prompt(qt, k, vt, kmean.reshape(batch, seq // MOBA_BLOCK, QKV_A), batch, seq)
    x = _proj_ffn([a_out, b_out], x, p["w_out"], p["norm_ffn"], p["w_gate"], p["w_up"], p["w_down"], 512)
    n_open = seq - ((seq - 1) // CHUNK_B) * CHUNK_B
    gv_open = gv.reshape(batch, seq, D_B)[:, seq - n_open:]
    kv = jnp.transpose(kv.reshape(batch, 2, H_A, D_A, seq), (0, 4, 1, 2, 3))
    return x, kv, gv_open


def _even_params(i, ev_norm_mix, ev_w_in, ev_gmlp_ln_g, ev_gmlp_ln_b, ev_w_spatial, ev_b_spatial, ev_w_out,
                 ev_norm_ffn, ev_w_gate, ev_w_up, ev_w_down):
    return dict(norm_mix=ev_norm_mix[i][None], w_in=ev_w_in[i].astype(BF16),
                ln_g=ev_gmlp_ln_g[i][None], ln_b=ev_gmlp_ln_b[i][None],
                w_sp=ev_w_spatial[i], b_sp=ev_b_spatial[i], w_out=ev_w_out[i].astype(BF16),
                norm_ffn=ev_norm_ffn[i][None], w_gate=_col_blocks(ev_w_gate[i].astype(BF16), FFN_TF),
                w_up=_col_blocks(ev_w_up[i].astype(BF16), FFN_TF), w_down=ev_w_down[i].astype(BF16))


O_CQ, O_CK, O_CV, O_CR, O_DQKV, O_DG, O_SM = 0, 256, 512, 1024, 1536, 2560, 3072
SM_DA, SM_DB = GATE_RANK, GATE_RANK + H_D
IN_ODD_PAD = O_SM + LANE
KEY_C = H_C * DK_C
PAIR = LANE // DK_C


def _log_sigmoid(x):
    return jnp.minimum(x, 0.0) - jnp.log1p(jnp.exp(-jnp.abs(x)))


def _split_bf16(a, terms):
    parts = []
    for _ in range(terms - 1):
        hi = a.astype(BF16)
        parts.append(hi)
        a = a - hi.astype(F32)
    parts.append(a.astype(BF16))
    return parts


def _dot_select(a, sel_bf16, terms):
    parts = _split_bf16(a, terms)
    out = _dot(parts[0], sel_bf16)
    for part in parts[1:]:
        out = out + _dot(part, sel_bf16)
    return out


def _select_dot(sel_bf16, a, terms):
    parts = _split_bf16(a, terms)
    out = _dot(sel_bf16, parts[0])
    for part in parts[1:]:
        out = out + _dot(sel_bf16, part)
    return out


def _unit_lower_inverses(mats):
    c = mats[0].shape[0]
    assert c == 64
    eye = (lax.broadcasted_iota(jnp.int32, (c, c), 0) == lax.broadcasted_iota(jnp.int32, (c, c), 1)).astype(F32)
    sp = lambda ms: [_split_bf16(m, 2) for m in ms]
    mul = lambda aa, bb: [_dot(ah, bh) + _dot(ah, bl) + _dot(al, bh) for (ah, al), (bh, bl) in zip(aa, bb)]
    x1 = [-a for a in mats]
    x1s = sp(x1)
    x2 = mul(x1s, x1s)
    x2s = sp(x2)
    x4 = mul(x2s, x2s)
    x3 = mul(x1s, x2s)
    x4s = sp(x4)
    x8 = mul(x4s, x4s)
    f01 = [eye + a + b + d for a, b, d in zip(x1, x2, x3)]
    x8s = sp(x8)
    x16 = mul(x8s, x8s)
    x12 = mul(x4s, x8s)
    g23 = [a + b + d for a, b, d in zip(x4, x8, x12)]
    x16s = sp(x16)
    x32 = mul(x16s, x16s)
    f0123 = [f + d for f, d in zip(f01, mul(sp(f01), sp(g23)))]
    x48 = mul(x16s, sp(x32))
    g45 = [a + b + d for a, b, d in zip(x16, x32, x48)]
    return [f + d for f, d in zip(f0123, mul(sp(f0123), sp(g45)))]


def _group_sumsq(y, width):
    n = y.shape[1]
    same = (lax.broadcasted_iota(jnp.int32, (n, n), 0) // width
            == lax.broadcasted_iota(jnp.int32, (n, n), 1) // width).astype(BF16)
    return _dot_select(y * y, same, 2)


def _odd_mixer_prompt_kernel(z_ref, wg2_ref, bg2_ref, gnorm_ref, cw_ref, alog_ref, dtb_ref, dnorm_ref,
                             o_ref, sg_ref, sd_ref, tail_ref,
                             stg_ref, std_ref, prev_ref, la_ref, qkv_ref, dla_ref, beta_ref):
    i = pl.program_id(1)
    ns, tc = z_ref.shape[0], z_ref.shape[1]
    c = CHUNK_LIN

    @pl.when(i == 0)
    def _():
        stg_ref[...] = jnp.zeros_like(stg_ref)
        std_ref[...] = jnp.zeros_like(std_ref)
        prev_ref[...] = jnp.zeros_like(prev_ref)

    row8 = lax.broadcasted_iota(jnp.int32, (8, C_CONV), 0)
    qscale = jnp.where(lax.broadcasted_iota(jnp.int32, (1, 2 * KEY_C), 1) < KEY_C, DK_D ** -0.5, 1.0)
    for sq in range(ns):
        small = z_ref[sq, :, O_SM:O_SM + LANE]
        pre = jnp.dot(small, wg2_ref[...], precision=HI, preferred_element_type=F32) + bg2_ref[...]
        la_ref[sq] = _log_sigmoid(pre) / GATE_NORM
        dla_ref[sq] = -jnp.exp(alog_ref[...]) * _softplus(small + dtb_ref[...])
        beta_ref[sq] = jax.nn.sigmoid(small)

        x = z_ref[sq, :, O_DQKV:O_DQKV + C_CONV]
        x8 = x[0:8]
        p8 = prev_ref[sq]
        y = x * cw_ref[CONV_W - 1:CONV_W, :]
        y8 = x8 * cw_ref[CONV_W - 1:CONV_W, :]
        for s in range(1, CONV_W):
            wrow = cw_ref[CONV_W - 1 - s:CONV_W - s, :]
            y = y + pltpu.roll(x, s, 0) * wrow
            y8 = y8 + jnp.where(row8 < s, pltpu.roll(p8, s, 0), pltpu.roll(x8, s, 0)) * wrow
        prev_ref[sq] = x[tc - 8:tc]
        y = _silu(y)
        y8 = _silu(y8)
        yqk = y[:, 0:2 * KEY_C]
        nrm = lax.rsqrt(_group_sumsq(yqk, DK_D) + EPS)
        qkv_ref[sq, :, 0:2 * KEY_C] = yqk * nrm * qscale
        qkv_ref[sq, :, 2 * KEY_C:] = y[:, 2 * KEY_C:]
        yqk8 = y8[:, 0:2 * KEY_C]
        qkv_ref[sq, 0:8, 0:2 * KEY_C] = yqk8 * lax.rsqrt(_group_sumsq(yqk8, DK_D) + EPS) * qscale
        qkv_ref[sq, 0:8, 2 * KEY_C:] = y8[:, 2 * KEY_C:]

    ri = lax.broadcasted_iota(jnp.int32, (c, c), 0)
    ci = lax.broadcasted_iota(jnp.int32, (c, c), 1)
    tril = ri >= ci
    strict = ri > ci
    lower = tril.astype(BF16)
    upper = (ri <= ci).astype(BF16)
    lane_head = lax.broadcasted_iota(jnp.int32, (c, LANE), 1) // DK_C
    lane_head_row = lax.broadcasted_iota(jnp.int32, (1, LANE), 1) // DK_C

    def chunk(ch, carry):
        rows = pl.ds(pl.multiple_of(ch * c, c), c)
        n_pair = H_C // PAIR
        seqs = range(ns)
        units = [(sq, h) + divmod(h, PAIR) for sq in seqs for h in range(H_C)]
        pairs = [(sq, p) for sq in seqs for p in range(n_pair)]
        unit_at = lambda sq, p, hh: (sq * H_C) + p * PAIR + hh
        masks = [lane_head == hh for hh in range(PAIR)]
        bcum_all = [_select_dot(lower, la_ref[sq, rows, :], 3) for sq in seqs]
        dla_parts = [_split_bf16(dla_ref[sq, rows, :], 3) for sq in seqs]
        g_cols = [_dot(lower, dp[0]) + _dot(lower, dp[1]) + _dot(lower, dp[2]) for dp in dla_parts]
        g_rows = [_dot_tn(dp[0], upper) + _dot_tn(dp[1], upper) + _dot_tn(dp[2], upper) for dp in dla_parts]
        st_d = {sp: std_ref[sp[0], sp[1]] for sp in pairs}
        st_db = {sp: st_d[sp].astype(BF16) for sp in pairs}
        dn = []
        for sq, h, p, hh in units:
            g_col = jnp.broadcast_to(g_cols[sq][:, SM_DA + h:SM_DA + h + 1], (c, LANE))
            g_row = jnp.broadcast_to(g_rows[sq][SM_DA + h:SM_DA + h + 1, :], (c, c))
            beta_b = jnp.broadcast_to(beta_ref[sq, rows, SM_DB + h:SM_DB + h + 1], (c, LANE))
            km = jnp.where(masks[hh], qkv_ref[sq, rows, KEY_C + p * LANE:KEY_C + (p + 1) * LANE], 0.0)
            qm = jnp.where(masks[hh], qkv_ref[sq, rows, p * LANE:(p + 1) * LANE], 0.0)
            dn.append(dict(g_col=g_col, g_row=g_row, beta_b=beta_b, km=km, qm=qm, kmb=km.astype(BF16),
                           kb=km * beta_b, eg=jnp.exp(g_col), g_last=g_col[c - 1:c, :]))
        kk = [_dot_nt(d["kb"].astype(BF16), d["kmb"]) for d in dn]
        qk = [_dot_nt(d["qm"].astype(BF16), d["kmb"]) for d in dn]
        d_inter = [_dot_nt((d["qm"] * d["eg"]).astype(BF16), st_db[(sq, p)]) for d, (sq, h, p, hh) in zip(dn, units)]
        decay = [jnp.exp(jnp.where(tril, d["g_col"][:, 0:c] - d["g_row"], NEG)) for d in dn]
        a_mats = [jnp.where(strict, m * dc, 0.0) for m, dc in zip(kk, decay)]
        st_g = {sp: stg_ref[sp[0], sp[1]] for sp in pairs}
        st_gb = {sp: st_g[sp].astype(BF16) for sp in pairs}
        gl_pair = {}
        for sq, p in pairs:
            bcum = bcum_all[sq][:, p * LANE:(p + 1) * LANE]
            b_end = bcum[c - 1:c, :]
            k = z_ref[sq, rows, O_CK + p * LANE:O_CK + (p + 1) * LANE]
            gl_pair[(sq, p)] = dict(
                q_in=z_ref[sq, rows, O_CQ + p * LANE:O_CQ + (p + 1) * LANE] * (DK_C ** -0.5) * jnp.exp(bcum),
                k_in=(k * jnp.exp(-bcum)).astype(BF16), k_end=k * jnp.exp(b_end - bcum), b_end=b_end)
        g_qm = [jnp.where(masks[hh], gl_pair[(sq, p)]["q_in"], 0.0).astype(BF16) for sq, h, p, hh in units]
        g_ke = [jnp.where(masks[hh], gl_pair[(sq, p)]["k_end"], 0.0).astype(BF16) for sq, h, p, hh in units]
        g_vb = [z_ref[sq, rows, O_CV + h * DV_C:O_CV + (h + 1) * DV_C].astype(BF16) for sq, h, p, hh in units]
        g_sc = [_dot_nt(g_qm[u], gl_pair[(sq, p)]["k_in"]) for u, (sq, h, p, hh) in enumerate(units)]
        g_inter = [_dot_nt(g_qm[u], st_gb[(sq, p)]) for u, (sq, h, p, hh) in enumerate(units)]
        g_upd = [_dot_tn(g_vb[u], g_ke[u]) for u in range(len(units))]
        g_intra = [_dot(jnp.where(tril, g_sc[u], 0.0).astype(BF16), g_vb[u]) for u in range(len(units))]
        t_invs = _unit_lower_inverses(a_mats)
        for u, (sq, h, p, hh) in enumerate(units):
            gate = _silu(z_ref[sq, rows, O_CR + h * DV_C:O_CR + (h + 1) * DV_C])
            o_ref[sq, rows, h * DV_C:(h + 1) * DV_C] = (
                _rms(g_intra[u] + g_inter[u], gnorm_ref[...]) * gate).astype(BF16)
        for sq, p in pairs:
            stg_ref[sq, p] = (st_g[(sq, p)] * jnp.exp(gl_pair[(sq, p)]["b_end"])
                              + g_upd[unit_at(sq, p, 0)] + g_upd[unit_at(sq, p, 1)])
        rhs = [jnp.concatenate([qkv_ref[sq, rows, 2 * KEY_C + h * DV_D:2 * KEY_C + (h + 1) * DV_D] * d["beta_b"],
                                d["kb"] * d["eg"]], axis=1).astype(BF16) for d, (sq, h, p, hh) in zip(dn, units)]
        sol = [_dot(t.astype(BF16), r) for t, r in zip(t_invs, rhs)]
        w_st = [_dot_nt(s[:, DV_D:].astype(BF16), st_db[(sq, p)]) for s, (sq, h, p, hh) in zip(sol, units)]
        v_new = [(s[:, 0:DV_D] - ws).astype(BF16) for s, ws in zip(sol, w_st)]
        d_intra = [_dot(jnp.where(tril, q * dc, 0.0).astype(BF16), vn) for q, dc, vn in zip(qk, decay, v_new)]
        d_upd = [_dot_tn(vn, (d["km"] * jnp.exp(d["g_last"] - d["g_col"])).astype(BF16)) for vn, d in zip(v_new, dn)]
        for u, (sq, h, p, hh) in enumerate(units):
            gate = _silu(z_ref[sq, rows, O_DG + h * DV_D:O_DG + (h + 1) * DV_D])
            col = H_C * DV_C + h * DV_D
            o_ref[sq, rows, col:col + DV_D] = (_rms(d_inter[u] + d_intra[u], dnorm_ref[...]) * gate).astype(BF16)
        for sq, p in pairs:
            u0, u1 = unit_at(sq, p, 0), unit_at(sq, p, 1)
            dec_row = jnp.where(lane_head_row == 0, jnp.exp(dn[u0]["g_last"]), jnp.exp(dn[u1]["g_last"]))
            std_ref[sq, p] = st_d[(sq, p)] * dec_row + d_upd[u0] + d_upd[u1]
        return carry

    lax.fori_loop(0, tc // c, chunk, 0)

    @pl.when(i == pl.num_programs(1) - 1)
    def _():
        for sq in range(ns):
            tail_ref[sq] = prev_ref[sq]
            for p in range(H_C // PAIR):
                tg = stg_ref[sq, p].T
                td = std_ref[sq, p].T
                for hh in range(PAIR):
                    sg_ref[sq, p * PAIR + hh] = tg[hh * DK_C:(hh + 1) * DK_C, :]
                    sd_ref[sq, p * PAIR + hh] = td[hh * DK_D:(hh + 1) * DK_D, :]


def _odd_mixer_prompt(z, p, batch, seq, tc):
    nt = seq // tc
    ns = 2 if batch % 2 == 0 else 1
    full = lambda a: pl.BlockSpec(a.shape, lambda b, i: (0,) * a.ndim)
    consts = [p["w_g2_pad"], p["b_g2"], p["gla_norm"], p["conv_w"], p["alog_slab"], p["dtb_slab"], p["delta_norm"]]
    st_spec = pl.BlockSpec((ns, H_C, DK_C, DV_C), lambda b, i: (b, 0, 0, 0))
    mix, s_gla, s_delta, tail = pl.pallas_call(
        _odd_mixer_prompt_kernel,
        grid=(batch // ns, nt),
        in_specs=[pl.BlockSpec((ns, tc, IN_ODD_PAD), lambda b, i: (b, i, 0))] + [full(a) for a in consts],
        out_specs=[pl.BlockSpec((ns, tc, D_MODEL), lambda b, i: (b, i, 0)), st_spec, st_spec,
                   pl.BlockSpec((ns, 8, C_CONV), lambda b, i: (b, 0, 0))],
        out_shape=[jax.ShapeDtypeStruct((batch, seq, D_MODEL), BF16),
                   jax.ShapeDtypeStruct((batch, H_C, DK_C, DV_C), F32),
                   jax.ShapeDtypeStruct((batch, H_D, DK_D, DV_D), F32),
                   jax.ShapeDtypeStruct((batch, 8, C_CONV), F32)],
        scratch_shapes=[pltpu.VMEM((ns, H_C // PAIR, DV_C, LANE), F32),
                        pltpu.VMEM((ns, H_D // PAIR, DV_D, LANE), F32),
                        pltpu.VMEM((ns, 8, C_CONV), F32),
                        pltpu.VMEM((ns, tc, KEY_C), F32),
                        pltpu.VMEM((ns, tc, C_CONV), F32),
                        pltpu.VMEM((ns, tc, LANE), F32),
                        pltpu.VMEM((ns, tc, LANE), F32)],
        compiler_params=_cparams("parallel", "arbitrary"),
        name="odd_mixer_prompt",
    )(z.reshape(batch, seq, IN_ODD_PAD), *consts)
    return mix.reshape(batch * seq, D_MODEL), s_gla, s_delta, tail


M_E1, M_E2, M_R1, M_R2, M_G1, M_G2 = (N_EXPERTS + j for j in range(6))


def _proj_router_kernel(mix_ref, x_ref, wo_ref, g_ref, wr_ref, x1_ref, hn_ref, meta_ref, cnt_ref, carry_ref):
    tm = x_ref.shape[0]

    @pl.when(pl.program_id(0) == 0)
    def _():
        carry_ref[...] = jnp.zeros_like(carry_ref)

    x1 = x_ref[...] + _dot(mix_ref[...], wo_ref[...])
    x1_ref[...] = x1
    hn = _rms(x1, g_ref[...])
    hn_ref[...] = hn.astype(hn_ref.dtype)
    lane = lax.broadcasted_iota(jnp.int32, (tm, LANE), 1)
    hn_hi, hn_lo = _split_bf16(hn, 2)
    wr_hi, wr_lo = _split_bf16(wr_ref[...], 2)
    logits = _dot(hn_hi, wr_hi) + _dot(hn_hi, wr_lo) + _dot(hn_lo, wr_hi)
    logits = jnp.where(lane < N_EXPERTS, logits, NEG)
    m1 = jnp.max(logits, axis=1, keepdims=True)
    e1 = jnp.min(jnp.where(logits == m1, lane, LANE), axis=1, keepdims=True)
    rest = jnp.where(lane == e1, NEG, logits)
    m2 = jnp.max(rest, axis=1, keepdims=True)
    e2 = jnp.min(jnp.where(rest == m2, lane, LANE), axis=1, keepdims=True)
    t = jnp.exp(m2 - m1)
    g1 = 1.0 / (1.0 + t)
    g2 = t / (1.0 + t)
    oh1 = lane == e1
    oh2 = lane == e2
    member = (oh1 | oh2).astype(F32)
    ri = lax.broadcasted_iota(jnp.int32, (tm, tm), 0)
    ci = lax.broadcasted_iota(jnp.int32, (tm, tm), 1)
    before = _dot((ri > ci).astype(BF16), member.astype(BF16)) + carry_ref[...]
    r1 = jnp.sum(jnp.where(oh1, before, 0.0), axis=1, keepdims=True)
    r2 = jnp.sum(jnp.where(oh2, before, 0.0), axis=1, keepdims=True)
    carry_ref[...] = carry_ref[...] + jnp.sum(member, axis=0, keepdims=True)
    cnt_ref[...] = carry_ref[...]
    meta = jnp.where(oh1, g1, 0.0) + jnp.where(oh2, g2, 0.0)
    meta = jnp.where(lane == M_E1, e1.astype(F32), meta)
    meta = jnp.where(lane == M_E2, e2.astype(F32), meta)
    meta = jnp.where(lane == M_R1, r1, meta)
    meta = jnp.where(lane == M_R2, r2, meta)
    meta = jnp.where(lane == M_G1, g1, meta)
    meta = jnp.where(lane == M_G2, g2, meta)
    meta_ref[...] = meta


def _proj_router(mix, x, w_out, g, w_router_pad, tm, hn_dtype):
    m, d = x.shape
    row = lambda width: pl.BlockSpec((tm, width), lambda i: (i, 0))
    full = lambda a: pl.BlockSpec(a.shape, lambda i: (0,) * a.ndim)
    return pl.pallas_call(
        _proj_router_kernel,
        grid=(m // tm,),
        in_specs=[row(d), row(d), full(w_out), full(g), full(w_router_pad)],
        out_specs=[row(d), row(d), row(LANE), pl.BlockSpec((1, LANE), lambda i: (0, 0))],
        out_shape=[jax.ShapeDtypeStruct((m, d), F32), jax.ShapeDtypeStruct((m, d), hn_dtype),
                   jax.ShapeDtypeStruct((m, LANE), F32), jax.ShapeDtypeStruct((1, LANE), F32)],
        scratch_shapes=[pltpu.VMEM((1, LANE), F32)],
        compiler_params=_cparams("arbitrary"),
        name="proj_router",
    )(mix, x, w_out, g, w_router_pad)


def _moe_dense_kernel(hn_ref, x1_ref, meta_ref, wg_ref, wu_ref, wd_ref, gf_ref, o_ref):
    e = pl.program_id(1)
    f = pl.program_id(2)

    @pl.when((e == 0) & (f == 0))
    def _():
        o_ref[...] = x1_ref[...]

    h = hn_ref[...]
    act = (_silu(_dot(h, wg_ref[...])) * _dot(h, wu_ref[...])).astype(BF16)
    lane = lax.broadcasted_iota(jnp.int32, meta_ref.shape, 1)
    gate = jnp.sum(jnp.where(lane == e, meta_ref[...], 0.0), axis=1, keepdims=True)
    o_ref[...] += gate * _dot(act, wd_ref[...])

    @pl.when((e == pl.num_programs(1) - 1) & (f == pl.num_programs(2) - 1))
    def _():
        o_ref[...] = _rms(o_ref[...], gf_ref[...])


def _moe_dense(hn, x1, meta, w_gate, w_up, w_down, g_final, tm):
    m, d = x1.shape
    n_e, _, ff = w_gate.shape
    tf = MOE_TF
    row = lambda width: pl.BlockSpec((tm, width), lambda i, e, f: (i, 0))
    return pl.pallas_call(
        _moe_dense_kernel,
        grid=(m // tm, n_e, ff // tf),
        in_specs=[row(d), row(d), row(LANE),
                  pl.BlockSpec((None, d, tf), lambda i, e, f: (e, 0, f)),
                  pl.BlockSpec((None, d, tf), lambda i, e, f: (e, 0, f)),
                  pl.BlockSpec((None, tf, d), lambda i, e, f: (e, f, 0)),
                  pl.BlockSpec((1, d), lambda i, e, f: (0, 0))],
        out_specs=row(d),
        out_shape=jax.ShapeDtypeStruct((m, d), F32),
        compiler_params=_cparams("parallel", "arbitrary", "arbitrary"),
        name="moe_dense",
    )(hn, x1, meta, w_gate, w_up, w_down, g_final)


def _moe_dispatch_kernel(dest_ref, hn_ref, xs_in_ref, xs_ref, sem):
    del xs_in_ref
    tm = hn_ref.shape[0]
    base = pl.program_id(0) * (TOP_K * tm)

    def row_copy(t, k):
        return pltpu.make_async_copy(hn_ref.at[pl.ds(t, 1)], xs_ref.at[pl.ds(dest_ref[base + k * tm + t], 1)], sem)

    def issue(t, carry):
        for k in range(TOP_K):
            row_copy(t, k).start()
        return carry

    def drain(t, carry):
        for k in range(TOP_K):
            row_copy(t, k).wait()
        return carry

    lax.fori_loop(0, tm, issue, 0, unroll=DMA_UNROLL)
    lax.fori_loop(0, tm, drain, 0, unroll=DMA_UNROLL)


def _moe_dispatch(dest, hn, n_rows, tm):
    m, d = hn.shape
    return pl.pallas_call(
        _moe_dispatch_kernel,
        grid_spec=pltpu.PrefetchScalarGridSpec(
            num_scalar_prefetch=1,
            grid=(m // tm,),
            in_specs=[pl.BlockSpec((tm, d), lambda i, dest: (i, 0)), pl.BlockSpec(memory_space=pl.ANY)],
            out_specs=pl.BlockSpec(memory_space=pl.ANY),
            scratch_shapes=[pltpu.SemaphoreType.DMA(())]),
        out_shape=jax.ShapeDtypeStruct((n_rows, d), hn.dtype),
        input_output_aliases={2: 0},
        compiler_params=_cparams("arbitrary"),
        name="moe_dispatch",
    )(dest, hn, jnp.zeros((n_rows, d), hn.dtype))


def _moe_grouped_kernel(te_ref, nv_ref, xs_ref, wg_ref, wu_ref, wd_ref, o_ref, xb_ref):
    del te_ref
    f = pl.program_id(1)

    @pl.when(f == 0)
    def _():
        o_ref[...] = jnp.zeros_like(o_ref)
        xb_ref[...] = xs_ref[...].astype(BF16)

    @pl.when(pl.program_id(0) < nv_ref[0])
    def _():
        h = xb_ref[...]
        act = (_silu(_dot(h, wg_ref[...])) * _dot(h, wu_ref[...])).astype(BF16)
        o_ref[...] += _dot(act, wd_ref[...])


def _moe_grouped(tile_expert, n_valid, xs, w_gate, w_up, w_down, tg):
    rows, d = xs.shape
    tf = MOE_TF
    nf = w_gate.shape[2] // tf
    fidx = lambda r, f, nv: jnp.where(r < nv[0], f, nf - 1)
    return pl.pallas_call(
        _moe_grouped_kernel,
        grid_spec=pltpu.PrefetchScalarGridSpec(
            num_scalar_prefetch=2,
            grid=(rows // tg, nf),
            in_specs=[pl.BlockSpec((tg, d), lambda r, f, te, nv: (r, 0)),
                      pl.BlockSpec((None, d, tf), lambda r, f, te, nv: (te[r], 0, fidx(r, f, nv))),
                      pl.BlockSpec((None, d, tf), lambda r, f, te, nv: (te[r], 0, fidx(r, f, nv))),
                      pl.BlockSpec((None, tf, d), lambda r, f, te, nv: (te[r], fidx(r, f, nv), 0))],
            out_specs=pl.BlockSpec((tg, d), lambda r, f, te, nv: (r, 0)),
            scratch_shapes=[pltpu.VMEM((tg, d), BF16)]),
        out_shape=jax.ShapeDtypeStruct((rows, d), F32),
        compiler_params=_cparams("parallel", "arbitrary"),
        name="moe_grouped",
    )(tile_expert, n_valid, xs, w_gate, w_up, w_down)


def _moe_combine_kernel(dest_ref, ys_ref, x1_ref, meta_ref, gf_ref, o_ref, buf_ref, sem):
    i = pl.program_id(0)
    tc = x1_ref.shape[0]
    rows = TOP_K * tc

    def row_copy(step, slot, j):
        return pltpu.make_async_copy(ys_ref.at[pl.ds(dest_ref[step * rows + j], 1)],
                                     buf_ref.at[slot, pl.ds(j, 1)], sem.at[slot])

    def issue(step, slot):
        def body(j, carry):
            row_copy(step, slot, j).start()
            return carry
        lax.fori_loop(0, rows, body, 0, unroll=DMA_UNROLL)

    def drain(step, slot):
        def body(j, carry):
            row_copy(step, slot, j).wait()
            return carry
        lax.fori_loop(0, rows, body, 0, unroll=DMA_UNROLL)

    slot = i % 2

    @pl.when(i == 0)
    def _():
        issue(0, 0)

    @pl.when(i + 1 < pl.num_programs(0))
    def _():
        issue(i + 1, 1 - slot)

    drain(i, slot)
    g1 = meta_ref[:, M_G1:M_G1 + 1]
    g2 = meta_ref[:, M_G2:M_G2 + 1]
    y = x1_ref[...] + g1 * buf_ref[slot, 0:tc] + g2 * buf_ref[slot, tc:rows]
    o_ref[...] = _rms(y, gf_ref[...])


def _moe_combine(dest, ys, x1, meta, g_final, tc):
    m, d = x1.shape
    row = lambda width: pl.BlockSpec((tc, width), lambda i, dest: (i, 0))
    return pl.pallas_call(
        _moe_combine_kernel,
        grid_spec=pltpu.PrefetchScalarGridSpec(
            num_scalar_prefetch=1,
            grid=(m // tc,),
            in_specs=[pl.BlockSpec(memory_space=pl.ANY), row(d), row(LANE),
                      pl.BlockSpec((1, d), lambda i, dest: (0, 0))],
            out_specs=row(d),
            scratch_shapes=[pltpu.VMEM((2, TOP_K * tc, d), F32), pltpu.SemaphoreType.DMA((2,))]),
        out_shape=jax.ShapeDtypeStruct((m, d), F32),
        compiler_params=_cparams("arbitrary"),
        name="moe_combine",
    )(dest, ys, x1, meta, g_final)


def _moe_routes(meta, counts, tile, tg):
    m = meta.shape[0]
    n_rows = TOP_K * m + N_EXPERTS * tg
    expert = meta[:, M_E1:M_E2 + 1].astype(jnp.int32)
    rank = meta[:, M_R1:M_R2 + 1].astype(jnp.int32)
    padded = (counts[0, :N_EXPERTS].astype(jnp.int32) + tg - 1) // tg * tg
    ends = jnp.cumsum(padded)
    dest = (ends - padded)[expert] + rank
    dest = jnp.transpose(dest.reshape(m // tile, tile, TOP_K), (0, 2, 1)).reshape(-1)
    n_valid = ends[-1] // tg
    tile_start = jnp.arange(n_rows // tg, dtype=jnp.int32) * tg
    probe = jnp.minimum(tile_start, ends[-1] - 1)
    tile_expert = jnp.sum(ends[None, :] <= probe[:, None], axis=1).astype(jnp.int32)
    return dest, tile_expert, n_valid.reshape(1).astype(jnp.int32), n_rows


def _odd_params(i, od_norm_mix, od_w_in, od_gla_w_gate2, od_gla_b_gate2, od_gla_norm, od_delta_conv,
                od_delta_a_log, od_delta_dt_bias, od_delta_norm, od_w_out, od_norm_ffn, od_router,
                od_w_gate, od_w_up, od_w_down):
    w = od_w_in[i]
    sizes = (KEY_C, KEY_C, H_C * DV_C, GATE_RANK, H_C * DV_C, C_CONV, H_D, H_D, H_D * DV_D)
    splits = tuple(sum(sizes[:j + 1]) for j in range(len(sizes) - 1))
    cq, ck, cv, c_lr, c_r, d_qkv, d_a, d_b, d_g = jnp.split(w, splits, axis=1)
    pad = jnp.zeros((w.shape[0], IN_ODD_PAD - w.shape[1]), w.dtype)
    w_in = jnp.concatenate([cq, ck, cv, c_r, d_qkv, d_g, c_lr, d_a, d_b, pad], axis=1).astype(BF16)
    slab = lambda v: jnp.zeros((1, LANE), F32).at[0, SM_DA:SM_DA + H_D].set(v)
    return dict(norm_mix=od_norm_mix[i][None], w_in=w_in,
                w_g2_pad=jnp.zeros((LANE, KEY_C), F32).at[:GATE_RANK].set(od_gla_w_gate2[i]),
                b_g2=od_gla_b_gate2[i][None], gla_norm=od_gla_norm[i][None], conv_w=od_delta_conv[i],
                alog_slab=slab(od_delta_a_log[i]), dtb_slab=slab(od_delta_dt_bias[i]),
                a_log=od_delta_a_log[i], dt_bias=od_delta_dt_bias[i],
                delta_norm=od_delta_norm[i][None], w_out=od_w_out[i].astype(BF16),
                norm_ffn=od_norm_ffn[i][None],
                router=jnp.zeros((D_MODEL, LANE), F32).at[:, :N_EXPERTS].set(od_router[i]),
                w_gate=od_w_gate[i].astype(BF16), w_up=od_w_up[i].astype(BF16), w_down=od_w_down[i].astype(BF16))


MOE_GROUP_TILE = 512
MOE_ROW_TILE = 256


def _moe(mix, x, p, g_final, tm):
    m = x.shape[0]
    if m < N_EXPERTS * MOE_GROUP_TILE:
        x1, hn, meta, _ = _proj_router(mix, x, p["w_out"], p["norm_ffn"], p["router"], tm, BF16)
        return _moe_dense(hn, x1, meta, p["w_gate"], p["w_up"], p["w_down"], g_final, tm)
    x1, hn, meta, counts = _proj_router(mix, x, p["w_out"], p["norm_ffn"], p["router"], tm, F32)
    dest, tile_expert, n_valid, n_rows = _moe_routes(meta, counts, MOE_ROW_TILE, MOE_GROUP_TILE)
    xs = _moe_dispatch(dest, hn, n_rows, MOE_ROW_TILE)
    ys = _moe_grouped(tile_expert, n_valid, xs, p["w_gate"], p["w_up"], p["w_down"], MOE_GROUP_TILE)
    return _moe_combine(dest, ys, x1, meta, g_final, MOE_ROW_TILE)


def _odd_layer_prompt(x, p, g_final, batch, seq):
    z = _norm_matmul(x, p["norm_mix"], p["w_in"], 512)
    mix, s_gla, s_delta, tail = _odd_mixer_prompt(z, p, batch, seq, 256)
    y = _moe(mix, x, p, g_final, 512)
    return y, s_gla, s_delta, tail[:, 8 - (CONV_W - 1):]


def _even_epilogue_sample_kernel(z_ref, c_ref, su_ref, sd_ref, lng_ref, lnb_ref, w0_ref, b0_ref,
                                 q_ref, kv_ref, gv_ref, bout_ref):
    c, su, sd = c_ref[...], su_ref[...], sd_ref[...]
    q_ref[...] = _rope(z_ref[:, 0:QKV_A], c, su, sd) * (D_A ** -0.5)
    kv_ref[:, 0:QKV_A] = _rope(z_ref[:, QKV_A:2 * QKV_A], c, su, sd)
    kv_ref[:, QKV_A:2 * QKV_A] = z_ref[:, 2 * QKV_A:3 * QKV_A]
    u = _gelu(z_ref[:, 3 * QKV_A:3 * QKV_A + D_B])
    gv = _layernorm(_gelu(z_ref[:, 3 * QKV_A + D_B:3 * QKV_A + 2 * D_B]), lng_ref[...], lnb_ref[...])
    gv_ref[...] = gv
    bout_ref[...] = (u * (gv * w0_ref[...] + b0_ref[...])).astype(BF16)


def _even_epilogue_sample(z, tables, ln_g, ln_b, w_sp, b_sp):
    m = z.shape[0]
    w0 = jnp.repeat(w_sp[:, 0, 0], D_BG)[None]
    b0 = jnp.repeat(b_sp[:, 0], D_BG)[None]
    return pl.pallas_call(
        _even_epilogue_sample_kernel,
        out_shape=[jax.ShapeDtypeStruct((m, QKV_A), F32), jax.ShapeDtypeStruct((m, 2 * QKV_A), F32),
                   jax.ShapeDtypeStruct((m, D_B), F32), jax.ShapeDtypeStruct((m, D_B), BF16)],
        compiler_params=pltpu.CompilerParams(vmem_limit_bytes=VMEM_LIMIT),
        name="even_epilogue_sample",
    )(z, *tables, ln_g, ln_b, w0, b0)


def _moba_sample_kernel(pt_ref, qt_ref, knt_ref, vnt_ref, *refs):
    del pt_ref
    n_pages = len(refs) - 2
    page_refs, o_ref, s_ref = refs[:n_pages], refs[n_pages], refs[n_pages + 1]
    pages_per_block = MOBA_BLOCK // PAGE_SIZE
    nb = n_pages // pages_per_block
    qt = qt_ref[0]
    for h in range(H_A):
        qcol = jnp.broadcast_to(qt[:, h:h + 1], (D_A, PAGE_SIZE))
        for j in range(n_pages):
            s_ref[h, j:j + 1, :] = jnp.sum(page_refs[j][0, 0, h] * qcol, axis=0, keepdims=True)
    lane = lax.broadcasted_iota(jnp.int32, (n_pages, LANE), 1)
    page_sums = jnp.zeros((n_pages, LANE), F32)
    for h in range(H_A):
        page_sums = jnp.where(lane == h, jnp.sum(s_ref[h], axis=1, keepdims=True), page_sums)
    pair = (lax.broadcasted_iota(jnp.int32, (nb, n_pages), 1) // pages_per_block
            == lax.broadcasted_iota(jnp.int32, (nb, n_pages), 0)).astype(F32)
    pair_t = (lax.broadcasted_iota(jnp.int32, (n_pages, nb), 0) // pages_per_block
              == lax.broadcasted_iota(jnp.int32, (n_pages, nb), 1)).astype(F32)
    gate = jnp.dot(pair, page_sums, precision=HI, preferred_element_type=F32)
    sel = _moba_select(gate, nb)
    sel_pages = jnp.dot(pair_t, sel, precision=HI, preferred_element_type=F32)
    own = jnp.sum(qt * knt_ref[0], axis=0, keepdims=True)
    vnt = vnt_ref[0]
    out_lane = lax.broadcasted_iota(jnp.int32, (D_A, LANE), 1)
    out = jnp.zeros((D_A, LANE), F32)
    for h in range(H_A):
        sm = jnp.where(sel_pages[:, h:h + 1] > 0.0, s_ref[h], NEG)
        s_own = own[:, h:h + 1]
        mx = jnp.maximum(jnp.max(jnp.max(sm, axis=1, keepdims=True), axis=0, keepdims=True), s_own)
        p = jnp.exp(sm - mx)
        p_own = jnp.exp(s_own - mx)
        denom = jnp.sum(jnp.sum(p, axis=1, keepdims=True), axis=0, keepdims=True) + p_own
        acc = jnp.zeros((D_A, PAGE_SIZE), F32)
        for j in range(n_pages):
            acc = acc + page_refs[j][0, 1, h] * p[j:j + 1, :]
        o = (jnp.sum(acc, axis=1, keepdims=True) + p_own * vnt[:, h:h + 1]) / denom
        out = jnp.where(out_lane == h, o, out)
    o_ref[0] = out


def _moba_sample(q, kv_new, cache, page_table):
    bs, n_pages = page_table.shape
    assert (n_pages * PAGE_SIZE) % MOBA_BLOCK == 0
    cache_t = jnp.transpose(cache, (0, 2, 3, 4, 1))
    page_spec = lambda j: pl.BlockSpec((1, 2, H_A, D_A, PAGE_SIZE),
                                       lambda b, pt: (pt[b * n_pages + j], 0, 0, 0, 0))
    col_spec = pl.BlockSpec((1, D_A, H_A), lambda b, pt: (b, 0, 0))
    heads_t = lambda a: jnp.transpose(a.reshape(bs, H_A, D_A), (0, 2, 1))
    out = pl.pallas_call(
        _moba_sample_kernel,
        grid_spec=pltpu.PrefetchScalarGridSpec(
            num_scalar_prefetch=1,
            grid=(bs,),
            in_specs=[col_spec, col_spec, col_spec] + [page_spec(j) for j in range(n_pages)],
            out_specs=pl.BlockSpec((1, D_A, LANE), lambda b, pt: (b, 0, 0)),
            scratch_shapes=[pltpu.VMEM((H_A, n_pages, PAGE_SIZE), F32)]),
        out_shape=jax.ShapeDtypeStruct((bs, D_A, LANE), F32),
        compiler_params=_cparams("parallel"),
        name="moba_sample",
    )(page_table.reshape(-1), heads_t(q), heads_t(kv_new[:, 0:QKV_A]), heads_t(kv_new[:, QKV_A:]),
      *([cache_t] * n_pages))
    return jnp.transpose(out[:, :, 0:H_A], (0, 2, 1)).reshape(bs, QKV_A).astype(BF16)


def _odd_mixer_sample_kernel(z_ref, buf_ref, sg_ref, sd_ref, wg2_ref, bg2_ref, gnorm_ref, cw_ref, alog_ref, dtb_ref,
                             dnorm_ref, o_ref, sgo_ref, sdo_ref):
    bt = z_ref.shape[0]
    stride = H_C * DK_C
    small = z_ref[:, O_SM:O_SM + LANE]
    pre = jnp.dot(small, wg2_ref[...], precision=HI, preferred_element_type=F32) + bg2_ref[...]
    a_all = jnp.exp(_log_sigmoid(pre) / GATE_NORM)
    g_all = -jnp.exp(alog_ref[...]) * _softplus(small + dtb_ref[...])
    beta_all = jax.nn.sigmoid(small)
    x = z_ref[:, O_DQKV:O_DQKV + C_CONV]
    y = x * cw_ref[CONV_W - 1:CONV_W, :]
    for j in range(CONV_W - 1):
        y = y + buf_ref[:, j, :] * cw_ref[j:j + 1, :]
    y = _silu(y)
    for h in range(H_C):
        ks = slice(h * DK_C, (h + 1) * DK_C)
        a = a_all[:, ks]
        q = z_ref[:, O_CQ + h * DK_C:O_CQ + (h + 1) * DK_C] * (DK_C ** -0.5)
        k = z_ref[:, O_CK + h * DK_C:O_CK + (h + 1) * DK_C]
        v = z_ref[:, O_CV + h * DV_C:O_CV + (h + 1) * DV_C]
        qa = q * a
        acc = jnp.sum(q * k, axis=1, keepdims=True) * v
        for kk in range(DK_C):
            rows = pl.ds(h * DK_C + kk, bt, stride=stride)
            srow = sg_ref[rows, :]
            acc = acc + qa[:, kk:kk + 1] * srow
            sgo_ref[rows, :] = a[:, kk:kk + 1] * srow + k[:, kk:kk + 1] * v
        gate = _silu(z_ref[:, O_CR + h * DV_C:O_CR + (h + 1) * DV_C])
        o_ref[:, h * DV_C:(h + 1) * DV_C] = (_rms(acc, gnorm_ref[...]) * gate).astype(BF16)
        yq = y[:, h * DK_D:(h + 1) * DK_D]
        yk = y[:, KEY_C + h * DK_D:KEY_C + (h + 1) * DK_D]
        dv = y[:, 2 * KEY_C + h * DV_D:2 * KEY_C + (h + 1) * DV_D]
        dq = yq * lax.rsqrt(jnp.sum(yq * yq, axis=1, keepdims=True) + EPS) * (DK_D ** -0.5)
        dk = yk * lax.rsqrt(jnp.sum(yk * yk, axis=1, keepdims=True) + EPS)
        beta = beta_all[:, SM_DB + h:SM_DB + h + 1]
        eg = jnp.exp(g_all[:, SM_DA + h:SM_DA + h + 1])
        w = dk * (beta * eg)
        qd = dq * eg
        ws = jnp.zeros((bt, DV_D), F32)
        qs = jnp.zeros((bt, DV_D), F32)
        for kk in range(DK_D):
            srow = sd_ref[pl.ds(h * DK_D + kk, bt, stride=stride), :]
            ws = ws + w[:, kk:kk + 1] * srow
            qs = qs + qd[:, kk:kk + 1] * srow
        v_new = dv * beta - ws
        o = qs + jnp.sum(dq * dk, axis=1, keepdims=True) * v_new
        for kk in range(DK_D):
            rows = pl.ds(h * DK_D + kk, bt, stride=stride)
            sdo_ref[rows, :] = sd_ref[rows, :] * eg + dk[:, kk:kk + 1] * v_new
        gate = _silu(z_ref[:, O_DG + h * DV_D:O_DG + (h + 1) * DV_D])
        col = H_C * DV_C + h * DV_D
        o_ref[:, col:col + DV_D] = (_rms(o, dnorm_ref[...]) * gate).astype(BF16)


def _odd_mixer_sample(z, conv_buf, s_gla, s_delta, p, bt):
    bs = z.shape[0]
    rows = H_C * DK_C
    full = lambda a: pl.BlockSpec(a.shape, lambda i: (0,) * a.ndim)
    consts = [p["w_g2_pad"], p["b_g2"], p["gla_norm"], p["conv_w"], p["alog_slab"], p["dtb_slab"], p["delta_norm"]]
    st_spec = pl.BlockSpec((bt * rows, DV_C), lambda i: (i, 0))
    mix, sg, sd = pl.pallas_call(
        _odd_mixer_sample_kernel,
        grid=(bs // bt,),
        in_specs=[pl.BlockSpec((bt, IN_ODD_PAD), lambda i: (i, 0)),
                  pl.BlockSpec((bt, CONV_W - 1, C_CONV), lambda i: (i, 0, 0)),
                  st_spec, st_spec] + [full(a) for a in consts],
        out_specs=[pl.BlockSpec((bt, D_MODEL), lambda i: (i, 0)), st_spec, st_spec],
        out_shape=[jax.ShapeDtypeStruct((bs, D_MODEL), BF16),
                   jax.ShapeDtypeStruct((bs * rows, DV_C), F32),
                   jax.ShapeDtypeStruct((bs * rows, DV_D), F32)],
        compiler_params=_cparams("parallel"),
        name="odd_mixer_sample",
    )(z, conv_buf, s_gla.reshape(bs * rows, DV_C), s_delta.reshape(bs * rows, DV_D), *consts)
    return mix, sg.reshape(s_gla.shape), sd.reshape(s_delta.shape)


def _sample_step(x, cache, page_table, s_gla, s_delta, conv_buf, ev, od, g_final):
    bs = x.shape[0]
    past = page_table.shape[1] * PAGE_SIZE
    z = _norm_matmul(x, ev["norm_mix"], ev["w_in"], bs)
    tables = _rope_tables(jnp.full((1,), past, jnp.int32))
    q, kv, gv, b_out = _even_epilogue_sample(z, tables, ev["ln_g"], ev["ln_b"], ev["w_sp"], ev["b_sp"])
    a_out = _moba_sample(q, kv, cache, page_table)
    x = _proj_ffn([a_out, b_out], x, ev["w_out"], ev["norm_ffn"], ev["w_gate"], ev["w_up"], ev["w_down"], bs)
    z = _norm_matmul(x, od["norm_mix"], od["w_in"], bs)
    mix, sg, sd = _odd_mixer_sample(z, conv_buf, s_gla, s_delta, od, 32)
    conv_new = jnp.concatenate([conv_buf[:, 1:], z[:, None, O_DQKV:O_DQKV + C_CONV]], axis=1)
    y = _moe(mix, x, od, g_final, bs)
    return y, kv, gv, sg, sd, conv_new


def kernel(x_prompt, x_sample, cache_kv, state_gla, state_delta, state_conv, page_table, ev_norm_mix, ev_w_in, ev_gmlp_ln_g, ev_gmlp_ln_b, ev_w_spatial, ev_b_spatial, ev_w_out, ev_norm_ffn, ev_w_gate, ev_w_up, ev_w_down, od_norm_mix, od_w_in, od_gla_w_gate2, od_gla_b_gate2, od_gla_norm, od_delta_conv, od_delta_a_log, od_delta_dt_bias, od_delta_norm, od_w_out, od_norm_ffn, od_router, od_w_gate, od_w_up, od_w_down, norm_final):
    bp, tp, d = x_prompt.shape
    ev = _even_params(0, ev_norm_mix, ev_w_in, ev_gmlp_ln_g, ev_gmlp_ln_b, ev_w_spatial, ev_b_spatial, ev_w_out,
                      ev_norm_ffn, ev_w_gate, ev_w_up, ev_w_down)
    od = _odd_params(0, od_norm_mix, od_w_in, od_gla_w_gate2, od_gla_b_gate2, od_gla_norm, od_delta_conv,
                     od_delta_a_log, od_delta_dt_bias, od_delta_norm, od_w_out, od_norm_ffn, od_router,
                     od_w_gate, od_w_up, od_w_down)
    bs, ts, _ = x_sample.shape
    assert ts == 1 and cache_kv.shape[0] == 1 and state_gla.shape[0] == 1
    xp, kv_p, gv_p = _even_layer_prompt(x_prompt.reshape(bp * tp, d), ev, bp, tp)
    yp, gla_p, dl_p, cv_p = _odd_layer_prompt(xp, od, norm_final[None], bp, tp)
    ys, kv_s, gv_s, gla_s, dl_s, cv_s = _sample_step(
        x_sample.reshape(bs, d), cache_kv[0], page_table, state_gla[0], state_delta[0], state_conv[0],
        ev, od, norm_final[None])
    return (yp.reshape(bp, tp, d), ys.reshape(bs, ts, d),
            kv_p[None], kv_s.reshape(1, bs, ts, 2, H_A, D_A),
            gv_p[None], gv_s.reshape(1, bs, ts, D_B),
            gla_p[None], gla_s[None], dl_p[None], dl_s[None], cv_p[None], cv_s[None])
```

```python
import functools
import math

import jax
import jax.numpy as jnp
from jax import lax
from jax.experimental import pallas as pl
from jax.experimental.pallas import tpu as pltpu

F32 = jnp.float32
BF16 = jnp.bfloat16
HI = lax.Precision.HIGHEST
EPS = 1e-6
NEG = -1e30

D_MODEL = 1024
PAGE_SIZE = 128
H_A, D_A = 8, 64
ROT_DIM = D_A // 4
ROPE_THETA = 500000.0
MOBA_BLOCK = 256
MOBA_TOPK = 3
G_B, D_BG = 8, 64
D_B = G_B * D_BG
CHUNK_B = 128
H_C, DK_C, DV_C = 4, 64, 128
GATE_RANK = 16
GATE_NORM = 16.0
H_D, DK_D, DV_D = 4, 64, 128
CONV_W = 4
C_CONV = 2 * H_D * DK_D + H_D * DV_D
CHUNK_LIN = 64
N_EXPERTS = 8
TOP_K = 2
QKV_A = H_A * D_A
IN_EVEN = 3 * QKV_A + 2 * D_B
FFN_TF = 1408
MOE_TF = 512
MOBA_LOOP_BLOCKS = 4
DMA_UNROLL = 8
LANE = 128
VMEM_LIMIT = 56 * 1024 * 1024


def _cparams(*sem):
    return pltpu.CompilerParams(dimension_semantics=sem, vmem_limit_bytes=VMEM_LIMIT)


def _rms(x, g):
    return x * lax.rsqrt(jnp.mean(x * x, axis=-1, keepdims=True) + EPS) * g


def _gelu(x):
    return 0.5 * x * (1.0 + lax.erf(x * (2.0 ** -0.5)))


def _silu(x):
    return x * jax.nn.sigmoid(x)


def _softplus(x):
    return jnp.maximum(x, 0.0) + jnp.log1p(jnp.exp(-jnp.abs(x)))


def _dot(a, b):
    return jnp.dot(a, b, preferred_element_type=F32)


def _dot_nt(a, b):
    return lax.dot_general(a, b, (((1,), (1,)), ((), ())), preferred_element_type=F32)


def _dot_tn(a, b):
    return lax.dot_general(a, b, (((0,), (0,)), ((), ())), preferred_element_type=F32)


def _norm_matmul_kernel(x_ref, g_ref, w_ref, o_ref):
    h = _rms(x_ref[...], g_ref[...]).astype(BF16)
    o_ref[...] = _dot(h, w_ref[...])


def _norm_matmul(x, g, w, tm):
    m, d = x.shape
    n = w.shape[1]
    return pl.pallas_call(
        _norm_matmul_kernel,
        grid=(m // tm,),
        in_specs=[pl.BlockSpec((tm, d), lambda i: (i, 0)),
                  pl.BlockSpec((1, d), lambda i: (0, 0)),
                  pl.BlockSpec((d, n), lambda i: (0, 0))],
        out_specs=pl.BlockSpec((tm, n), lambda i: (i, 0)),
        out_shape=jax.ShapeDtypeStruct((m, n), F32),
        compiler_params=_cparams("parallel"),
        name="norm_matmul",
    )(x, g, w)


def _rope_tables(pos):
    half = ROT_DIM // 2
    inv = ROPE_THETA ** (-jnp.arange(half, dtype=F32) / half)
    ang = pos.astype(F32)[:, None] * inv[None, :]
    cos, sin = jnp.cos(ang), jnp.sin(ang)
    t = pos.shape[0]
    one = jnp.ones((t, D_A - ROT_DIM), F32)
    zero_h = jnp.zeros((t, half), F32)
    zero_r = jnp.zeros((t, D_A - ROT_DIM), F32)
    c = jnp.concatenate([cos, cos, one], axis=1)
    s_up = jnp.concatenate([-sin, zero_h, zero_r], axis=1)
    s_dn = jnp.concatenate([zero_h, sin, zero_r], axis=1)
    rep = LANE // D_A
    return jnp.tile(c, (1, rep)), jnp.tile(s_up, (1, rep)), jnp.tile(s_dn, (1, rep))


def _rope(x, c, s_up, s_dn):
    half = ROT_DIM // 2
    outs = []
    for j in range(x.shape[1] // LANE):
        xs = x[:, j * LANE:(j + 1) * LANE]
        up = pltpu.roll(xs, LANE - half, 1)
        dn = pltpu.roll(xs, half, 1)
        outs.append(xs * c + up * s_up + dn * s_dn)
    return jnp.concatenate(outs, axis=1)


def _layernorm(x, g, b):
    mu = jnp.mean(x, axis=-1, keepdims=True)
    xc = x - mu
    var = jnp.mean(xc * xc, axis=-1, keepdims=True)
    return xc * lax.rsqrt(var + EPS) * g + b


def _even_epilogue_kernel(z_ref, c_ref, su_ref, sd_ref, lng_ref, lnb_ref, wsp_ref, bspt_ref,
                          qt_ref, k_ref, vt_ref, kv_ref, kmean_ref, gv_ref, bout_ref):
    tm = z_ref.shape[0]
    c, su, sd = c_ref[...], su_ref[...], sd_ref[...]
    q = _rope(z_ref[:, 0:QKV_A], c, su, sd) * (D_A ** -0.5)
    k = _rope(z_ref[:, QKV_A:2 * QKV_A], c, su, sd)
    v = z_ref[:, 2 * QKV_A:3 * QKV_A]
    qt_ref[0] = q.T.astype(BF16)
    k_ref[...] = k.astype(BF16)
    vt = v.T
    vt_ref[0] = vt.astype(BF16)
    kv_ref[0, 0:QKV_A, :] = k.T
    kv_ref[0, QKV_A:2 * QKV_A, :] = vt
    for blk in range(tm // MOBA_BLOCK):
        kmean_ref[blk] = jnp.mean(k[blk * MOBA_BLOCK:(blk + 1) * MOBA_BLOCK], axis=0, keepdims=True)
    u = _gelu(z_ref[:, 3 * QKV_A:3 * QKV_A + D_B])
    gv = _layernorm(_gelu(z_ref[:, 3 * QKV_A + D_B:3 * QKV_A + 2 * D_B]), lng_ref[...], lnb_ref[...])
    gv_ref[...] = gv
    gvb = gv.astype(BF16)
    row = lax.broadcasted_iota(jnp.int32, (CHUNK_B, CHUNK_B), 0)
    col = lax.broadcasted_iota(jnp.int32, (CHUNK_B, CHUNK_B), 1)
    group = lax.broadcasted_iota(jnp.int32, (CHUNK_B, D_B), 1) // D_BG
    w = [jnp.where(row >= col, wsp_ref[g], 0.0).astype(BF16) for g in range(G_B)]
    for ch in range(tm // CHUNK_B):
        gvc = gvb[ch * CHUNK_B:(ch + 1) * CHUNK_B]
        mixed = jnp.zeros((CHUNK_B, D_B), F32)
        for g in range(G_B):
            mixed = jnp.where(group == g, _dot(w[g], gvc) + bspt_ref[:, g:g + 1], mixed)
        bout_ref[ch * CHUNK_B:(ch + 1) * CHUNK_B, :] = (u[ch * CHUNK_B:(ch + 1) * CHUNK_B] * mixed).astype(BF16)


def _even_epilogue(z, tables, ln_g, ln_b, w_sp, b_sp, batch, seq, tm):
    n = batch * seq
    nt = seq // tm
    nblk = tm // MOBA_BLOCK
    tab_spec = pl.BlockSpec((tm, LANE), lambda b, i: (i, 0))
    row_spec = lambda width: pl.BlockSpec((tm, width), lambda b, i: (b * nt + i, 0))
    t_spec = pl.BlockSpec((1, QKV_A, tm), lambda b, i: (b, 0, i))
    return pl.pallas_call(
        _even_epilogue_kernel,
        grid=(batch, nt),
        in_specs=[row_spec(IN_EVEN), tab_spec, tab_spec, tab_spec,
                  pl.BlockSpec((1, D_B), lambda b, i: (0, 0)),
                  pl.BlockSpec((1, D_B), lambda b, i: (0, 0)),
                  pl.BlockSpec((G_B, CHUNK_B, CHUNK_B), lambda b, i: (0, 0, 0)),
                  pl.BlockSpec((CHUNK_B, G_B), lambda b, i: (0, 0))],
        out_specs=[t_spec, row_spec(QKV_A), t_spec, pl.BlockSpec((1, 2 * QKV_A, tm), lambda b, i: (b, 0, i)),
                   pl.BlockSpec((nblk, 1, QKV_A), lambda b, i: (b * nt + i, 0, 0)),
                   row_spec(D_B), row_spec(D_B)],
        out_shape=[jax.ShapeDtypeStruct((batch, QKV_A, seq), BF16),
                   jax.ShapeDtypeStruct((n, QKV_A), BF16),
                   jax.ShapeDtypeStruct((batch, QKV_A, seq), BF16),
                   jax.ShapeDtypeStruct((batch, 2 * QKV_A, seq), F32),
                   jax.ShapeDtypeStruct((n // MOBA_BLOCK, 1, QKV_A), F32),
                   jax.ShapeDtypeStruct((n, D_B), F32),
                   jax.ShapeDtypeStruct((n, D_B), BF16)],
        compiler_params=_cparams("parallel", "parallel"),
        name="even_epilogue",
    )(z, *tables, ln_g, ln_b, w_sp, b_sp.T)


def _moba_select(gate, n_own):
    nb = gate.shape[0]
    blk = lax.broadcasted_iota(jnp.int32, gate.shape, 0)
    elig = blk < n_own
    gm = jnp.where(elig, gate, NEG)
    rank = jnp.zeros(gate.shape, F32)
    for m in range(nb):
        gm_m = gm[m:m + 1, :]
        ahead = (gm_m > gm) | ((gm_m == gm) & (m < blk))
        rank = rank + ahead.astype(F32)
    return (elig & (rank < MOBA_TOPK)).astype(F32)


def _moba_prompt_kernel(qt_ref, k_ref, vt_ref, kmean_ref, o_ref, bias_ref):
    i = pl.program_id(2)
    tq = MOBA_BLOCK
    n_head = LANE // D_A
    n_split = tq // LANE
    own = pl.multiple_of(i * tq, tq)
    key_i = lax.broadcasted_iota(jnp.int32, (tq, LANE), 0)
    qry_i = lax.broadcasted_iota(jnp.int32, (tq, LANE), 1)
    head_lanes = [slice(hh * D_A, (hh + 1) * D_A) for hh in range(n_head)]
    streams = [(hh, qh) for hh in range(n_head) for qh in range(n_split)]
    qts = [qt_ref[0, head_lanes[hh], :] for hh in range(n_head)]
    queries = [qts[hh][:, qh * LANE:(qh + 1) * LANE] for hh, qh in streams]
    k_own = [k_ref[pl.ds(own, tq), head_lanes[hh]] for hh in range(n_head)]
    own_scores = [_dot(k_own[hh], queries[idx]) for idx, (hh, qh) in enumerate(streams)]
    gates = [jnp.dot(kmean_ref[0, :, head_lanes[hh]], qts[hh].astype(F32), precision=HI, preferred_element_type=F32)
             for hh in range(n_head)]
    for hh in range(n_head):
        bias = jnp.where(_moba_select(gates[hh], i) > 0.0, 0.0, NEG)
        for qh in range(n_split):
            bias_ref[hh, qh] = bias[:, qh * LANE:(qh + 1) * LANE]
    own_probs, own_stats = [], []
    for idx, (hh, qh) in enumerate(streams):
        s = jnp.where(key_i <= qry_i + qh * LANE, own_scores[idx], NEG)
        m = jnp.max(s, axis=0, keepdims=True)
        p = jnp.exp(s - m)
        own_probs.append(p.astype(BF16))
        own_stats.append((m, jnp.sum(p, axis=0, keepdims=True)))
    vt_own = [vt_ref[0, head_lanes[hh], pl.ds(own, tq)] for hh in range(n_head)]
    state = []
    for idx, (hh, qh) in enumerate(streams):
        state += [own_stats[idx][0], own_stats[idx][1], _dot(vt_own[hh], own_probs[idx])]

    def body(j0, carry, nk):
        tk = nk * tq
        start = pl.multiple_of(j0 * tq, tq)
        k_j = [k_ref[pl.ds(start, tk), head_lanes[hh]] for hh in range(n_head)]
        scores = [_dot(k_j[hh], queries[hh * n_split + qh]) for hh, qh in streams]
        probs, stats = [], []
        for idx, (hh, qh) in enumerate(streams):
            m, l = carry[3 * idx], carry[3 * idx + 1]
            s = [scores[idx][b * tq:(b + 1) * tq] + bias_ref[hh, qh, pl.ds(j0 + b, 1), :] for b in range(nk)]
            m_new = m
            for sb in s:
                m_new = jnp.maximum(m_new, jnp.max(sb, axis=0, keepdims=True))
            alpha = jnp.exp(m - m_new)
            p = [jnp.exp(sb - m_new) for sb in s]
            l_new = l * alpha
            for pb in p:
                l_new = l_new + jnp.sum(pb, axis=0, keepdims=True)
            probs.append(jnp.concatenate([pb.astype(BF16) for pb in p], axis=0))
            stats.append((m_new, l_new, alpha))
        vt_j = [vt_ref[0, head_lanes[hh], pl.ds(start, tk)] for hh in range(n_head)]
        pv = [_dot(vt_j[hh], probs[idx]) for idx, (hh, qh) in enumerate(streams)]
        out = []
        for idx in range(len(streams)):
            m_new, l_new, alpha = stats[idx]
            out += [m_new, l_new, carry[3 * idx + 2] * alpha + pv[idx]]
        return tuple(out)

    state = tuple(state)
    done = 0
    nk = MOBA_LOOP_BLOCKS
    while nk >= 1:
        n_group = (i - done) // nk
        state = lax.fori_loop(0, n_group, lambda t, c, nk=nk, done=done: body(done + t * nk, c, nk), state)
        done = done + n_group * nk
        nk //= 2
    outs = []
    for hh in range(n_head):
        halves = [state[3 * (hh * n_split + qh) + 2] / state[3 * (hh * n_split + qh) + 1] for qh in range(n_split)]
        outs.append(jnp.concatenate(halves, axis=1))
    o_ref[...] = jnp.concatenate(outs, axis=0).T.astype(BF16)


def _moba_prompt(qt, k, vt, kmean, batch, seq):
    nq = seq // MOBA_BLOCK
    hp = QKV_A // LANE
    return pl.pallas_call(
        _moba_prompt_kernel,
        grid=(batch, hp, nq),
        in_specs=[pl.BlockSpec((1, LANE, MOBA_BLOCK), lambda b, h, i: (b, h, i)),
                  pl.BlockSpec((seq, LANE), lambda b, h, i: (b, h)),
                  pl.BlockSpec((1, LANE, seq), lambda b, h, i: (b, h, 0)),
                  pl.BlockSpec((1, nq, LANE), lambda b, h, i: (b, 0, h))],
        out_specs=pl.BlockSpec((MOBA_BLOCK, LANE), lambda b, h, i: (b * nq + i, h)),
        out_shape=jax.ShapeDtypeStruct((batch * seq, QKV_A), BF16),
        scratch_shapes=[pltpu.VMEM((LANE // D_A, MOBA_BLOCK // LANE, nq, LANE), F32)],
        compiler_params=_cparams("parallel", "parallel", "arbitrary"),
        name="moba_prompt",
    )(qt, k, vt, kmean)


def _proj_ffn_kernel(*refs, n_mix):
    mix_refs = refs[:n_mix]
    x_ref, wo_ref, g_ref, wg_ref, wu_ref, wd_ref, o_ref, hn_ref = refs[n_mix:]

    @pl.when(pl.program_id(1) == 0)
    def _():
        x1 = x_ref[...]
        off = 0
        for r in mix_refs:
            width = r.shape[1]
            x1 = x1 + _dot(r[...], wo_ref[off:off + width, :])
            off += width
        o_ref[...] = x1
        hn_ref[...] = _rms(x1, g_ref[...]).astype(BF16)

    h = hn_ref[...]
    act = (_silu(_dot(h, wg_ref[...])) * _dot(h, wu_ref[...])).astype(BF16)
    o_ref[...] += _dot(act, wd_ref[...])


def _col_blocks(w, tf):
    *lead, d, ff = w.shape
    n = len(lead)
    return jnp.transpose(w.reshape(*lead, d, ff // tf, tf), (*range(n), n + 1, n, n + 2))


def _proj_ffn(mixes, x, w_out, g, w_gate, w_up, w_down, tm):
    m, d = x.shape
    nf, _, tf = w_gate.shape
    mix_specs = [pl.BlockSpec((tm, a.shape[1]), lambda i, f: (i, 0)) for a in mixes]
    return pl.pallas_call(
        functools.partial(_proj_ffn_kernel, n_mix=len(mixes)),
        grid=(m // tm, nf),
        in_specs=mix_specs + [
            pl.BlockSpec((tm, d), lambda i, f: (i, 0)),
            pl.BlockSpec(w_out.shape, lambda i, f: (0, 0)),
            pl.BlockSpec((1, d), lambda i, f: (0, 0)),
            pl.BlockSpec((None, d, tf), lambda i, f: (f, 0, 0)),
            pl.BlockSpec((None, d, tf), lambda i, f: (f, 0, 0)),
            pl.BlockSpec((tf, d), lambda i, f: (f, 0))],
        out_specs=pl.BlockSpec((tm, d), lambda i, f: (i, 0)),
        out_shape=jax.ShapeDtypeStruct((m, d), F32),
        scratch_shapes=[pltpu.VMEM((tm, d), BF16)],
        compiler_params=_cparams("parallel", "arbitrary"),
        name="proj_ffn",
    )(*mixes, x, w_out, g, w_gate, w_up, w_down)


def _even_layer_prompt(x, p, batch, seq):
    z = _norm_matmul(x, p["norm_mix"], p["w_in"], 512)
    tables = _rope_tables(jnp.arange(seq, dtype=jnp.int32))
    qt, k, vt, kv, kmean, gv, b_out = _even_epilogue(
        z, tables, p["ln_g"], p["ln_b"], p["w_sp"], p["b_sp"], batch, seq, 256)
    a_out = _moba_prompt(qt, k, vt, kmean.reshape(batch, seq // MOBA_BLOCK, QKV_A), batch, seq)
    x = _proj_ffn([a_out, b_out], x, p["w_out"], p["norm_ffn"], p["w_gate"], p["w_up"], p["w_down"], 512)
    n_open = seq - ((seq - 1) // CHUNK_B) * CHUNK_B
    gv_open = gv.reshape(batch, seq, D_B)[:, seq - n_open:]
    kv = jnp.transpose(kv.reshape(batch, 2, H_A, D_A, seq), (0, 4, 1, 2, 3))
    return x, kv, gv_open


def _even_params(i, ev_norm_mix, ev_w_in, ev_gmlp_ln_g, ev_gmlp_ln_b, ev_w_spatial, ev_b_spatial, ev_w_out,
                 ev_norm_ffn, ev_w_gate, ev_w_up, ev_w_down):
    return dict(norm_mix=ev_norm_mix[i][None], w_in=ev_w_in[i].astype(BF16),
                ln_g=ev_gmlp_ln_g[i][None], ln_b=ev_gmlp_ln_b[i][None],
                w_sp=ev_w_spatial[i], b_sp=ev_b_spatial[i], w_out=ev_w_out[i].astype(BF16),
                norm_ffn=ev_norm_ffn[i][None], w_gate=_col_blocks(ev_w_gate[i].astype(BF16), FFN_TF),
                w_up=_col_blocks(ev_w_up[i].astype(BF16), FFN_TF), w_down=ev_w_down[i].astype(BF16))


O_CQ, O_CK, O_CV, O_CR, O_DQKV, O_DG, O_SM = 0, 256, 512, 1024, 1536, 2560, 3072
SM_DA, SM_DB = GATE_RANK, GATE_RANK + H_D
IN_ODD_PAD = O_SM + LANE
KEY_C = H_C * DK_C
PAIR = LANE // DK_C


def _log_sigmoid(x):
    return jnp.minimum(x, 0.0) - jnp.log1p(jnp.exp(-jnp.abs(x)))


def _split_bf16(a, terms):
    parts = []
    for _ in range(terms - 1):
        hi = a.astype(BF16)
        parts.append(hi)
        a = a - hi.astype(F32)
    parts.append(a.astype(BF16))
    return parts


def _dot_select(a, sel_bf16, terms):
    parts = _split_bf16(a, terms)
    out = _dot(parts[0], sel_bf16)
    for part in parts[1:]:
        out = out + _dot(part, sel_bf16)
    return out


def _select_dot(sel_bf16, a, terms):
    parts = _split_bf16(a, terms)
    out = _dot(sel_bf16, parts[0])
    for part in parts[1:]:
        out = out + _dot(sel_bf16, part)
    return out


def _unit_lower_inverses(mats):
    c = mats[0].shape[0]
    assert c == 64
    eye = (lax.broadcasted_iota(jnp.int32, (c, c), 0) == lax.broadcasted_iota(jnp.int32, (c, c), 1)).astype(F32)
    sp = lambda ms: [_split_bf16(m, 2) for m in ms]
    mul = lambda aa, bb: [_dot(ah, bh) + _dot(ah, bl) + _dot(al, bh) for (ah, al), (bh, bl) in zip(aa, bb)]
    x1 = [-a for a in mats]
    x1s = sp(x1)
    x2 = mul(x1s, x1s)
    x2s = sp(x2)
    x4 = mul(x2s, x2s)
    x3 = mul(x1s, x2s)
    x4s = sp(x4)
    x8 = mul(x4s, x4s)
    f01 = [eye + a + b + d for a, b, d in zip(x1, x2, x3)]
    x8s = sp(x8)
    x16 = mul(x8s, x8s)
    x12 = mul(x4s, x8s)
    g23 = [a + b + d for a, b, d in zip(x4, x8, x12)]
    x16s = sp(x16)
    x32 = mul(x16s, x16s)
    f0123 = [f + d for f, d in zip(f01, mul(sp(f01), sp(g23)))]
    x48 = mul(x16s, sp(x32))
    g45 = [a + b + d for a, b, d in zip(x16, x32, x48)]
    return [f + d for f, d in zip(f0123, mul(sp(f0123), sp(g45)))]


def _group_sumsq(y, width):
    n = y.shape[1]
    same = (lax.broadcasted_iota(jnp.int32, (n, n), 0) // width
            == lax.broadcasted_iota(jnp.int32, (n, n), 1) // width).astype(BF16)
    return _dot_select(y * y, same, 2)


def _odd_mixer_prompt_kernel(z_ref, wg2_ref, bg2_ref, gnorm_ref, cw_ref, alog_ref, dtb_ref, dnorm_ref,
                             o_ref, sg_ref, sd_ref, tail_ref,
                             stg_ref, std_ref, prev_ref, la_ref, qkv_ref, dla_ref, beta_ref):
    i = pl.program_id(1)
    ns, tc = z_ref.shape[0], z_ref.shape[1]
    c = CHUNK_LIN

    @pl.when(i == 0)
    def _():
        stg_ref[...] = jnp.zeros_like(stg_ref)
        std_ref[...] = jnp.zeros_like(std_ref)
        prev_ref[...] = jnp.zeros_like(prev_ref)

    row8 = lax.broadcasted_iota(jnp.int32, (8, C_CONV), 0)
    qscale = jnp.where(lax.broadcasted_iota(jnp.int32, (1, 2 * KEY_C), 1) < KEY_C, DK_D ** -0.5, 1.0)
    for sq in range(ns):
        small = z_ref[sq, :, O_SM:O_SM + LANE]
        pre = jnp.dot(small, wg2_ref[...], precision=HI, preferred_element_type=F32) + bg2_ref[...]
        la_ref[sq] = _log_sigmoid(pre) / GATE_NORM
        dla_ref[sq] = -jnp.exp(alog_ref[...]) * _softplus(small + dtb_ref[...])
        beta_ref[sq] = jax.nn.sigmoid(small)

        x = z_ref[sq, :, O_DQKV:O_DQKV + C_CONV]
        x8 = x[0:8]
        p8 = prev_ref[sq]
        y = x * cw_ref[CONV_W - 1:CONV_W, :]
        y8 = x8 * cw_ref[CONV_W - 1:CONV_W, :]
        for s in range(1, CONV_W):
            wrow = cw_ref[CONV_W - 1 - s:CONV_W - s, :]
            y = y + pltpu.roll(x, s, 0) * wrow
            y8 = y8 + jnp.where(row8 < s, pltpu.roll(p8, s, 0), pltpu.roll(x8, s, 0)) * wrow
        prev_ref[sq] = x[tc - 8:tc]
        y = _silu(y)
        y8 = _silu(y8)
        yqk = y[:, 0:2 * KEY_C]
        nrm = lax.rsqrt(_group_sumsq(yqk, DK_D) + EPS)
        qkv_ref[sq, :, 0:2 * KEY_C] = yqk * nrm * qscale
        qkv_ref[sq, :, 2 * KEY_C:] = y[:, 2 * KEY_C:]
        yqk8 = y8[:, 0:2 * KEY_C]
        qkv_ref[sq, 0:8, 0:2 * KEY_C] = yqk8 * lax.rsqrt(_group_sumsq(yqk8, DK_D) + EPS) * qscale
        qkv_ref[sq, 0:8, 2 * KEY_C:] = y8[:, 2 * KEY_C:]

    ri = lax.broadcasted_iota(jnp.int32, (c, c), 0)
    ci = lax.broadcasted_iota(jnp.int32, (c, c), 1)
    tril = ri >= ci
    strict = ri > ci
    lower = tril.astype(BF16)
    upper = (ri <= ci).astype(BF16)
    lane_head = lax.broadcasted_iota(jnp.int32, (c, LANE), 1) // DK_C
    lane_head_row = lax.broadcasted_iota(jnp.int32, (1, LANE), 1) // DK_C

    def chunk(ch, carry):
        rows = pl.ds(pl.multiple_of(ch * c, c), c)
        n_pair = H_C // PAIR
        seqs = range(ns)
        units = [(sq, h) + divmod(h, PAIR) for sq in seqs for h in range(H_C)]
        pairs = [(sq, p) for sq in seqs for p in range(n_pair)]
        unit_at = lambda sq, p, hh: (sq * H_C) + p * PAIR + hh
        masks = [lane_head == hh for hh in range(PAIR)]
        bcum_all = [_select_dot(lower, la_ref[sq, rows, :], 3) for sq in seqs]
        dla_parts = [_split_bf16(dla_ref[sq, rows, :], 3) for sq in seqs]
        g_cols = [_dot(lower, dp[0]) + _dot(lower, dp[1]) + _dot(lower, dp[2]) for dp in dla_parts]
        g_rows = [_dot_tn(dp[0], upper) + _dot_tn(dp[1], upper) + _dot_tn(dp[2], upper) for dp in dla_parts]
        st_d = {sp: std_ref[sp[0], sp[1]] for sp in pairs}
        st_db = {sp: st_d[sp].astype(BF16) for sp in pairs}
        dn = []
        for sq, h, p, hh in units:
            g_col = jnp.broadcast_to(g_cols[sq][:, SM_DA + h:SM_DA + h + 1], (c, LANE))
            g_row = jnp.broadcast_to(g_rows[sq][SM_DA + h:SM_DA + h + 1, :], (c, c))
            beta_b = jnp.broadcast_to(beta_ref[sq, rows, SM_DB + h:SM_DB + h + 1], (c, LANE))
            km = jnp.where(masks[hh], qkv_ref[sq, rows, KEY_C + p * LANE:KEY_C + (p + 1) * LANE], 0.0)
            qm = jnp.where(masks[hh], qkv_ref[sq, rows, p * LANE:(p + 1) * LANE], 0.0)
            dn.append(dict(g_col=g_col, g_row=g_row, beta_b=beta_b, km=km, qm=qm, kmb=km.astype(BF16),
                           kb=km * beta_b, eg=jnp.exp(g_col), g_last=g_col[c - 1:c, :]))
        kk = [_dot_nt(d["kb"].astype(BF16), d["kmb"]) for d in dn]
        qk = [_dot_nt(d["qm"].astype(BF16), d["kmb"]) for d in dn]
        d_inter = [_dot_nt((d["qm"] * d["eg"]).astype(BF16), st_db[(sq, p)]) for d, (sq, h, p, hh) in zip(dn, units)]
        decay = [jnp.exp(jnp.where(tril, d["g_col"][:, 0:c] - d["g_row"], NEG)) for d in dn]
        a_mats = [jnp.where(strict, m * dc, 0.0) for m, dc in zip(kk, decay)]
        st_g = {sp: stg_ref[sp[0], sp[1]] for sp in pairs}
        st_gb = {sp: st_g[sp].astype(BF16) for sp in pairs}
        gl_pair = {}
        for sq, p in pairs:
            bcum = bcum_all[sq][:, p * LANE:(p + 1) * LANE]
            b_end = bcum[c - 1:c, :]
            k = z_ref[sq, rows, O_CK + p * LANE:O_CK + (p + 1) * LANE]
            gl_pair[(sq, p)] = dict(
                q_in=z_ref[sq, rows, O_CQ + p * LANE:O_CQ + (p + 1) * LANE] * (DK_C ** -0.5) * jnp.exp(bcum),
                k_in=(k * jnp.exp(-bcum)).astype(BF16), k_end=k * jnp.exp(b_end - bcum), b_end=b_end)
        g_qm = [jnp.where(masks[hh], gl_pair[(sq, p)]["q_in"], 0.0).astype(BF16) for sq, h, p, hh in units]
        g_ke = [jnp.where(masks[hh], gl_pair[(sq, p)]["k_end"], 0.0).astype(BF16) for sq, h, p, hh in units]
        g_vb = [z_ref[sq, rows, O_CV + h * DV_C:O_CV + (h + 1) * DV_C].astype(BF16) for sq, h, p, hh in units]
        g_sc = [_dot_nt(g_qm[u], gl_pair[(sq, p)]["k_in"]) for u, (sq, h, p, hh) in enumerate(units)]
        g_inter = [_dot_nt(g_qm[u], st_gb[(sq, p)]) for u, (sq, h, p, hh) in enumerate(units)]
        g_upd = [_dot_tn(g_vb[u], g_ke[u]) for u in range(len(units))]
        g_intra = [_dot(jnp.where(tril, g_sc[u], 0.0).astype(BF16), g_vb[u]) for u in range(len(units))]
        t_invs = _unit_lower_inverses(a_mats)
        for u, (sq, h, p, hh) in enumerate(units):
            gate = _silu(z_ref[sq, rows, O_CR + h * DV_C:O_CR + (h + 1) * DV_C])
            o_ref[sq, rows, h * DV_C:(h + 1) * DV_C] = (
                _rms(g_intra[u] + g_inter[u], gnorm_ref[...]) * gate).astype(BF16)
        for sq, p in pairs:
            stg_ref[sq, p] = (st_g[(sq, p)] * jnp.exp(gl_pair[(sq, p)]["b_end"])
                              + g_upd[unit_at(sq, p, 0)] + g_upd[unit_at(sq, p, 1)])
        rhs = [jnp.concatenate([qkv_ref[sq, rows, 2 * KEY_C + h * DV_D:2 * KEY_C + (h + 1) * DV_D] * d["beta_b"],
                                d["kb"] * d["eg"]], axis=1).astype(BF16) for d, (sq, h, p, hh) in zip(dn, units)]
        sol = [_dot(t.astype(BF16), r) for t, r in zip(t_invs, rhs)]
        w_st = [_dot_nt(s[:, DV_D:].astype(BF16), st_db[(sq, p)]) for s, (sq, h, p, hh) in zip(sol, units)]
        v_new = [(s[:, 0:DV_D] - ws).astype(BF16) for s, ws in zip(sol, w_st)]
        d_intra = [_dot(jnp.where(tril, q * dc, 0.0).astype(BF16), vn) for q, dc, vn in zip(qk, decay, v_new)]
        d_upd = [_dot_tn(vn, (d["km"] * jnp.exp(d["g_last"] - d["g_col"])).astype(BF16)) for vn, d in zip(v_new, dn)]
        for u, (sq, h, p, hh) in enumerate(units):
            gate = _silu(z_ref[sq, rows, O_DG + h * DV_D:O_DG + (h + 1) * DV_D])
            col = H_C * DV_C + h * DV_D
            o_ref[sq, rows, col:col + DV_D] = (_rms(d_inter[u] + d_intra[u], dnorm_ref[...]) * gate).astype(BF16)
        for sq, p in pairs:
            u0, u1 = unit_at(sq, p, 0), unit_at(sq, p, 1)
            dec_row = jnp.where(lane_head_row == 0, jnp.exp(dn[u0]["g_last"]), jnp.exp(dn[u1]["g_last"]))
            std_ref[sq, p] = st_d[(sq, p)] * dec_row + d_upd[u0] + d_upd[u1]
        return carry

    lax.fori_loop(0, tc // c, chunk, 0)

    @pl.when(i == pl.num_programs(1) - 1)
    def _():
        for sq in range(ns):
            tail_ref[sq] = prev_ref[sq]
            for p in range(H_C // PAIR):
                tg = stg_ref[sq, p].T
                td = std_ref[sq, p].T
                for hh in range(PAIR):
                    sg_ref[sq, p * PAIR + hh] = tg[hh * DK_C:(hh + 1) * DK_C, :]
                    sd_ref[sq, p * PAIR + hh] = td[hh * DK_D:(hh + 1) * DK_D, :]


def _odd_mixer_prompt(z, p, batch, seq, tc):
    nt = seq // tc
    ns = 2 if batch % 2 == 0 else 1
    full = lambda a: pl.BlockSpec(a.shape, lambda b, i: (0,) * a.ndim)
    consts = [p["w_g2_pad"], p["b_g2"], p["gla_norm"], p["conv_w"], p["alog_slab"], p["dtb_slab"], p["delta_norm"]]
    st_spec = pl.BlockSpec((ns, H_C, DK_C, DV_C), lambda b, i: (b, 0, 0, 0))
    mix, s_gla, s_delta, tail = pl.pallas_call(
        _odd_mixer_prompt_kernel,
        grid=(batch // ns, nt),
        in_specs=[pl.BlockSpec((ns, tc, IN_ODD_PAD), lambda b, i: (b, i, 0))] + [full(a) for a in consts],
        out_specs=[pl.BlockSpec((ns, tc, D_MODEL), lambda b, i: (b, i, 0)), st_spec, st_spec,
                   pl.BlockSpec((ns, 8, C_CONV), lambda b, i: (b, 0, 0))],
        out_shape=[jax.ShapeDtypeStruct((batch, seq, D_MODEL), BF16),
                   jax.ShapeDtypeStruct((batch, H_C, DK_C, DV_C), F32),
                   jax.ShapeDtypeStruct((batch, H_D, DK_D, DV_D), F32),
                   jax.ShapeDtypeStruct((batch, 8, C_CONV), F32)],
        scratch_shapes=[pltpu.VMEM((ns, H_C // PAIR, DV_C, LANE), F32),
                        pltpu.VMEM((ns, H_D // PAIR, DV_D, LANE), F32),
                        pltpu.VMEM((ns, 8, C_CONV), F32),
                        pltpu.VMEM((ns, tc, KEY_C), F32),
                        pltpu.VMEM((ns, tc, C_CONV), F32),
                        pltpu.VMEM((ns, tc, LANE), F32),
                        pltpu.VMEM((ns, tc, LANE), F32)],
        compiler_params=_cparams("parallel", "arbitrary"),
        name="odd_mixer_prompt",
    )(z.reshape(batch, seq, IN_ODD_PAD), *consts)
    return mix.reshape(batch * seq, D_MODEL), s_gla, s_delta, tail


M_E1, M_E2, M_R1, M_R2, M_G1, M_G2 = (N_EXPERTS + j for j in range(6))


def _proj_router_kernel(mix_ref, x_ref, wo_ref, g_ref, wr_ref, x1_ref, hn_ref, meta_ref, cnt_ref, carry_ref):
    tm = x_ref.shape[0]

    @pl.when(pl.program_id(0) == 0)
    def _():
        carry_ref[...] = jnp.zeros_like(carry_ref)

    x1 = x_ref[...] + _dot(mix_ref[...], wo_ref[...])
    x1_ref[...] = x1
    hn = _rms(x1, g_ref[...])
    hn_ref[...] = hn.astype(hn_ref.dtype)
    lane = lax.broadcasted_iota(jnp.int32, (tm, LANE), 1)
    hn_hi, hn_lo = _split_bf16(hn, 2)
    wr_hi, wr_lo = _split_bf16(wr_ref[...], 2)
    logits = _dot(hn_hi, wr_hi) + _dot(hn_hi, wr_lo) + _dot(hn_lo, wr_hi)
    logits = jnp.where(lane < N_EXPERTS, logits, NEG)
    m1 = jnp.max(logits, axis=1, keepdims=True)
    e1 = jnp.min(jnp.where(logits == m1, lane, LANE), axis=1, keepdims=True)
    rest = jnp.where(lane == e1, NEG, logits)
    m2 = jnp.max(rest, axis=1, keepdims=True)
    e2 = jnp.min(jnp.where(rest == m2, lane, LANE), axis=1, keepdims=True)
    t = jnp.exp(m2 - m1)
    g1 = 1.0 / (1.0 + t)
    g2 = t / (1.0 + t)
    oh1 = lane == e1
    oh2 = lane == e2
    member = (oh1 | oh2).astype(F32)
    ri = lax.broadcasted_iota(jnp.int32, (tm, tm), 0)
    ci = lax.broadcasted_iota(jnp.int32, (tm, tm), 1)
    before = _dot((ri > ci).astype(BF16), member.astype(BF16)) + carry_ref[...]
    r1 = jnp.sum(jnp.where(oh1, before, 0.0), axis=1, keepdims=True)
    r2 = jnp.sum(jnp.where(oh2, before, 0.0), axis=1, keepdims=True)
    carry_ref[...] = carry_ref[...] + jnp.sum(member, axis=0, keepdims=True)
    cnt_ref[...] = carry_ref[...]
    meta = jnp.where(oh1, g1, 0.0) + jnp.where(oh2, g2, 0.0)
    meta = jnp.where(lane == M_E1, e1.astype(F32), meta)
    meta = jnp.where(lane == M_E2, e2.astype(F32), meta)
    meta = jnp.where(lane == M_R1, r1, meta)
    meta = jnp.where(lane == M_R2, r2, meta)
    meta = jnp.where(lane == M_G1, g1, meta)
    meta = jnp.where(lane == M_G2, g2, meta)
    meta_ref[...] = meta


def _proj_router(mix, x, w_out, g, w_router_pad, tm, hn_dtype):
    m, d = x.shape
    row = lambda width: pl.BlockSpec((tm, width), lambda i: (i, 0))
    full = lambda a: pl.BlockSpec(a.shape, lambda i: (0,) * a.ndim)
    return pl.pallas_call(
        _proj_router_kernel,
        grid=(m // tm,),
        in_specs=[row(d), row(d), full(w_out), full(g), full(w_router_pad)],
        out_specs=[row(d), row(d), row(LANE), pl.BlockSpec((1, LANE), lambda i: (0, 0))],
        out_shape=[jax.ShapeDtypeStruct((m, d), F32), jax.ShapeDtypeStruct((m, d), hn_dtype),
                   jax.ShapeDtypeStruct((m, LANE), F32), jax.ShapeDtypeStruct((1, LANE), F32)],
        scratch_shapes=[pltpu.VMEM((1, LANE), F32)],
        compiler_params=_cparams("arbitrary"),
        name="proj_router",
    )(mix, x, w_out, g, w_router_pad)


def _moe_dense_kernel(hn_ref, x1_ref, meta_ref, wg_ref, wu_ref, wd_ref, gf_ref, o_ref):
    e = pl.program_id(1)
    f = pl.program_id(2)

    @pl.when((e == 0) & (f == 0))
    def _():
        o_ref[...] = x1_ref[...]

    h = hn_ref[...]
    act = (_silu(_dot(h, wg_ref[...].astype(BF16))) * _dot(h, wu_ref[...].astype(BF16))).astype(BF16)
    lane = lax.broadcasted_iota(jnp.int32, meta_ref.shape, 1)
    gate = jnp.sum(jnp.where(lane == e, meta_ref[...], 0.0), axis=1, keepdims=True)
    o_ref[...] += gate * _dot(act, wd_ref[...].astype(BF16))

    @pl.when((e == pl.num_programs(1) - 1) & (f == pl.num_programs(2) - 1))
    def _():
        o_ref[...] = _rms(o_ref[...], gf_ref[...])


def _moe_dense(hn, x1, meta, w_gate, w_up, w_down, g_final, tm):
    m, d = x1.shape
    n_e, _, ff = w_gate.shape
    tf = MOE_TF
    row = lambda width: pl.BlockSpec((tm, width), lambda i, e, f: (i, 0))
    return pl.pallas_call(
        _moe_dense_kernel,
        grid=(m // tm, n_e, ff // tf),
        in_specs=[row(d), row(d), row(LANE),
                  pl.BlockSpec((None, d, tf), lambda i, e, f: (e, 0, f)),
                  pl.BlockSpec((None, d, tf), lambda i, e, f: (e, 0, f)),
                  pl.BlockSpec((None, tf, d), lambda i, e, f: (e, f, 0)),
                  pl.BlockSpec((1, d), lambda i, e, f: (0, 0))],
        out_specs=row(d),
        out_shape=jax.ShapeDtypeStruct((m, d), F32),
        compiler_params=_cparams("parallel", "arbitrary", "arbitrary"),
        name="moe_dense",
    )(hn, x1, meta, w_gate, w_up, w_down, g_final)


def _moe_dispatch_kernel(dest_ref, hn_ref, xs_in_ref, xs_ref, sem):
    del xs_in_ref
    tm = hn_ref.shape[0]
    base = pl.program_id(0) * (TOP_K * tm)

    def row_copy(t, k):
        return pltpu.make_async_copy(hn_ref.at[pl.ds(t, 1)], xs_ref.at[pl.ds(dest_ref[base + k * tm + t], 1)], sem)

    def issue(t, carry):
        for k in range(TOP_K):
            row_copy(t, k).start()
        return carry

    def drain(t, carry):
        for k in range(TOP_K):
            row_copy(t, k).wait()
        return carry

    lax.fori_loop(0, tm, issue, 0, unroll=DMA_UNROLL)
    lax.fori_loop(0, tm, drain, 0, unroll=DMA_UNROLL)


def _moe_dispatch(dest, hn, n_rows, tm):
    m, d = hn.shape
    return pl.pallas_call(
        _moe_dispatch_kernel,
        grid_spec=pltpu.PrefetchScalarGridSpec(
            num_scalar_prefetch=1,
            grid=(m // tm,),
            in_specs=[pl.BlockSpec((tm, d), lambda i, dest: (i, 0)), pl.BlockSpec(memory_space=pl.ANY)],
            out_specs=pl.BlockSpec(memory_space=pl.ANY),
            scratch_shapes=[pltpu.SemaphoreType.DMA(())]),
        out_shape=jax.ShapeDtypeStruct((n_rows, d), hn.dtype),
        input_output_aliases={2: 0},
        compiler_params=_cparams("arbitrary"),
        name="moe_dispatch",
    )(dest, hn, jnp.zeros((n_rows, d), hn.dtype))


def _moe_grouped_kernel(te_ref, nv_ref, xs_ref, wg_ref, wu_ref, wd_ref, o_ref, xb_ref):
    del te_ref
    f = pl.program_id(1)

    @pl.when(f == 0)
    def _():
        o_ref[...] = jnp.zeros_like(o_ref)
        xb_ref[...] = xs_ref[...].astype(BF16)

    @pl.when(pl.program_id(0) < nv_ref[0])
    def _():
        h = xb_ref[...]
        act = (_silu(_dot(h, wg_ref[...].astype(BF16))) * _dot(h, wu_ref[...].astype(BF16))).astype(BF16)
        o_ref[...] += _dot(act, wd_ref[...].astype(BF16))


def _moe_grouped(tile_expert, n_valid, xs, w_gate, w_up, w_down, tg):
    rows, d = xs.shape
    tf = MOE_TF
    nf = w_gate.shape[2] // tf
    fidx = lambda r, f, nv: jnp.where(r < nv[0], f, nf - 1)
    return pl.pallas_call(
        _moe_grouped_kernel,
        grid_spec=pltpu.PrefetchScalarGridSpec(
            num_scalar_prefetch=2,
            grid=(rows // tg, nf),
            in_specs=[pl.BlockSpec((tg, d), lambda r, f, te, nv: (r, 0)),
                      pl.BlockSpec((None, d, tf), lambda r, f, te, nv: (te[r], 0, fidx(r, f, nv))),
                      pl.BlockSpec((None, d, tf), lambda r, f, te, nv: (te[r], 0, fidx(r, f, nv))),
                      pl.BlockSpec((None, tf, d), lambda r, f, te, nv: (te[r], fidx(r, f, nv), 0))],
            out_specs=pl.BlockSpec((tg, d), lambda r, f, te, nv: (r, 0)),
            scratch_shapes=[pltpu.VMEM((tg, d), BF16)]),
        out_shape=jax.ShapeDtypeStruct((rows, d), F32),
        compiler_params=_cparams("parallel", "arbitrary"),
        name="moe_grouped",
    )(tile_expert, n_valid, xs, w_gate, w_up, w_down)


def _moe_combine_kernel(dest_ref, ys_ref, x1_ref, meta_ref, gf_ref, o_ref, buf_ref, sem):
    i = pl.program_id(0)
    tc = x1_ref.shape[0]
    rows = TOP_K * tc

    def row_copy(step, slot, j):
        return pltpu.make_async_copy(ys_ref.at[pl.ds(dest_ref[step * rows + j], 1)],
                                     buf_ref.at[slot, pl.ds(j, 1)], sem.at[slot])

    def issue(step, slot):
        def body(j, carry):
            row_copy(step, slot, j).start()
            return carry
        lax.fori_loop(0, rows, body, 0, unroll=DMA_UNROLL)

    def drain(step, slot):
        def body(j, carry):
            row_copy(step, slot, j).wait()
            return carry
        lax.fori_loop(0, rows, body, 0, unroll=DMA_UNROLL)

    slot = i % 2

    @pl.when(i == 0)
    def _():
        issue(0, 0)

    @pl.when(i + 1 < pl.num_programs(0))
    def _():
        issue(i + 1, 1 - slot)

    drain(i, slot)
    g1 = meta_ref[:, M_G1:M_G1 + 1]
    g2 = meta_ref[:, M_G2:M_G2 + 1]
    y = x1_ref[...] + g1 * buf_ref[slot, 0:tc] + g2 * buf_ref[slot, tc:rows]
    o_ref[...] = _rms(y, gf_ref[...])


def _moe_combine(dest, ys, x1, meta, g_final, tc):
    m, d = x1.shape
    row = lambda width: pl.BlockSpec((tc, width), lambda i, dest: (i, 0))
    return pl.pallas_call(
        _moe_combine_kernel,
        grid_spec=pltpu.PrefetchScalarGridSpec(
            num_scalar_prefetch=1,
            grid=(m // tc,),
            in_specs=[pl.BlockSpec(memory_space=pl.ANY), row(d), row(LANE),
                      pl.BlockSpec((1, d), lambda i, dest: (0, 0))],
            out_specs=row(d),
            scratch_shapes=[pltpu.VMEM((2, TOP_K * tc, d), F32), pltpu.SemaphoreType.DMA((2,))]),
        out_shape=jax.ShapeDtypeStruct((m, d), F32),
        compiler_params=_cparams("arbitrary"),
        name="moe_combine",
    )(dest, ys, x1, meta, g_final)


def _moe_routes(meta, counts, tile, tg):
    m = meta.shape[0]
    n_rows = TOP_K * m + N_EXPERTS * tg
    expert = meta[:, M_E1:M_E2 + 1].astype(jnp.int32)
    rank = meta[:, M_R1:M_R2 + 1].astype(jnp.int32)
    padded = (counts[0, :N_EXPERTS].astype(jnp.int32) + tg - 1) // tg * tg
    ends = jnp.cumsum(padded)
    dest = (ends - padded)[expert] + rank
    dest = jnp.transpose(dest.reshape(m // tile, tile, TOP_K), (0, 2, 1)).reshape(-1)
    n_valid = ends[-1] // tg
    tile_start = jnp.arange(n_rows // tg, dtype=jnp.int32) * tg
    probe = jnp.minimum(tile_start, ends[-1] - 1)
    tile_expert = jnp.sum(ends[None, :] <= probe[:, None], axis=1).astype(jnp.int32)
    return dest, tile_expert, n_valid.reshape(1).astype(jnp.int32), n_rows


def _odd_params(i, od_norm_mix, od_w_in, od_gla_w_gate2, od_gla_b_gate2, od_gla_norm, od_delta_conv,
                od_delta_a_log, od_delta_dt_bias, od_delta_norm, od_w_out, od_norm_ffn, od_router,
                od_w_gate, od_w_up, od_w_down):
    w = od_w_in[i]
    sizes = (KEY_C, KEY_C, H_C * DV_C, GATE_RANK, H_C * DV_C, C_CONV, H_D, H_D, H_D * DV_D)
    splits = tuple(sum(sizes[:j + 1]) for j in range(len(sizes) - 1))
    cq, ck, cv, c_lr, c_r, d_qkv, d_a, d_b, d_g = jnp.split(w, splits, axis=1)
    pad = jnp.zeros((w.shape[0], IN_ODD_PAD - w.shape[1]), w.dtype)
    w_in = jnp.concatenate([cq, ck, cv, c_r, d_qkv, d_g, c_lr, d_a, d_b, pad], axis=1).astype(BF16)
    slab = lambda v: jnp.zeros((1, LANE), F32).at[0, SM_DA:SM_DA + H_D].set(v)
    return dict(norm_mix=od_norm_mix[i][None], w_in=w_in,
                w_g2_pad=jnp.zeros((LANE, KEY_C), F32).at[:GATE_RANK].set(od_gla_w_gate2[i]),
                b_g2=od_gla_b_gate2[i][None], gla_norm=od_gla_norm[i][None], conv_w=od_delta_conv[i],
                alog_slab=slab(od_delta_a_log[i]), dtb_slab=slab(od_delta_dt_bias[i]),
                a_log=od_delta_a_log[i], dt_bias=od_delta_dt_bias[i],
                delta_norm=od_delta_norm[i][None], w_out=od_w_out[i].astype(BF16),
                norm_ffn=od_norm_ffn[i][None],
                router=jnp.zeros((D_MODEL, LANE), F32).at[:, :N_EXPERTS].set(od_router[i]),
                w_gate=od_w_gate[i], w_up=od_w_up[i], w_down=od_w_down[i])


MOE_GROUP_TILE = 1024
MOE_ROW_TILE = 256


def _moe(mix, x, p, g_final, tm):
    m = x.shape[0]
    if m < N_EXPERTS * MOE_GROUP_TILE:
        x1, hn, meta, _ = _proj_router(mix, x, p["w_out"], p["norm_ffn"], p["router"], tm, BF16)
        return _moe_dense(hn, x1, meta, p["w_gate"], p["w_up"], p["w_down"], g_final, tm)
    x1, hn, meta, counts = _proj_router(mix, x, p["w_out"], p["norm_ffn"], p["router"], tm, F32)
    dest, tile_expert, n_valid, n_rows = _moe_routes(meta, counts, MOE_ROW_TILE, MOE_GROUP_TILE)
    xs = _moe_dispatch(dest, hn, n_rows, MOE_ROW_TILE)
    ys = _moe_grouped(tile_expert, n_valid, xs, p["w_gate"], p["w_up"], p["w_down"], MOE_GROUP_TILE)
    return _moe_combine(dest, ys, x1, meta, g_final, MOE_ROW_TILE)


def _odd_layer_prompt(x, p, g_final, batch, seq):
    z = _norm_matmul(x, p["norm_mix"], p["w_in"], 512)
    mix, s_gla, s_delta, tail = _odd_mixer_prompt(z, p, batch, seq, 256)
    y = _moe(mix, x, p, g_final, 512)
    return y, s_gla, s_delta, tail[:, 8 - (CONV_W - 1):]


def _even_epilogue_sample_kernel(z_ref, c_ref, su_ref, sd_ref, lng_ref, lnb_ref, w0_ref, b0_ref,
                                 q_ref, kv_ref, gv_ref, bout_ref):
    c, su, sd = c_ref[...], su_ref[...], sd_ref[...]
    q_ref[...] = _rope(z_ref[:, 0:QKV_A], c, su, sd) * (D_A ** -0.5)
    kv_ref[:, 0:QKV_A] = _rope(z_ref[:, QKV_A:2 * QKV_A], c, su, sd)
    kv_ref[:, QKV_A:2 * QKV_A] = z_ref[:, 2 * QKV_A:3 * QKV_A]
    u = _gelu(z_ref[:, 3 * QKV_A:3 * QKV_A + D_B])
    gv = _layernorm(_gelu(z_ref[:, 3 * QKV_A + D_B:3 * QKV_A + 2 * D_B]), lng_ref[...], lnb_ref[...])
    gv_ref[...] = gv
    bout_ref[...] = (u * (gv * w0_ref[...] + b0_ref[...])).astype(BF16)


def _even_epilogue_sample(z, tables, ln_g, ln_b, w_sp, b_sp):
    m = z.shape[0]
    w0 = jnp.repeat(w_sp[:, 0, 0], D_BG)[None]
    b0 = jnp.repeat(b_sp[:, 0], D_BG)[None]
    return pl.pallas_call(
        _even_epilogue_sample_kernel,
        out_shape=[jax.ShapeDtypeStruct((m, QKV_A), F32), jax.ShapeDtypeStruct((m, 2 * QKV_A), F32),
                   jax.ShapeDtypeStruct((m, D_B), F32), jax.ShapeDtypeStruct((m, D_B), BF16)],
        compiler_params=pltpu.CompilerParams(vmem_limit_bytes=VMEM_LIMIT),
        name="even_epilogue_sample",
    )(z, *tables, ln_g, ln_b, w0, b0)


def _moba_sample_kernel(pt_ref, qt_ref, knt_ref, vnt_ref, *refs):
    del pt_ref
    n_pages = len(refs) - 2
    page_refs, o_ref, s_ref = refs[:n_pages], refs[n_pages], refs[n_pages + 1]
    pages_per_block = MOBA_BLOCK // PAGE_SIZE
    nb = n_pages // pages_per_block
    qt = qt_ref[0]
    for h in range(H_A):
        qcol = jnp.broadcast_to(qt[:, h:h + 1], (D_A, PAGE_SIZE))
        for j in range(n_pages):
            s_ref[h, j:j + 1, :] = jnp.sum(page_refs[j][0, 0, h] * qcol, axis=0, keepdims=True)
    lane = lax.broadcasted_iota(jnp.int32, (n_pages, LANE), 1)
    page_sums = jnp.zeros((n_pages, LANE), F32)
    for h in range(H_A):
        page_sums = jnp.where(lane == h, jnp.sum(s_ref[h], axis=1, keepdims=True), page_sums)
    pair = (lax.broadcasted_iota(jnp.int32, (nb, n_pages), 1) // pages_per_block
            == lax.broadcasted_iota(jnp.int32, (nb, n_pages), 0)).astype(F32)
    pair_t = (lax.broadcasted_iota(jnp.int32, (n_pages, nb), 0) // pages_per_block
              == lax.broadcasted_iota(jnp.int32, (n_pages, nb), 1)).astype(F32)
    gate = jnp.dot(pair, page_sums, precision=HI, preferred_element_type=F32)
    sel = _moba_select(gate, nb)
    sel_pages = jnp.dot(pair_t, sel, precision=HI, preferred_element_type=F32)
    own = jnp.sum(qt * knt_ref[0], axis=0, keepdims=True)
    vnt = vnt_ref[0]
    out_lane = lax.broadcasted_iota(jnp.int32, (D_A, LANE), 1)
    out = jnp.zeros((D_A, LANE), F32)
    for h in range(H_A):
        sm = jnp.where(sel_pages[:, h:h + 1] > 0.0, s_ref[h], NEG)
        s_own = own[:, h:h + 1]
        mx = jnp.maximum(jnp.max(jnp.max(sm, axis=1, keepdims=True), axis=0, keepdims=True), s_own)
        p = jnp.exp(sm - mx)
        p_own = jnp.exp(s_own - mx)
        denom = jnp.sum(jnp.sum(p, axis=1, keepdims=True), axis=0, keepdims=True) + p_own
        acc = jnp.zeros((D_A, PAGE_SIZE), F32)
        for j in range(n_pages):
            acc = acc + page_refs[j][0, 1, h] * p[j:j + 1, :]
        o = (jnp.sum(acc, axis=1, keepdims=True) + p_own * vnt[:, h:h + 1]) / denom
        out = jnp.where(out_lane == h, o, out)
    o_ref[0] = out


def _moba_sample(q, kv_new, cache, page_table):
    bs, n_pages = page_table.shape
    assert (n_pages * PAGE_SIZE) % MOBA_BLOCK == 0
    cache_t = jnp.transpose(cache, (0, 2, 3, 4, 1))
    page_spec = lambda j: pl.BlockSpec((1, 2, H_A, D_A, PAGE_SIZE),
                                       lambda b, pt: (pt[b * n_pages + j], 0, 0, 0, 0))
    col_spec = pl.BlockSpec((1, D_A, H_A), lambda b, pt: (b, 0, 0))
    heads_t = lambda a: jnp.transpose(a.reshape(bs, H_A, D_A), (0, 2, 1))
    out = pl.pallas_call(
        _moba_sample_kernel,
        grid_spec=pltpu.PrefetchScalarGridSpec(
            num_scalar_prefetch=1,
            grid=(bs,),
            in_specs=[col_spec, col_spec, col_spec] + [page_spec(j) for j in range(n_pages)],
            out_specs=pl.BlockSpec((1, D_A, LANE), lambda b, pt: (b, 0, 0)),
            scratch_shapes=[pltpu.VMEM((H_A, n_pages, PAGE_SIZE), F32)]),
        out_shape=jax.ShapeDtypeStruct((bs, D_A, LANE), F32),
        compiler_params=_cparams("parallel"),
        name="moba_sample",
    )(page_table.reshape(-1), heads_t(q), heads_t(kv_new[:, 0:QKV_A]), heads_t(kv_new[:, QKV_A:]),
      *([cache_t] * n_pages))
    return jnp.transpose(out[:, :, 0:H_A], (0, 2, 1)).reshape(bs, QKV_A).astype(BF16)


def _odd_mixer_sample_kernel(z_ref, buf_ref, sg_ref, sd_ref, wg2_ref, bg2_ref, gnorm_ref, cw_ref, alog_ref, dtb_ref,
                             dnorm_ref, o_ref, sgo_ref, sdo_ref):
    bt = z_ref.shape[0]
    stride = H_C * DK_C
    small = z_ref[:, O_SM:O_SM + LANE]
    pre = jnp.dot(small, wg2_ref[...], precision=HI, preferred_element_type=F32) + bg2_ref[...]
    a_all = jnp.exp(_log_sigmoid(pre) / GATE_NORM)
    g_all = -jnp.exp(alog_ref[...]) * _softplus(small + dtb_ref[...])
    beta_all = jax.nn.sigmoid(small)
    x = z_ref[:, O_DQKV:O_DQKV + C_CONV]
    y = x * cw_ref[CONV_W - 1:CONV_W, :]
    for j in range(CONV_W - 1):
        y = y + buf_ref[:, j, :] * cw_ref[j:j + 1, :]
    y = _silu(y)
    for h in range(H_C):
        ks = slice(h * DK_C, (h + 1) * DK_C)
        a = a_all[:, ks]
        q = z_ref[:, O_CQ + h * DK_C:O_CQ + (h + 1) * DK_C] * (DK_C ** -0.5)
        k = z_ref[:, O_CK + h * DK_C:O_CK + (h + 1) * DK_C]
        v = z_ref[:, O_CV + h * DV_C:O_CV + (h + 1) * DV_C]
        qa = q * a
        acc = jnp.sum(q * k, axis=1, keepdims=True) * v
        for kk in range(DK_C):
            rows = pl.ds(h * DK_C + kk, bt, stride=stride)
            srow = sg_ref[rows, :]
            acc = acc + qa[:, kk:kk + 1] * srow
            sgo_ref[rows, :] = a[:, kk:kk + 1] * srow + k[:, kk:kk + 1] * v
        gate = _silu(z_ref[:, O_CR + h * DV_C:O_CR + (h + 1) * DV_C])
        o_ref[:, h * DV_C:(h + 1) * DV_C] = (_rms(acc, gnorm_ref[...]) * gate).astype(BF16)
        yq = y[:, h * DK_D:(h + 1) * DK_D]
        yk = y[:, KEY_C + h * DK_D:KEY_C + (h + 1) * DK_D]
        dv = y[:, 2 * KEY_C + h * DV_D:2 * KEY_C + (h + 1) * DV_D]
        dq = yq * lax.rsqrt(jnp.sum(yq * yq, axis=1, keepdims=True) + EPS) * (DK_D ** -0.5)
        dk = yk * lax.rsqrt(jnp.sum(yk * yk, axis=1, keepdims=True) + EPS)
        beta = beta_all[:, SM_DB + h:SM_DB + h + 1]
        eg = jnp.exp(g_all[:, SM_DA + h:SM_DA + h + 1])
        w = dk * (beta * eg)
        qd = dq * eg
        ws = jnp.zeros((bt, DV_D), F32)
        qs = jnp.zeros((bt, DV_D), F32)
        for kk in range(DK_D):
            srow = sd_ref[pl.ds(h * DK_D + kk, bt, stride=stride), :]
            ws = ws + w[:, kk:kk + 1] * srow
            qs = qs + qd[:, kk:kk + 1] * srow
        v_new = dv * beta - ws
        o = qs + jnp.sum(dq * dk, axis=1, keepdims=True) * v_new
        for kk in range(DK_D):
            rows = pl.ds(h * DK_D + kk, bt, stride=stride)
            sdo_ref[rows, :] = sd_ref[rows, :] * eg + dk[:, kk:kk + 1] * v_new
        gate = _silu(z_ref[:, O_DG + h * DV_D:O_DG + (h + 1) * DV_D])
        col = H_C * DV_C + h * DV_D
        o_ref[:, col:col + DV_D] = (_rms(o, dnorm_ref[...]) * gate).astype(BF16)


def _odd_mixer_sample(z, conv_buf, s_gla, s_delta, p, bt):
    bs = z.shape[0]
    rows = H_C * DK_C
    full = lambda a: pl.BlockSpec(a.shape, lambda i: (0,) * a.ndim)
    consts = [p["w_g2_pad"], p["b_g2"], p["gla_norm"], p["conv_w"], p["alog_slab"], p["dtb_slab"], p["delta_norm"]]
    st_spec = pl.BlockSpec((bt * rows, DV_C), lambda i: (i, 0))
    mix, sg, sd = pl.pallas_call(
        _odd_mixer_sample_kernel,
        grid=(bs // bt,),
        in_specs=[pl.BlockSpec((bt, IN_ODD_PAD), lambda i: (i, 0)),
                  pl.BlockSpec((bt, CONV_W - 1, C_CONV), lambda i: (i, 0, 0)),
                  st_spec, st_spec] + [full(a) for a in consts],
        out_specs=[pl.BlockSpec((bt, D_MODEL), lambda i: (i, 0)), st_spec, st_spec],
        out_shape=[jax.ShapeDtypeStruct((bs, D_MODEL), BF16),
                   jax.ShapeDtypeStruct((bs * rows, DV_C), F32),
                   jax.ShapeDtypeStruct((bs * rows, DV_D), F32)],
        compiler_params=_cparams("parallel"),
        name="odd_mixer_sample",
    )(z, conv_buf, s_gla.reshape(bs * rows, DV_C), s_delta.reshape(bs * rows, DV_D), *consts)
    return mix, sg.reshape(s_gla.shape), sd.reshape(s_delta.shape)


def _sample_step(x, cache, page_table, s_gla, s_delta, conv_buf, ev, od, g_final):
    bs = x.shape[0]
    past = page_table.shape[1] * PAGE_SIZE
    z = _norm_matmul(x, ev["norm_mix"], ev["w_in"], bs)
    tables = _rope_tables(jnp.full((1,), past, jnp.int32))
    q, kv, gv, b_out = _even_epilogue_sample(z, tables, ev["ln_g"], ev["ln_b"], ev["w_sp"], ev["b_sp"])
    a_out = _moba_sample(q, kv, cache, page_table)
    x = _proj_ffn([a_out, b_out], x, ev["w_out"], ev["norm_ffn"], ev["w_gate"], ev["w_up"], ev["w_down"], bs)
    z = _norm_matmul(x, od["norm_mix"], od["w_in"], bs)
    mix, sg, sd = _odd_mixer_sample(z, conv_buf, s_gla, s_delta, od, 32)
    conv_new = jnp.concatenate([conv_buf[:, 1:], z[:, None, O_DQKV:O_DQKV + C_CONV]], axis=1)
    y = _moe(mix, x, od, g_final, bs)
    return y, kv, gv, sg, sd, conv_new


def kernel(x_prompt, x_sample, cache_kv, state_gla, state_delta, state_conv, page_table, ev_norm_mix, ev_w_in, ev_gmlp_ln_g, ev_gmlp_ln_b, ev_w_spatial, ev_b_spatial, ev_w_out, ev_norm_ffn, ev_w_gate, ev_w_up, ev_w_down, od_norm_mix, od_w_in, od_gla_w_gate2, od_gla_b_gate2, od_gla_norm, od_delta_conv, od_delta_a_log, od_delta_dt_bias, od_delta_norm, od_w_out, od_norm_ffn, od_router, od_w_gate, od_w_up, od_w_down, norm_final):
    bp, tp, d = x_prompt.shape
    ev = _even_params(0, ev_norm_mix, ev_w_in, ev_gmlp_ln_g, ev_gmlp_ln_b, ev_w_spatial, ev_b_spatial, ev_w_out,
                      ev_norm_ffn, ev_w_gate, ev_w_up, ev_w_down)
    od = _odd_params(0, od_norm_mix, od_w_in, od_gla_w_gate2, od_gla_b_gate2, od_gla_norm, od_delta_conv,
                     od_delta_a_log, od_delta_dt_bias, od_delta_norm, od_w_out, od_norm_ffn, od_router,
                     od_w_gate, od_w_up, od_w_down)
    bs, ts, _ = x_sample.shape
    assert ts == 1 and cache_kv.shape[0] == 1 and state_gla.shape[0] == 1
    xp, kv_p, gv_p = _even_layer_prompt(x_prompt.reshape(bp * tp, d), ev, bp, tp)
    yp, gla_p, dl_p, cv_p = _odd_layer_prompt(xp, od, norm_final[None], bp, tp)
    ys, kv_s, gv_s, gla_s, dl_s, cv_s = _sample_step(
        x_sample.reshape(bs, d), cache_kv[0], page_table, state_gla[0], state_delta[0], state_conv[0],
        ev, od, norm_final[None])
    return (yp.reshape(bp, tp, d), ys.reshape(bs, ts, d),
            kv_p[None], kv_s.reshape(1, bs, ts, 2, H_A, D_A),
            gv_p[None], gv_s.reshape(1, bs, ts, D_B),
            gla_p[None], gla_s[None], dl_p[None], dl_s[None], cv_p[None], cv_s[None])
```

```python
import functools
import math

import jax
import jax.numpy as jnp
from jax import lax
from jax.experimental import pallas as pl
from jax.experimental.pallas import tpu as pltpu

F32 = jnp.float32
BF16 = jnp.bfloat16
HI = lax.Precision.HIGHEST
EPS = 1e-6
NEG = -1e30

D_MODEL = 1024
PAGE_SIZE = 128
H_A, D_A = 8, 64
ROT_DIM = D_A // 4
ROPE_THETA = 500000.0
MOBA_BLOCK = 256
MOBA_TOPK = 3
G_B, D_BG = 8, 64
D_B = G_B * D_BG
CHUNK_B = 128
H_C, DK_C, DV_C = 4, 64, 128
GATE_RANK = 16
GATE_NORM = 16.0
H_D, DK_D, DV_D = 4, 64, 128
CONV_W = 4
C_CONV = 2 * H_D * DK_D + H_D * DV_D
CHUNK_LIN = 64
N_EXPERTS = 8
TOP_K = 2
QKV_A = H_A * D_A
IN_EVEN = 3 * QKV_A + 2 * D_B
FFN_TF = 1408
MOE_TF = 512
MOBA_LOOP_BLOCKS = 4
DMA_UNROLL = 8
LANE = 128
VMEM_LIMIT = 56 * 1024 * 1024


def _cparams(*sem):
    return pltpu.CompilerParams(dimension_semantics=sem, vmem_limit_bytes=VMEM_LIMIT)


def _rms(x, g):
    return x * lax.rsqrt(jnp.mean(x * x, axis=-1, keepdims=True) + EPS) * g


def _gelu(x):
    return 0.5 * x * (1.0 + lax.erf(x * (2.0 ** -0.5)))


def _silu(x):
    return x * jax.nn.sigmoid(x)


def _softplus(x):
    return jnp.maximum(x, 0.0) + jnp.log1p(jnp.exp(-jnp.abs(x)))


def _dot(a, b):
    return jnp.dot(a, b, preferred_element_type=F32)


def _dot_nt(a, b):
    return lax.dot_general(a, b, (((1,), (1,)), ((), ())), preferred_element_type=F32)


def _dot_tn(a, b):
    return lax.dot_general(a, b, (((0,), (0,)), ((), ())), preferred_element_type=F32)


def _norm_matmul_kernel(x_ref, g_ref, w_ref, o_ref):
    h = _rms(x_ref[...], g_ref[...]).astype(BF16)
    o_ref[...] = _dot(h, w_ref[...])


def _norm_matmul(x, g, w, tm):
    m, d = x.shape
    n = w.shape[1]
    return pl.pallas_call(
        _norm_matmul_kernel,
        grid=(m // tm,),
        in_specs=[pl.BlockSpec((tm, d), lambda i: (i, 0)),
                  pl.BlockSpec((1, d), lambda i: (0, 0)),
                  pl.BlockSpec((d, n), lambda i: (0, 0))],
        out_specs=pl.BlockSpec((tm, n), lambda i: (i, 0)),
        out_shape=jax.ShapeDtypeStruct((m, n), F32),
        compiler_params=_cparams("parallel"),
        name="norm_matmul",
    )(x, g, w)


def _rope_tables(pos):
    half = ROT_DIM // 2
    inv = ROPE_THETA ** (-jnp.arange(half, dtype=F32) / half)
    ang = pos.astype(F32)[:, None] * inv[None, :]
    cos, sin = jnp.cos(ang), jnp.sin(ang)
    t = pos.shape[0]
    one = jnp.ones((t, D_A - ROT_DIM), F32)
    zero_h = jnp.zeros((t, half), F32)
    zero_r = jnp.zeros((t, D_A - ROT_DIM), F32)
    c = jnp.concatenate([cos, cos, one], axis=1)
    s_up = jnp.concatenate([-sin, zero_h, zero_r], axis=1)
    s_dn = jnp.concatenate([zero_h, sin, zero_r], axis=1)
    rep = LANE // D_A
    return jnp.tile(c, (1, rep)), jnp.tile(s_up, (1, rep)), jnp.tile(s_dn, (1, rep))


def _rope(x, c, s_up, s_dn):
    half = ROT_DIM // 2
    outs = []
    for j in range(x.shape[1] // LANE):
        xs = x[:, j * LANE:(j + 1) * LANE]
        up = pltpu.roll(xs, LANE - half, 1)
        dn = pltpu.roll(xs, half, 1)
        outs.append(xs * c + up * s_up + dn * s_dn)
    return jnp.concatenate(outs, axis=1)


def _layernorm(x, g, b):
    mu = jnp.mean(x, axis=-1, keepdims=True)
    xc = x - mu
    var = jnp.mean(xc * xc, axis=-1, keepdims=True)
    return xc * lax.rsqrt(var + EPS) * g + b


def _even_epilogue_kernel(z_ref, c_ref, su_ref, sd_ref, lng_ref, lnb_ref, wsp_ref, bspt_ref,
                          qt_ref, k_ref, vt_ref, kv_ref, kmean_ref, gv_ref, bout_ref):
    tm = z_ref.shape[0]
    c, su, sd = c_ref[...], su_ref[...], sd_ref[...]
    q = _rope(z_ref[:, 0:QKV_A], c, su, sd) * (D_A ** -0.5)
    k = _rope(z_ref[:, QKV_A:2 * QKV_A], c, su, sd)
    v = z_ref[:, 2 * QKV_A:3 * QKV_A]
    qt_ref[0] = q.T.astype(BF16)
    k_ref[...] = k.astype(BF16)
    vt = v.T
    vt_ref[0] = vt.astype(BF16)
    kv_ref[0, 0:QKV_A, :] = k.T
    kv_ref[0, QKV_A:2 * QKV_A, :] = vt
    for blk in range(tm // MOBA_BLOCK):
        kmean_ref[blk] = jnp.mean(k[blk * MOBA_BLOCK:(blk + 1) * MOBA_BLOCK], axis=0, keepdims=True)
    u = _gelu(z_ref[:, 3 * QKV_A:3 * QKV_A + D_B])
    gv = _layernorm(_gelu(z_ref[:, 3 * QKV_A + D_B:3 * QKV_A + 2 * D_B]), lng_ref[...], lnb_ref[...])
    gv_ref[...] = gv
    gvb = gv.astype(BF16)
    row = lax.broadcasted_iota(jnp.int32, (CHUNK_B, CHUNK_B), 0)
    col = lax.broadcasted_iota(jnp.int32, (CHUNK_B, CHUNK_B), 1)
    group = lax.broadcasted_iota(jnp.int32, (CHUNK_B, D_B), 1) // D_BG
    w = [jnp.where(row >= col, wsp_ref[g], 0.0).astype(BF16) for g in range(G_B)]
    for ch in range(tm // CHUNK_B):
        gvc = gvb[ch * CHUNK_B:(ch + 1) * CHUNK_B]
        mixed = jnp.zeros((CHUNK_B, D_B), F32)
        for g in range(G_B):
            mixed = jnp.where(group == g, _dot(w[g], gvc) + bspt_ref[:, g:g + 1], mixed)
        bout_ref[ch * CHUNK_B:(ch + 1) * CHUNK_B, :] = (u[ch * CHUNK_B:(ch + 1) * CHUNK_B] * mixed).astype(BF16)


def _even_epilogue(z, tables, ln_g, ln_b, w_sp, b_sp, batch, seq, tm):
    n = batch * seq
    nt = seq // tm
    nblk = tm // MOBA_BLOCK
    tab_spec = pl.BlockSpec((tm, LANE), lambda b, i: (i, 0))
    row_spec = lambda width: pl.BlockSpec((tm, width), lambda b, i: (b * nt + i, 0))
    t_spec = pl.BlockSpec((1, QKV_A, tm), lambda b, i: (b, 0, i))
    return pl.pallas_call(
        _even_epilogue_kernel,
        grid=(batch, nt),
        in_specs=[row_spec(IN_EVEN), tab_spec, tab_spec, tab_spec,
                  pl.BlockSpec((1, D_B), lambda b, i: (0, 0)),
                  pl.BlockSpec((1, D_B), lambda b, i: (0, 0)),
                  pl.BlockSpec((G_B, CHUNK_B, CHUNK_B), lambda b, i: (0, 0, 0)),
                  pl.BlockSpec((CHUNK_B, G_B), lambda b, i: (0, 0))],
        out_specs=[t_spec, row_spec(QKV_A), t_spec, pl.BlockSpec((1, 2 * QKV_A, tm), lambda b, i: (b, 0, i)),
                   pl.BlockSpec((nblk, 1, QKV_A), lambda b, i: (b * nt + i, 0, 0)),
                   row_spec(D_B), row_spec(D_B)],
        out_shape=[jax.ShapeDtypeStruct((batch, QKV_A, seq), BF16),
                   jax.ShapeDtypeStruct((n, QKV_A), BF16),
                   jax.ShapeDtypeStruct((batch, QKV_A, seq), BF16),
                   jax.ShapeDtypeStruct((batch, 2 * QKV_A, seq), F32),
                   jax.ShapeDtypeStruct((n // MOBA_BLOCK, 1, QKV_A), F32),
                   jax.ShapeDtypeStruct((n, D_B), F32),
                   jax.ShapeDtypeStruct((n, D_B), BF16)],
        compiler_params=_cparams("parallel", "parallel"),
        name="even_epilogue",
    )(z, *tables, ln_g, ln_b, w_sp, b_sp.T)


def _moba_select(gate, n_own):
    nb = gate.shape[0]
    blk = lax.broadcasted_iota(jnp.int32, gate.shape, 0)
    elig = blk < n_own
    gm = jnp.where(elig, gate, NEG)
    rank = jnp.zeros(gate.shape, F32)
    for m in range(nb):
        gm_m = gm[m:m + 1, :]
        ahead = (gm_m > gm) | ((gm_m == gm) & (m < blk))
        rank = rank + ahead.astype(F32)
    return (elig & (rank < MOBA_TOPK)).astype(F32)


def _moba_prompt_kernel(qt_ref, k_ref, vt_ref, kmean_ref, o_ref, bias_ref):
    i = pl.program_id(2)
    tq = MOBA_BLOCK
    n_head = LANE // D_A
    n_split = tq // LANE
    own = pl.multiple_of(i * tq, tq)
    key_i = lax.broadcasted_iota(jnp.int32, (tq, LANE), 0)
    qry_i = lax.broadcasted_iota(jnp.int32, (tq, LANE), 1)
    head_lanes = [slice(hh * D_A, (hh + 1) * D_A) for hh in range(n_head)]
    streams = [(hh, qh) for hh in range(n_head) for qh in range(n_split)]
    qts = [qt_ref[0, head_lanes[hh], :] for hh in range(n_head)]
    queries = [qts[hh][:, qh * LANE:(qh + 1) * LANE] for hh, qh in streams]
    k_own = [k_ref[pl.ds(own, tq), head_lanes[hh]] for hh in range(n_head)]
    own_scores = [_dot(k_own[hh], queries[idx]) for idx, (hh, qh) in enumerate(streams)]
    gates = [jnp.dot(kmean_ref[0, :, head_lanes[hh]], qts[hh].astype(F32), precision=HI, preferred_element_type=F32)
             for hh in range(n_head)]
    for hh in range(n_head):
        bias = jnp.where(_moba_select(gates[hh], i) > 0.0, 0.0, NEG)
        for qh in range(n_split):
            bias_ref[hh, qh] = bias[:, qh * LANE:(qh + 1) * LANE]
    own_probs, own_stats = [], []
    for idx, (hh, qh) in enumerate(streams):
        s = jnp.where(key_i <= qry_i + qh * LANE, own_scores[idx], NEG)
        m = jnp.max(s, axis=0, keepdims=True)
        p = jnp.exp(s - m)
        own_probs.append(p.astype(BF16))
        own_stats.append((m, jnp.sum(p, axis=0, keepdims=True)))
    vt_own = [vt_ref[0, head_lanes[hh], pl.ds(own, tq)] for hh in range(n_head)]
    state = []
    for idx, (hh, qh) in enumerate(streams):
        state += [own_stats[idx][0], own_stats[idx][1], _dot(vt_own[hh], own_probs[idx])]

    def body(j0, carry, nk):
        tk = nk * tq
        start = pl.multiple_of(j0 * tq, tq)
        k_j = [k_ref[pl.ds(start, tk), head_lanes[hh]] for hh in range(n_head)]
        scores = [_dot(k_j[hh], queries[hh * n_split + qh]) for hh, qh in streams]
        probs, stats = [], []
        for idx, (hh, qh) in enumerate(streams):
            m, l = carry[3 * idx], carry[3 * idx + 1]
            s = [scores[idx][b * tq:(b + 1) * tq] + bias_ref[hh, qh, pl.ds(j0 + b, 1), :] for b in range(nk)]
            m_new = m
            for sb in s:
                m_new = jnp.maximum(m_new, jnp.max(sb, axis=0, keepdims=True))
            alpha = jnp.exp(m - m_new)
            p = [jnp.exp(sb - m_new) for sb in s]
            l_new = l * alpha
            for pb in p:
                l_new = l_new + jnp.sum(pb, axis=0, keepdims=True)
            probs.append(jnp.concatenate([pb.astype(BF16) for pb in p], axis=0))
            stats.append((m_new, l_new, alpha))
        vt_j = [vt_ref[0, head_lanes[hh], pl.ds(start, tk)] for hh in range(n_head)]
        pv = [_dot(vt_j[hh], probs[idx]) for idx, (hh, qh) in enumerate(streams)]
        out = []
        for idx in range(len(streams)):
            m_new, l_new, alpha = stats[idx]
            out += [m_new, l_new, carry[3 * idx + 2] * alpha + pv[idx]]
        return tuple(out)

    state = tuple(state)
    done = 0
    nk = MOBA_LOOP_BLOCKS
    while nk >= 1:
        n_group = (i - done) // nk
        state = lax.fori_loop(0, n_group, lambda t, c, nk=nk, done=done: body(done + t * nk, c, nk), state)
        done = done + n_group * nk
        nk //= 2
    outs = []
    for hh in range(n_head):
        halves = [state[3 * (hh * n_split + qh) + 2] / state[3 * (hh * n_split + qh) + 1] for qh in range(n_split)]
        outs.append(jnp.concatenate(halves, axis=1))
    o_ref[...] = jnp.concatenate(outs, axis=0).T.astype(BF16)


def _moba_prompt(qt, k, vt, kmean, batch, seq):
    nq = seq // MOBA_BLOCK
    hp = QKV_A // LANE
    return pl.pallas_call(
        _moba_prompt_kernel,
        grid=(batch, hp, nq),
        in_specs=[pl.BlockSpec((1, LANE, MOBA_BLOCK), lambda b, h, i: (b, h, i)),
                  pl.BlockSpec((seq, LANE), lambda b, h, i: (b, h)),
                  pl.BlockSpec((1, LANE, seq), lambda b, h, i: (b, h, 0)),
                  pl.BlockSpec((1, nq, LANE), lambda b, h, i: (b, 0, h))],
        out_specs=pl.BlockSpec((MOBA_BLOCK, LANE), lambda b, h, i: (b * nq + i, h)),
        out_shape=jax.ShapeDtypeStruct((batch * seq, QKV_A), BF16),
        scratch_shapes=[pltpu.VMEM((LANE // D_A, MOBA_BLOCK // LANE, nq, LANE), F32)],
        compiler_params=_cparams("parallel", "parallel", "arbitrary"),
        name="moba_prompt",
    )(qt, k, vt, kmean)


def _proj_ffn_kernel(*refs, n_mix):
    mix_refs = refs[:n_mix]
    x_ref, wo_ref, g_ref, wg_ref, wu_ref, wd_ref, o_ref, hn_ref = refs[n_mix:]

    @pl.when(pl.program_id(1) == 0)
    def _():
        x1 = x_ref[...]
        off = 0
        for r in mix_refs:
            width = r.shape[1]
            x1 = x1 + _dot(r[...], wo_ref[off:off + width, :])
            off += width
        o_ref[...] = x1
        hn_ref[...] = _rms(x1, g_ref[...]).astype(BF16)

    h = hn_ref[...]
    act = (_silu(_dot(h, wg_ref[...])) * _dot(h, wu_ref[...])).astype(BF16)
    o_ref[...] += _dot(act, wd_ref[...])


def _col_blocks(w, tf):
    *lead, d, ff = w.shape
    n = len(lead)
    return jnp.transpose(w.reshape(*lead, d, ff // tf, tf), (*range(n), n + 1, n, n + 2))


def _proj_ffn(mixes, x, w_out, g, w_gate, w_up, w_down, tm):
    m, d = x.shape
    nf, _, tf = w_gate.shape
    mix_specs = [pl.BlockSpec((tm, a.shape[1]), lambda i, f: (i, 0)) for a in mixes]
    return pl.pallas_call(
        functools.partial(_proj_ffn_kernel, n_mix=len(mixes)),
        grid=(m // tm, nf),
        in_specs=mix_specs + [
            pl.BlockSpec((tm, d), lambda i, f: (i, 0)),
            pl.BlockSpec(w_out.shape, lambda i, f: (0, 0)),
            pl.BlockSpec((1, d), lambda i, f: (0, 0)),
            pl.BlockSpec((None, d, tf), lambda i, f: (f, 0, 0)),
            pl.BlockSpec((None, d, tf), lambda i, f: (f, 0, 0)),
            pl.BlockSpec((tf, d), lambda i, f: (f, 0))],
        out_specs=pl.BlockSpec((tm, d), lambda i, f: (i, 0)),
        out_shape=jax.ShapeDtypeStruct((m, d), F32),
        scratch_shapes=[pltpu.VMEM((tm, d), BF16)],
        compiler_params=_cparams("parallel", "arbitrary"),
        name="proj_ffn",
    )(*mixes, x, w_out, g, w_gate, w_up, w_down)


def _even_layer_prompt(x, p, batch, seq):
    z = _norm_matmul(x, p["norm_mix"], p["w_in"], 512)
    tables = _rope_tables(jnp.arange(seq, dtype=jnp.int32))
    qt, k, vt, kv, kmean, gv, b_out = _even_epilogue(
        z, tables, p["ln_g"], p["ln_b"], p["w_sp"], p["b_sp"], batch, seq, 256)
    a_out = _moba_prompt(qt, k, vt, kmean.reshape(batch, seq // MOBA_BLOCK, QKV_A), batch, seq)
    x = _proj_ffn([a_out, b_out], x, p["w_out"], p["norm_ffn"], p["w_gate"], p["w_up"], p["w_down"], 512)
    n_open = seq - ((seq - 1) // CHUNK_B) * CHUNK_B
    gv_open = gv.reshape(batch, seq, D_B)[:, seq - n_open:]
    kv = jnp.transpose(kv.reshape(batch, 2, H_A, D_A, seq), (0, 4, 1, 2, 3))
    return x, kv, gv_open


def _even_params(i, ev_norm_mix, ev_w_in, ev_gmlp_ln_g, ev_gmlp_ln_b, ev_w_spatial, ev_b_spatial, ev_w_out,
                 ev_norm_ffn, ev_w_gate, ev_w_up, ev_w_down):
    return dict(norm_mix=ev_norm_mix[i][None], w_in=ev_w_in[i].astype(BF16),
                ln_g=ev_gmlp_ln_g[i][None], ln_b=ev_gmlp_ln_b[i][None],
                w_sp=ev_w_spatial[i], b_sp=ev_b_spatial[i], w_out=ev_w_out[i].astype(BF16),
                norm_ffn=ev_norm_ffn[i][None], w_gate=_col_blocks(ev_w_gate[i].astype(BF16), FFN_TF),
                w_up=_col_blocks(ev_w_up[i].astype(BF16), FFN_TF), w_down=ev_w_down[i].astype(BF16))


O_CQ, O_CK, O_CV, O_CR, O_DQKV, O_DG, O_SM = 0, 256, 512, 1024, 1536, 2560, 3072
SM_DA, SM_DB = GATE_RANK, GATE_RANK + H_D
IN_ODD_PAD = O_SM + LANE
KEY_C = H_C * DK_C
PAIR = LANE // DK_C


def _log_sigmoid(x):
    return jnp.minimum(x, 0.0) - jnp.log1p(jnp.exp(-jnp.abs(x)))


def _split_bf16(a, terms):
    parts = []
    for _ in range(terms - 1):
        hi = a.astype(BF16)
        parts.append(hi)
        a = a - hi.astype(F32)
    parts.append(a.astype(BF16))
    return parts


def _dot_select(a, sel_bf16, terms):
    parts = _split_bf16(a, terms)
    out = _dot(parts[0], sel_bf16)
    for part in parts[1:]:
        out = out + _dot(part, sel_bf16)
    return out


def _select_dot(sel_bf16, a, terms):
    parts = _split_bf16(a, terms)
    out = _dot(sel_bf16, parts[0])
    for part in parts[1:]:
        out = out + _dot(sel_bf16, part)
    return out


def _unit_lower_inverses(mats):
    c = mats[0].shape[0]
    assert c == 64
    eye = (lax.broadcasted_iota(jnp.int32, (c, c), 0) == lax.broadcasted_iota(jnp.int32, (c, c), 1)).astype(F32)
    sp = lambda ms: [_split_bf16(m, 2) for m in ms]
    mul = lambda aa, bb: [_dot(ah, bh) + _dot(ah, bl) + _dot(al, bh) for (ah, al), (bh, bl) in zip(aa, bb)]
    x1 = [-a for a in mats]
    x1s = sp(x1)
    x2 = mul(x1s, x1s)
    x2s = sp(x2)
    x4 = mul(x2s, x2s)
    x3 = mul(x1s, x2s)
    x4s = sp(x4)
    x8 = mul(x4s, x4s)
    f01 = [eye + a + b + d for a, b, d in zip(x1, x2, x3)]
    x8s = sp(x8)
    x16 = mul(x8s, x8s)
    x12 = mul(x4s, x8s)
    g23 = [a + b + d for a, b, d in zip(x4, x8, x12)]
    x16s = sp(x16)
    x32 = mul(x16s, x16s)
    f0123 = [f + d for f, d in zip(f01, mul(sp(f01), sp(g23)))]
    x48 = mul(x16s, sp(x32))
    g45 = [a + b + d for a, b, d in zip(x16, x32, x48)]
    return [f + d for f, d in zip(f0123, mul(sp(f0123), sp(g45)))]


def _group_sumsq(y, width):
    n = y.shape[1]
    same = (lax.broadcasted_iota(jnp.int32, (n, n), 0) // width
            == lax.broadcasted_iota(jnp.int32, (n, n), 1) // width).astype(BF16)
    return _dot_select(y * y, same, 2)


def _odd_mixer_prompt_kernel(z_ref, wg2_ref, bg2_ref, gnorm_ref, cw_ref, alog_ref, dtb_ref, dnorm_ref,
                             o_ref, sg_ref, sd_ref, tail_ref,
                             stg_ref, std_ref, prev_ref, la_ref, qkv_ref, dla_ref, beta_ref):
    i = pl.program_id(1)
    ns, tc = z_ref.shape[0], z_ref.shape[1]
    c = CHUNK_LIN

    @pl.when(i == 0)
    def _():
        stg_ref[...] = jnp.zeros_like(stg_ref)
        std_ref[...] = jnp.zeros_like(std_ref)
        prev_ref[...] = jnp.zeros_like(prev_ref)

    row8 = lax.broadcasted_iota(jnp.int32, (8, C_CONV), 0)
    qscale = jnp.where(lax.broadcasted_iota(jnp.int32, (1, 2 * KEY_C), 1) < KEY_C, DK_D ** -0.5, 1.0)
    for sq in range(ns):
        small = z_ref[sq, :, O_SM:O_SM + LANE]
        pre = jnp.dot(small, wg2_ref[...], precision=HI, preferred_element_type=F32) + bg2_ref[...]
        la_ref[sq] = _log_sigmoid(pre) / GATE_NORM
        dla_ref[sq] = -jnp.exp(alog_ref[...]) * _softplus(small + dtb_ref[...])
        beta_ref[sq] = jax.nn.sigmoid(small)

        x = z_ref[sq, :, O_DQKV:O_DQKV + C_CONV]
        x8 = x[0:8]
        p8 = prev_ref[sq]
        y = x * cw_ref[CONV_W - 1:CONV_W, :]
        y8 = x8 * cw_ref[CONV_W - 1:CONV_W, :]
        for s in range(1, CONV_W):
            wrow = cw_ref[CONV_W - 1 - s:CONV_W - s, :]
            y = y + pltpu.roll(x, s, 0) * wrow
            y8 = y8 + jnp.where(row8 < s, pltpu.roll(p8, s, 0), pltpu.roll(x8, s, 0)) * wrow
        prev_ref[sq] = x[tc - 8:tc]
        y = _silu(y)
        y8 = _silu(y8)
        yqk = y[:, 0:2 * KEY_C]
        nrm = lax.rsqrt(_group_sumsq(yqk, DK_D) + EPS)
        qkv_ref[sq, :, 0:2 * KEY_C] = yqk * nrm * qscale
        qkv_ref[sq, :, 2 * KEY_C:] = y[:, 2 * KEY_C:]
        yqk8 = y8[:, 0:2 * KEY_C]
        qkv_ref[sq, 0:8, 0:2 * KEY_C] = yqk8 * lax.rsqrt(_group_sumsq(yqk8, DK_D) + EPS) * qscale
        qkv_ref[sq, 0:8, 2 * KEY_C:] = y8[:, 2 * KEY_C:]

    ri = lax.broadcasted_iota(jnp.int32, (c, c), 0)
    ci = lax.broadcasted_iota(jnp.int32, (c, c), 1)
    tril = ri >= ci
    strict = ri > ci
    lower = tril.astype(BF16)
    upper = (ri <= ci).astype(BF16)
    lane_head = lax.broadcasted_iota(jnp.int32, (c, LANE), 1) // DK_C
    lane_head_row = lax.broadcasted_iota(jnp.int32, (1, LANE), 1) // DK_C

    def chunk(ch, carry):
        rows = pl.ds(pl.multiple_of(ch * c, c), c)
        n_pair = H_C // PAIR
        seqs = range(ns)
        units = [(sq, h) + divmod(h, PAIR) for sq in seqs for h in range(H_C)]
        pairs = [(sq, p) for sq in seqs for p in range(n_pair)]
        unit_at = lambda sq, p, hh: (sq * H_C) + p * PAIR + hh
        masks = [lane_head == hh for hh in range(PAIR)]
        bcum_all = [_select_dot(lower, la_ref[sq, rows, :], 3) for sq in seqs]
        dla_parts = [_split_bf16(dla_ref[sq, rows, :], 3) for sq in seqs]
        g_cols = [_dot(lower, dp[0]) + _dot(lower, dp[1]) + _dot(lower, dp[2]) for dp in dla_parts]
        g_rows = [_dot_tn(dp[0], upper) + _dot_tn(dp[1], upper) + _dot_tn(dp[2], upper) for dp in dla_parts]
        st_d = {sp: std_ref[sp[0], sp[1]] for sp in pairs}
        st_db = {sp: st_d[sp].astype(BF16) for sp in pairs}
        dn = []
        for sq, h, p, hh in units:
            g_col = jnp.broadcast_to(g_cols[sq][:, SM_DA + h:SM_DA + h + 1], (c, LANE))
            g_row = jnp.broadcast_to(g_rows[sq][SM_DA + h:SM_DA + h + 1, :], (c, c))
            beta_b = jnp.broadcast_to(beta_ref[sq, rows, SM_DB + h:SM_DB + h + 1], (c, LANE))
            km = jnp.where(masks[hh], qkv_ref[sq, rows, KEY_C + p * LANE:KEY_C + (p + 1) * LANE], 0.0)
            qm = jnp.where(masks[hh], qkv_ref[sq, rows, p * LANE:(p + 1) * LANE], 0.0)
            dn.append(dict(g_col=g_col, g_row=g_row, beta_b=beta_b, km=km, qm=qm, kmb=km.astype(BF16),
                           kb=km * beta_b, eg=jnp.exp(g_col), g_last=g_col[c - 1:c, :]))
        kk = [_dot_nt(d["kb"].astype(BF16), d["kmb"]) for d in dn]
        qk = [_dot_nt(d["qm"].astype(BF16), d["kmb"]) for d in dn]
        d_inter = [_dot_nt((d["qm"] * d["eg"]).astype(BF16), st_db[(sq, p)]) for d, (sq, h, p, hh) in zip(dn, units)]
        decay = [jnp.exp(jnp.where(tril, d["g_col"][:, 0:c] - d["g_row"], NEG)) for d in dn]
        a_mats = [jnp.where(strict, m * dc, 0.0) for m, dc in zip(kk, decay)]
        st_g = {sp: stg_ref[sp[0], sp[1]] for sp in pairs}
        st_gb = {sp: st_g[sp].astype(BF16) for sp in pairs}
        gl_pair = {}
        for sq, p in pairs:
            bcum = bcum_all[sq][:, p * LANE:(p + 1) * LANE]
            b_end = bcum[c - 1:c, :]
            k = z_ref[sq, rows, O_CK + p * LANE:O_CK + (p + 1) * LANE]
            gl_pair[(sq, p)] = dict(
                q_in=z_ref[sq, rows, O_CQ + p * LANE:O_CQ + (p + 1) * LANE] * (DK_C ** -0.5) * jnp.exp(bcum),
                k_in=(k * jnp.exp(-bcum)).astype(BF16), k_end=k * jnp.exp(b_end - bcum), b_end=b_end)
        g_qm = [jnp.where(masks[hh], gl_pair[(sq, p)]["q_in"], 0.0).astype(BF16) for sq, h, p, hh in units]
        g_ke = [jnp.where(masks[hh], gl_pair[(sq, p)]["k_end"], 0.0).astype(BF16) for sq, h, p, hh in units]
        g_vb = [z_ref[sq, rows, O_CV + h * DV_C:O_CV + (h + 1) * DV_C].astype(BF16) for sq, h, p, hh in units]
        g_sc = [_dot_nt(g_qm[u], gl_pair[(sq, p)]["k_in"]) for u, (sq, h, p, hh) in enumerate(units)]
        g_inter = [_dot_nt(g_qm[u], st_gb[(sq, p)]) for u, (sq, h, p, hh) in enumerate(units)]
        g_upd = [_dot_tn(g_vb[u], g_ke[u]) for u in range(len(units))]
        g_intra = [_dot(jnp.where(tril, g_sc[u], 0.0).astype(BF16), g_vb[u]) for u in range(len(units))]
        t_invs = _unit_lower_inverses(a_mats)
        for u, (sq, h, p, hh) in enumerate(units):
            gate = _silu(z_ref[sq, rows, O_CR + h * DV_C:O_CR + (h + 1) * DV_C])
            o_ref[sq, rows, h * DV_C:(h + 1) * DV_C] = (
                _rms(g_intra[u] + g_inter[u], gnorm_ref[...]) * gate).astype(BF16)
        for sq, p in pairs:
            stg_ref[sq, p] = (st_g[(sq, p)] * jnp.exp(gl_pair[(sq, p)]["b_end"])
                              + g_upd[unit_at(sq, p, 0)] + g_upd[unit_at(sq, p, 1)])
        rhs = [jnp.concatenate([qkv_ref[sq, rows, 2 * KEY_C + h * DV_D:2 * KEY_C + (h + 1) * DV_D] * d["beta_b"],
                                d["kb"] * d["eg"]], axis=1).astype(BF16) for d, (sq, h, p, hh) in zip(dn, units)]
        sol = [_dot(t.astype(BF16), r) for t, r in zip(t_invs, rhs)]
        w_st = [_dot_nt(s[:, DV_D:].astype(BF16), st_db[(sq, p)]) for s, (sq, h, p, hh) in zip(sol, units)]
        v_new = [(s[:, 0:DV_D] - ws).astype(BF16) for s, ws in zip(sol, w_st)]
        d_intra = [_dot(jnp.where(tril, q * dc, 0.0).astype(BF16), vn) for q, dc, vn in zip(qk, decay, v_new)]
        d_upd = [_dot_tn(vn, (d["km"] * jnp.exp(d["g_last"] - d["g_col"])).astype(BF16)) for vn, d in zip(v_new, dn)]
        for u, (sq, h, p, hh) in enumerate(units):
            gate = _silu(z_ref[sq, rows, O_DG + h * DV_D:O_DG + (h + 1) * DV_D])
            col = H_C * DV_C + h * DV_D
            o_ref[sq, rows, col:col + DV_D] = (_rms(d_inter[u] + d_intra[u], dnorm_ref[...]) * gate).astype(BF16)
        for sq, p in pairs:
            u0, u1 = unit_at(sq, p, 0), unit_at(sq, p, 1)
            dec_row = jnp.where(lane_head_row == 0, jnp.exp(dn[u0]["g_last"]), jnp.exp(dn[u1]["g_last"]))
            std_ref[sq, p] = st_d[(sq, p)] * dec_row + d_upd[u0] + d_upd[u1]
        return carry

    lax.fori_loop(0, tc // c, chunk, 0)

    @pl.when(i == pl.num_programs(1) - 1)
    def _():
        for sq in range(ns):
            tail_ref[sq] = prev_ref[sq]
            for p in range(H_C // PAIR):
                tg = stg_ref[sq, p].T
                td = std_ref[sq, p].T
                for hh in range(PAIR):
                    sg_ref[sq, p * PAIR + hh] = tg[hh * DK_C:(hh + 1) * DK_C, :]
                    sd_ref[sq, p * PAIR + hh] = td[hh * DK_D:(hh + 1) * DK_D, :]


def _odd_mixer_prompt(z, p, batch, seq, tc):
    nt = seq // tc
    ns = 2 if batch % 2 == 0 else 1
    full = lambda a: pl.BlockSpec(a.shape, lambda b, i: (0,) * a.ndim)
    consts = [p["w_g2_pad"], p["b_g2"], p["gla_norm"], p["conv_w"], p["alog_slab"], p["dtb_slab"], p["delta_norm"]]
    st_spec = pl.BlockSpec((ns, H_C, DK_C, DV_C), lambda b, i: (b, 0, 0, 0))
    mix, s_gla, s_delta, tail = pl.pallas_call(
        _odd_mixer_prompt_kernel,
        grid=(batch // ns, nt),
        in_specs=[pl.BlockSpec((ns, tc, IN_ODD_PAD), lambda b, i: (b, i, 0))] + [full(a) for a in consts],
        out_specs=[pl.BlockSpec((ns, tc, D_MODEL), lambda b, i: (b, i, 0)), st_spec, st_spec,
                   pl.BlockSpec((ns, 8, C_CONV), lambda b, i: (b, 0, 0))],
        out_shape=[jax.ShapeDtypeStruct((batch, seq, D_MODEL), BF16),
                   jax.ShapeDtypeStruct((batch, H_C, DK_C, DV_C), F32),
                   jax.ShapeDtypeStruct((batch, H_D, DK_D, DV_D), F32),
                   jax.ShapeDtypeStruct((batch, 8, C_CONV), F32)],
        scratch_shapes=[pltpu.VMEM((ns, H_C // PAIR, DV_C, LANE), F32),
                        pltpu.VMEM((ns, H_D // PAIR, DV_D, LANE), F32),
                        pltpu.VMEM((ns, 8, C_CONV), F32),
                        pltpu.VMEM((ns, tc, KEY_C), F32),
                        pltpu.VMEM((ns, tc, C_CONV), F32),
                        pltpu.VMEM((ns, tc, LANE), F32),
                        pltpu.VMEM((ns, tc, LANE), F32)],
        compiler_params=_cparams("parallel", "arbitrary"),
        name="odd_mixer_prompt",
    )(z.reshape(batch, seq, IN_ODD_PAD), *consts)
    return mix.reshape(batch * seq, D_MODEL), s_gla, s_delta, tail


M_E1, M_E2, M_R1, M_R2, M_G1, M_G2 = (N_EXPERTS + j for j in range(6))


def _proj_router_kernel(mix_ref, x_ref, wo_ref, g_ref, wr_ref, x1_ref, hn_ref, meta_ref, cnt_ref, carry_ref):
    tm = x_ref.shape[0]

    @pl.when(pl.program_id(0) == 0)
    def _():
        carry_ref[...] = jnp.zeros_like(carry_ref)

    x1 = x_ref[...] + _dot(mix_ref[...], wo_ref[...])
    x1_ref[...] = x1
    hn = _rms(x1, g_ref[...])
    hn_ref[...] = hn.astype(hn_ref.dtype)
    lane = lax.broadcasted_iota(jnp.int32, (tm, LANE), 1)
    hn_hi, hn_lo = _split_bf16(hn, 2)
    wr_hi, wr_lo = _split_bf16(wr_ref[...], 2)
    logits = _dot(hn_hi, wr_hi) + _dot(hn_hi, wr_lo) + _dot(hn_lo, wr_hi)
    logits = jnp.where(lane < N_EXPERTS, logits, NEG)
    m1 = jnp.max(logits, axis=1, keepdims=True)
    e1 = jnp.min(jnp.where(logits == m1, lane, LANE), axis=1, keepdims=True)
    rest = jnp.where(lane == e1, NEG, logits)
    m2 = jnp.max(rest, axis=1, keepdims=True)
    e2 = jnp.min(jnp.where(rest == m2, lane, LANE), axis=1, keepdims=True)
    t = jnp.exp(m2 - m1)
    g1 = 1.0 / (1.0 + t)
    g2 = t / (1.0 + t)
    oh1 = lane == e1
    oh2 = lane == e2
    member = (oh1 | oh2).astype(F32)
    ri = lax.broadcasted_iota(jnp.int32, (tm, tm), 0)
    ci = lax.broadcasted_iota(jnp.int32, (tm, tm), 1)
    before = _dot((ri > ci).astype(BF16), member.astype(BF16)) + carry_ref[...]
    r1 = jnp.sum(jnp.where(oh1, before, 0.0), axis=1, keepdims=True)
    r2 = jnp.sum(jnp.where(oh2, before, 0.0), axis=1, keepdims=True)
    carry_ref[...] = carry_ref[...] + jnp.sum(member, axis=0, keepdims=True)
    cnt_ref[...] = carry_ref[...]
    meta = jnp.where(oh1, g1, 0.0) + jnp.where(oh2, g2, 0.0)
    meta = jnp.where(lane == M_E1, e1.astype(F32), meta)
    meta = jnp.where(lane == M_E2, e2.astype(F32), meta)
    meta = jnp.where(lane == M_R1, r1, meta)
    meta = jnp.where(lane == M_R2, r2, meta)
    meta = jnp.where(lane == M_G1, g1, meta)
    meta = jnp.where(lane == M_G2, g2, meta)
    meta_ref[...] = meta


def _proj_router(mix, x, w_out, g, w_router_pad, tm, hn_dtype):
    m, d = x.shape
    row = lambda width: pl.BlockSpec((tm, width), lambda i: (i, 0))
    full = lambda a: pl.BlockSpec(a.shape, lambda i: (0,) * a.ndim)
    return pl.pallas_call(
        _proj_router_kernel,
        grid=(m // tm,),
        in_specs=[row(d), row(d), full(w_out), full(g), full(w_router_pad)],
        out_specs=[row(d), row(d), row(LANE), pl.BlockSpec((1, LANE), lambda i: (0, 0))],
        out_shape=[jax.ShapeDtypeStruct((m, d), F32), jax.ShapeDtypeStruct((m, d), hn_dtype),
                   jax.ShapeDtypeStruct((m, LANE), F32), jax.ShapeDtypeStruct((1, LANE), F32)],
        scratch_shapes=[pltpu.VMEM((1, LANE), F32)],
        compiler_params=_cparams("arbitrary"),
        name="proj_router",
    )(mix, x, w_out, g, w_router_pad)


def _moe_dense_kernel(hn_ref, x1_ref, meta_ref, wg_ref, wu_ref, wd_ref, gf_ref, o_ref):
    e = pl.program_id(1)
    f = pl.program_id(2)

    @pl.when((e == 0) & (f == 0))
    def _():
        o_ref[...] = x1_ref[...]

    h = hn_ref[...]
    act = (_silu(_dot(h, wg_ref[...].astype(BF16))) * _dot(h, wu_ref[...].astype(BF16))).astype(BF16)
    lane = lax.broadcasted_iota(jnp.int32, meta_ref.shape, 1)
    gate = jnp.sum(jnp.where(lane == e, meta_ref[...], 0.0), axis=1, keepdims=True)
    o_ref[...] += gate * _dot(act, wd_ref[...].astype(BF16))

    @pl.when((e == pl.num_programs(1) - 1) & (f == pl.num_programs(2) - 1))
    def _():
        o_ref[...] = _rms(o_ref[...], gf_ref[...])


def _moe_dense(hn, x1, meta, w_gate, w_up, w_down, g_final, tm):
    m, d = x1.shape
    n_e, _, ff = w_gate.shape
    tf = MOE_TF
    row = lambda width: pl.BlockSpec((tm, width), lambda i, e, f: (i, 0))
    return pl.pallas_call(
        _moe_dense_kernel,
        grid=(m // tm, n_e, ff // tf),
        in_specs=[row(d), row(d), row(LANE),
                  pl.BlockSpec((None, d, tf), lambda i, e, f: (e, 0, f)),
                  pl.BlockSpec((None, d, tf), lambda i, e, f: (e, 0, f)),
                  pl.BlockSpec((None, tf, d), lambda i, e, f: (e, f, 0)),
                  pl.BlockSpec((1, d), lambda i, e, f: (0, 0))],
        out_specs=row(d),
        out_shape=jax.ShapeDtypeStruct((m, d), F32),
        compiler_params=_cparams("parallel", "arbitrary", "arbitrary"),
        name="moe_dense",
    )(hn, x1, meta, w_gate, w_up, w_down, g_final)


def _moe_dispatch_kernel(dest_ref, hn_ref, xs_in_ref, xs_ref, sem):
    del xs_in_ref
    tm = hn_ref.shape[0]
    base = pl.program_id(0) * (TOP_K * tm)

    def row_copy(t, k):
        return pltpu.make_async_copy(hn_ref.at[pl.ds(t, 1)], xs_ref.at[pl.ds(dest_ref[base + k * tm + t], 1)], sem)

    def issue(t, carry):
        for k in range(TOP_K):
            row_copy(t, k).start()
        return carry

    def drain(t, carry):
        for k in range(TOP_K):
            row_copy(t, k).wait()
        return carry

    lax.fori_loop(0, tm, issue, 0, unroll=DMA_UNROLL)
    lax.fori_loop(0, tm, drain, 0, unroll=DMA_UNROLL)


def _moe_dispatch(dest, hn, n_rows, tm):
    m, d = hn.shape
    return pl.pallas_call(
        _moe_dispatch_kernel,
        grid_spec=pltpu.PrefetchScalarGridSpec(
            num_scalar_prefetch=1,
            grid=(m // tm,),
            in_specs=[pl.BlockSpec((tm, d), lambda i, dest: (i, 0)), pl.BlockSpec(memory_space=pl.ANY)],
            out_specs=pl.BlockSpec(memory_space=pl.ANY),
            scratch_shapes=[pltpu.SemaphoreType.DMA(())]),
        out_shape=jax.ShapeDtypeStruct((n_rows, d), hn.dtype),
        input_output_aliases={2: 0},
        compiler_params=_cparams("arbitrary"),
        name="moe_dispatch",
    )(dest, hn, jnp.zeros((n_rows, d), hn.dtype))


def _moe_grouped_kernel(te_ref, tr_ref, xs_ref, wg_ref, wu_ref, wd_ref, o_ref, xb_ref):
    del te_ref
    f = pl.program_id(1)
    used = tr_ref[pl.program_id(0)]
    half = xs_ref.shape[0] // 2

    @pl.when(f == 0)
    def _():
        o_ref[...] = jnp.zeros_like(o_ref)
        xb_ref[...] = xs_ref[...].astype(BF16)

    def swiglu_rows(rows):
        h = xb_ref[rows, :]
        act = (_silu(_dot(h, wg_ref[...].astype(BF16))) * _dot(h, wu_ref[...].astype(BF16))).astype(BF16)
        o_ref[rows, :] += _dot(act, wd_ref[...].astype(BF16))

    @pl.when(used > half)
    def _():
        swiglu_rows(slice(None))

    @pl.when((used > 0) & (used <= half))
    def _():
        swiglu_rows(slice(0, half))


def _moe_grouped(tile_expert, tile_rows, xs, w_gate, w_up, w_down, tg):
    rows, d = xs.shape
    tf = MOE_TF
    nf = w_gate.shape[2] // tf
    fidx = lambda r, f, tr: jnp.where(tr[r] > 0, f, nf - 1)
    return pl.pallas_call(
        _moe_grouped_kernel,
        grid_spec=pltpu.PrefetchScalarGridSpec(
            num_scalar_prefetch=2,
            grid=(rows // tg, nf),
            in_specs=[pl.BlockSpec((tg, d), lambda r, f, te, tr: (r, 0)),
                      pl.BlockSpec((None, d, tf), lambda r, f, te, tr: (te[r], 0, fidx(r, f, tr))),
                      pl.BlockSpec((None, d, tf), lambda r, f, te, tr: (te[r], 0, fidx(r, f, tr))),
                      pl.BlockSpec((None, tf, d), lambda r, f, te, tr: (te[r], fidx(r, f, tr), 0))],
            out_specs=pl.BlockSpec((tg, d), lambda r, f, te, tr: (r, 0)),
            scratch_shapes=[pltpu.VMEM((tg, d), BF16)]),
        out_shape=jax.ShapeDtypeStruct((rows, d), F32),
        compiler_params=_cparams("parallel", "arbitrary"),
        name="moe_grouped",
    )(tile_expert, tile_rows, xs, w_gate, w_up, w_down)


def _moe_combine_kernel(dest_ref, ys_ref, x1_ref, meta_ref, gf_ref, o_ref, buf_ref, sem):
    i = pl.program_id(0)
    tc = x1_ref.shape[0]
    rows = TOP_K * tc

    def row_copy(step, slot, j):
        return pltpu.make_async_copy(ys_ref.at[pl.ds(dest_ref[step * rows + j], 1)],
                                     buf_ref.at[slot, pl.ds(j, 1)], sem.at[slot])

    def issue(step, slot):
        def body(j, carry):
            row_copy(step, slot, j).start()
            return carry
        lax.fori_loop(0, rows, body, 0, unroll=DMA_UNROLL)

    def drain(step, slot):
        def body(j, carry):
            row_copy(step, slot, j).wait()
            return carry
        lax.fori_loop(0, rows, body, 0, unroll=DMA_UNROLL)

    slot = i % 2

    @pl.when(i == 0)
    def _():
        issue(0, 0)

    @pl.when(i + 1 < pl.num_programs(0))
    def _():
        issue(i + 1, 1 - slot)

    drain(i, slot)
    g1 = meta_ref[:, M_G1:M_G1 + 1]
    g2 = meta_ref[:, M_G2:M_G2 + 1]
    y = x1_ref[...] + g1 * buf_ref[slot, 0:tc] + g2 * buf_ref[slot, tc:rows]
    o_ref[...] = _rms(y, gf_ref[...])


def _moe_combine(dest, ys, x1, meta, g_final, tc):
    m, d = x1.shape
    row = lambda width: pl.BlockSpec((tc, width), lambda i, dest: (i, 0))
    return pl.pallas_call(
        _moe_combine_kernel,
        grid_spec=pltpu.PrefetchScalarGridSpec(
            num_scalar_prefetch=1,
            grid=(m // tc,),
            in_specs=[pl.BlockSpec(memory_space=pl.ANY), row(d), row(LANE),
                      pl.BlockSpec((1, d), lambda i, dest: (0, 0))],
            out_specs=row(d),
            scratch_shapes=[pltpu.VMEM((2, TOP_K * tc, d), F32), pltpu.SemaphoreType.DMA((2,))]),
        out_shape=jax.ShapeDtypeStruct((m, d), F32),
        compiler_params=_cparams("arbitrary"),
        name="moe_combine",
    )(dest, ys, x1, meta, g_final)


def _moe_routes(meta, counts, tile, tg):
    m = meta.shape[0]
    n_rows = TOP_K * m + N_EXPERTS * tg
    expert = meta[:, M_E1:M_E2 + 1].astype(jnp.int32)
    rank = meta[:, M_R1:M_R2 + 1].astype(jnp.int32)
    count = counts[0, :N_EXPERTS].astype(jnp.int32)
    padded = (count + tg - 1) // tg * tg
    ends = jnp.cumsum(padded)
    dest = (ends - padded)[expert] + rank
    dest = jnp.transpose(dest.reshape(m // tile, tile, TOP_K), (0, 2, 1)).reshape(-1)
    tile_start = jnp.arange(n_rows // tg, dtype=jnp.int32) * tg
    probe = jnp.minimum(tile_start, ends[-1] - 1)
    tile_expert = jnp.sum(ends[None, :] <= probe[:, None], axis=1).astype(jnp.int32)
    tile_rows = jnp.clip((ends - padded + count)[tile_expert] - tile_start, 0, tg).astype(jnp.int32)
    return dest, tile_expert, tile_rows, n_rows


def _odd_params(i, od_norm_mix, od_w_in, od_gla_w_gate2, od_gla_b_gate2, od_gla_norm, od_delta_conv,
                od_delta_a_log, od_delta_dt_bias, od_delta_norm, od_w_out, od_norm_ffn, od_router,
                od_w_gate, od_w_up, od_w_down):
    w = od_w_in[i]
    sizes = (KEY_C, KEY_C, H_C * DV_C, GATE_RANK, H_C * DV_C, C_CONV, H_D, H_D, H_D * DV_D)
    splits = tuple(sum(sizes[:j + 1]) for j in range(len(sizes) - 1))
    cq, ck, cv, c_lr, c_r, d_qkv, d_a, d_b, d_g = jnp.split(w, splits, axis=1)
    pad = jnp.zeros((w.shape[0], IN_ODD_PAD - w.shape[1]), w.dtype)
    w_in = jnp.concatenate([cq, ck, cv, c_r, d_qkv, d_g, c_lr, d_a, d_b, pad], axis=1).astype(BF16)
    slab = lambda v: jnp.zeros((1, LANE), F32).at[0, SM_DA:SM_DA + H_D].set(v)
    return dict(norm_mix=od_norm_mix[i][None], w_in=w_in,
                w_g2_pad=jnp.zeros((LANE, KEY_C), F32).at[:GATE_RANK].set(od_gla_w_gate2[i]),
                b_g2=od_gla_b_gate2[i][None], gla_norm=od_gla_norm[i][None], conv_w=od_delta_conv[i],
                alog_slab=slab(od_delta_a_log[i]), dtb_slab=slab(od_delta_dt_bias[i]),
                a_log=od_delta_a_log[i], dt_bias=od_delta_dt_bias[i],
                delta_norm=od_delta_norm[i][None], w_out=od_w_out[i].astype(BF16),
                norm_ffn=od_norm_ffn[i][None],
                router=jnp.zeros((D_MODEL, LANE), F32).at[:, :N_EXPERTS].set(od_router[i]),
                w_gate=od_w_gate[i], w_up=od_w_up[i], w_down=od_w_down[i])


MOE_GROUP_TILE = 1024
MOE_ROW_TILE = 256


def _moe(mix, x, p, g_final, tm):
    m = x.shape[0]
    if m < N_EXPERTS * MOE_GROUP_TILE:
        x1, hn, meta, _ = _proj_router(mix, x, p["w_out"], p["norm_ffn"], p["router"], tm, BF16)
        return _moe_dense(hn, x1, meta, p["w_gate"], p["w_up"], p["w_down"], g_final, tm)
    x1, hn, meta, counts = _proj_router(mix, x, p["w_out"], p["norm_ffn"], p["router"], tm, F32)
    dest, tile_expert, tile_rows, n_rows = _moe_routes(meta, counts, MOE_ROW_TILE, MOE_GROUP_TILE)
    xs = _moe_dispatch(dest, hn, n_rows, MOE_ROW_TILE)
    ys = _moe_grouped(tile_expert, tile_rows, xs, p["w_gate"], p["w_up"], p["w_down"], MOE_GROUP_TILE)
    return _moe_combine(dest, ys, x1, meta, g_final, MOE_ROW_TILE)


def _odd_layer_prompt(x, p, g_final, batch, seq):
    z = _norm_matmul(x, p["norm_mix"], p["w_in"], 512)
    mix, s_gla, s_delta, tail = _odd_mixer_prompt(z, p, batch, seq, 256)
    y = _moe(mix, x, p, g_final, 512)
    return y, s_gla, s_delta, tail[:, 8 - (CONV_W - 1):]


def _even_epilogue_sample_kernel(z_ref, c_ref, su_ref, sd_ref, lng_ref, lnb_ref, w0_ref, b0_ref,
                                 q_ref, kv_ref, gv_ref, bout_ref):
    c, su, sd = c_ref[...], su_ref[...], sd_ref[...]
    q_ref[...] = _rope(z_ref[:, 0:QKV_A], c, su, sd) * (D_A ** -0.5)
    kv_ref[:, 0:QKV_A] = _rope(z_ref[:, QKV_A:2 * QKV_A], c, su, sd)
    kv_ref[:, QKV_A:2 * QKV_A] = z_ref[:, 2 * QKV_A:3 * QKV_A]
    u = _gelu(z_ref[:, 3 * QKV_A:3 * QKV_A + D_B])
    gv = _layernorm(_gelu(z_ref[:, 3 * QKV_A + D_B:3 * QKV_A + 2 * D_B]), lng_ref[...], lnb_ref[...])
    gv_ref[...] = gv
    bout_ref[...] = (u * (gv * w0_ref[...] + b0_ref[...])).astype(BF16)


def _even_epilogue_sample(z, tables, ln_g, ln_b, w_sp, b_sp):
    m = z.shape[0]
    w0 = jnp.repeat(w_sp[:, 0, 0], D_BG)[None]
    b0 = jnp.repeat(b_sp[:, 0], D_BG)[None]
    return pl.pallas_call(
        _even_epilogue_sample_kernel,
        out_shape=[jax.ShapeDtypeStruct((m, QKV_A), F32), jax.ShapeDtypeStruct((m, 2 * QKV_A), F32),
                   jax.ShapeDtypeStruct((m, D_B), F32), jax.ShapeDtypeStruct((m, D_B), BF16)],
        compiler_params=pltpu.CompilerParams(vmem_limit_bytes=VMEM_LIMIT),
        name="even_epilogue_sample",
    )(z, *tables, ln_g, ln_b, w0, b0)


def _moba_sample_kernel(pt_ref, qt_ref, knt_ref, vnt_ref, *refs):
    del pt_ref
    n_pages = len(refs) - 2
    page_refs, o_ref, s_ref = refs[:n_pages], refs[n_pages], refs[n_pages + 1]
    pages_per_block = MOBA_BLOCK // PAGE_SIZE
    nb = n_pages // pages_per_block
    qt = qt_ref[0]
    for h in range(H_A):
        qcol = jnp.broadcast_to(qt[:, h:h + 1], (D_A, PAGE_SIZE))
        for j in range(n_pages):
            s_ref[h, j:j + 1, :] = jnp.sum(page_refs[j][0, 0, h] * qcol, axis=0, keepdims=True)
    lane = lax.broadcasted_iota(jnp.int32, (n_pages, LANE), 1)
    page_sums = jnp.zeros((n_pages, LANE), F32)
    for h in range(H_A):
        page_sums = jnp.where(lane == h, jnp.sum(s_ref[h], axis=1, keepdims=True), page_sums)
    pair = (lax.broadcasted_iota(jnp.int32, (nb, n_pages), 1) // pages_per_block
            == lax.broadcasted_iota(jnp.int32, (nb, n_pages), 0)).astype(F32)
    pair_t = (lax.broadcasted_iota(jnp.int32, (n_pages, nb), 0) // pages_per_block
              == lax.broadcasted_iota(jnp.int32, (n_pages, nb), 1)).astype(F32)
    gate = jnp.dot(pair, page_sums, precision=HI, preferred_element_type=F32)
    sel = _moba_select(gate, nb)
    sel_pages = jnp.dot(pair_t, sel, precision=HI, preferred_element_type=F32)
    own = jnp.sum(qt * knt_ref[0], axis=0, keepdims=True)
    vnt = vnt_ref[0]
    out_lane = lax.broadcasted_iota(jnp.int32, (D_A, LANE), 1)
    out = jnp.zeros((D_A, LANE), F32)
    for h in range(H_A):
        sm = jnp.where(sel_pages[:, h:h + 1] > 0.0, s_ref[h], NEG)
        s_own = own[:, h:h + 1]
        mx = jnp.maximum(jnp.max(jnp.max(sm, axis=1, keepdims=True), axis=0, keepdims=True), s_own)
        p = jnp.exp(sm - mx)
        p_own = jnp.exp(s_own - mx)
        denom = jnp.sum(jnp.sum(p, axis=1, keepdims=True), axis=0, keepdims=True) + p_own
        acc = jnp.zeros((D_A, PAGE_SIZE), F32)
        for j in range(n_pages):
            acc = acc + page_refs[j][0, 1, h] * p[j:j + 1, :]
        o = (jnp.sum(acc, axis=1, keepdims=True) + p_own * vnt[:, h:h + 1]) / denom
        out = jnp.where(out_lane == h, o, out)
    o_ref[0] = out


def _moba_sample(q, kv_new, cache, page_table):
    bs, n_pages = page_table.shape
    assert (n_pages * PAGE_SIZE) % MOBA_BLOCK == 0
    cache_t = jnp.transpose(cache, (0, 2, 3, 4, 1))
    page_spec = lambda j: pl.BlockSpec((1, 2, H_A, D_A, PAGE_SIZE),
                                       lambda b, pt: (pt[b * n_pages + j], 0, 0, 0, 0))
    col_spec = pl.BlockSpec((1, D_A, H_A), lambda b, pt: (b, 0, 0))
    heads_t = lambda a: jnp.transpose(a.reshape(bs, H_A, D_A), (0, 2, 1))
    out = pl.pallas_call(
        _moba_sample_kernel,
        grid_spec=pltpu.PrefetchScalarGridSpec(
            num_scalar_prefetch=1,
            grid=(bs,),
            in_specs=[col_spec, col_spec, col_spec] + [page_spec(j) for j in range(n_pages)],
            out_specs=pl.BlockSpec((1, D_A, LANE), lambda b, pt: (b, 0, 0)),
            scratch_shapes=[pltpu.VMEM((H_A, n_pages, PAGE_SIZE), F32)]),
        out_shape=jax.ShapeDtypeStruct((bs, D_A, LANE), F32),
        compiler_params=_cparams("parallel"),
        name="moba_sample",
    )(page_table.reshape(-1), heads_t(q), heads_t(kv_new[:, 0:QKV_A]), heads_t(kv_new[:, QKV_A:]),
      *([cache_t] * n_pages))
    return jnp.transpose(out[:, :, 0:H_A], (0, 2, 1)).reshape(bs, QKV_A).astype(BF16)


def _odd_mixer_sample_kernel(z_ref, buf_ref, sg_ref, sd_ref, wg2_ref, bg2_ref, gnorm_ref, cw_ref, alog_ref, dtb_ref,
                             dnorm_ref, o_ref, sgo_ref, sdo_ref):
    bt = z_ref.shape[0]
    stride = H_C * DK_C
    small = z_ref[:, O_SM:O_SM + LANE]
    pre = jnp.dot(small, wg2_ref[...], precision=HI, preferred_element_type=F32) + bg2_ref[...]
    a_all = jnp.exp(_log_sigmoid(pre) / GATE_NORM)
    g_all = -jnp.exp(alog_ref[...]) * _softplus(small + dtb_ref[...])
    beta_all = jax.nn.sigmoid(small)
    x = z_ref[:, O_DQKV:O_DQKV + C_CONV]
    y = x * cw_ref[CONV_W - 1:CONV_W, :]
    for j in range(CONV_W - 1):
        y = y + buf_ref[:, j, :] * cw_ref[j:j + 1, :]
    y = _silu(y)
    for h in range(H_C):
        ks = slice(h * DK_C, (h + 1) * DK_C)
        a = a_all[:, ks]
        q = z_ref[:, O_CQ + h * DK_C:O_CQ + (h + 1) * DK_C] * (DK_C ** -0.5)
        k = z_ref[:, O_CK + h * DK_C:O_CK + (h + 1) * DK_C]
        v = z_ref[:, O_CV + h * DV_C:O_CV + (h + 1) * DV_C]
        qa = q * a
        acc = jnp.sum(q * k, axis=1, keepdims=True) * v
        for kk in range(DK_C):
            rows = pl.ds(h * DK_C + kk, bt, stride=stride)
            srow = sg_ref[rows, :]
            acc = acc + qa[:, kk:kk + 1] * srow
            sgo_ref[rows, :] = a[:, kk:kk + 1] * srow + k[:, kk:kk + 1] * v
        gate = _silu(z_ref[:, O_CR + h * DV_C:O_CR + (h + 1) * DV_C])
        o_ref[:, h * DV_C:(h + 1) * DV_C] = (_rms(acc, gnorm_ref[...]) * gate).astype(BF16)
        yq = y[:, h * DK_D:(h + 1) * DK_D]
        yk = y[:, KEY_C + h * DK_D:KEY_C + (h + 1) * DK_D]
        dv = y[:, 2 * KEY_C + h * DV_D:2 * KEY_C + (h + 1) * DV_D]
        dq = yq * lax.rsqrt(jnp.sum(yq * yq, axis=1, keepdims=True) + EPS) * (DK_D ** -0.5)
        dk = yk * lax.rsqrt(jnp.sum(yk * yk, axis=1, keepdims=True) + EPS)
        beta = beta_all[:, SM_DB + h:SM_DB + h + 1]
        eg = jnp.exp(g_all[:, SM_DA + h:SM_DA + h + 1])
        w = dk * (beta * eg)
        qd = dq * eg
        ws = jnp.zeros((bt, DV_D), F32)
        qs = jnp.zeros((bt, DV_D), F32)
        for kk in range(DK_D):
            srow = sd_ref[pl.ds(h * DK_D + kk, bt, stride=stride), :]
            ws = ws + w[:, kk:kk + 1] * srow
            qs = qs + qd[:, kk:kk + 1] * srow
        v_new = dv * beta - ws
        o = qs + jnp.sum(dq * dk, axis=1, keepdims=True) * v_new
        for kk in range(DK_D):
            rows = pl.ds(h * DK_D + kk, bt, stride=stride)
            sdo_ref[rows, :] = sd_ref[rows, :] * eg + dk[:, kk:kk + 1] * v_new
        gate = _silu(z_ref[:, O_DG + h * DV_D:O_DG + (h + 1) * DV_D])
        col = H_C * DV_C + h * DV_D
        o_ref[:, col:col + DV_D] = (_rms(o, dnorm_ref[...]) * gate).astype(BF16)


def _odd_mixer_sample(z, conv_buf, s_gla, s_delta, p, bt):
    bs = z.shape[0]
    rows = H_C * DK_C
    full = lambda a: pl.BlockSpec(a.shape, lambda i: (0,) * a.ndim)
    consts = [p["w_g2_pad"], p["b_g2"], p["gla_norm"], p["conv_w"], p["alog_slab"], p["dtb_slab"], p["delta_norm"]]
    st_spec = pl.BlockSpec((bt * rows, DV_C), lambda i: (i, 0))
    mix, sg, sd = pl.pallas_call(
        _odd_mixer_sample_kernel,
        grid=(bs // bt,),
        in_specs=[pl.BlockSpec((bt, IN_ODD_PAD), lambda i: (i, 0)),
                  pl.BlockSpec((bt, CONV_W - 1, C_CONV), lambda i: (i, 0, 0)),
                  st_spec, st_spec] + [full(a) for a in consts],
        out_specs=[pl.BlockSpec((bt, D_MODEL), lambda i: (i, 0)), st_spec, st_spec],
        out_shape=[jax.ShapeDtypeStruct((bs, D_MODEL), BF16),
                   jax.ShapeDtypeStruct((bs * rows, DV_C), F32),
                   jax.ShapeDtypeStruct((bs * rows, DV_D), F32)],
        compiler_params=_cparams("parallel"),
        name="odd_mixer_sample",
    )(z, conv_buf, s_gla.reshape(bs * rows, DV_C), s_delta.reshape(bs * rows, DV_D), *consts)
    return mix, sg.reshape(s_gla.shape), sd.reshape(s_delta.shape)


def _sample_step(x, cache, page_table, s_gla, s_delta, conv_buf, ev, od, g_final):
    bs = x.shape[0]
    past = page_table.shape[1] * PAGE_SIZE
    z = _norm_matmul(x, ev["norm_mix"], ev["w_in"], bs)
    tables = _rope_tables(jnp.full((1,), past, jnp.int32))
    q, kv, gv, b_out = _even_epilogue_sample(z, tables, ev["ln_g"], ev["ln_b"], ev["w_sp"], ev["b_sp"])
    a_out = _moba_sample(q, kv, cache, page_table)
    x = _proj_ffn([a_out, b_out], x, ev["w_out"], ev["norm_ffn"], ev["w_gate"], ev["w_up"], ev["w_down"], bs)
    z = _norm_matmul(x, od["norm_mix"], od["w_in"], bs)
    mix, sg, sd = _odd_mixer_sample(z, conv_buf, s_gla, s_delta, od, 32)
    conv_new = jnp.concatenate([conv_buf[:, 1:], z[:, None, O_DQKV:O_DQKV + C_CONV]], axis=1)
    y = _moe(mix, x, od, g_final, bs)
    return y, kv, gv, sg, sd, conv_new


def kernel(x_prompt, x_sample, cache_kv, state_gla, state_delta, state_conv, page_table, ev_norm_mix, ev_w_in, ev_gmlp_ln_g, ev_gmlp_ln_b, ev_w_spatial, ev_b_spatial, ev_w_out, ev_norm_ffn, ev_w_gate, ev_w_up, ev_w_down, od_norm_mix, od_w_in, od_gla_w_gate2, od_gla_b_gate2, od_gla_norm, od_delta_conv, od_delta_a_log, od_delta_dt_bias, od_delta_norm, od_w_out, od_norm_ffn, od_router, od_w_gate, od_w_up, od_w_down, norm_final):
    bp, tp, d = x_prompt.shape
    ev = _even_params(0, ev_norm_mix, ev_w_in, ev_gmlp_ln_g, ev_gmlp_ln_b, ev_w_spatial, ev_b_spatial, ev_w_out,
                      ev_norm_ffn, ev_w_gate, ev_w_up, ev_w_down)
    od = _odd_params(0, od_norm_mix, od_w_in, od_gla_w_gate2, od_gla_b_gate2, od_gla_norm, od_delta_conv,
                     od_delta_a_log, od_delta_dt_bias, od_delta_norm, od_w_out, od_norm_ffn, od_router,
                     od_w_gate, od_w_up, od_w_down)
    bs, ts, _ = x_sample.shape
    assert ts == 1 and cache_kv.shape[0] == 1 and state_gla.shape[0] == 1
    xp, kv_p, gv_p = _even_layer_prompt(x_prompt.reshape(bp * tp, d), ev, bp, tp)
    yp, gla_p, dl_p, cv_p = _odd_layer_prompt(xp, od, norm_final[None], bp, tp)
    ys, kv_s, gv_s, gla_s, dl_s, cv_s = _sample_step(
        x_sample.reshape(bs, d), cache_kv[0], page_table, state_gla[0], state_delta[0], state_conv[0],
        ev, od, norm_final[None])
    return (yp.reshape(bp, tp, d), ys.reshape(bs, ts, d),
            kv_p[None], kv_s.reshape(1, bs, ts, 2, H_A, D_A),
            gv_p[None], gv_s.reshape(1, bs, ts, D_B),
            gla_p[None], gla_s[None], dl_p[None], dl_s[None], cv_p[None], cv_s[None])
```

```python
import functools
import math

import jax
import jax.numpy as jnp
from jax import lax
from jax.experimental import pallas as pl
from jax.experimental.pallas import tpu as pltpu

F32 = jnp.float32
BF16 = jnp.bfloat16
HI = lax.Precision.HIGHEST
EPS = 1e-6
NEG = -1e30

D_MODEL = 1024
PAGE_SIZE = 128
H_A, D_A = 8, 64
ROT_DIM = D_A // 4
ROPE_THETA = 500000.0
MOBA_BLOCK = 256
MOBA_TOPK = 3
G_B, D_BG = 8, 64
D_B = G_B * D_BG
CHUNK_B = 128
H_C, DK_C, DV_C = 4, 64, 128
GATE_RANK = 16
GATE_NORM = 16.0
H_D, DK_D, DV_D = 4, 64, 128
CONV_W = 4
C_CONV = 2 * H_D * DK_D + H_D * DV_D
CHUNK_LIN = 64
N_EXPERTS = 8
TOP_K = 2
QKV_A = H_A * D_A
IN_EVEN = 3 * QKV_A + 2 * D_B
TOKEN_TILE = 512
EPILOGUE_TILE = 256
MIXER_TILE = 256
SAMPLE_MIXER_TILE = 32
FFN_TF = 1408
MOE_TF = 512
MOBA_LOOP_BLOCKS = 4
DMA_UNROLL = 8
LANE = 128
VMEM_LIMIT = 56 * 1024 * 1024


def _cparams(*sem):
    return pltpu.CompilerParams(dimension_semantics=sem, vmem_limit_bytes=VMEM_LIMIT)


def _rms(x, g):
    return x * lax.rsqrt(jnp.mean(x * x, axis=-1, keepdims=True) + EPS) * g


def _gelu(x):
    return 0.5 * x * (1.0 + lax.erf(x * (2.0 ** -0.5)))


def _silu(x):
    return x * jax.nn.sigmoid(x)


def _softplus(x):
    return jnp.maximum(x, 0.0) + jnp.log1p(jnp.exp(-jnp.abs(x)))


def _dot(a, b):
    return jnp.dot(a, b, preferred_element_type=F32)


def _dot_nt(a, b):
    return lax.dot_general(a, b, (((1,), (1,)), ((), ())), preferred_element_type=F32)


def _dot_tn(a, b):
    return lax.dot_general(a, b, (((0,), (0,)), ((), ())), preferred_element_type=F32)


def _norm_matmul_kernel(x_ref, g_ref, w_ref, o_ref):
    h = _rms(x_ref[...], g_ref[...]).astype(BF16)
    o_ref[...] = _dot(h, w_ref[...])


def _norm_matmul(x, g, w, tm):
    m, d = x.shape
    n = w.shape[1]
    return pl.pallas_call(
        _norm_matmul_kernel,
        grid=(m // tm,),
        in_specs=[pl.BlockSpec((tm, d), lambda i: (i, 0)),
                  pl.BlockSpec((1, d), lambda i: (0, 0)),
                  pl.BlockSpec((d, n), lambda i: (0, 0))],
        out_specs=pl.BlockSpec((tm, n), lambda i: (i, 0)),
        out_shape=jax.ShapeDtypeStruct((m, n), F32),
        compiler_params=_cparams("parallel"),
        name="norm_matmul",
    )(x, g, w)


def _rope_tables(pos):
    half = ROT_DIM // 2
    inv = ROPE_THETA ** (-jnp.arange(half, dtype=F32) / half)
    ang = pos.astype(F32)[:, None] * inv[None, :]
    cos, sin = jnp.cos(ang), jnp.sin(ang)
    t = pos.shape[0]
    one = jnp.ones((t, D_A - ROT_DIM), F32)
    zero_h = jnp.zeros((t, half), F32)
    zero_r = jnp.zeros((t, D_A - ROT_DIM), F32)
    c = jnp.concatenate([cos, cos, one], axis=1)
    s_up = jnp.concatenate([-sin, zero_h, zero_r], axis=1)
    s_dn = jnp.concatenate([zero_h, sin, zero_r], axis=1)
    rep = LANE // D_A
    return jnp.tile(c, (1, rep)), jnp.tile(s_up, (1, rep)), jnp.tile(s_dn, (1, rep))


def _rope(x, c, s_up, s_dn):
    half = ROT_DIM // 2
    outs = []
    for j in range(x.shape[1] // LANE):
        xs = x[:, j * LANE:(j + 1) * LANE]
        up = pltpu.roll(xs, LANE - half, 1)
        dn = pltpu.roll(xs, half, 1)
        outs.append(xs * c + up * s_up + dn * s_dn)
    return jnp.concatenate(outs, axis=1)


def _layernorm(x, g, b):
    mu = jnp.mean(x, axis=-1, keepdims=True)
    xc = x - mu
    var = jnp.mean(xc * xc, axis=-1, keepdims=True)
    return xc * lax.rsqrt(var + EPS) * g + b


def _even_epilogue_kernel(z_ref, c_ref, su_ref, sd_ref, lng_ref, lnb_ref, wsp_ref, bspt_ref,
                          qt_ref, k_ref, vt_ref, kv_ref, kmean_ref, gv_ref, bout_ref):
    tm = z_ref.shape[0]
    c, su, sd = c_ref[...], su_ref[...], sd_ref[...]
    q = _rope(z_ref[:, 0:QKV_A], c, su, sd) * (D_A ** -0.5)
    k = _rope(z_ref[:, QKV_A:2 * QKV_A], c, su, sd)
    v = z_ref[:, 2 * QKV_A:3 * QKV_A]
    qt_ref[0] = q.T.astype(BF16)
    k_ref[...] = k.astype(BF16)
    vt = v.T
    vt_ref[0] = vt.astype(BF16)
    kv_ref[0, 0:QKV_A, :] = k.T
    kv_ref[0, QKV_A:2 * QKV_A, :] = vt
    for blk in range(tm // MOBA_BLOCK):
        kmean_ref[blk] = jnp.mean(k[blk * MOBA_BLOCK:(blk + 1) * MOBA_BLOCK], axis=0, keepdims=True)
    u = _gelu(z_ref[:, 3 * QKV_A:3 * QKV_A + D_B])
    gv = _layernorm(_gelu(z_ref[:, 3 * QKV_A + D_B:3 * QKV_A + 2 * D_B]), lng_ref[...], lnb_ref[...])
    gv_ref[...] = gv
    gvb = gv.astype(BF16)
    row = lax.broadcasted_iota(jnp.int32, (CHUNK_B, CHUNK_B), 0)
    col = lax.broadcasted_iota(jnp.int32, (CHUNK_B, CHUNK_B), 1)
    group = lax.broadcasted_iota(jnp.int32, (CHUNK_B, D_B), 1) // D_BG
    w = [jnp.where(row >= col, wsp_ref[g], 0.0).astype(BF16) for g in range(G_B)]
    for ch in range(tm // CHUNK_B):
        gvc = gvb[ch * CHUNK_B:(ch + 1) * CHUNK_B]
        mixed = jnp.zeros((CHUNK_B, D_B), F32)
        for g in range(G_B):
            mixed = jnp.where(group == g, _dot(w[g], gvc) + bspt_ref[:, g:g + 1], mixed)
        bout_ref[ch * CHUNK_B:(ch + 1) * CHUNK_B, :] = (u[ch * CHUNK_B:(ch + 1) * CHUNK_B] * mixed).astype(BF16)


def _even_epilogue(z, tables, ln_g, ln_b, w_sp, b_sp, batch, seq, tm):
    n = batch * seq
    nt = seq // tm
    nblk = tm // MOBA_BLOCK
    tab_spec = pl.BlockSpec((tm, LANE), lambda b, i: (i, 0))
    row_spec = lambda width: pl.BlockSpec((tm, width), lambda b, i: (b * nt + i, 0))
    t_spec = pl.BlockSpec((1, QKV_A, tm), lambda b, i: (b, 0, i))
    return pl.pallas_call(
        _even_epilogue_kernel,
        grid=(batch, nt),
        in_specs=[row_spec(IN_EVEN), tab_spec, tab_spec, tab_spec,
                  pl.BlockSpec((1, D_B), lambda b, i: (0, 0)),
                  pl.BlockSpec((1, D_B), lambda b, i: (0, 0)),
                  pl.BlockSpec((G_B, CHUNK_B, CHUNK_B), lambda b, i: (0, 0, 0)),
                  pl.BlockSpec((CHUNK_B, G_B), lambda b, i: (0, 0))],
        out_specs=[t_spec, row_spec(QKV_A), t_spec, pl.BlockSpec((1, 2 * QKV_A, tm), lambda b, i: (b, 0, i)),
                   pl.BlockSpec((nblk, 1, QKV_A), lambda b, i: (b * nt + i, 0, 0)),
                   row_spec(D_B), row_spec(D_B)],
        out_shape=[jax.ShapeDtypeStruct((batch, QKV_A, seq), BF16),
                   jax.ShapeDtypeStruct((n, QKV_A), BF16),
                   jax.ShapeDtypeStruct((batch, QKV_A, seq), BF16),
                   jax.ShapeDtypeStruct((batch, 2 * QKV_A, seq), F32),
                   jax.ShapeDtypeStruct((n // MOBA_BLOCK, 1, QKV_A), F32),
                   jax.ShapeDtypeStruct((n, D_B), F32),
                   jax.ShapeDtypeStruct((n, D_B), BF16)],
        compiler_params=_cparams("parallel", "parallel"),
        name="even_epilogue",
    )(z, *tables, ln_g, ln_b, w_sp, b_sp.T)


def _moba_select(gate, n_own):
    nb = gate.shape[0]
    blk = lax.broadcasted_iota(jnp.int32, gate.shape, 0)
    elig = blk < n_own
    gm = jnp.where(elig, gate, NEG)
    rank = jnp.zeros(gate.shape, F32)
    for m in range(nb):
        gm_m = gm[m:m + 1, :]
        ahead = (gm_m > gm) | ((gm_m == gm) & (m < blk))
        rank = rank + ahead.astype(F32)
    return (elig & (rank < MOBA_TOPK)).astype(F32)


def _moba_prompt_kernel(qt_ref, k_ref, vt_ref, kmean_ref, o_ref, bias_ref):
    i = pl.program_id(2)
    tq = MOBA_BLOCK
    n_head = LANE // D_A
    n_split = tq // LANE
    own = pl.multiple_of(i * tq, tq)
    key_i = lax.broadcasted_iota(jnp.int32, (tq, LANE), 0)
    qry_i = lax.broadcasted_iota(jnp.int32, (tq, LANE), 1)
    head_lanes = [slice(hh * D_A, (hh + 1) * D_A) for hh in range(n_head)]
    streams = [(hh, qh) for hh in range(n_head) for qh in range(n_split)]
    qts = [qt_ref[0, head_lanes[hh], :] for hh in range(n_head)]
    queries = [qts[hh][:, qh * LANE:(qh + 1) * LANE] for hh, qh in streams]
    k_own = [k_ref[pl.ds(own, tq), head_lanes[hh]] for hh in range(n_head)]
    own_scores = [_dot(k_own[hh], queries[idx]) for idx, (hh, qh) in enumerate(streams)]
    gates = [jnp.dot(kmean_ref[0, :, head_lanes[hh]], qts[hh].astype(F32), precision=HI, preferred_element_type=F32)
             for hh in range(n_head)]
    for hh in range(n_head):
        bias = jnp.where(_moba_select(gates[hh], i) > 0.0, 0.0, NEG)
        for qh in range(n_split):
            bias_ref[hh, qh] = bias[:, qh * LANE:(qh + 1) * LANE]
    own_probs, own_stats = [], []
    for idx, (hh, qh) in enumerate(streams):
        s = jnp.where(key_i <= qry_i + qh * LANE, own_scores[idx], NEG)
        m = jnp.max(s, axis=0, keepdims=True)
        p = jnp.exp(s - m)
        own_probs.append(p.astype(BF16))
        own_stats.append((m, jnp.sum(p, axis=0, keepdims=True)))
    vt_own = [vt_ref[0, head_lanes[hh], pl.ds(own, tq)] for hh in range(n_head)]
    state = []
    for idx, (hh, qh) in enumerate(streams):
        state += [own_stats[idx][0], own_stats[idx][1], _dot(vt_own[hh], own_probs[idx])]

    def body(j0, carry, nk):
        tk = nk * tq
        start = pl.multiple_of(j0 * tq, tq)
        k_j = [k_ref[pl.ds(start, tk), head_lanes[hh]] for hh in range(n_head)]
        scores = [_dot(k_j[hh], queries[hh * n_split + qh]) for hh, qh in streams]
        probs, stats = [], []
        for idx, (hh, qh) in enumerate(streams):
            m, l = carry[3 * idx], carry[3 * idx + 1]
            s = [scores[idx][b * tq:(b + 1) * tq] + bias_ref[hh, qh, pl.ds(j0 + b, 1), :] for b in range(nk)]
            m_new = m
            for sb in s:
                m_new = jnp.maximum(m_new, jnp.max(sb, axis=0, keepdims=True))
            alpha = jnp.exp(m - m_new)
            p = [jnp.exp(sb - m_new) for sb in s]
            l_new = l * alpha
            for pb in p:
                l_new = l_new + jnp.sum(pb, axis=0, keepdims=True)
            probs.append(jnp.concatenate([pb.astype(BF16) for pb in p], axis=0))
            stats.append((m_new, l_new, alpha))
        vt_j = [vt_ref[0, head_lanes[hh], pl.ds(start, tk)] for hh in range(n_head)]
        pv = [_dot(vt_j[hh], probs[idx]) for idx, (hh, qh) in enumerate(streams)]
        out = []
        for idx in range(len(streams)):
            m_new, l_new, alpha = stats[idx]
            out += [m_new, l_new, carry[3 * idx + 2] * alpha + pv[idx]]
        return tuple(out)

    state = tuple(state)
    done = 0
    nk = MOBA_LOOP_BLOCKS
    while nk >= 1:
        n_group = (i - done) // nk
        state = lax.fori_loop(0, n_group, lambda t, c, nk=nk, done=done: body(done + t * nk, c, nk), state)
        done = done + n_group * nk
        nk //= 2
    outs = []
    for hh in range(n_head):
        halves = [state[3 * (hh * n_split + qh) + 2] / state[3 * (hh * n_split + qh) + 1] for qh in range(n_split)]
        outs.append(jnp.concatenate(halves, axis=1))
    o_ref[...] = jnp.concatenate(outs, axis=0).T.astype(BF16)


def _moba_prompt(qt, k, vt, kmean, batch, seq):
    nq = seq // MOBA_BLOCK
    hp = QKV_A // LANE
    return pl.pallas_call(
        _moba_prompt_kernel,
        grid=(batch, hp, nq),
        in_specs=[pl.BlockSpec((1, LANE, MOBA_BLOCK), lambda b, h, i: (b, h, i)),
                  pl.BlockSpec((seq, LANE), lambda b, h, i: (b, h)),
                  pl.BlockSpec((1, LANE, seq), lambda b, h, i: (b, h, 0)),
                  pl.BlockSpec((1, nq, LANE), lambda b, h, i: (b, 0, h))],
        out_specs=pl.BlockSpec((MOBA_BLOCK, LANE), lambda b, h, i: (b * nq + i, h)),
        out_shape=jax.ShapeDtypeStruct((batch * seq, QKV_A), BF16),
        scratch_shapes=[pltpu.VMEM((LANE // D_A, MOBA_BLOCK // LANE, nq, LANE), F32)],
        compiler_params=_cparams("parallel", "parallel", "arbitrary"),
        name="moba_prompt",
    )(qt, k, vt, kmean)


def _proj_ffn_kernel(*refs, n_mix):
    mix_refs = refs[:n_mix]
    x_ref, wo_ref, g_ref, wg_ref, wu_ref, wd_ref, o_ref, hn_ref = refs[n_mix:]

    @pl.when(pl.program_id(1) == 0)
    def _():
        x1 = x_ref[...]
        off = 0
        for r in mix_refs:
            width = r.shape[1]
            x1 = x1 + _dot(r[...], wo_ref[off:off + width, :])
            off += width
        o_ref[...] = x1
        hn_ref[...] = _rms(x1, g_ref[...]).astype(BF16)

    h = hn_ref[...]
    act = (_silu(_dot(h, wg_ref[...])) * _dot(h, wu_ref[...])).astype(BF16)
    o_ref[...] += _dot(act, wd_ref[...])


def _col_blocks(w, tf):
    *lead, d, ff = w.shape
    n = len(lead)
    return jnp.transpose(w.reshape(*lead, d, ff // tf, tf), (*range(n), n + 1, n, n + 2))


def _proj_ffn(mixes, x, w_out, g, w_gate, w_up, w_down, tm):
    m, d = x.shape
    nf, _, tf = w_gate.shape
    mix_specs = [pl.BlockSpec((tm, a.shape[1]), lambda i, f: (i, 0)) for a in mixes]
    return pl.pallas_call(
        functools.partial(_proj_ffn_kernel, n_mix=len(mixes)),
        grid=(m // tm, nf),
        in_specs=mix_specs + [
            pl.BlockSpec((tm, d), lambda i, f: (i, 0)),
            pl.BlockSpec(w_out.shape, lambda i, f: (0, 0)),
            pl.BlockSpec((1, d), lambda i, f: (0, 0)),
            pl.BlockSpec((None, d, tf), lambda i, f: (f, 0, 0)),
            pl.BlockSpec((None, d, tf), lambda i, f: (f, 0, 0)),
            pl.BlockSpec((tf, d), lambda i, f: (f, 0))],
        out_specs=pl.BlockSpec((tm, d), lambda i, f: (i, 0)),
        out_shape=jax.ShapeDtypeStruct((m, d), F32),
        scratch_shapes=[pltpu.VMEM((tm, d), BF16)],
        compiler_params=_cparams("parallel", "arbitrary"),
        name="proj_ffn",
    )(*mixes, x, w_out, g, w_gate, w_up, w_down)


def _even_layer_prompt(x, p, batch, seq):
    z = _norm_matmul(x, p["norm_mix"], p["w_in"], TOKEN_TILE)
    tables = _rope_tables(jnp.arange(seq, dtype=jnp.int32))
    qt, k, vt, kv, kmean, gv, b_out = _even_epilogue(
        z, tables, p["ln_g"], p["ln_b"], p["w_sp"], p["b_sp"], batch, seq, EPILOGUE_TILE)
    a_out = _moba_prompt(qt, k, vt, kmean.reshape(batch, seq // MOBA_BLOCK, QKV_A), batch, seq)
    x = _proj_ffn([a_out, b_out], x, p["w_out"], p["norm_ffn"], p["w_gate"], p["w_up"], p["w_down"], TOKEN_TILE)
    n_open = seq - ((seq - 1) // CHUNK_B) * CHUNK_B
    gv_open = gv.reshape(batch, seq, D_B)[:, seq - n_open:]
    kv = jnp.transpose(kv.reshape(batch, 2, H_A, D_A, seq), (0, 4, 1, 2, 3))
    return x, kv, gv_open


def _even_params(i, ev_norm_mix, ev_w_in, ev_gmlp_ln_g, ev_gmlp_ln_b, ev_w_spatial, ev_b_spatial, ev_w_out,
                 ev_norm_ffn, ev_w_gate, ev_w_up, ev_w_down):
    return dict(norm_mix=ev_norm_mix[i][None], w_in=ev_w_in[i].astype(BF16),
                ln_g=ev_gmlp_ln_g[i][None], ln_b=ev_gmlp_ln_b[i][None],
                w_sp=ev_w_spatial[i], b_sp=ev_b_spatial[i], w_out=ev_w_out[i].astype(BF16),
                norm_ffn=ev_norm_ffn[i][None], w_gate=_col_blocks(ev_w_gate[i].astype(BF16), FFN_TF),
                w_up=_col_blocks(ev_w_up[i].astype(BF16), FFN_TF), w_down=ev_w_down[i].astype(BF16))


O_CQ, O_CK, O_CV, O_CR, O_DQKV, O_DG, O_SM = 0, 256, 512, 1024, 1536, 2560, 3072
SM_DA, SM_DB = GATE_RANK, GATE_RANK + H_D
IN_ODD_PAD = O_SM + LANE
KEY_C = H_C * DK_C
PAIR = LANE // DK_C


def _log_sigmoid(x):
    return jnp.minimum(x, 0.0) - jnp.log1p(jnp.exp(-jnp.abs(x)))


def _split_bf16(a, terms):
    parts = []
    for _ in range(terms - 1):
        hi = a.astype(BF16)
        parts.append(hi)
        a = a - hi.astype(F32)
    parts.append(a.astype(BF16))
    return parts


def _dot_select(a, sel_bf16, terms):
    parts = _split_bf16(a, terms)
    out = _dot(parts[0], sel_bf16)
    for part in parts[1:]:
        out = out + _dot(part, sel_bf16)
    return out


def _select_dot(sel_bf16, a, terms):
    parts = _split_bf16(a, terms)
    out = _dot(sel_bf16, parts[0])
    for part in parts[1:]:
        out = out + _dot(sel_bf16, part)
    return out


def _unit_lower_inverses(mats):
    c = mats[0].shape[0]
    assert c == 64
    eye = (lax.broadcasted_iota(jnp.int32, (c, c), 0) == lax.broadcasted_iota(jnp.int32, (c, c), 1)).astype(F32)
    sp = lambda ms: [_split_bf16(m, 2) for m in ms]
    mul = lambda aa, bb: [_dot(ah, bh) + _dot(ah, bl) + _dot(al, bh) for (ah, al), (bh, bl) in zip(aa, bb)]
    x1 = [-a for a in mats]
    x1s = sp(x1)
    x2 = mul(x1s, x1s)
    x2s = sp(x2)
    x4 = mul(x2s, x2s)
    x3 = mul(x1s, x2s)
    x4s = sp(x4)
    x8 = mul(x4s, x4s)
    f01 = [eye + a + b + d for a, b, d in zip(x1, x2, x3)]
    x8s = sp(x8)
    x16 = mul(x8s, x8s)
    x12 = mul(x4s, x8s)
    g23 = [a + b + d for a, b, d in zip(x4, x8, x12)]
    x16s = sp(x16)
    x32 = mul(x16s, x16s)
    f0123 = [f + d for f, d in zip(f01, mul(sp(f01), sp(g23)))]
    x48 = mul(x16s, sp(x32))
    g45 = [a + b + d for a, b, d in zip(x16, x32, x48)]
    return [f + d for f, d in zip(f0123, mul(sp(f0123), sp(g45)))]


def _group_sumsq(y, width):
    n = y.shape[1]
    same = (lax.broadcasted_iota(jnp.int32, (n, n), 0) // width
            == lax.broadcasted_iota(jnp.int32, (n, n), 1) // width).astype(BF16)
    return _dot_select(y * y, same, 2)


def _odd_mixer_prompt_kernel(z_ref, wg2_ref, bg2_ref, gnorm_ref, cw_ref, alog_ref, dtb_ref, dnorm_ref,
                             o_ref, sg_ref, sd_ref, tail_ref,
                             stg_ref, std_ref, prev_ref, la_ref, qkv_ref, dla_ref, beta_ref):
    i = pl.program_id(1)
    ns, tc = z_ref.shape[0], z_ref.shape[1]
    c = CHUNK_LIN

    @pl.when(i == 0)
    def _():
        stg_ref[...] = jnp.zeros_like(stg_ref)
        std_ref[...] = jnp.zeros_like(std_ref)
        prev_ref[...] = jnp.zeros_like(prev_ref)

    row8 = lax.broadcasted_iota(jnp.int32, (8, C_CONV), 0)
    qscale = jnp.where(lax.broadcasted_iota(jnp.int32, (1, 2 * KEY_C), 1) < KEY_C, DK_D ** -0.5, 1.0)
    for sq in range(ns):
        small = z_ref[sq, :, O_SM:O_SM + LANE]
        pre = jnp.dot(small, wg2_ref[...], precision=HI, preferred_element_type=F32) + bg2_ref[...]
        la_ref[sq] = _log_sigmoid(pre) / GATE_NORM
        dla_ref[sq] = -jnp.exp(alog_ref[...]) * _softplus(small + dtb_ref[...])
        beta_ref[sq] = jax.nn.sigmoid(small)

        x = z_ref[sq, :, O_DQKV:O_DQKV + C_CONV]
        x8 = x[0:8]
        p8 = prev_ref[sq]
        y = x * cw_ref[CONV_W - 1:CONV_W, :]
        y8 = x8 * cw_ref[CONV_W - 1:CONV_W, :]
        for s in range(1, CONV_W):
            wrow = cw_ref[CONV_W - 1 - s:CONV_W - s, :]
            y = y + pltpu.roll(x, s, 0) * wrow
            y8 = y8 + jnp.where(row8 < s, pltpu.roll(p8, s, 0), pltpu.roll(x8, s, 0)) * wrow
        prev_ref[sq] = x[tc - 8:tc]
        y = _silu(y)
        y8 = _silu(y8)
        yqk = y[:, 0:2 * KEY_C]
        nrm = lax.rsqrt(_group_sumsq(yqk, DK_D) + EPS)
        qkv_ref[sq, :, 0:2 * KEY_C] = yqk * nrm * qscale
        qkv_ref[sq, :, 2 * KEY_C:] = y[:, 2 * KEY_C:]
        yqk8 = y8[:, 0:2 * KEY_C]
        qkv_ref[sq, 0:8, 0:2 * KEY_C] = yqk8 * lax.rsqrt(_group_sumsq(yqk8, DK_D) + EPS) * qscale
        qkv_ref[sq, 0:8, 2 * KEY_C:] = y8[:, 2 * KEY_C:]

    ri = lax.broadcasted_iota(jnp.int32, (c, c), 0)
    ci = lax.broadcasted_iota(jnp.int32, (c, c), 1)
    tril = ri >= ci
    strict = ri > ci
    lower = tril.astype(BF16)
    upper = (ri <= ci).astype(BF16)
    lane_head = lax.broadcasted_iota(jnp.int32, (c, LANE), 1) // DK_C
    lane_head_row = lax.broadcasted_iota(jnp.int32, (1, LANE), 1) // DK_C

    def chunk(ch, carry):
        rows = pl.ds(pl.multiple_of(ch * c, c), c)
        n_pair = H_C // PAIR
        seqs = range(ns)
        units = [(sq, h) + divmod(h, PAIR) for sq in seqs for h in range(H_C)]
        pairs = [(sq, p) for sq in seqs for p in range(n_pair)]
        unit_at = lambda sq, p, hh: (sq * H_C) + p * PAIR + hh
        masks = [lane_head == hh for hh in range(PAIR)]
        bcum_all = [_select_dot(lower, la_ref[sq, rows, :], 3) for sq in seqs]
        dla_parts = [_split_bf16(dla_ref[sq, rows, :], 3) for sq in seqs]
        g_cols = [_dot(lower, dp[0]) + _dot(lower, dp[1]) + _dot(lower, dp[2]) for dp in dla_parts]
        g_rows = [_dot_tn(dp[0], upper) + _dot_tn(dp[1], upper) + _dot_tn(dp[2], upper) for dp in dla_parts]
        st_d = {sp: std_ref[sp[0], sp[1]] for sp in pairs}
        st_db = {sp: st_d[sp].astype(BF16) for sp in pairs}
        dn = []
        for sq, h, p, hh in units:
            g_col = jnp.broadcast_to(g_cols[sq][:, SM_DA + h:SM_DA + h + 1], (c, LANE))
            g_row = jnp.broadcast_to(g_rows[sq][SM_DA + h:SM_DA + h + 1, :], (c, c))
            beta_b = jnp.broadcast_to(beta_ref[sq, rows, SM_DB + h:SM_DB + h + 1], (c, LANE))
            km = jnp.where(masks[hh], qkv_ref[sq, rows, KEY_C + p * LANE:KEY_C + (p + 1) * LANE], 0.0)
            qm = jnp.where(masks[hh], qkv_ref[sq, rows, p * LANE:(p + 1) * LANE], 0.0)
            dn.append(dict(g_col=g_col, g_row=g_row, beta_b=beta_b, km=km, qm=qm, kmb=km.astype(BF16),
                           kb=km * beta_b, eg=jnp.exp(g_col), g_last=g_col[c - 1:c, :]))
        kk = [_dot_nt(d["kb"].astype(BF16), d["kmb"]) for d in dn]
        qk = [_dot_nt(d["qm"].astype(BF16), d["kmb"]) for d in dn]
        d_inter = [_dot_nt((d["qm"] * d["eg"]).astype(BF16), st_db[(sq, p)]) for d, (sq, h, p, hh) in zip(dn, units)]
        decay = [jnp.exp(jnp.where(tril, d["g_col"][:, 0:c] - d["g_row"], NEG)) for d in dn]
        a_mats = [jnp.where(strict, m * dc, 0.0) for m, dc in zip(kk, decay)]
        st_g = {sp: stg_ref[sp[0], sp[1]] for sp in pairs}
        st_gb = {sp: st_g[sp].astype(BF16) for sp in pairs}
        gl_pair = {}
        for sq, p in pairs:
            bcum = bcum_all[sq][:, p * LANE:(p + 1) * LANE]
            b_end = bcum[c - 1:c, :]
            k = z_ref[sq, rows, O_CK + p * LANE:O_CK + (p + 1) * LANE]
            gl_pair[(sq, p)] = dict(
                q_in=z_ref[sq, rows, O_CQ + p * LANE:O_CQ + (p + 1) * LANE] * (DK_C ** -0.5) * jnp.exp(bcum),
                k_in=(k * jnp.exp(-bcum)).astype(BF16), k_end=k * jnp.exp(b_end - bcum), b_end=b_end)
        g_qm = [jnp.where(masks[hh], gl_pair[(sq, p)]["q_in"], 0.0).astype(BF16) for sq, h, p, hh in units]
        g_ke = [jnp.where(masks[hh], gl_pair[(sq, p)]["k_end"], 0.0).astype(BF16) for sq, h, p, hh in units]
        g_vb = [z_ref[sq, rows, O_CV + h * DV_C:O_CV + (h + 1) * DV_C].astype(BF16) for sq, h, p, hh in units]
        g_sc = [_dot_nt(g_qm[u], gl_pair[(sq, p)]["k_in"]) for u, (sq, h, p, hh) in enumerate(units)]
        g_inter = [_dot_nt(g_qm[u], st_gb[(sq, p)]) for u, (sq, h, p, hh) in enumerate(units)]
        g_upd = [_dot_tn(g_vb[u], g_ke[u]) for u in range(len(units))]
        g_intra = [_dot(jnp.where(tril, g_sc[u], 0.0).astype(BF16), g_vb[u]) for u in range(len(units))]
        t_invs = _unit_lower_inverses(a_mats)
        for u, (sq, h, p, hh) in enumerate(units):
            gate = _silu(z_ref[sq, rows, O_CR + h * DV_C:O_CR + (h + 1) * DV_C])
            o_ref[sq, rows, h * DV_C:(h + 1) * DV_C] = (
                _rms(g_intra[u] + g_inter[u], gnorm_ref[...]) * gate).astype(BF16)
        for sq, p in pairs:
            stg_ref[sq, p] = (st_g[(sq, p)] * jnp.exp(gl_pair[(sq, p)]["b_end"])
                              + g_upd[unit_at(sq, p, 0)] + g_upd[unit_at(sq, p, 1)])
        rhs = [jnp.concatenate([qkv_ref[sq, rows, 2 * KEY_C + h * DV_D:2 * KEY_C + (h + 1) * DV_D] * d["beta_b"],
                                d["kb"] * d["eg"]], axis=1).astype(BF16) for d, (sq, h, p, hh) in zip(dn, units)]
        sol = [_dot(t.astype(BF16), r) for t, r in zip(t_invs, rhs)]
        w_st = [_dot_nt(s[:, DV_D:].astype(BF16), st_db[(sq, p)]) for s, (sq, h, p, hh) in zip(sol, units)]
        v_new = [(s[:, 0:DV_D] - ws).astype(BF16) for s, ws in zip(sol, w_st)]
        d_intra = [_dot(jnp.where(tril, q * dc, 0.0).astype(BF16), vn) for q, dc, vn in zip(qk, decay, v_new)]
        d_upd = [_dot_tn(vn, (d["km"] * jnp.exp(d["g_last"] - d["g_col"])).astype(BF16)) for vn, d in zip(v_new, dn)]
        for u, (sq, h, p, hh) in enumerate(units):
            gate = _silu(z_ref[sq, rows, O_DG + h * DV_D:O_DG + (h + 1) * DV_D])
            col = H_C * DV_C + h * DV_D
            o_ref[sq, rows, col:col + DV_D] = (_rms(d_inter[u] + d_intra[u], dnorm_ref[...]) * gate).astype(BF16)
        for sq, p in pairs:
            u0, u1 = unit_at(sq, p, 0), unit_at(sq, p, 1)
            dec_row = jnp.where(lane_head_row == 0, jnp.exp(dn[u0]["g_last"]), jnp.exp(dn[u1]["g_last"]))
            std_ref[sq, p] = st_d[(sq, p)] * dec_row + d_upd[u0] + d_upd[u1]
        return carry

    lax.fori_loop(0, tc // c, chunk, 0)

    @pl.when(i == pl.num_programs(1) - 1)
    def _():
        for sq in range(ns):
            tail_ref[sq] = prev_ref[sq]
            for p in range(H_C // PAIR):
                tg = stg_ref[sq, p].T
                td = std_ref[sq, p].T
                for hh in range(PAIR):
                    sg_ref[sq, p * PAIR + hh] = tg[hh * DK_C:(hh + 1) * DK_C, :]
                    sd_ref[sq, p * PAIR + hh] = td[hh * DK_D:(hh + 1) * DK_D, :]


def _odd_mixer_prompt(z, p, batch, seq, tc):
    nt = seq // tc
    ns = 2 if batch % 2 == 0 else 1
    full = lambda a: pl.BlockSpec(a.shape, lambda b, i: (0,) * a.ndim)
    consts = [p["w_g2_pad"], p["b_g2"], p["gla_norm"], p["conv_w"], p["alog_slab"], p["dtb_slab"], p["delta_norm"]]
    st_spec = pl.BlockSpec((ns, H_C, DK_C, DV_C), lambda b, i: (b, 0, 0, 0))
    mix, s_gla, s_delta, tail = pl.pallas_call(
        _odd_mixer_prompt_kernel,
        grid=(batch // ns, nt),
        in_specs=[pl.BlockSpec((ns, tc, IN_ODD_PAD), lambda b, i: (b, i, 0))] + [full(a) for a in consts],
        out_specs=[pl.BlockSpec((ns, tc, D_MODEL), lambda b, i: (b, i, 0)), st_spec, st_spec,
                   pl.BlockSpec((ns, 8, C_CONV), lambda b, i: (b, 0, 0))],
        out_shape=[jax.ShapeDtypeStruct((batch, seq, D_MODEL), BF16),
                   jax.ShapeDtypeStruct((batch, H_C, DK_C, DV_C), F32),
                   jax.ShapeDtypeStruct((batch, H_D, DK_D, DV_D), F32),
                   jax.ShapeDtypeStruct((batch, 8, C_CONV), F32)],
        scratch_shapes=[pltpu.VMEM((ns, H_C // PAIR, DV_C, LANE), F32),
                        pltpu.VMEM((ns, H_D // PAIR, DV_D, LANE), F32),
                        pltpu.VMEM((ns, 8, C_CONV), F32),
                        pltpu.VMEM((ns, tc, KEY_C), F32),
                        pltpu.VMEM((ns, tc, C_CONV), F32),
                        pltpu.VMEM((ns, tc, LANE), F32),
                        pltpu.VMEM((ns, tc, LANE), F32)],
        compiler_params=_cparams("parallel", "arbitrary"),
        name="odd_mixer_prompt",
    )(z.reshape(batch, seq, IN_ODD_PAD), *consts)
    return mix.reshape(batch * seq, D_MODEL), s_gla, s_delta, tail


M_E1, M_E2, M_R1, M_R2, M_G1, M_G2 = (N_EXPERTS + j for j in range(6))


def _proj_router_kernel(mix_ref, x_ref, wo_ref, g_ref, wr_ref, x1_ref, hn_ref, meta_ref, cnt_ref, carry_ref):
    tm = x_ref.shape[0]

    @pl.when(pl.program_id(0) == 0)
    def _():
        carry_ref[...] = jnp.zeros_like(carry_ref)

    x1 = x_ref[...] + _dot(mix_ref[...], wo_ref[...])
    x1_ref[...] = x1
    hn = _rms(x1, g_ref[...])
    hn_ref[...] = hn.astype(hn_ref.dtype)
    lane = lax.broadcasted_iota(jnp.int32, (tm, LANE), 1)
    hn_hi, hn_lo = _split_bf16(hn, 2)
    wr_hi, wr_lo = _split_bf16(wr_ref[...], 2)
    logits = _dot(hn_hi, wr_hi) + _dot(hn_hi, wr_lo) + _dot(hn_lo, wr_hi)
    logits = jnp.where(lane < N_EXPERTS, logits, NEG)
    m1 = jnp.max(logits, axis=1, keepdims=True)
    e1 = jnp.min(jnp.where(logits == m1, lane, LANE), axis=1, keepdims=True)
    rest = jnp.where(lane == e1, NEG, logits)
    m2 = jnp.max(rest, axis=1, keepdims=True)
    e2 = jnp.min(jnp.where(rest == m2, lane, LANE), axis=1, keepdims=True)
    t = jnp.exp(m2 - m1)
    g1 = 1.0 / (1.0 + t)
    g2 = t / (1.0 + t)
    oh1 = lane == e1
    oh2 = lane == e2
    member = (oh1 | oh2).astype(F32)
    ri = lax.broadcasted_iota(jnp.int32, (tm, tm), 0)
    ci = lax.broadcasted_iota(jnp.int32, (tm, tm), 1)
    before = _dot((ri > ci).astype(BF16), member.astype(BF16)) + carry_ref[...]
    r1 = jnp.sum(jnp.where(oh1, before, 0.0), axis=1, keepdims=True)
    r2 = jnp.sum(jnp.where(oh2, before, 0.0), axis=1, keepdims=True)
    carry_ref[...] = carry_ref[...] + jnp.sum(member, axis=0, keepdims=True)
    cnt_ref[...] = carry_ref[...]
    meta = jnp.where(oh1, g1, 0.0) + jnp.where(oh2, g2, 0.0)
    meta = jnp.where(lane == M_E1, e1.astype(F32), meta)
    meta = jnp.where(lane == M_E2, e2.astype(F32), meta)
    meta = jnp.where(lane == M_R1, r1, meta)
    meta = jnp.where(lane == M_R2, r2, meta)
    meta = jnp.where(lane == M_G1, g1, meta)
    meta = jnp.where(lane == M_G2, g2, meta)
    meta_ref[...] = meta


def _proj_router(mix, x, w_out, g, w_router_pad, tm, hn_dtype):
    m, d = x.shape
    row = lambda width: pl.BlockSpec((tm, width), lambda i: (i, 0))
    full = lambda a: pl.BlockSpec(a.shape, lambda i: (0,) * a.ndim)
    return pl.pallas_call(
        _proj_router_kernel,
        grid=(m // tm,),
        in_specs=[row(d), row(d), full(w_out), full(g), full(w_router_pad)],
        out_specs=[row(d), row(d), row(LANE), pl.BlockSpec((1, LANE), lambda i: (0, 0))],
        out_shape=[jax.ShapeDtypeStruct((m, d), F32), jax.ShapeDtypeStruct((m, d), hn_dtype),
                   jax.ShapeDtypeStruct((m, LANE), F32), jax.ShapeDtypeStruct((1, LANE), F32)],
        scratch_shapes=[pltpu.VMEM((1, LANE), F32)],
        compiler_params=_cparams("arbitrary"),
        name="proj_router",
    )(mix, x, w_out, g, w_router_pad)


def _moe_dense_kernel(hn_ref, x1_ref, meta_ref, wg_ref, wu_ref, wd_ref, gf_ref, o_ref):
    e = pl.program_id(1)
    f = pl.program_id(2)

    @pl.when((e == 0) & (f == 0))
    def _():
        o_ref[...] = x1_ref[...]

    h = hn_ref[...]
    act = (_silu(_dot(h, wg_ref[...].astype(BF16))) * _dot(h, wu_ref[...].astype(BF16))).astype(BF16)
    lane = lax.broadcasted_iota(jnp.int32, meta_ref.shape, 1)
    gate = jnp.sum(jnp.where(lane == e, meta_ref[...], 0.0), axis=1, keepdims=True)
    o_ref[...] += gate * _dot(act, wd_ref[...].astype(BF16))

    @pl.when((e == pl.num_programs(1) - 1) & (f == pl.num_programs(2) - 1))
    def _():
        o_ref[...] = _rms(o_ref[...], gf_ref[...])


def _moe_dense(hn, x1, meta, w_gate, w_up, w_down, g_final, tm):
    m, d = x1.shape
    n_e, _, ff = w_gate.shape
    tf = MOE_TF
    row = lambda width: pl.BlockSpec((tm, width), lambda i, e, f: (i, 0))
    return pl.pallas_call(
        _moe_dense_kernel,
        grid=(m // tm, n_e, ff // tf),
        in_specs=[row(d), row(d), row(LANE),
                  pl.BlockSpec((None, d, tf), lambda i, e, f: (e, 0, f)),
                  pl.BlockSpec((None, d, tf), lambda i, e, f: (e, 0, f)),
                  pl.BlockSpec((None, tf, d), lambda i, e, f: (e, f, 0)),
                  pl.BlockSpec((1, d), lambda i, e, f: (0, 0))],
        out_specs=row(d),
        out_shape=jax.ShapeDtypeStruct((m, d), F32),
        compiler_params=_cparams("parallel", "arbitrary", "arbitrary"),
        name="moe_dense",
    )(hn, x1, meta, w_gate, w_up, w_down, g_final)


def _moe_dispatch_kernel(dest_ref, hn_ref, xs_in_ref, xs_ref, sem):
    del xs_in_ref
    tm = hn_ref.shape[0]
    base = pl.program_id(0) * (TOP_K * tm)

    def row_copy(t, k):
        return pltpu.make_async_copy(hn_ref.at[pl.ds(t, 1)], xs_ref.at[pl.ds(dest_ref[base + k * tm + t], 1)], sem)

    def issue(t, carry):
        for k in range(TOP_K):
            row_copy(t, k).start()
        return carry

    def drain(t, carry):
        for k in range(TOP_K):
            row_copy(t, k).wait()
        return carry

    lax.fori_loop(0, tm, issue, 0, unroll=DMA_UNROLL)
    lax.fori_loop(0, tm, drain, 0, unroll=DMA_UNROLL)


def _moe_dispatch(dest, hn, n_rows, tm):
    m, d = hn.shape
    return pl.pallas_call(
        _moe_dispatch_kernel,
        grid_spec=pltpu.PrefetchScalarGridSpec(
            num_scalar_prefetch=1,
            grid=(m // tm,),
            in_specs=[pl.BlockSpec((tm, d), lambda i, dest: (i, 0)), pl.BlockSpec(memory_space=pl.ANY)],
            out_specs=pl.BlockSpec(memory_space=pl.ANY),
            scratch_shapes=[pltpu.SemaphoreType.DMA(())]),
        out_shape=jax.ShapeDtypeStruct((n_rows, d), hn.dtype),
        input_output_aliases={2: 0},
        compiler_params=_cparams("arbitrary"),
        name="moe_dispatch",
    )(dest, hn, jnp.zeros((n_rows, d), hn.dtype))


def _moe_grouped_kernel(te_ref, tr_ref, xs_ref, wg_ref, wu_ref, wd_ref, o_ref, xb_ref):
    del te_ref
    f = pl.program_id(1)
    used = tr_ref[pl.program_id(0)]
    half = xs_ref.shape[0] // 2

    @pl.when(f == 0)
    def _():
        o_ref[...] = jnp.zeros_like(o_ref)
        xb_ref[...] = xs_ref[...].astype(BF16)

    def swiglu_rows(rows):
        h = xb_ref[rows, :]
        act = (_silu(_dot(h, wg_ref[...].astype(BF16))) * _dot(h, wu_ref[...].astype(BF16))).astype(BF16)
        o_ref[rows, :] += _dot(act, wd_ref[...].astype(BF16))

    @pl.when(used > half)
    def _():
        swiglu_rows(slice(None))

    @pl.when((used > 0) & (used <= half))
    def _():
        swiglu_rows(slice(0, half))


def _moe_grouped(tile_expert, tile_rows, xs, w_gate, w_up, w_down, tg):
    rows, d = xs.shape
    tf = MOE_TF
    nf = w_gate.shape[2] // tf
    fidx = lambda r, f, tr: jnp.where(tr[r] > 0, f, nf - 1)
    return pl.pallas_call(
        _moe_grouped_kernel,
        grid_spec=pltpu.PrefetchScalarGridSpec(
            num_scalar_prefetch=2,
            grid=(rows // tg, nf),
            in_specs=[pl.BlockSpec((tg, d), lambda r, f, te, tr: (r, 0)),
                      pl.BlockSpec((None, d, tf), lambda r, f, te, tr: (te[r], 0, fidx(r, f, tr))),
                      pl.BlockSpec((None, d, tf), lambda r, f, te, tr: (te[r], 0, fidx(r, f, tr))),
                      pl.BlockSpec((None, tf, d), lambda r, f, te, tr: (te[r], fidx(r, f, tr), 0))],
            out_specs=pl.BlockSpec((tg, d), lambda r, f, te, tr: (r, 0)),
            scratch_shapes=[pltpu.VMEM((tg, d), BF16)]),
        out_shape=jax.ShapeDtypeStruct((rows, d), F32),
        compiler_params=_cparams("parallel", "arbitrary"),
        name="moe_grouped",
    )(tile_expert, tile_rows, xs, w_gate, w_up, w_down)


def _moe_combine_kernel(dest_ref, ys_ref, x1_ref, meta_ref, gf_ref, o_ref, buf_ref, sem):
    i = pl.program_id(0)
    tc = x1_ref.shape[0]
    rows = TOP_K * tc

    def row_copy(step, slot, j):
        return pltpu.make_async_copy(ys_ref.at[pl.ds(dest_ref[step * rows + j], 1)],
                                     buf_ref.at[slot, pl.ds(j, 1)], sem.at[slot])

    def issue(step, slot):
        def body(j, carry):
            row_copy(step, slot, j).start()
            return carry
        lax.fori_loop(0, rows, body, 0, unroll=DMA_UNROLL)

    def drain(step, slot):
        def body(j, carry):
            row_copy(step, slot, j).wait()
            return carry
        lax.fori_loop(0, rows, body, 0, unroll=DMA_UNROLL)

    slot = i % 2

    @pl.when(i == 0)
    def _():
        issue(0, 0)

    @pl.when(i + 1 < pl.num_programs(0))
    def _():
        issue(i + 1, 1 - slot)

    drain(i, slot)
    g1 = meta_ref[:, M_G1:M_G1 + 1]
    g2 = meta_ref[:, M_G2:M_G2 + 1]
    y = x1_ref[...] + g1 * buf_ref[slot, 0:tc] + g2 * buf_ref[slot, tc:rows]
    o_ref[...] = _rms(y, gf_ref[...])


def _moe_combine(dest, ys, x1, meta, g_final, tc):
    m, d = x1.shape
    row = lambda width: pl.BlockSpec((tc, width), lambda i, dest: (i, 0))
    return pl.pallas_call(
        _moe_combine_kernel,
        grid_spec=pltpu.PrefetchScalarGridSpec(
            num_scalar_prefetch=1,
            grid=(m // tc,),
            in_specs=[pl.BlockSpec(memory_space=pl.ANY), row(d), row(LANE),
                      pl.BlockSpec((1, d), lambda i, dest: (0, 0))],
            out_specs=row(d),
            scratch_shapes=[pltpu.VMEM((2, TOP_K * tc, d), F32), pltpu.SemaphoreType.DMA((2,))]),
        out_shape=jax.ShapeDtypeStruct((m, d), F32),
        compiler_params=_cparams("arbitrary"),
        name="moe_combine",
    )(dest, ys, x1, meta, g_final)


def _moe_routes(meta, counts, tile, tg):
    m = meta.shape[0]
    n_rows = TOP_K * m + N_EXPERTS * tg
    expert = meta[:, M_E1:M_E2 + 1].astype(jnp.int32)
    rank = meta[:, M_R1:M_R2 + 1].astype(jnp.int32)
    count = counts[0, :N_EXPERTS].astype(jnp.int32)
    padded = (count + tg - 1) // tg * tg
    ends = jnp.cumsum(padded)
    dest = (ends - padded)[expert] + rank
    dest = jnp.transpose(dest.reshape(m // tile, tile, TOP_K), (0, 2, 1)).reshape(-1)
    tile_start = jnp.arange(n_rows // tg, dtype=jnp.int32) * tg
    probe = jnp.minimum(tile_start, ends[-1] - 1)
    tile_expert = jnp.sum(ends[None, :] <= probe[:, None], axis=1).astype(jnp.int32)
    tile_rows = jnp.clip((ends - padded + count)[tile_expert] - tile_start, 0, tg).astype(jnp.int32)
    return dest, tile_expert, tile_rows, n_rows


def _odd_params(i, od_norm_mix, od_w_in, od_gla_w_gate2, od_gla_b_gate2, od_gla_norm, od_delta_conv,
                od_delta_a_log, od_delta_dt_bias, od_delta_norm, od_w_out, od_norm_ffn, od_router,
                od_w_gate, od_w_up, od_w_down):
    w = od_w_in[i]
    sizes = (KEY_C, KEY_C, H_C * DV_C, GATE_RANK, H_C * DV_C, C_CONV, H_D, H_D, H_D * DV_D)
    splits = tuple(sum(sizes[:j + 1]) for j in range(len(sizes) - 1))
    cq, ck, cv, c_lr, c_r, d_qkv, d_a, d_b, d_g = jnp.split(w, splits, axis=1)
    pad = jnp.zeros((w.shape[0], IN_ODD_PAD - w.shape[1]), w.dtype)
    w_in = jnp.concatenate([cq, ck, cv, c_r, d_qkv, d_g, c_lr, d_a, d_b, pad], axis=1).astype(BF16)
    slab = lambda v: jnp.zeros((1, LANE), F32).at[0, SM_DA:SM_DA + H_D].set(v)
    return dict(norm_mix=od_norm_mix[i][None], w_in=w_in,
                w_g2_pad=jnp.zeros((LANE, KEY_C), F32).at[:GATE_RANK].set(od_gla_w_gate2[i]),
                b_g2=od_gla_b_gate2[i][None], gla_norm=od_gla_norm[i][None], conv_w=od_delta_conv[i],
                alog_slab=slab(od_delta_a_log[i]), dtb_slab=slab(od_delta_dt_bias[i]),
                a_log=od_delta_a_log[i], dt_bias=od_delta_dt_bias[i],
                delta_norm=od_delta_norm[i][None], w_out=od_w_out[i].astype(BF16),
                norm_ffn=od_norm_ffn[i][None],
                router=jnp.zeros((D_MODEL, LANE), F32).at[:, :N_EXPERTS].set(od_router[i]),
                w_gate=od_w_gate[i], w_up=od_w_up[i], w_down=od_w_down[i])


MOE_GROUP_TILE = 1024
MOE_ROW_TILE = 512


def _moe(mix, x, p, g_final, tm):
    m = x.shape[0]
    if m < N_EXPERTS * MOE_GROUP_TILE:
        x1, hn, meta, _ = _proj_router(mix, x, p["w_out"], p["norm_ffn"], p["router"], tm, BF16)
        return _moe_dense(hn, x1, meta, p["w_gate"], p["w_up"], p["w_down"], g_final, tm)
    x1, hn, meta, counts = _proj_router(mix, x, p["w_out"], p["norm_ffn"], p["router"], tm, F32)
    dest, tile_expert, tile_rows, n_rows = _moe_routes(meta, counts, MOE_ROW_TILE, MOE_GROUP_TILE)
    xs = _moe_dispatch(dest, hn, n_rows, MOE_ROW_TILE)
    ys = _moe_grouped(tile_expert, tile_rows, xs, p["w_gate"], p["w_up"], p["w_down"], MOE_GROUP_TILE)
    return _moe_combine(dest, ys, x1, meta, g_final, MOE_ROW_TILE)


def _odd_layer_prompt(x, p, g_final, batch, seq):
    z = _norm_matmul(x, p["norm_mix"], p["w_in"], TOKEN_TILE)
    mix, s_gla, s_delta, tail = _odd_mixer_prompt(z, p, batch, seq, MIXER_TILE)
    y = _moe(mix, x, p, g_final, TOKEN_TILE)
    return y, s_gla, s_delta, tail[:, 8 - (CONV_W - 1):]


def _even_epilogue_sample_kernel(z_ref, c_ref, su_ref, sd_ref, lng_ref, lnb_ref, w0_ref, b0_ref,
                                 q_ref, kv_ref, gv_ref, bout_ref):
    c, su, sd = c_ref[...], su_ref[...], sd_ref[...]
    q_ref[...] = _rope(z_ref[:, 0:QKV_A], c, su, sd) * (D_A ** -0.5)
    kv_ref[:, 0:QKV_A] = _rope(z_ref[:, QKV_A:2 * QKV_A], c, su, sd)
    kv_ref[:, QKV_A:2 * QKV_A] = z_ref[:, 2 * QKV_A:3 * QKV_A]
    u = _gelu(z_ref[:, 3 * QKV_A:3 * QKV_A + D_B])
    gv = _layernorm(_gelu(z_ref[:, 3 * QKV_A + D_B:3 * QKV_A + 2 * D_B]), lng_ref[...], lnb_ref[...])
    gv_ref[...] = gv
    bout_ref[...] = (u * (gv * w0_ref[...] + b0_ref[...])).astype(BF16)


def _even_epilogue_sample(z, tables, ln_g, ln_b, w_sp, b_sp):
    m = z.shape[0]
    w0 = jnp.repeat(w_sp[:, 0, 0], D_BG)[None]
    b0 = jnp.repeat(b_sp[:, 0], D_BG)[None]
    return pl.pallas_call(
        _even_epilogue_sample_kernel,
        out_shape=[jax.ShapeDtypeStruct((m, QKV_A), F32), jax.ShapeDtypeStruct((m, 2 * QKV_A), F32),
                   jax.ShapeDtypeStruct((m, D_B), F32), jax.ShapeDtypeStruct((m, D_B), BF16)],
        compiler_params=pltpu.CompilerParams(vmem_limit_bytes=VMEM_LIMIT),
        name="even_epilogue_sample",
    )(z, *tables, ln_g, ln_b, w0, b0)


def _moba_sample_kernel(pt_ref, qt_ref, knt_ref, vnt_ref, *refs):
    del pt_ref
    n_pages = len(refs) - 2
    page_refs, o_ref, s_ref = refs[:n_pages], refs[n_pages], refs[n_pages + 1]
    pages_per_block = MOBA_BLOCK // PAGE_SIZE
    nb = n_pages // pages_per_block
    qt = qt_ref[0]
    for h in range(H_A):
        qcol = jnp.broadcast_to(qt[:, h:h + 1], (D_A, PAGE_SIZE))
        for j in range(n_pages):
            s_ref[h, j:j + 1, :] = jnp.sum(page_refs[j][0, 0, h] * qcol, axis=0, keepdims=True)
    lane = lax.broadcasted_iota(jnp.int32, (n_pages, LANE), 1)
    page_sums = jnp.zeros((n_pages, LANE), F32)
    for h in range(H_A):
        page_sums = jnp.where(lane == h, jnp.sum(s_ref[h], axis=1, keepdims=True), page_sums)
    pair = (lax.broadcasted_iota(jnp.int32, (nb, n_pages), 1) // pages_per_block
            == lax.broadcasted_iota(jnp.int32, (nb, n_pages), 0)).astype(F32)
    pair_t = (lax.broadcasted_iota(jnp.int32, (n_pages, nb), 0) // pages_per_block
              == lax.broadcasted_iota(jnp.int32, (n_pages, nb), 1)).astype(F32)
    gate = jnp.dot(pair, page_sums, precision=HI, preferred_element_type=F32)
    sel = _moba_select(gate, nb)
    sel_pages = jnp.dot(pair_t, sel, precision=HI, preferred_element_type=F32)
    own = jnp.sum(qt * knt_ref[0], axis=0, keepdims=True)
    vnt = vnt_ref[0]
    out_lane = lax.broadcasted_iota(jnp.int32, (D_A, LANE), 1)
    out = jnp.zeros((D_A, LANE), F32)
    for h in range(H_A):
        sm = jnp.where(sel_pages[:, h:h + 1] > 0.0, s_ref[h], NEG)
        s_own = own[:, h:h + 1]
        mx = jnp.maximum(jnp.max(jnp.max(sm, axis=1, keepdims=True), axis=0, keepdims=True), s_own)
        p = jnp.exp(sm - mx)
        p_own = jnp.exp(s_own - mx)
        denom = jnp.sum(jnp.sum(p, axis=1, keepdims=True), axis=0, keepdims=True) + p_own
        acc = jnp.zeros((D_A, PAGE_SIZE), F32)
        for j in range(n_pages):
            acc = acc + page_refs[j][0, 1, h] * p[j:j + 1, :]
        o = (jnp.sum(acc, axis=1, keepdims=True) + p_own * vnt[:, h:h + 1]) / denom
        out = jnp.where(out_lane == h, o, out)
    o_ref[0] = out


def _moba_sample(q, kv_new, cache, page_table):
    bs, n_pages = page_table.shape
    assert (n_pages * PAGE_SIZE) % MOBA_BLOCK == 0
    cache_t = jnp.transpose(cache, (0, 2, 3, 4, 1))
    page_spec = lambda j: pl.BlockSpec((1, 2, H_A, D_A, PAGE_SIZE),
                                       lambda b, pt: (pt[b * n_pages + j], 0, 0, 0, 0))
    col_spec = pl.BlockSpec((1, D_A, H_A), lambda b, pt: (b, 0, 0))
    heads_t = lambda a: jnp.transpose(a.reshape(bs, H_A, D_A), (0, 2, 1))
    out = pl.pallas_call(
        _moba_sample_kernel,
        grid_spec=pltpu.PrefetchScalarGridSpec(
            num_scalar_prefetch=1,
            grid=(bs,),
            in_specs=[col_spec, col_spec, col_spec] + [page_spec(j) for j in range(n_pages)],
            out_specs=pl.BlockSpec((1, D_A, LANE), lambda b, pt: (b, 0, 0)),
            scratch_shapes=[pltpu.VMEM((H_A, n_pages, PAGE_SIZE), F32)]),
        out_shape=jax.ShapeDtypeStruct((bs, D_A, LANE), F32),
        compiler_params=_cparams("parallel"),
        name="moba_sample",
    )(page_table.reshape(-1), heads_t(q), heads_t(kv_new[:, 0:QKV_A]), heads_t(kv_new[:, QKV_A:]),
      *([cache_t] * n_pages))
    return jnp.transpose(out[:, :, 0:H_A], (0, 2, 1)).reshape(bs, QKV_A).astype(BF16)


def _odd_mixer_sample_kernel(z_ref, buf_ref, sg_ref, sd_ref, wg2_ref, bg2_ref, gnorm_ref, cw_ref, alog_ref, dtb_ref,
                             dnorm_ref, o_ref, sgo_ref, sdo_ref):
    bt = z_ref.shape[0]
    stride = H_C * DK_C
    small = z_ref[:, O_SM:O_SM + LANE]
    pre = jnp.dot(small, wg2_ref[...], precision=HI, preferred_element_type=F32) + bg2_ref[...]
    a_all = jnp.exp(_log_sigmoid(pre) / GATE_NORM)
    g_all = -jnp.exp(alog_ref[...]) * _softplus(small + dtb_ref[...])
    beta_all = jax.nn.sigmoid(small)
    x = z_ref[:, O_DQKV:O_DQKV + C_CONV]
    y = x * cw_ref[CONV_W - 1:CONV_W, :]
    for j in range(CONV_W - 1):
        y = y + buf_ref[:, j, :] * cw_ref[j:j + 1, :]
    y = _silu(y)
    for h in range(H_C):
        ks = slice(h * DK_C, (h + 1) * DK_C)
        a = a_all[:, ks]
        q = z_ref[:, O_CQ + h * DK_C:O_CQ + (h + 1) * DK_C] * (DK_C ** -0.5)
        k = z_ref[:, O_CK + h * DK_C:O_CK + (h + 1) * DK_C]
        v = z_ref[:, O_CV + h * DV_C:O_CV + (h + 1) * DV_C]
        qa = q * a
        acc = jnp.sum(q * k, axis=1, keepdims=True) * v
        for kk in range(DK_C):
            rows = pl.ds(h * DK_C + kk, bt, stride=stride)
            srow = sg_ref[rows, :]
            acc = acc + qa[:, kk:kk + 1] * srow
            sgo_ref[rows, :] = a[:, kk:kk + 1] * srow + k[:, kk:kk + 1] * v
        gate = _silu(z_ref[:, O_CR + h * DV_C:O_CR + (h + 1) * DV_C])
        o_ref[:, h * DV_C:(h + 1) * DV_C] = (_rms(acc, gnorm_ref[...]) * gate).astype(BF16)
        yq = y[:, h * DK_D:(h + 1) * DK_D]
        yk = y[:, KEY_C + h * DK_D:KEY_C + (h + 1) * DK_D]
        dv = y[:, 2 * KEY_C + h * DV_D:2 * KEY_C + (h + 1) * DV_D]
        dq = yq * lax.rsqrt(jnp.sum(yq * yq, axis=1, keepdims=True) + EPS) * (DK_D ** -0.5)
        dk = yk * lax.rsqrt(jnp.sum(yk * yk, axis=1, keepdims=True) + EPS)
        beta = beta_all[:, SM_DB + h:SM_DB + h + 1]
        eg = jnp.exp(g_all[:, SM_DA + h:SM_DA + h + 1])
        w = dk * (beta * eg)
        qd = dq * eg
        ws = jnp.zeros((bt, DV_D), F32)
        qs = jnp.zeros((bt, DV_D), F32)
        for kk in range(DK_D):
            srow = sd_ref[pl.ds(h * DK_D + kk, bt, stride=stride), :]
            ws = ws + w[:, kk:kk + 1] * srow
            qs = qs + qd[:, kk:kk + 1] * srow
        v_new = dv * beta - ws
        o = qs + jnp.sum(dq * dk, axis=1, keepdims=True) * v_new
        for kk in range(DK_D):
            rows = pl.ds(h * DK_D + kk, bt, stride=stride)
            sdo_ref[rows, :] = sd_ref[rows, :] * eg + dk[:, kk:kk + 1] * v_new
        gate = _silu(z_ref[:, O_DG + h * DV_D:O_DG + (h + 1) * DV_D])
        col = H_C * DV_C + h * DV_D
        o_ref[:, col:col + DV_D] = (_rms(o, dnorm_ref[...]) * gate).astype(BF16)


def _odd_mixer_sample(z, conv_buf, s_gla, s_delta, p, bt):
    bs = z.shape[0]
    rows = H_C * DK_C
    full = lambda a: pl.BlockSpec(a.shape, lambda i: (0,) * a.ndim)
    consts = [p["w_g2_pad"], p["b_g2"], p["gla_norm"], p["conv_w"], p["alog_slab"], p["dtb_slab"], p["delta_norm"]]
    st_spec = pl.BlockSpec((bt * rows, DV_C), lambda i: (i, 0))
    mix, sg, sd = pl.pallas_call(
        _odd_mixer_sample_kernel,
        grid=(bs // bt,),
        in_specs=[pl.BlockSpec((bt, IN_ODD_PAD), lambda i: (i, 0)),
                  pl.BlockSpec((bt, CONV_W - 1, C_CONV), lambda i: (i, 0, 0)),
                  st_spec, st_spec] + [full(a) for a in consts],
        out_specs=[pl.BlockSpec((bt, D_MODEL), lambda i: (i, 0)), st_spec, st_spec],
        out_shape=[jax.ShapeDtypeStruct((bs, D_MODEL), BF16),
                   jax.ShapeDtypeStruct((bs * rows, DV_C), F32),
                   jax.ShapeDtypeStruct((bs * rows, DV_D), F32)],
        compiler_params=_cparams("parallel"),
        name="odd_mixer_sample",
    )(z, conv_buf, s_gla.reshape(bs * rows, DV_C), s_delta.reshape(bs * rows, DV_D), *consts)
    return mix, sg.reshape(s_gla.shape), sd.reshape(s_delta.shape)


def _sample_step(x, cache, page_table, s_gla, s_delta, conv_buf, ev, od, g_final):
    bs = x.shape[0]
    past = page_table.shape[1] * PAGE_SIZE
    z = _norm_matmul(x, ev["norm_mix"], ev["w_in"], bs)
    tables = _rope_tables(jnp.full((1,), past, jnp.int32))
    q, kv, gv, b_out = _even_epilogue_sample(z, tables, ev["ln_g"], ev["ln_b"], ev["w_sp"], ev["b_sp"])
    a_out = _moba_sample(q, kv, cache, page_table)
    x = _proj_ffn([a_out, b_out], x, ev["w_out"], ev["norm_ffn"], ev["w_gate"], ev["w_up"], ev["w_down"], bs)
    z = _norm_matmul(x, od["norm_mix"], od["w_in"], bs)
    mix, sg, sd = _odd_mixer_sample(z, conv_buf, s_gla, s_delta, od, SAMPLE_MIXER_TILE)
    conv_new = jnp.concatenate([conv_buf[:, 1:], z[:, None, O_DQKV:O_DQKV + C_CONV]], axis=1)
    y = _moe(mix, x, od, g_final, bs)
    return y, kv, gv, sg, sd, conv_new


def kernel(x_prompt, x_sample, cache_kv, state_gla, state_delta, state_conv, page_table, ev_norm_mix, ev_w_in, ev_gmlp_ln_g, ev_gmlp_ln_b, ev_w_spatial, ev_b_spatial, ev_w_out, ev_norm_ffn, ev_w_gate, ev_w_up, ev_w_down, od_norm_mix, od_w_in, od_gla_w_gate2, od_gla_b_gate2, od_gla_norm, od_delta_conv, od_delta_a_log, od_delta_dt_bias, od_delta_norm, od_w_out, od_norm_ffn, od_router, od_w_gate, od_w_up, od_w_down, norm_final):
    bp, tp, d = x_prompt.shape
    ev = _even_params(0, ev_norm_mix, ev_w_in, ev_gmlp_ln_g, ev_gmlp_ln_b, ev_w_spatial, ev_b_spatial, ev_w_out,
                      ev_norm_ffn, ev_w_gate, ev_w_up, ev_w_down)
    od = _odd_params(0, od_norm_mix, od_w_in, od_gla_w_gate2, od_gla_b_gate2, od_gla_norm, od_delta_conv,
                     od_delta_a_log, od_delta_dt_bias, od_delta_norm, od_w_out, od_norm_ffn, od_router,
                     od_w_gate, od_w_up, od_w_down)
    bs, ts, _ = x_sample.shape
    assert ts == 1 and cache_kv.shape[0] == 1 and state_gla.shape[0] == 1
    xp, kv_p, gv_p = _even_layer_prompt(x_prompt.reshape(bp * tp, d), ev, bp, tp)
    yp, gla_p, dl_p, cv_p = _odd_layer_prompt(xp, od, norm_final[None], bp, tp)
    ys, kv_s, gv_s, gla_s, dl_s, cv_s = _sample_step(
        x_sample.reshape(bs, d), cache_kv[0], page_table, state_gla[0], state_delta[0], state_conv[0],
        ev, od, norm_final[None])
    return (yp.reshape(bp, tp, d), ys.reshape(bs, ts, d),
            kv_p[None], kv_s.reshape(1, bs, ts, 2, H_A, D_A),
            gv_p[None], gv_s.reshape(1, bs, ts, D_B),
            gla_p[None], gla_s[None], dl_p[None], dl_s[None], cv_p[None], cv_s[None])
```

```python
import functools
import math

import jax
import jax.numpy as jnp
from jax import lax
from jax.experimental import pallas as pl
from jax.experimental.pallas import tpu as pltpu

F32 = jnp.float32
BF16 = jnp.bfloat16
HI = lax.Precision.HIGHEST
EPS = 1e-6
NEG = -1e30

D_MODEL = 1024
PAGE_SIZE = 128
H_A, D_A = 8, 64
ROT_DIM = D_A // 4
ROPE_THETA = 500000.0
MOBA_BLOCK = 256
MOBA_TOPK = 3
G_B, D_BG = 8, 64
D_B = G_B * D_BG
CHUNK_B = 128
H_C, DK_C, DV_C = 4, 64, 128
GATE_RANK = 16
GATE_NORM = 16.0
H_D, DK_D, DV_D = 4, 64, 128
CONV_W = 4
C_CONV = 2 * H_D * DK_D + H_D * DV_D
CHUNK_LIN = 64
N_EXPERTS = 8
TOP_K = 2
QKV_A = H_A * D_A
IN_EVEN = 3 * QKV_A + 2 * D_B
TOKEN_TILE = 512
EPILOGUE_TILE = 256
MIXER_TILE = 256
SAMPLE_MIXER_TILE = 32
FFN_TF = 1408
MOE_TF = 896
MOBA_LOOP_BLOCKS = 4
DMA_UNROLL = 8
LANE = 128
VMEM_LIMIT = 56 * 1024 * 1024


def _cparams(*sem):
    return pltpu.CompilerParams(dimension_semantics=sem, vmem_limit_bytes=VMEM_LIMIT)


def _rms(x, g):
    return x * lax.rsqrt(jnp.mean(x * x, axis=-1, keepdims=True) + EPS) * g


def _gelu(x):
    return 0.5 * x * (1.0 + lax.erf(x * (2.0 ** -0.5)))


def _silu(x):
    return x * jax.nn.sigmoid(x)


def _softplus(x):
    return jnp.maximum(x, 0.0) + jnp.log1p(jnp.exp(-jnp.abs(x)))


def _dot(a, b):
    return jnp.dot(a, b, preferred_element_type=F32)


def _dot_nt(a, b):
    return lax.dot_general(a, b, (((1,), (1,)), ((), ())), preferred_element_type=F32)


def _dot_tn(a, b):
    return lax.dot_general(a, b, (((0,), (0,)), ((), ())), preferred_element_type=F32)


def _norm_matmul_kernel(x_ref, g_ref, w_ref, o_ref):
    h = _rms(x_ref[...], g_ref[...]).astype(BF16)
    o_ref[...] = _dot(h, w_ref[...])


def _norm_matmul(x, g, w, tm):
    m, d = x.shape
    n = w.shape[1]
    return pl.pallas_call(
        _norm_matmul_kernel,
        grid=(m // tm,),
        in_specs=[pl.BlockSpec((tm, d), lambda i: (i, 0)),
                  pl.BlockSpec((1, d), lambda i: (0, 0)),
                  pl.BlockSpec((d, n), lambda i: (0, 0))],
        out_specs=pl.BlockSpec((tm, n), lambda i: (i, 0)),
        out_shape=jax.ShapeDtypeStruct((m, n), F32),
        compiler_params=_cparams("parallel"),
        name="norm_matmul",
    )(x, g, w)


def _rope_tables(pos):
    half = ROT_DIM // 2
    inv = ROPE_THETA ** (-jnp.arange(half, dtype=F32) / half)
    ang = pos.astype(F32)[:, None] * inv[None, :]
    cos, sin = jnp.cos(ang), jnp.sin(ang)
    t = pos.shape[0]
    one = jnp.ones((t, D_A - ROT_DIM), F32)
    zero_h = jnp.zeros((t, half), F32)
    zero_r = jnp.zeros((t, D_A - ROT_DIM), F32)
    c = jnp.concatenate([cos, cos, one], axis=1)
    s_up = jnp.concatenate([-sin, zero_h, zero_r], axis=1)
    s_dn = jnp.concatenate([zero_h, sin, zero_r], axis=1)
    rep = LANE // D_A
    return jnp.tile(c, (1, rep)), jnp.tile(s_up, (1, rep)), jnp.tile(s_dn, (1, rep))


def _rope(x, c, s_up, s_dn):
    half = ROT_DIM // 2
    outs = []
    for j in range(x.shape[1] // LANE):
        xs = x[:, j * LANE:(j + 1) * LANE]
        up = pltpu.roll(xs, LANE - half, 1)
        dn = pltpu.roll(xs, half, 1)
        outs.append(xs * c + up * s_up + dn * s_dn)
    return jnp.concatenate(outs, axis=1)


def _layernorm(x, g, b):
    mu = jnp.mean(x, axis=-1, keepdims=True)
    xc = x - mu
    var = jnp.mean(xc * xc, axis=-1, keepdims=True)
    return xc * lax.rsqrt(var + EPS) * g + b


def _even_epilogue_kernel(z_ref, c_ref, su_ref, sd_ref, lng_ref, lnb_ref, wsp_ref, bspt_ref,
                          qt_ref, k_ref, vt_ref, kv_ref, kmean_ref, gv_ref, bout_ref):
    tm = z_ref.shape[0]
    c, su, sd = c_ref[...], su_ref[...], sd_ref[...]
    q = _rope(z_ref[:, 0:QKV_A], c, su, sd) * (D_A ** -0.5)
    k = _rope(z_ref[:, QKV_A:2 * QKV_A], c, su, sd)
    v = z_ref[:, 2 * QKV_A:3 * QKV_A]
    qt_ref[0] = q.T.astype(BF16)
    k_ref[...] = k.astype(BF16)
    vt = v.T
    vt_ref[0] = vt.astype(BF16)
    kv_ref[0, 0:QKV_A, :] = k.T
    kv_ref[0, QKV_A:2 * QKV_A, :] = vt
    for blk in range(tm // MOBA_BLOCK):
        kmean_ref[blk] = jnp.mean(k[blk * MOBA_BLOCK:(blk + 1) * MOBA_BLOCK], axis=0, keepdims=True)
    u = _gelu(z_ref[:, 3 * QKV_A:3 * QKV_A + D_B])
    gv = _layernorm(_gelu(z_ref[:, 3 * QKV_A + D_B:3 * QKV_A + 2 * D_B]), lng_ref[...], lnb_ref[...])
    gv_ref[...] = gv
    gvb = gv.astype(BF16)
    row = lax.broadcasted_iota(jnp.int32, (CHUNK_B, CHUNK_B), 0)
    col = lax.broadcasted_iota(jnp.int32, (CHUNK_B, CHUNK_B), 1)
    group = lax.broadcasted_iota(jnp.int32, (CHUNK_B, D_B), 1) // D_BG
    w = [jnp.where(row >= col, wsp_ref[g], 0.0).astype(BF16) for g in range(G_B)]
    for ch in range(tm // CHUNK_B):
        gvc = gvb[ch * CHUNK_B:(ch + 1) * CHUNK_B]
        mixed = jnp.zeros((CHUNK_B, D_B), F32)
        for g in range(G_B):
            mixed = jnp.where(group == g, _dot(w[g], gvc) + bspt_ref[:, g:g + 1], mixed)
        bout_ref[ch * CHUNK_B:(ch + 1) * CHUNK_B, :] = (u[ch * CHUNK_B:(ch + 1) * CHUNK_B] * mixed).astype(BF16)


def _even_epilogue(z, tables, ln_g, ln_b, w_sp, b_sp, batch, seq, tm):
    n = batch * seq
    nt = seq // tm
    nblk = tm // MOBA_BLOCK
    tab_spec = pl.BlockSpec((tm, LANE), lambda b, i: (i, 0))
    row_spec = lambda width: pl.BlockSpec((tm, width), lambda b, i: (b * nt + i, 0))
    t_spec = pl.BlockSpec((1, QKV_A, tm), lambda b, i: (b, 0, i))
    return pl.pallas_call(
        _even_epilogue_kernel,
        grid=(batch, nt),
        in_specs=[row_spec(IN_EVEN), tab_spec, tab_spec, tab_spec,
                  pl.BlockSpec((1, D_B), lambda b, i: (0, 0)),
                  pl.BlockSpec((1, D_B), lambda b, i: (0, 0)),
                  pl.BlockSpec((G_B, CHUNK_B, CHUNK_B), lambda b, i: (0, 0, 0)),
                  pl.BlockSpec((CHUNK_B, G_B), lambda b, i: (0, 0))],
        out_specs=[t_spec, row_spec(QKV_A), t_spec, pl.BlockSpec((1, 2 * QKV_A, tm), lambda b, i: (b, 0, i)),
                   pl.BlockSpec((nblk, 1, QKV_A), lambda b, i: (b * nt + i, 0, 0)),
                   row_spec(D_B), row_spec(D_B)],
        out_shape=[jax.ShapeDtypeStruct((batch, QKV_A, seq), BF16),
                   jax.ShapeDtypeStruct((n, QKV_A), BF16),
                   jax.ShapeDtypeStruct((batch, QKV_A, seq), BF16),
                   jax.ShapeDtypeStruct((batch, 2 * QKV_A, seq), F32),
                   jax.ShapeDtypeStruct((n // MOBA_BLOCK, 1, QKV_A), F32),
                   jax.ShapeDtypeStruct((n, D_B), F32),
                   jax.ShapeDtypeStruct((n, D_B), BF16)],
        compiler_params=_cparams("parallel", "parallel"),
        name="even_epilogue",
    )(z, *tables, ln_g, ln_b, w_sp, b_sp.T)


def _moba_select(gate, n_own):
    nb = gate.shape[0]
    blk = lax.broadcasted_iota(jnp.int32, gate.shape, 0)
    elig = blk < n_own
    gm = jnp.where(elig, gate, NEG)
    rank = jnp.zeros(gate.shape, F32)
    for m in range(nb):
        gm_m = gm[m:m + 1, :]
        ahead = (gm_m > gm) | ((gm_m == gm) & (m < blk))
        rank = rank + ahead.astype(F32)
    return (elig & (rank < MOBA_TOPK)).astype(F32)


def _moba_prompt_kernel(qt_ref, k_ref, vt_ref, kmean_ref, o_ref, bias_ref):
    i = pl.program_id(2)
    tq = MOBA_BLOCK
    n_head = LANE // D_A
    n_split = tq // LANE
    own = pl.multiple_of(i * tq, tq)
    key_i = lax.broadcasted_iota(jnp.int32, (tq, LANE), 0)
    qry_i = lax.broadcasted_iota(jnp.int32, (tq, LANE), 1)
    head_lanes = [slice(hh * D_A, (hh + 1) * D_A) for hh in range(n_head)]
    streams = [(hh, qh) for hh in range(n_head) for qh in range(n_split)]
    qts = [qt_ref[0, head_lanes[hh], :] for hh in range(n_head)]
    queries = [qts[hh][:, qh * LANE:(qh + 1) * LANE] for hh, qh in streams]
    k_own = [k_ref[pl.ds(own, tq), head_lanes[hh]] for hh in range(n_head)]
    own_scores = [_dot(k_own[hh], queries[idx]) for idx, (hh, qh) in enumerate(streams)]
    gates = [jnp.dot(kmean_ref[0, :, head_lanes[hh]], qts[hh].astype(F32), precision=HI, preferred_element_type=F32)
             for hh in range(n_head)]
    for hh in range(n_head):
        bias = jnp.where(_moba_select(gates[hh], i) > 0.0, 0.0, NEG)
        for qh in range(n_split):
            bias_ref[hh, qh] = bias[:, qh * LANE:(qh + 1) * LANE]
    own_probs, own_stats = [], []
    for idx, (hh, qh) in enumerate(streams):
        s = jnp.where(key_i <= qry_i + qh * LANE, own_scores[idx], NEG)
        m = jnp.max(s, axis=0, keepdims=True)
        p = jnp.exp(s - m)
        own_probs.append(p.astype(BF16))
        own_stats.append((m, jnp.sum(p, axis=0, keepdims=True)))
    vt_own = [vt_ref[0, head_lanes[hh], pl.ds(own, tq)] for hh in range(n_head)]
    state = []
    for idx, (hh, qh) in enumerate(streams):
        state += [own_stats[idx][0], own_stats[idx][1], _dot(vt_own[hh], own_probs[idx])]

    def body(j0, carry, nk):
        tk = nk * tq
        start = pl.multiple_of(j0 * tq, tq)
        k_j = [k_ref[pl.ds(start, tk), head_lanes[hh]] for hh in range(n_head)]
        scores = [_dot(k_j[hh], queries[hh * n_split + qh]) for hh, qh in streams]
        probs, stats = [], []
        for idx, (hh, qh) in enumerate(streams):
            m, l = carry[3 * idx], carry[3 * idx + 1]
            s = [scores[idx][b * tq:(b + 1) * tq] + bias_ref[hh, qh, pl.ds(j0 + b, 1), :] for b in range(nk)]
            m_new = m
            for sb in s:
                m_new = jnp.maximum(m_new, jnp.max(sb, axis=0, keepdims=True))
            alpha = jnp.exp(m - m_new)
            p = [jnp.exp(sb - m_new) for sb in s]
            l_new = l * alpha
            for pb in p:
                l_new = l_new + jnp.sum(pb, axis=0, keepdims=True)
            probs.append(jnp.concatenate([pb.astype(BF16) for pb in p], axis=0))
            stats.append((m_new, l_new, alpha))
        vt_j = [vt_ref[0, head_lanes[hh], pl.ds(start, tk)] for hh in range(n_head)]
        pv = [_dot(vt_j[hh], probs[idx]) for idx, (hh, qh) in enumerate(streams)]
        out = []
        for idx in range(len(streams)):
            m_new, l_new, alpha = stats[idx]
            out += [m_new, l_new, carry[3 * idx + 2] * alpha + pv[idx]]
        return tuple(out)

    state = tuple(state)
    done = 0
    nk = MOBA_LOOP_BLOCKS
    while nk >= 1:
        n_group = (i - done) // nk
        state = lax.fori_loop(0, n_group, lambda t, c, nk=nk, done=done: body(done + t * nk, c, nk), state)
        done = done + n_group * nk
        nk //= 2
    outs = []
    for hh in range(n_head):
        halves = [state[3 * (hh * n_split + qh) + 2] / state[3 * (hh * n_split + qh) + 1] for qh in range(n_split)]
        outs.append(jnp.concatenate(halves, axis=1))
    o_ref[...] = jnp.concatenate(outs, axis=0).T.astype(BF16)


def _moba_prompt(qt, k, vt, kmean, batch, seq):
    nq = seq // MOBA_BLOCK
    hp = QKV_A // LANE
    return pl.pallas_call(
        _moba_prompt_kernel,
        grid=(batch, hp, nq),
        in_specs=[pl.BlockSpec((1, LANE, MOBA_BLOCK), lambda b, h, i: (b, h, i)),
                  pl.BlockSpec((seq, LANE), lambda b, h, i: (b, h)),
                  pl.BlockSpec((1, LANE, seq), lambda b, h, i: (b, h, 0)),
                  pl.BlockSpec((1, nq, LANE), lambda b, h, i: (b, 0, h))],
        out_specs=pl.BlockSpec((MOBA_BLOCK, LANE), lambda b, h, i: (b * nq + i, h)),
        out_shape=jax.ShapeDtypeStruct((batch * seq, QKV_A), BF16),
        scratch_shapes=[pltpu.VMEM((LANE // D_A, MOBA_BLOCK // LANE, nq, LANE), F32)],
        compiler_params=_cparams("parallel", "parallel", "arbitrary"),
        name="moba_prompt",
    )(qt, k, vt, kmean)


def _proj_ffn_kernel(*refs, n_mix):
    mix_refs = refs[:n_mix]
    x_ref, wo_ref, g_ref, wg_ref, wu_ref, wd_ref, o_ref, hn_ref = refs[n_mix:]

    @pl.when(pl.program_id(1) == 0)
    def _():
        x1 = x_ref[...]
        off = 0
        for r in mix_refs:
            width = r.shape[1]
            x1 = x1 + _dot(r[...], wo_ref[off:off + width, :])
            off += width
        o_ref[...] = x1
        hn_ref[...] = _rms(x1, g_ref[...]).astype(BF16)

    h = hn_ref[...]
    act = (_silu(_dot(h, wg_ref[...])) * _dot(h, wu_ref[...])).astype(BF16)
    o_ref[...] += _dot(act, wd_ref[...])


def _col_blocks(w, tf):
    *lead, d, ff = w.shape
    n = len(lead)
    return jnp.transpose(w.reshape(*lead, d, ff // tf, tf), (*range(n), n + 1, n, n + 2))


def _proj_ffn(mixes, x, w_out, g, w_gate, w_up, w_down, tm):
    m, d = x.shape
    nf, _, tf = w_gate.shape
    mix_specs = [pl.BlockSpec((tm, a.shape[1]), lambda i, f: (i, 0)) for a in mixes]
    return pl.pallas_call(
        functools.partial(_proj_ffn_kernel, n_mix=len(mixes)),
        grid=(m // tm, nf),
        in_specs=mix_specs + [
            pl.BlockSpec((tm, d), lambda i, f: (i, 0)),
            pl.BlockSpec(w_out.shape, lambda i, f: (0, 0)),
            pl.BlockSpec((1, d), lambda i, f: (0, 0)),
            pl.BlockSpec((None, d, tf), lambda i, f: (f, 0, 0)),
            pl.BlockSpec((None, d, tf), lambda i, f: (f, 0, 0)),
            pl.BlockSpec((tf, d), lambda i, f: (f, 0))],
        out_specs=pl.BlockSpec((tm, d), lambda i, f: (i, 0)),
        out_shape=jax.ShapeDtypeStruct((m, d), F32),
        scratch_shapes=[pltpu.VMEM((tm, d), BF16)],
        compiler_params=_cparams("parallel", "arbitrary"),
        name="proj_ffn",
    )(*mixes, x, w_out, g, w_gate, w_up, w_down)


def _even_layer_prompt(x, p, batch, seq):
    z = _norm_matmul(x, p["norm_mix"], p["w_in"], TOKEN_TILE)
    tables = _rope_tables(jnp.arange(seq, dtype=jnp.int32))
    qt, k, vt, kv, kmean, gv, b_out = _even_epilogue(
        z, tables, p["ln_g"], p["ln_b"], p["w_sp"], p["b_sp"], batch, seq, EPILOGUE_TILE)
    a_out = _moba_prompt(qt, k, vt, kmean.reshape(batch, seq // MOBA_BLOCK, QKV_A), batch, seq)
    x = _proj_ffn([a_out, b_out], x, p["w_out"], p["norm_ffn"], p["w_gate"], p["w_up"], p["w_down"], TOKEN_TILE)
    n_open = seq - ((seq - 1) // CHUNK_B) * CHUNK_B
    gv_open = gv.reshape(batch, seq, D_B)[:, seq - n_open:]
    kv = jnp.transpose(kv.reshape(batch, 2, H_A, D_A, seq), (0, 4, 1, 2, 3))
    return x, kv, gv_open


def _even_params(i, ev_norm_mix, ev_w_in, ev_gmlp_ln_g, ev_gmlp_ln_b, ev_w_spatial, ev_b_spatial, ev_w_out,
                 ev_norm_ffn, ev_w_gate, ev_w_up, ev_w_down):
    return dict(norm_mix=ev_norm_mix[i][None], w_in=ev_w_in[i].astype(BF16),
                ln_g=ev_gmlp_ln_g[i][None], ln_b=ev_gmlp_ln_b[i][None],
                w_sp=ev_w_spatial[i], b_sp=ev_b_spatial[i], w_out=ev_w_out[i].astype(BF16),
                norm_ffn=ev_norm_ffn[i][None], w_gate=_col_blocks(ev_w_gate[i].astype(BF16), FFN_TF),
                w_up=_col_blocks(ev_w_up[i].astype(BF16), FFN_TF), w_down=ev_w_down[i].astype(BF16))


O_CQ, O_CK, O_CV, O_CR, O_DQKV, O_DG, O_SM = 0, 256, 512, 1024, 1536, 2560, 3072
SM_DA, SM_DB = GATE_RANK, GATE_RANK + H_D
IN_ODD_PAD = O_SM + LANE
KEY_C = H_C * DK_C
PAIR = LANE // DK_C


def _log_sigmoid(x):
    return jnp.minimum(x, 0.0) - jnp.log1p(jnp.exp(-jnp.abs(x)))


def _split_bf16(a, terms):
    parts = []
    for _ in range(terms - 1):
        hi = a.astype(BF16)
        parts.append(hi)
        a = a - hi.astype(F32)
    parts.append(a.astype(BF16))
    return parts


def _dot_select(a, sel_bf16, terms):
    parts = _split_bf16(a, terms)
    out = _dot(parts[0], sel_bf16)
    for part in parts[1:]:
        out = out + _dot(part, sel_bf16)
    return out


def _select_dot(sel_bf16, a, terms):
    parts = _split_bf16(a, terms)
    out = _dot(sel_bf16, parts[0])
    for part in parts[1:]:
        out = out + _dot(sel_bf16, part)
    return out


def _unit_lower_inverses(mats):
    c = mats[0].shape[0]
    assert c == 64
    eye = (lax.broadcasted_iota(jnp.int32, (c, c), 0) == lax.broadcasted_iota(jnp.int32, (c, c), 1)).astype(F32)
    sp = lambda ms: [_split_bf16(m, 2) for m in ms]
    mul = lambda aa, bb: [_dot(ah, bh) + _dot(ah, bl) + _dot(al, bh) for (ah, al), (bh, bl) in zip(aa, bb)]
    x1 = [-a for a in mats]
    x1s = sp(x1)
    x2 = mul(x1s, x1s)
    x2s = sp(x2)
    x4 = mul(x2s, x2s)
    x3 = mul(x1s, x2s)
    x4s = sp(x4)
    x8 = mul(x4s, x4s)
    f01 = [eye + a + b + d for a, b, d in zip(x1, x2, x3)]
    x8s = sp(x8)
    x16 = mul(x8s, x8s)
    x12 = mul(x4s, x8s)
    g23 = [a + b + d for a, b, d in zip(x4, x8, x12)]
    x16s = sp(x16)
    x32 = mul(x16s, x16s)
    f0123 = [f + d for f, d in zip(f01, mul(sp(f01), sp(g23)))]
    x48 = mul(x16s, sp(x32))
    g45 = [a + b + d for a, b, d in zip(x16, x32, x48)]
    return [f + d for f, d in zip(f0123, mul(sp(f0123), sp(g45)))]


def _group_sumsq(y, width):
    n = y.shape[1]
    same = (lax.broadcasted_iota(jnp.int32, (n, n), 0) // width
            == lax.broadcasted_iota(jnp.int32, (n, n), 1) // width).astype(BF16)
    return _dot_select(y * y, same, 2)


def _odd_mixer_prompt_kernel(z_ref, wg2_ref, bg2_ref, gnorm_ref, cw_ref, alog_ref, dtb_ref, dnorm_ref,
                             o_ref, sg_ref, sd_ref, tail_ref,
                             stg_ref, std_ref, prev_ref, la_ref, qkv_ref, dla_ref, beta_ref):
    i = pl.program_id(1)
    ns, tc = z_ref.shape[0], z_ref.shape[1]
    c = CHUNK_LIN

    @pl.when(i == 0)
    def _():
        stg_ref[...] = jnp.zeros_like(stg_ref)
        std_ref[...] = jnp.zeros_like(std_ref)
        prev_ref[...] = jnp.zeros_like(prev_ref)

    row8 = lax.broadcasted_iota(jnp.int32, (8, C_CONV), 0)
    qscale = jnp.where(lax.broadcasted_iota(jnp.int32, (1, 2 * KEY_C), 1) < KEY_C, DK_D ** -0.5, 1.0)
    for sq in range(ns):
        small = z_ref[sq, :, O_SM:O_SM + LANE]
        pre = jnp.dot(small, wg2_ref[...], precision=HI, preferred_element_type=F32) + bg2_ref[...]
        la_ref[sq] = _log_sigmoid(pre) / GATE_NORM
        dla_ref[sq] = -jnp.exp(alog_ref[...]) * _softplus(small + dtb_ref[...])
        beta_ref[sq] = jax.nn.sigmoid(small)

        x = z_ref[sq, :, O_DQKV:O_DQKV + C_CONV]
        x8 = x[0:8]
        p8 = prev_ref[sq]
        y = x * cw_ref[CONV_W - 1:CONV_W, :]
        y8 = x8 * cw_ref[CONV_W - 1:CONV_W, :]
        for s in range(1, CONV_W):
            wrow = cw_ref[CONV_W - 1 - s:CONV_W - s, :]
            y = y + pltpu.roll(x, s, 0) * wrow
            y8 = y8 + jnp.where(row8 < s, pltpu.roll(p8, s, 0), pltpu.roll(x8, s, 0)) * wrow
        prev_ref[sq] = x[tc - 8:tc]
        y = _silu(y)
        y8 = _silu(y8)
        yqk = y[:, 0:2 * KEY_C]
        nrm = lax.rsqrt(_group_sumsq(yqk, DK_D) + EPS)
        qkv_ref[sq, :, 0:2 * KEY_C] = yqk * nrm * qscale
        qkv_ref[sq, :, 2 * KEY_C:] = y[:, 2 * KEY_C:]
        yqk8 = y8[:, 0:2 * KEY_C]
        qkv_ref[sq, 0:8, 0:2 * KEY_C] = yqk8 * lax.rsqrt(_group_sumsq(yqk8, DK_D) + EPS) * qscale
        qkv_ref[sq, 0:8, 2 * KEY_C:] = y8[:, 2 * KEY_C:]

    ri = lax.broadcasted_iota(jnp.int32, (c, c), 0)
    ci = lax.broadcasted_iota(jnp.int32, (c, c), 1)
    tril = ri >= ci
    strict = ri > ci
    lower = tril.astype(BF16)
    upper = (ri <= ci).astype(BF16)
    lane_head = lax.broadcasted_iota(jnp.int32, (c, LANE), 1) // DK_C
    lane_head_row = lax.broadcasted_iota(jnp.int32, (1, LANE), 1) // DK_C

    def chunk(ch, carry):
        rows = pl.ds(pl.multiple_of(ch * c, c), c)
        n_pair = H_C // PAIR
        seqs = range(ns)
        units = [(sq, h) + divmod(h, PAIR) for sq in seqs for h in range(H_C)]
        pairs = [(sq, p) for sq in seqs for p in range(n_pair)]
        unit_at = lambda sq, p, hh: (sq * H_C) + p * PAIR + hh
        masks = [lane_head == hh for hh in range(PAIR)]
        bcum_all = [_select_dot(lower, la_ref[sq, rows, :], 3) for sq in seqs]
        dla_parts = [_split_bf16(dla_ref[sq, rows, :], 3) for sq in seqs]
        g_cols = [_dot(lower, dp[0]) + _dot(lower, dp[1]) + _dot(lower, dp[2]) for dp in dla_parts]
        g_rows = [_dot_tn(dp[0], upper) + _dot_tn(dp[1], upper) + _dot_tn(dp[2], upper) for dp in dla_parts]
        st_d = {sp: std_ref[sp[0], sp[1]] for sp in pairs}
        st_db = {sp: st_d[sp].astype(BF16) for sp in pairs}
        dn = []
        for sq, h, p, hh in units:
            g_col = jnp.broadcast_to(g_cols[sq][:, SM_DA + h:SM_DA + h + 1], (c, LANE))
            g_row = jnp.broadcast_to(g_rows[sq][SM_DA + h:SM_DA + h + 1, :], (c, c))
            beta_b = jnp.broadcast_to(beta_ref[sq, rows, SM_DB + h:SM_DB + h + 1], (c, LANE))
            km = jnp.where(masks[hh], qkv_ref[sq, rows, KEY_C + p * LANE:KEY_C + (p + 1) * LANE], 0.0)
            qm = jnp.where(masks[hh], qkv_ref[sq, rows, p * LANE:(p + 1) * LANE], 0.0)
            dn.append(dict(g_col=g_col, g_row=g_row, beta_b=beta_b, km=km, qm=qm, kmb=km.astype(BF16),
                           kb=km * beta_b, eg=jnp.exp(g_col), g_last=g_col[c - 1:c, :]))
        kk = [_dot_nt(d["kb"].astype(BF16), d["kmb"]) for d in dn]
        qk = [_dot_nt(d["qm"].astype(BF16), d["kmb"]) for d in dn]
        d_inter = [_dot_nt((d["qm"] * d["eg"]).astype(BF16), st_db[(sq, p)]) for d, (sq, h, p, hh) in zip(dn, units)]
        decay = [jnp.exp(jnp.where(tril, d["g_col"][:, 0:c] - d["g_row"], NEG)) for d in dn]
        a_mats = [jnp.where(strict, m * dc, 0.0) for m, dc in zip(kk, decay)]
        st_g = {sp: stg_ref[sp[0], sp[1]] for sp in pairs}
        st_gb = {sp: st_g[sp].astype(BF16) for sp in pairs}
        gl_pair = {}
        for sq, p in pairs:
            bcum = bcum_all[sq][:, p * LANE:(p + 1) * LANE]
            b_end = bcum[c - 1:c, :]
            k = z_ref[sq, rows, O_CK + p * LANE:O_CK + (p + 1) * LANE]
            gl_pair[(sq, p)] = dict(
                q_in=z_ref[sq, rows, O_CQ + p * LANE:O_CQ + (p + 1) * LANE] * (DK_C ** -0.5) * jnp.exp(bcum),
                k_in=(k * jnp.exp(-bcum)).astype(BF16), k_end=k * jnp.exp(b_end - bcum), b_end=b_end)
        g_qm = [jnp.where(masks[hh], gl_pair[(sq, p)]["q_in"], 0.0).astype(BF16) for sq, h, p, hh in units]
        g_ke = [jnp.where(masks[hh], gl_pair[(sq, p)]["k_end"], 0.0).astype(BF16) for sq, h, p, hh in units]
        g_vb = [z_ref[sq, rows, O_CV + h * DV_C:O_CV + (h + 1) * DV_C].astype(BF16) for sq, h, p, hh in units]
        g_sc = [_dot_nt(g_qm[u], gl_pair[(sq, p)]["k_in"]) for u, (sq, h, p, hh) in enumerate(units)]
        g_inter = [_dot_nt(g_qm[u], st_gb[(sq, p)]) for u, (sq, h, p, hh) in enumerate(units)]
        g_upd = [_dot_tn(g_vb[u], g_ke[u]) for u in range(len(units))]
        g_intra = [_dot(jnp.where(tril, g_sc[u], 0.0).astype(BF16), g_vb[u]) for u in range(len(units))]
        t_invs = _unit_lower_inverses(a_mats)
        for u, (sq, h, p, hh) in enumerate(units):
            gate = _silu(z_ref[sq, rows, O_CR + h * DV_C:O_CR + (h + 1) * DV_C])
            o_ref[sq, rows, h * DV_C:(h + 1) * DV_C] = (
                _rms(g_intra[u] + g_inter[u], gnorm_ref[...]) * gate).astype(BF16)
        for sq, p in pairs:
            stg_ref[sq, p] = (st_g[(sq, p)] * jnp.exp(gl_pair[(sq, p)]["b_end"])
                              + g_upd[unit_at(sq, p, 0)] + g_upd[unit_at(sq, p, 1)])
        rhs = [jnp.concatenate([qkv_ref[sq, rows, 2 * KEY_C + h * DV_D:2 * KEY_C + (h + 1) * DV_D] * d["beta_b"],
                                d["kb"] * d["eg"]], axis=1).astype(BF16) for d, (sq, h, p, hh) in zip(dn, units)]
        sol = [_dot(t.astype(BF16), r) for t, r in zip(t_invs, rhs)]
        w_st = [_dot_nt(s[:, DV_D:].astype(BF16), st_db[(sq, p)]) for s, (sq, h, p, hh) in zip(sol, units)]
        v_new = [(s[:, 0:DV_D] - ws).astype(BF16) for s, ws in zip(sol, w_st)]
        d_intra = [_dot(jnp.where(tril, q * dc, 0.0).astype(BF16), vn) for q, dc, vn in zip(qk, decay, v_new)]
        d_upd = [_dot_tn(vn, (d["km"] * jnp.exp(d["g_last"] - d["g_col"])).astype(BF16)) for vn, d in zip(v_new, dn)]
        for u, (sq, h, p, hh) in enumerate(units):
            gate = _silu(z_ref[sq, rows, O_DG + h * DV_D:O_DG + (h + 1) * DV_D])
            col = H_C * DV_C + h * DV_D
            o_ref[sq, rows, col:col + DV_D] = (_rms(d_inter[u] + d_intra[u], dnorm_ref[...]) * gate).astype(BF16)
        for sq, p in pairs:
            u0, u1 = unit_at(sq, p, 0), unit_at(sq, p, 1)
            dec_row = jnp.where(lane_head_row == 0, jnp.exp(dn[u0]["g_last"]), jnp.exp(dn[u1]["g_last"]))
            std_ref[sq, p] = st_d[(sq, p)] * dec_row + d_upd[u0] + d_upd[u1]
        return carry

    lax.fori_loop(0, tc // c, chunk, 0)

    @pl.when(i == pl.num_programs(1) - 1)
    def _():
        for sq in range(ns):
            tail_ref[sq] = prev_ref[sq]
            for p in range(H_C // PAIR):
                tg = stg_ref[sq, p].T
                td = std_ref[sq, p].T
                for hh in range(PAIR):
                    sg_ref[sq, p * PAIR + hh] = tg[hh * DK_C:(hh + 1) * DK_C, :]
                    sd_ref[sq, p * PAIR + hh] = td[hh * DK_D:(hh + 1) * DK_D, :]


def _odd_mixer_prompt(z, p, batch, seq, tc):
    nt = seq // tc
    ns = 2 if batch % 2 == 0 else 1
    full = lambda a: pl.BlockSpec(a.shape, lambda b, i: (0,) * a.ndim)
    consts = [p["w_g2_pad"], p["b_g2"], p["gla_norm"], p["conv_w"], p["alog_slab"], p["dtb_slab"], p["delta_norm"]]
    st_spec = pl.BlockSpec((ns, H_C, DK_C, DV_C), lambda b, i: (b, 0, 0, 0))
    mix, s_gla, s_delta, tail = pl.pallas_call(
        _odd_mixer_prompt_kernel,
        grid=(batch // ns, nt),
        in_specs=[pl.BlockSpec((ns, tc, IN_ODD_PAD), lambda b, i: (b, i, 0))] + [full(a) for a in consts],
        out_specs=[pl.BlockSpec((ns, tc, D_MODEL), lambda b, i: (b, i, 0)), st_spec, st_spec,
                   pl.BlockSpec((ns, 8, C_CONV), lambda b, i: (b, 0, 0))],
        out_shape=[jax.ShapeDtypeStruct((batch, seq, D_MODEL), BF16),
                   jax.ShapeDtypeStruct((batch, H_C, DK_C, DV_C), F32),
                   jax.ShapeDtypeStruct((batch, H_D, DK_D, DV_D), F32),
                   jax.ShapeDtypeStruct((batch, 8, C_CONV), F32)],
        scratch_shapes=[pltpu.VMEM((ns, H_C // PAIR, DV_C, LANE), F32),
                        pltpu.VMEM((ns, H_D // PAIR, DV_D, LANE), F32),
                        pltpu.VMEM((ns, 8, C_CONV), F32),
                        pltpu.VMEM((ns, tc, KEY_C), F32),
                        pltpu.VMEM((ns, tc, C_CONV), F32),
                        pltpu.VMEM((ns, tc, LANE), F32),
                        pltpu.VMEM((ns, tc, LANE), F32)],
        compiler_params=_cparams("parallel", "arbitrary"),
        name="odd_mixer_prompt",
    )(z.reshape(batch, seq, IN_ODD_PAD), *consts)
    return mix.reshape(batch * seq, D_MODEL), s_gla, s_delta, tail


M_E1, M_E2, M_R1, M_R2, M_G1, M_G2 = (N_EXPERTS + j for j in range(6))


def _proj_router_kernel(mix_ref, x_ref, wo_ref, g_ref, wr_ref, x1_ref, hn_ref, meta_ref, cnt_ref, carry_ref):
    tm = x_ref.shape[0]

    @pl.when(pl.program_id(0) == 0)
    def _():
        carry_ref[...] = jnp.zeros_like(carry_ref)

    x1 = x_ref[...] + _dot(mix_ref[...], wo_ref[...])
    x1_ref[...] = x1
    hn = _rms(x1, g_ref[...])
    hn_ref[...] = hn.astype(hn_ref.dtype)
    lane = lax.broadcasted_iota(jnp.int32, (tm, LANE), 1)
    hn_hi, hn_lo = _split_bf16(hn, 2)
    wr_hi, wr_lo = _split_bf16(wr_ref[...], 2)
    logits = _dot(hn_hi, wr_hi) + _dot(hn_hi, wr_lo) + _dot(hn_lo, wr_hi)
    logits = jnp.where(lane < N_EXPERTS, logits, NEG)
    m1 = jnp.max(logits, axis=1, keepdims=True)
    e1 = jnp.min(jnp.where(logits == m1, lane, LANE), axis=1, keepdims=True)
    rest = jnp.where(lane == e1, NEG, logits)
    m2 = jnp.max(rest, axis=1, keepdims=True)
    e2 = jnp.min(jnp.where(rest == m2, lane, LANE), axis=1, keepdims=True)
    t = jnp.exp(m2 - m1)
    g1 = 1.0 / (1.0 + t)
    g2 = t / (1.0 + t)
    oh1 = lane == e1
    oh2 = lane == e2
    member = (oh1 | oh2).astype(F32)
    ri = lax.broadcasted_iota(jnp.int32, (tm, tm), 0)
    ci = lax.broadcasted_iota(jnp.int32, (tm, tm), 1)
    before = _dot((ri > ci).astype(BF16), member.astype(BF16)) + carry_ref[...]
    r1 = jnp.sum(jnp.where(oh1, before, 0.0), axis=1, keepdims=True)
    r2 = jnp.sum(jnp.where(oh2, before, 0.0), axis=1, keepdims=True)
    carry_ref[...] = carry_ref[...] + jnp.sum(member, axis=0, keepdims=True)
    cnt_ref[...] = carry_ref[...]
    meta = jnp.where(oh1, g1, 0.0) + jnp.where(oh2, g2, 0.0)
    meta = jnp.where(lane == M_E1, e1.astype(F32), meta)
    meta = jnp.where(lane == M_E2, e2.astype(F32), meta)
    meta = jnp.where(lane == M_R1, r1, meta)
    meta = jnp.where(lane == M_R2, r2, meta)
    meta = jnp.where(lane == M_G1, g1, meta)
    meta = jnp.where(lane == M_G2, g2, meta)
    meta_ref[...] = meta


def _proj_router(mix, x, w_out, g, w_router_pad, tm, hn_dtype):
    m, d = x.shape
    row = lambda width: pl.BlockSpec((tm, width), lambda i: (i, 0))
    full = lambda a: pl.BlockSpec(a.shape, lambda i: (0,) * a.ndim)
    return pl.pallas_call(
        _proj_router_kernel,
        grid=(m // tm,),
        in_specs=[row(d), row(d), full(w_out), full(g), full(w_router_pad)],
        out_specs=[row(d), row(d), row(LANE), pl.BlockSpec((1, LANE), lambda i: (0, 0))],
        out_shape=[jax.ShapeDtypeStruct((m, d), F32), jax.ShapeDtypeStruct((m, d), hn_dtype),
                   jax.ShapeDtypeStruct((m, LANE), F32), jax.ShapeDtypeStruct((1, LANE), F32)],
        scratch_shapes=[pltpu.VMEM((1, LANE), F32)],
        compiler_params=_cparams("arbitrary"),
        name="proj_router",
    )(mix, x, w_out, g, w_router_pad)


def _moe_dense_kernel(hn_ref, x1_ref, meta_ref, wg_ref, wu_ref, wd_ref, gf_ref, o_ref):
    e = pl.program_id(1)
    f = pl.program_id(2)

    @pl.when((e == 0) & (f == 0))
    def _():
        o_ref[...] = x1_ref[...]

    h = hn_ref[...]
    act = (_silu(_dot(h, wg_ref[...].astype(BF16))) * _dot(h, wu_ref[...].astype(BF16))).astype(BF16)
    lane = lax.broadcasted_iota(jnp.int32, meta_ref.shape, 1)
    gate = jnp.sum(jnp.where(lane == e, meta_ref[...], 0.0), axis=1, keepdims=True)
    o_ref[...] += gate * _dot(act, wd_ref[...].astype(BF16))

    @pl.when((e == pl.num_programs(1) - 1) & (f == pl.num_programs(2) - 1))
    def _():
        o_ref[...] = _rms(o_ref[...], gf_ref[...])


def _moe_dense(hn, x1, meta, w_gate, w_up, w_down, g_final, tm):
    m, d = x1.shape
    n_e, _, ff = w_gate.shape
    tf = MOE_TF
    row = lambda width: pl.BlockSpec((tm, width), lambda i, e, f: (i, 0))
    return pl.pallas_call(
        _moe_dense_kernel,
        grid=(m // tm, n_e, ff // tf),
        in_specs=[row(d), row(d), row(LANE),
                  pl.BlockSpec((None, d, tf), lambda i, e, f: (e, 0, f)),
                  pl.BlockSpec((None, d, tf), lambda i, e, f: (e, 0, f)),
                  pl.BlockSpec((None, tf, d), lambda i, e, f: (e, f, 0)),
                  pl.BlockSpec((1, d), lambda i, e, f: (0, 0))],
        out_specs=row(d),
        out_shape=jax.ShapeDtypeStruct((m, d), F32),
        compiler_params=_cparams("parallel", "arbitrary", "arbitrary"),
        name="moe_dense",
    )(hn, x1, meta, w_gate, w_up, w_down, g_final)


def _moe_dispatch_kernel(dest_ref, hn_ref, xs_in_ref, xs_ref, sem):
    del xs_in_ref
    tm = hn_ref.shape[0]
    base = pl.program_id(0) * (TOP_K * tm)

    def row_copy(t, k):
        return pltpu.make_async_copy(hn_ref.at[pl.ds(t, 1)], xs_ref.at[pl.ds(dest_ref[base + k * tm + t], 1)], sem)

    def issue(t, carry):
        for k in range(TOP_K):
            row_copy(t, k).start()
        return carry

    def drain(t, carry):
        for k in range(TOP_K):
            row_copy(t, k).wait()
        return carry

    lax.fori_loop(0, tm, issue, 0, unroll=DMA_UNROLL)
    lax.fori_loop(0, tm, drain, 0, unroll=DMA_UNROLL)


def _moe_dispatch(dest, hn, n_rows, tm):
    m, d = hn.shape
    return pl.pallas_call(
        _moe_dispatch_kernel,
        grid_spec=pltpu.PrefetchScalarGridSpec(
            num_scalar_prefetch=1,
            grid=(m // tm,),
            in_specs=[pl.BlockSpec((tm, d), lambda i, dest: (i, 0)), pl.BlockSpec(memory_space=pl.ANY)],
            out_specs=pl.BlockSpec(memory_space=pl.ANY),
            scratch_shapes=[pltpu.SemaphoreType.DMA(())]),
        out_shape=jax.ShapeDtypeStruct((n_rows, d), hn.dtype),
        input_output_aliases={2: 0},
        compiler_params=_cparams("arbitrary"),
        name="moe_dispatch",
    )(dest, hn, jnp.zeros((n_rows, d), hn.dtype))


def _moe_grouped_kernel(te_ref, tr_ref, xs_ref, wg_ref, wu_ref, wd_ref, o_ref, xb_ref):
    del te_ref
    f = pl.program_id(1)
    used = tr_ref[pl.program_id(0)]
    half = xs_ref.shape[0] // 2

    @pl.when(f == 0)
    def _():
        o_ref[...] = jnp.zeros_like(o_ref)
        xb_ref[...] = xs_ref[...].astype(BF16)

    def swiglu_rows(rows):
        h = xb_ref[rows, :]
        act = (_silu(_dot(h, wg_ref[...].astype(BF16))) * _dot(h, wu_ref[...].astype(BF16))).astype(BF16)
        o_ref[rows, :] += _dot(act, wd_ref[...].astype(BF16))

    @pl.when(used > half)
    def _():
        swiglu_rows(slice(None))

    @pl.when((used > 0) & (used <= half))
    def _():
        swiglu_rows(slice(0, half))


def _moe_grouped(tile_expert, tile_rows, xs, w_gate, w_up, w_down, tg):
    rows, d = xs.shape
    tf = MOE_TF
    nf = w_gate.shape[2] // tf
    fidx = lambda r, f, tr: jnp.where(tr[r] > 0, f, nf - 1)
    return pl.pallas_call(
        _moe_grouped_kernel,
        grid_spec=pltpu.PrefetchScalarGridSpec(
            num_scalar_prefetch=2,
            grid=(rows // tg, nf),
            in_specs=[pl.BlockSpec((tg, d), lambda r, f, te, tr: (r, 0)),
                      pl.BlockSpec((None, d, tf), lambda r, f, te, tr: (te[r], 0, fidx(r, f, tr))),
                      pl.BlockSpec((None, d, tf), lambda r, f, te, tr: (te[r], 0, fidx(r, f, tr))),
                      pl.BlockSpec((None, tf, d), lambda r, f, te, tr: (te[r], fidx(r, f, tr), 0))],
            out_specs=pl.BlockSpec((tg, d), lambda r, f, te, tr: (r, 0)),
            scratch_shapes=[pltpu.VMEM((tg, d), BF16)]),
        out_shape=jax.ShapeDtypeStruct((rows, d), F32),
        compiler_params=_cparams("parallel", "arbitrary"),
        name="moe_grouped",
    )(tile_expert, tile_rows, xs, w_gate, w_up, w_down)


def _moe_combine_kernel(dest_ref, ys_ref, x1_ref, meta_ref, gf_ref, o_ref, buf_ref, sem):
    i = pl.program_id(0)
    tc = x1_ref.shape[0]
    rows = TOP_K * tc

    def row_copy(step, slot, j):
        return pltpu.make_async_copy(ys_ref.at[pl.ds(dest_ref[step * rows + j], 1)],
                                     buf_ref.at[slot, pl.ds(j, 1)], sem.at[slot])

    def issue(step, slot):
        def body(j, carry):
            row_copy(step, slot, j).start()
            return carry
        lax.fori_loop(0, rows, body, 0, unroll=DMA_UNROLL)

    def drain(step, slot):
        def body(j, carry):
            row_copy(step, slot, j).wait()
            return carry
        lax.fori_loop(0, rows, body, 0, unroll=DMA_UNROLL)

    slot = i % 2

    @pl.when(i == 0)
    def _():
        issue(0, 0)

    @pl.when(i + 1 < pl.num_programs(0))
    def _():
        issue(i + 1, 1 - slot)

    drain(i, slot)
    g1 = meta_ref[:, M_G1:M_G1 + 1]
    g2 = meta_ref[:, M_G2:M_G2 + 1]
    y = x1_ref[...] + g1 * buf_ref[slot, 0:tc] + g2 * buf_ref[slot, tc:rows]
    o_ref[...] = _rms(y, gf_ref[...])


def _moe_combine(dest, ys, x1, meta, g_final, tc):
    m, d = x1.shape
    row = lambda width: pl.BlockSpec((tc, width), lambda i, dest: (i, 0))
    return pl.pallas_call(
        _moe_combine_kernel,
        grid_spec=pltpu.PrefetchScalarGridSpec(
            num_scalar_prefetch=1,
            grid=(m // tc,),
            in_specs=[pl.BlockSpec(memory_space=pl.ANY), row(d), row(LANE),
                      pl.BlockSpec((1, d), lambda i, dest: (0, 0))],
            out_specs=row(d),
            scratch_shapes=[pltpu.VMEM((2, TOP_K * tc, d), F32), pltpu.SemaphoreType.DMA((2,))]),
        out_shape=jax.ShapeDtypeStruct((m, d), F32),
        compiler_params=_cparams("arbitrary"),
        name="moe_combine",
    )(dest, ys, x1, meta, g_final)


def _moe_routes(meta, counts, tile, tg):
    m = meta.shape[0]
    n_rows = TOP_K * m + N_EXPERTS * tg
    expert = meta[:, M_E1:M_E2 + 1].astype(jnp.int32)
    rank = meta[:, M_R1:M_R2 + 1].astype(jnp.int32)
    count = counts[0, :N_EXPERTS].astype(jnp.int32)
    padded = (count + tg - 1) // tg * tg
    ends = jnp.cumsum(padded)
    dest = (ends - padded)[expert] + rank
    dest = jnp.transpose(dest.reshape(m // tile, tile, TOP_K), (0, 2, 1)).reshape(-1)
    tile_start = jnp.arange(n_rows // tg, dtype=jnp.int32) * tg
    probe = jnp.minimum(tile_start, ends[-1] - 1)
    tile_expert = jnp.sum(ends[None, :] <= probe[:, None], axis=1).astype(jnp.int32)
    tile_rows = jnp.clip((ends - padded + count)[tile_expert] - tile_start, 0, tg).astype(jnp.int32)
    return dest, tile_expert, tile_rows, n_rows


def _odd_params(i, od_norm_mix, od_w_in, od_gla_w_gate2, od_gla_b_gate2, od_gla_norm, od_delta_conv,
                od_delta_a_log, od_delta_dt_bias, od_delta_norm, od_w_out, od_norm_ffn, od_router,
                od_w_gate, od_w_up, od_w_down):
    w = od_w_in[i]
    sizes = (KEY_C, KEY_C, H_C * DV_C, GATE_RANK, H_C * DV_C, C_CONV, H_D, H_D, H_D * DV_D)
    splits = tuple(sum(sizes[:j + 1]) for j in range(len(sizes) - 1))
    cq, ck, cv, c_lr, c_r, d_qkv, d_a, d_b, d_g = jnp.split(w, splits, axis=1)
    pad = jnp.zeros((w.shape[0], IN_ODD_PAD - w.shape[1]), w.dtype)
    w_in = jnp.concatenate([cq, ck, cv, c_r, d_qkv, d_g, c_lr, d_a, d_b, pad], axis=1).astype(BF16)
    slab = lambda v: jnp.zeros((1, LANE), F32).at[0, SM_DA:SM_DA + H_D].set(v)
    return dict(norm_mix=od_norm_mix[i][None], w_in=w_in,
                w_g2_pad=jnp.zeros((LANE, KEY_C), F32).at[:GATE_RANK].set(od_gla_w_gate2[i]),
                b_g2=od_gla_b_gate2[i][None], gla_norm=od_gla_norm[i][None], conv_w=od_delta_conv[i],
                alog_slab=slab(od_delta_a_log[i]), dtb_slab=slab(od_delta_dt_bias[i]),
                a_log=od_delta_a_log[i], dt_bias=od_delta_dt_bias[i],
                delta_norm=od_delta_norm[i][None], w_out=od_w_out[i].astype(BF16),
                norm_ffn=od_norm_ffn[i][None],
                router=jnp.zeros((D_MODEL, LANE), F32).at[:, :N_EXPERTS].set(od_router[i]),
                w_gate=od_w_gate[i], w_up=od_w_up[i], w_down=od_w_down[i])


MOE_GROUP_TILE = 1024
MOE_ROW_TILE = 512


def _moe(mix, x, p, g_final, tm):
    m = x.shape[0]
    if m < N_EXPERTS * MOE_GROUP_TILE:
        x1, hn, meta, _ = _proj_router(mix, x, p["w_out"], p["norm_ffn"], p["router"], tm, BF16)
        return _moe_dense(hn, x1, meta, p["w_gate"], p["w_up"], p["w_down"], g_final, tm)
    x1, hn, meta, counts = _proj_router(mix, x, p["w_out"], p["norm_ffn"], p["router"], tm, F32)
    dest, tile_expert, tile_rows, n_rows = _moe_routes(meta, counts, MOE_ROW_TILE, MOE_GROUP_TILE)
    xs = _moe_dispatch(dest, hn, n_rows, MOE_ROW_TILE)
    ys = _moe_grouped(tile_expert, tile_rows, xs, p["w_gate"], p["w_up"], p["w_down"], MOE_GROUP_TILE)
    return _moe_combine(dest, ys, x1, meta, g_final, MOE_ROW_TILE)


def _odd_layer_prompt(x, p, g_final, batch, seq):
    z = _norm_matmul(x, p["norm_mix"], p["w_in"], TOKEN_TILE)
    mix, s_gla, s_delta, tail = _odd_mixer_prompt(z, p, batch, seq, MIXER_TILE)
    y = _moe(mix, x, p, g_final, TOKEN_TILE)
    return y, s_gla, s_delta, tail[:, 8 - (CONV_W - 1):]


def _even_epilogue_sample_kernel(z_ref, c_ref, su_ref, sd_ref, lng_ref, lnb_ref, w0_ref, b0_ref,
                                 q_ref, kv_ref, gv_ref, bout_ref):
    c, su, sd = c_ref[...], su_ref[...], sd_ref[...]
    q_ref[...] = _rope(z_ref[:, 0:QKV_A], c, su, sd) * (D_A ** -0.5)
    kv_ref[:, 0:QKV_A] = _rope(z_ref[:, QKV_A:2 * QKV_A], c, su, sd)
    kv_ref[:, QKV_A:2 * QKV_A] = z_ref[:, 2 * QKV_A:3 * QKV_A]
    u = _gelu(z_ref[:, 3 * QKV_A:3 * QKV_A + D_B])
    gv = _layernorm(_gelu(z_ref[:, 3 * QKV_A + D_B:3 * QKV_A + 2 * D_B]), lng_ref[...], lnb_ref[...])
    gv_ref[...] = gv
    bout_ref[...] = (u * (gv * w0_ref[...] + b0_ref[...])).astype(BF16)


def _even_epilogue_sample(z, tables, ln_g, ln_b, w_sp, b_sp):
    m = z.shape[0]
    w0 = jnp.repeat(w_sp[:, 0, 0], D_BG)[None]
    b0 = jnp.repeat(b_sp[:, 0], D_BG)[None]
    return pl.pallas_call(
        _even_epilogue_sample_kernel,
        out_shape=[jax.ShapeDtypeStruct((m, QKV_A), F32), jax.ShapeDtypeStruct((m, 2 * QKV_A), F32),
                   jax.ShapeDtypeStruct((m, D_B), F32), jax.ShapeDtypeStruct((m, D_B), BF16)],
        compiler_params=pltpu.CompilerParams(vmem_limit_bytes=VMEM_LIMIT),
        name="even_epilogue_sample",
    )(z, *tables, ln_g, ln_b, w0, b0)


def _moba_sample_kernel(pt_ref, qt_ref, knt_ref, vnt_ref, *refs):
    del pt_ref
    n_pages = len(refs) - 2
    page_refs, o_ref, s_ref = refs[:n_pages], refs[n_pages], refs[n_pages + 1]
    pages_per_block = MOBA_BLOCK // PAGE_SIZE
    nb = n_pages // pages_per_block
    qt = qt_ref[0]
    for h in range(H_A):
        qcol = jnp.broadcast_to(qt[:, h:h + 1], (D_A, PAGE_SIZE))
        for j in range(n_pages):
            s_ref[h, j:j + 1, :] = jnp.sum(page_refs[j][0, 0, h] * qcol, axis=0, keepdims=True)
    lane = lax.broadcasted_iota(jnp.int32, (n_pages, LANE), 1)
    page_sums = jnp.zeros((n_pages, LANE), F32)
    for h in range(H_A):
        page_sums = jnp.where(lane == h, jnp.sum(s_ref[h], axis=1, keepdims=True), page_sums)
    pair = (lax.broadcasted_iota(jnp.int32, (nb, n_pages), 1) // pages_per_block
            == lax.broadcasted_iota(jnp.int32, (nb, n_pages), 0)).astype(F32)
    pair_t = (lax.broadcasted_iota(jnp.int32, (n_pages, nb), 0) // pages_per_block
              == lax.broadcasted_iota(jnp.int32, (n_pages, nb), 1)).astype(F32)
    gate = jnp.dot(pair, page_sums, precision=HI, preferred_element_type=F32)
    sel = _moba_select(gate, nb)
    sel_pages = jnp.dot(pair_t, sel, precision=HI, preferred_element_type=F32)
    own = jnp.sum(qt * knt_ref[0], axis=0, keepdims=True)
    vnt = vnt_ref[0]
    out_lane = lax.broadcasted_iota(jnp.int32, (D_A, LANE), 1)
    out = jnp.zeros((D_A, LANE), F32)
    for h in range(H_A):
        sm = jnp.where(sel_pages[:, h:h + 1] > 0.0, s_ref[h], NEG)
        s_own = own[:, h:h + 1]
        mx = jnp.maximum(jnp.max(jnp.max(sm, axis=1, keepdims=True), axis=0, keepdims=True), s_own)
        p = jnp.exp(sm - mx)
        p_own = jnp.exp(s_own - mx)
        denom = jnp.sum(jnp.sum(p, axis=1, keepdims=True), axis=0, keepdims=True) + p_own
        acc = jnp.zeros((D_A, PAGE_SIZE), F32)
        for j in range(n_pages):
            acc = acc + page_refs[j][0, 1, h] * p[j:j + 1, :]
        o = (jnp.sum(acc, axis=1, keepdims=True) + p_own * vnt[:, h:h + 1]) / denom
        out = jnp.where(out_lane == h, o, out)
    o_ref[0] = out


def _moba_sample(q, kv_new, cache, page_table):
    bs, n_pages = page_table.shape
    assert (n_pages * PAGE_SIZE) % MOBA_BLOCK == 0
    cache_t = jnp.transpose(cache, (0, 2, 3, 4, 1))
    page_spec = lambda j: pl.BlockSpec((1, 2, H_A, D_A, PAGE_SIZE),
                                       lambda b, pt: (pt[b * n_pages + j], 0, 0, 0, 0))
    col_spec = pl.BlockSpec((1, D_A, H_A), lambda b, pt: (b, 0, 0))
    heads_t = lambda a: jnp.transpose(a.reshape(bs, H_A, D_A), (0, 2, 1))
    out = pl.pallas_call(
        _moba_sample_kernel,
        grid_spec=pltpu.PrefetchScalarGridSpec(
            num_scalar_prefetch=1,
            grid=(bs,),
            in_specs=[col_spec, col_spec, col_spec] + [page_spec(j) for j in range(n_pages)],
            out_specs=pl.BlockSpec((1, D_A, LANE), lambda b, pt: (b, 0, 0)),
            scratch_shapes=[pltpu.VMEM((H_A, n_pages, PAGE_SIZE), F32)]),
        out_shape=jax.ShapeDtypeStruct((bs, D_A, LANE), F32),
        compiler_params=_cparams("parallel"),
        name="moba_sample",
    )(page_table.reshape(-1), heads_t(q), heads_t(kv_new[:, 0:QKV_A]), heads_t(kv_new[:, QKV_A:]),
      *([cache_t] * n_pages))
    return jnp.transpose(out[:, :, 0:H_A], (0, 2, 1)).reshape(bs, QKV_A).astype(BF16)


def _odd_mixer_sample_kernel(z_ref, buf_ref, sg_ref, sd_ref, wg2_ref, bg2_ref, gnorm_ref, cw_ref, alog_ref, dtb_ref,
                             dnorm_ref, o_ref, sgo_ref, sdo_ref):
    bt = z_ref.shape[0]
    stride = H_C * DK_C
    small = z_ref[:, O_SM:O_SM + LANE]
    pre = jnp.dot(small, wg2_ref[...], precision=HI, preferred_element_type=F32) + bg2_ref[...]
    a_all = jnp.exp(_log_sigmoid(pre) / GATE_NORM)
    g_all = -jnp.exp(alog_ref[...]) * _softplus(small + dtb_ref[...])
    beta_all = jax.nn.sigmoid(small)
    x = z_ref[:, O_DQKV:O_DQKV + C_CONV]
    y = x * cw_ref[CONV_W - 1:CONV_W, :]
    for j in range(CONV_W - 1):
        y = y + buf_ref[:, j, :] * cw_ref[j:j + 1, :]
    y = _silu(y)
    for h in range(H_C):
        ks = slice(h * DK_C, (h + 1) * DK_C)
        a = a_all[:, ks]
        q = z_ref[:, O_CQ + h * DK_C:O_CQ + (h + 1) * DK_C] * (DK_C ** -0.5)
        k = z_ref[:, O_CK + h * DK_C:O_CK + (h + 1) * DK_C]
        v = z_ref[:, O_CV + h * DV_C:O_CV + (h + 1) * DV_C]
        qa = q * a
        acc = jnp.sum(q * k, axis=1, keepdims=True) * v
        for kk in range(DK_C):
            rows = pl.ds(h * DK_C + kk, bt, stride=stride)
            srow = sg_ref[rows, :]
            acc = acc + qa[:, kk:kk + 1] * srow
            sgo_ref[rows, :] = a[:, kk:kk + 1] * srow + k[:, kk:kk + 1] * v
        gate = _silu(z_ref[:, O_CR + h * DV_C:O_CR + (h + 1) * DV_C])
        o_ref[:, h * DV_C:(h + 1) * DV_C] = (_rms(acc, gnorm_ref[...]) * gate).astype(BF16)
        yq = y[:, h * DK_D:(h + 1) * DK_D]
        yk = y[:, KEY_C + h * DK_D:KEY_C + (h + 1) * DK_D]
        dv = y[:, 2 * KEY_C + h * DV_D:2 * KEY_C + (h + 1) * DV_D]
        dq = yq * lax.rsqrt(jnp.sum(yq * yq, axis=1, keepdims=True) + EPS) * (DK_D ** -0.5)
        dk = yk * lax.rsqrt(jnp.sum(yk * yk, axis=1, keepdims=True) + EPS)
        beta = beta_all[:, SM_DB + h:SM_DB + h + 1]
        eg = jnp.exp(g_all[:, SM_DA + h:SM_DA + h + 1])
        w = dk * (beta * eg)
        qd = dq * eg
        ws = jnp.zeros((bt, DV_D), F32)
        qs = jnp.zeros((bt, DV_D), F32)
        for kk in range(DK_D):
            srow = sd_ref[pl.ds(h * DK_D + kk, bt, stride=stride), :]
            ws = ws + w[:, kk:kk + 1] * srow
            qs = qs + qd[:, kk:kk + 1] * srow
        v_new = dv * beta - ws
        o = qs + jnp.sum(dq * dk, axis=1, keepdims=True) * v_new
        for kk in range(DK_D):
            rows = pl.ds(h * DK_D + kk, bt, stride=stride)
            sdo_ref[rows, :] = sd_ref[rows, :] * eg + dk[:, kk:kk + 1] * v_new
        gate = _silu(z_ref[:, O_DG + h * DV_D:O_DG + (h + 1) * DV_D])
        col = H_C * DV_C + h * DV_D
        o_ref[:, col:col + DV_D] = (_rms(o, dnorm_ref[...]) * gate).astype(BF16)


def _odd_mixer_sample(z, conv_buf, s_gla, s_delta, p, bt):
    bs = z.shape[0]
    rows = H_C * DK_C
    full = lambda a: pl.BlockSpec(a.shape, lambda i: (0,) * a.ndim)
    consts = [p["w_g2_pad"], p["b_g2"], p["gla_norm"], p["conv_w"], p["alog_slab"], p["dtb_slab"], p["delta_norm"]]
    st_spec = pl.BlockSpec((bt * rows, DV_C), lambda i: (i, 0))
    mix, sg, sd = pl.pallas_call(
        _odd_mixer_sample_kernel,
        grid=(bs // bt,),
        in_specs=[pl.BlockSpec((bt, IN_ODD_PAD), lambda i: (i, 0)),
                  pl.BlockSpec((bt, CONV_W - 1, C_CONV), lambda i: (i, 0, 0)),
                  st_spec, st_spec] + [full(a) for a in consts],
        out_specs=[pl.BlockSpec((bt, D_MODEL), lambda i: (i, 0)), st_spec, st_spec],
        out_shape=[jax.ShapeDtypeStruct((bs, D_MODEL), BF16),
                   jax.ShapeDtypeStruct((bs * rows, DV_C), F32),
                   jax.ShapeDtypeStruct((bs * rows, DV_D), F32)],
        compiler_params=_cparams("parallel"),
        name="odd_mixer_sample",
    )(z, conv_buf, s_gla.reshape(bs * rows, DV_C), s_delta.reshape(bs * rows, DV_D), *consts)
    return mix, sg.reshape(s_gla.shape), sd.reshape(s_delta.shape)


def _sample_step(x, cache, page_table, s_gla, s_delta, conv_buf, ev, od, g_final):
    bs = x.shape[0]
    past = page_table.shape[1] * PAGE_SIZE
    z = _norm_matmul(x, ev["norm_mix"], ev["w_in"], bs)
    tables = _rope_tables(jnp.full((1,), past, jnp.int32))
    q, kv, gv, b_out = _even_epilogue_sample(z, tables, ev["ln_g"], ev["ln_b"], ev["w_sp"], ev["b_sp"])
    a_out = _moba_sample(q, kv, cache, page_table)
    x = _proj_ffn([a_out, b_out], x, ev["w_out"], ev["norm_ffn"], ev["w_gate"], ev["w_up"], ev["w_down"], bs)
    z = _norm_matmul(x, od["norm_mix"], od["w_in"], bs)
    mix, sg, sd = _odd_mixer_sample(z, conv_buf, s_gla, s_delta, od, SAMPLE_MIXER_TILE)
    conv_new = jnp.concatenate([conv_buf[:, 1:], z[:, None, O_DQKV:O_DQKV + C_CONV]], axis=1)
    y = _moe(mix, x, od, g_final, bs)
    return y, kv, gv, sg, sd, conv_new


def kernel(x_prompt, x_sample, cache_kv, state_gla, state_delta, state_conv, page_table, ev_norm_mix, ev_w_in, ev_gmlp_ln_g, ev_gmlp_ln_b, ev_w_spatial, ev_b_spatial, ev_w_out, ev_norm_ffn, ev_w_gate, ev_w_up, ev_w_down, od_norm_mix, od_w_in, od_gla_w_gate2, od_gla_b_gate2, od_gla_norm, od_delta_conv, od_delta_a_log, od_delta_dt_bias, od_delta_norm, od_w_out, od_norm_ffn, od_router, od_w_gate, od_w_up, od_w_down, norm_final):
    bp, tp, d = x_prompt.shape
    ev = _even_params(0, ev_norm_mix, ev_w_in, ev_gmlp_ln_g, ev_gmlp_ln_b, ev_w_spatial, ev_b_spatial, ev_w_out,
                      ev_norm_ffn, ev_w_gate, ev_w_up, ev_w_down)
    od = _odd_params(0, od_norm_mix, od_w_in, od_gla_w_gate2, od_gla_b_gate2, od_gla_norm, od_delta_conv,
                     od_delta_a_log, od_delta_dt_bias, od_delta_norm, od_w_out, od_norm_ffn, od_router,
                     od_w_gate, od_w_up, od_w_down)
    bs, ts, _ = x_sample.shape
    assert ts == 1 and cache_kv.shape[0] == 1 and state_gla.shape[0] == 1
    xp, kv_p, gv_p = _even_layer_prompt(x_prompt.reshape(bp * tp, d), ev, bp, tp)
    yp, gla_p, dl_p, cv_p = _odd_layer_prompt(xp, od, norm_final[None], bp, tp)
    ys, kv_s, gv_s, gla_s, dl_s, cv_s = _sample_step(
        x_sample.reshape(bs, d), cache_kv[0], page_table, state_gla[0], state_delta[0], state_conv[0],
        ev, od, norm_final[None])
    return (yp.reshape(bp, tp, d), ys.reshape(bs, ts, d),
            kv_p[None], kv_s.reshape(1, bs, ts, 2, H_A, D_A),
            gv_p[None], gv_s.reshape(1, bs, ts, D_B),
            gla_p[None], gla_s[None], dl_p[None], dl_s[None], cv_p[None], cv_s[None])
```

```python
import functools
import math

import jax
import jax.numpy as jnp
from jax import lax
from jax.experimental import pallas as pl
from jax.experimental.pallas import tpu as pltpu

F32 = jnp.float32
BF16 = jnp.bfloat16
HI = lax.Precision.HIGHEST
EPS = 1e-6
NEG = -1e30

D_MODEL = 1024
PAGE_SIZE = 128
H_A, D_A = 8, 64
ROT_DIM = D_A // 4
ROPE_THETA = 500000.0
MOBA_BLOCK = 256
MOBA_TOPK = 3
G_B, D_BG = 8, 64
D_B = G_B * D_BG
CHUNK_B = 128
H_C, DK_C, DV_C = 4, 64, 128
GATE_RANK = 16
GATE_NORM = 16.0
H_D, DK_D, DV_D = 4, 64, 128
CONV_W = 4
C_CONV = 2 * H_D * DK_D + H_D * DV_D
CHUNK_LIN = 64
N_EXPERTS = 8
TOP_K = 2
QKV_A = H_A * D_A
IN_EVEN = 3 * QKV_A + 2 * D_B
TOKEN_TILE = 512
EPILOGUE_TILE = 256
MIXER_TILE = 256
SAMPLE_MIXER_TILE = 32
FFN_TF = 1408
MOE_TF = 512
MOBA_LOOP_BLOCKS = 4
DMA_UNROLL = 8
LANE = 128
VMEM_LIMIT = 56 * 1024 * 1024


def _cparams(*sem):
    return pltpu.CompilerParams(dimension_semantics=sem, vmem_limit_bytes=VMEM_LIMIT)


def _rms(x, g):
    return x * lax.rsqrt(jnp.mean(x * x, axis=-1, keepdims=True) + EPS) * g


def _gelu(x):
    return 0.5 * x * (1.0 + lax.erf(x * (2.0 ** -0.5)))


def _silu(x):
    return x * jax.nn.sigmoid(x)


def _softplus(x):
    return jnp.maximum(x, 0.0) + jnp.log1p(jnp.exp(-jnp.abs(x)))


def _dot(a, b):
    return jnp.dot(a, b, preferred_element_type=F32)


def _dot_nt(a, b):
    return lax.dot_general(a, b, (((1,), (1,)), ((), ())), preferred_element_type=F32)


def _dot_tn(a, b):
    return lax.dot_general(a, b, (((0,), (0,)), ((), ())), preferred_element_type=F32)


def _norm_matmul_kernel(x_ref, g_ref, w_ref, o_ref):
    h = _rms(x_ref[...], g_ref[...]).astype(BF16)
    o_ref[...] = _dot(h, w_ref[...])


def _norm_matmul(x, g, w, tm):
    m, d = x.shape
    n = w.shape[1]
    return pl.pallas_call(
        _norm_matmul_kernel,
        grid=(m // tm,),
        in_specs=[pl.BlockSpec((tm, d), lambda i: (i, 0)),
                  pl.BlockSpec((1, d), lambda i: (0, 0)),
                  pl.BlockSpec((d, n), lambda i: (0, 0))],
        out_specs=pl.BlockSpec((tm, n), lambda i: (i, 0)),
        out_shape=jax.ShapeDtypeStruct((m, n), F32),
        compiler_params=_cparams("parallel"),
        name="norm_matmul",
    )(x, g, w)


def _rope_tables(pos):
    half = ROT_DIM // 2
    inv = ROPE_THETA ** (-jnp.arange(half, dtype=F32) / half)
    ang = pos.astype(F32)[:, None] * inv[None, :]
    cos, sin = jnp.cos(ang), jnp.sin(ang)
    t = pos.shape[0]
    one = jnp.ones((t, D_A - ROT_DIM), F32)
    zero_h = jnp.zeros((t, half), F32)
    zero_r = jnp.zeros((t, D_A - ROT_DIM), F32)
    c = jnp.concatenate([cos, cos, one], axis=1)
    s_up = jnp.concatenate([-sin, zero_h, zero_r], axis=1)
    s_dn = jnp.concatenate([zero_h, sin, zero_r], axis=1)
    rep = LANE // D_A
    return jnp.tile(c, (1, rep)), jnp.tile(s_up, (1, rep)), jnp.tile(s_dn, (1, rep))


def _rope(x, c, s_up, s_dn):
    half = ROT_DIM // 2
    outs = []
    for j in range(x.shape[1] // LANE):
        xs = x[:, j * LANE:(j + 1) * LANE]
        up = pltpu.roll(xs, LANE - half, 1)
        dn = pltpu.roll(xs, half, 1)
        outs.append(xs * c + up * s_up + dn * s_dn)
    return jnp.concatenate(outs, axis=1)


def _layernorm(x, g, b):
    mu = jnp.mean(x, axis=-1, keepdims=True)
    xc = x - mu
    var = jnp.mean(xc * xc, axis=-1, keepdims=True)
    return xc * lax.rsqrt(var + EPS) * g + b


def _even_epilogue_kernel(z_ref, c_ref, su_ref, sd_ref, lng_ref, lnb_ref, wsp_ref, bspt_ref,
                          qt_ref, k_ref, vt_ref, kv_ref, kmean_ref, gv_ref, bout_ref):
    tm = z_ref.shape[0]
    c, su, sd = c_ref[...], su_ref[...], sd_ref[...]
    q = _rope(z_ref[:, 0:QKV_A], c, su, sd) * (D_A ** -0.5)
    k = _rope(z_ref[:, QKV_A:2 * QKV_A], c, su, sd)
    v = z_ref[:, 2 * QKV_A:3 * QKV_A]
    qt_ref[0] = q.T.astype(BF16)
    k_ref[...] = k.astype(BF16)
    vt = v.T
    vt_ref[0] = vt.astype(BF16)
    kv_ref[0, 0:QKV_A, :] = k.T
    kv_ref[0, QKV_A:2 * QKV_A, :] = vt
    for blk in range(tm // MOBA_BLOCK):
        kmean_ref[blk] = jnp.mean(k[blk * MOBA_BLOCK:(blk + 1) * MOBA_BLOCK], axis=0, keepdims=True)
    u = _gelu(z_ref[:, 3 * QKV_A:3 * QKV_A + D_B])
    gv = _layernorm(_gelu(z_ref[:, 3 * QKV_A + D_B:3 * QKV_A + 2 * D_B]), lng_ref[...], lnb_ref[...])
    gv_ref[...] = gv
    gvb = gv.astype(BF16)
    row = lax.broadcasted_iota(jnp.int32, (CHUNK_B, CHUNK_B), 0)
    col = lax.broadcasted_iota(jnp.int32, (CHUNK_B, CHUNK_B), 1)
    group = lax.broadcasted_iota(jnp.int32, (CHUNK_B, D_B), 1) // D_BG
    w = [jnp.where(row >= col, wsp_ref[g], 0.0).astype(BF16) for g in range(G_B)]
    for ch in range(tm // CHUNK_B):
        gvc = gvb[ch * CHUNK_B:(ch + 1) * CHUNK_B]
        mixed = jnp.zeros((CHUNK_B, D_B), F32)
        for g in range(G_B):
            mixed = jnp.where(group == g, _dot(w[g], gvc) + bspt_ref[:, g:g + 1], mixed)
        bout_ref[ch * CHUNK_B:(ch + 1) * CHUNK_B, :] = (u[ch * CHUNK_B:(ch + 1) * CHUNK_B] * mixed).astype(BF16)


def _even_epilogue(z, tables, ln_g, ln_b, w_sp, b_sp, batch, seq, tm):
    n = batch * seq
    nt = seq // tm
    nblk = tm // MOBA_BLOCK
    tab_spec = pl.BlockSpec((tm, LANE), lambda b, i: (i, 0))
    row_spec = lambda width: pl.BlockSpec((tm, width), lambda b, i: (b * nt + i, 0))
    t_spec = pl.BlockSpec((1, QKV_A, tm), lambda b, i: (b, 0, i))
    return pl.pallas_call(
        _even_epilogue_kernel,
        grid=(batch, nt),
        in_specs=[row_spec(IN_EVEN), tab_spec, tab_spec, tab_spec,
                  pl.BlockSpec((1, D_B), lambda b, i: (0, 0)),
                  pl.BlockSpec((1, D_B), lambda b, i: (0, 0)),
                  pl.BlockSpec((G_B, CHUNK_B, CHUNK_B), lambda b, i: (0, 0, 0)),
                  pl.BlockSpec((CHUNK_B, G_B), lambda b, i: (0, 0))],
        out_specs=[t_spec, row_spec(QKV_A), t_spec, pl.BlockSpec((1, 2 * QKV_A, tm), lambda b, i: (b, 0, i)),
                   pl.BlockSpec((nblk, 1, QKV_A), lambda b, i: (b * nt + i, 0, 0)),
                   row_spec(D_B), row_spec(D_B)],
        out_shape=[jax.ShapeDtypeStruct((batch, QKV_A, seq), BF16),
                   jax.ShapeDtypeStruct((n, QKV_A), BF16),
                   jax.ShapeDtypeStruct((batch, QKV_A, seq), BF16),
                   jax.ShapeDtypeStruct((batch, 2 * QKV_A, seq), F32),
                   jax.ShapeDtypeStruct((n // MOBA_BLOCK, 1, QKV_A), F32),
                   jax.ShapeDtypeStruct((n, D_B), F32),
                   jax.ShapeDtypeStruct((n, D_B), BF16)],
        compiler_params=_cparams("parallel", "parallel"),
        name="even_epilogue",
    )(z, *tables, ln_g, ln_b, w_sp, b_sp.T)


def _moba_select(gate, n_own):
    nb = gate.shape[0]
    blk = lax.broadcasted_iota(jnp.int32, gate.shape, 0)
    elig = blk < n_own
    gm = jnp.where(elig, gate, NEG)
    rank = jnp.zeros(gate.shape, F32)
    for m in range(nb):
        gm_m = gm[m:m + 1, :]
        ahead = (gm_m > gm) | ((gm_m == gm) & (m < blk))
        rank = rank + ahead.astype(F32)
    return (elig & (rank < MOBA_TOPK)).astype(F32)


def _moba_prompt_kernel(qt_ref, k_ref, vt_ref, kmean_ref, o_ref, bias_ref):
    i = pl.program_id(2)
    tq = MOBA_BLOCK
    n_head = LANE // D_A
    n_split = tq // LANE
    own = pl.multiple_of(i * tq, tq)
    key_i = lax.broadcasted_iota(jnp.int32, (tq, LANE), 0)
    qry_i = lax.broadcasted_iota(jnp.int32, (tq, LANE), 1)
    head_lanes = [slice(hh * D_A, (hh + 1) * D_A) for hh in range(n_head)]
    streams = [(hh, qh) for hh in range(n_head) for qh in range(n_split)]
    qts = [qt_ref[0, head_lanes[hh], :] for hh in range(n_head)]
    queries = [qts[hh][:, qh * LANE:(qh + 1) * LANE] for hh, qh in streams]
    k_own = [k_ref[pl.ds(own, tq), head_lanes[hh]] for hh in range(n_head)]
    own_scores = [_dot(k_own[hh], queries[idx]) for idx, (hh, qh) in enumerate(streams)]
    gates = [jnp.dot(kmean_ref[0, :, head_lanes[hh]], qts[hh].astype(F32), precision=HI, preferred_element_type=F32)
             for hh in range(n_head)]
    for hh in range(n_head):
        bias = jnp.where(_moba_select(gates[hh], i) > 0.0, 0.0, NEG)
        for qh in range(n_split):
            bias_ref[hh, qh] = bias[:, qh * LANE:(qh + 1) * LANE]
    own_probs, own_stats = [], []
    for idx, (hh, qh) in enumerate(streams):
        s = jnp.where(key_i <= qry_i + qh * LANE, own_scores[idx], NEG)
        m = jnp.max(s, axis=0, keepdims=True)
        p = jnp.exp(s - m)
        own_probs.append(p.astype(BF16))
        own_stats.append((m, jnp.sum(p, axis=0, keepdims=True)))
    vt_own = [vt_ref[0, head_lanes[hh], pl.ds(own, tq)] for hh in range(n_head)]
    state = []
    for idx, (hh, qh) in enumerate(streams):
        state += [own_stats[idx][0], own_stats[idx][1], _dot(vt_own[hh], own_probs[idx])]

    def body(j0, carry, nk):
        tk = nk * tq
        start = pl.multiple_of(j0 * tq, tq)
        k_j = [k_ref[pl.ds(start, tk), head_lanes[hh]] for hh in range(n_head)]
        scores = [_dot(k_j[hh], queries[hh * n_split + qh]) for hh, qh in streams]
        probs, stats = [], []
        for idx, (hh, qh) in enumerate(streams):
            m, l = carry[3 * idx], carry[3 * idx + 1]
            s = [scores[idx][b * tq:(b + 1) * tq] + bias_ref[hh, qh, pl.ds(j0 + b, 1), :] for b in range(nk)]
            m_new = m
            for sb in s:
                m_new = jnp.maximum(m_new, jnp.max(sb, axis=0, keepdims=True))
            alpha = jnp.exp(m - m_new)
            p = [jnp.exp(sb - m_new) for sb in s]
            l_new = l * alpha
            for pb in p:
                l_new = l_new + jnp.sum(pb, axis=0, keepdims=True)
            probs.append(jnp.concatenate([pb.astype(BF16) for pb in p], axis=0))
            stats.append((m_new, l_new, alpha))
        vt_j = [vt_ref[0, head_lanes[hh], pl.ds(start, tk)] for hh in range(n_head)]
        pv = [_dot(vt_j[hh], probs[idx]) for idx, (hh, qh) in enumerate(streams)]
        out = []
        for idx in range(len(streams)):
            m_new, l_new, alpha = stats[idx]
            out += [m_new, l_new, carry[3 * idx + 2] * alpha + pv[idx]]
        return tuple(out)

    state = tuple(state)
    done = 0
    nk = MOBA_LOOP_BLOCKS
    while nk >= 1:
        n_group = (i - done) // nk
        state = lax.fori_loop(0, n_group, lambda t, c, nk=nk, done=done: body(done + t * nk, c, nk), state)
        done = done + n_group * nk
        nk //= 2
    outs = []
    for hh in range(n_head):
        halves = [state[3 * (hh * n_split + qh) + 2] / state[3 * (hh * n_split + qh) + 1] for qh in range(n_split)]
        outs.append(jnp.concatenate(halves, axis=1))
    o_ref[...] = jnp.concatenate(outs, axis=0).T.astype(BF16)


def _moba_prompt(qt, k, vt, kmean, batch, seq):
    nq = seq // MOBA_BLOCK
    hp = QKV_A // LANE
    return pl.pallas_call(
        _moba_prompt_kernel,
        grid=(batch, hp, nq),
        in_specs=[pl.BlockSpec((1, LANE, MOBA_BLOCK), lambda b, h, i: (b, h, i)),
                  pl.BlockSpec((seq, LANE), lambda b, h, i: (b, h)),
                  pl.BlockSpec((1, LANE, seq), lambda b, h, i: (b, h, 0)),
                  pl.BlockSpec((1, nq, LANE), lambda b, h, i: (b, 0, h))],
        out_specs=pl.BlockSpec((MOBA_BLOCK, LANE), lambda b, h, i: (b * nq + i, h)),
        out_shape=jax.ShapeDtypeStruct((batch * seq, QKV_A), BF16),
        scratch_shapes=[pltpu.VMEM((LANE // D_A, MOBA_BLOCK // LANE, nq, LANE), F32)],
        compiler_params=_cparams("parallel", "parallel", "arbitrary"),
        name="moba_prompt",
    )(qt, k, vt, kmean)


def _proj_ffn_kernel(*refs, n_mix):
    mix_refs = refs[:n_mix]
    x_ref, wo_ref, g_ref, wg_ref, wu_ref, wd_ref, o_ref, hn_ref = refs[n_mix:]

    @pl.when(pl.program_id(1) == 0)
    def _():
        x1 = x_ref[...]
        off = 0
        for r in mix_refs:
            width = r.shape[1]
            x1 = x1 + _dot(r[...], wo_ref[off:off + width, :])
            off += width
        o_ref[...] = x1
        hn_ref[...] = _rms(x1, g_ref[...]).astype(BF16)

    h = hn_ref[...]
    act = (_silu(_dot(h, wg_ref[...])) * _dot(h, wu_ref[...])).astype(BF16)
    o_ref[...] += _dot(act, wd_ref[...])


def _col_blocks(w, tf):
    *lead, d, ff = w.shape
    n = len(lead)
    return jnp.transpose(w.reshape(*lead, d, ff // tf, tf), (*range(n), n + 1, n, n + 2))


def _proj_ffn(mixes, x, w_out, g, w_gate, w_up, w_down, tm):
    m, d = x.shape
    nf, _, tf = w_gate.shape
    mix_specs = [pl.BlockSpec((tm, a.shape[1]), lambda i, f: (i, 0)) for a in mixes]
    return pl.pallas_call(
        functools.partial(_proj_ffn_kernel, n_mix=len(mixes)),
        grid=(m // tm, nf),
        in_specs=mix_specs + [
            pl.BlockSpec((tm, d), lambda i, f: (i, 0)),
            pl.BlockSpec(w_out.shape, lambda i, f: (0, 0)),
            pl.BlockSpec((1, d), lambda i, f: (0, 0)),
            pl.BlockSpec((None, d, tf), lambda i, f: (f, 0, 0)),
            pl.BlockSpec((None, d, tf), lambda i, f: (f, 0, 0)),
            pl.BlockSpec((tf, d), lambda i, f: (f, 0))],
        out_specs=pl.BlockSpec((tm, d), lambda i, f: (i, 0)),
        out_shape=jax.ShapeDtypeStruct((m, d), F32),
        scratch_shapes=[pltpu.VMEM((tm, d), BF16)],
        compiler_params=_cparams("parallel", "arbitrary"),
        name="proj_ffn",
    )(*mixes, x, w_out, g, w_gate, w_up, w_down)


def _even_layer_prompt(x, p, batch, seq):
    z = _norm_matmul(x, p["norm_mix"], p["w_in"], TOKEN_TILE)
    tables = _rope_tables(jnp.arange(seq, dtype=jnp.int32))
    qt, k, vt, kv, kmean, gv, b_out = _even_epilogue(
        z, tables, p["ln_g"], p["ln_b"], p["w_sp"], p["b_sp"], batch, seq, EPILOGUE_TILE)
    a_out = _moba_prompt(qt, k, vt, kmean.reshape(batch, seq // MOBA_BLOCK, QKV_A), batch, seq)
    x = _proj_ffn([a_out, b_out], x, p["w_out"], p["norm_ffn"], p["w_gate"], p["w_up"], p["w_down"], TOKEN_TILE)
    n_open = seq - ((seq - 1) // CHUNK_B) * CHUNK_B
    gv_open = gv.reshape(batch, seq, D_B)[:, seq - n_open:]
    kv = jnp.transpose(kv.reshape(batch, 2, H_A, D_A, seq), (0, 4, 1, 2, 3))
    return x, kv, gv_open


def _even_params(i, ev_norm_mix, ev_w_in, ev_gmlp_ln_g, ev_gmlp_ln_b, ev_w_spatial, ev_b_spatial, ev_w_out,
                 ev_norm_ffn, ev_w_gate, ev_w_up, ev_w_down):
    return dict(norm_mix=ev_norm_mix[i][None], w_in=ev_w_in[i].astype(BF16),
                ln_g=ev_gmlp_ln_g[i][None], ln_b=ev_gmlp_ln_b[i][None],
                w_sp=ev_w_spatial[i], b_sp=ev_b_spatial[i], w_out=ev_w_out[i].astype(BF16),
                norm_ffn=ev_norm_ffn[i][None], w_gate=_col_blocks(ev_w_gate[i].astype(BF16), FFN_TF),
                w_up=_col_blocks(ev_w_up[i].astype(BF16), FFN_TF), w_down=ev_w_down[i].astype(BF16))


O_CQ, O_CK, O_CV, O_CR, O_DQKV, O_DG, O_SM = 0, 256, 512, 1024, 1536, 2560, 3072
SM_DA, SM_DB = GATE_RANK, GATE_RANK + H_D
IN_ODD_PAD = O_SM + LANE
KEY_C = H_C * DK_C
PAIR = LANE // DK_C


def _log_sigmoid(x):
    return jnp.minimum(x, 0.0) - jnp.log1p(jnp.exp(-jnp.abs(x)))


def _split_bf16(a, terms):
    parts = []
    for _ in range(terms - 1):
        hi = a.astype(BF16)
        parts.append(hi)
        a = a - hi.astype(F32)
    parts.append(a.astype(BF16))
    return parts


def _dot_select(a, sel_bf16, terms):
    parts = _split_bf16(a, terms)
    out = _dot(parts[0], sel_bf16)
    for part in parts[1:]:
        out = out + _dot(part, sel_bf16)
    return out


def _select_dot(sel_bf16, a, terms):
    parts = _split_bf16(a, terms)
    out = _dot(sel_bf16, parts[0])
    for part in parts[1:]:
        out = out + _dot(sel_bf16, part)
    return out


def _unit_lower_inverses(mats):
    c = mats[0].shape[0]
    assert c == 64
    eye = (lax.broadcasted_iota(jnp.int32, (c, c), 0) == lax.broadcasted_iota(jnp.int32, (c, c), 1)).astype(F32)
    sp = lambda ms: [_split_bf16(m, 2) for m in ms]
    mul = lambda aa, bb: [_dot(ah, bh) + _dot(ah, bl) + _dot(al, bh) for (ah, al), (bh, bl) in zip(aa, bb)]
    x1 = [-a for a in mats]
    x1s = sp(x1)
    x2 = mul(x1s, x1s)
    x2s = sp(x2)
    x4 = mul(x2s, x2s)
    x3 = mul(x1s, x2s)
    x4s = sp(x4)
    x8 = mul(x4s, x4s)
    f01 = [eye + a + b + d for a, b, d in zip(x1, x2, x3)]
    x8s = sp(x8)
    x16 = mul(x8s, x8s)
    x12 = mul(x4s, x8s)
    g23 = [a + b + d for a, b, d in zip(x4, x8, x12)]
    x16s = sp(x16)
    x32 = mul(x16s, x16s)
    f0123 = [f + d for f, d in zip(f01, mul(sp(f01), sp(g23)))]
    x48 = mul(x16s, sp(x32))
    g45 = [a + b + d for a, b, d in zip(x16, x32, x48)]
    return [f + d for f, d in zip(f0123, mul(sp(f0123), sp(g45)))]


def _group_sumsq(y, width):
    n = y.shape[1]
    same = (lax.broadcasted_iota(jnp.int32, (n, n), 0) // width
            == lax.broadcasted_iota(jnp.int32, (n, n), 1) // width).astype(BF16)
    return _dot_select(y * y, same, 2)


def _odd_mixer_prompt_kernel(z_ref, wg2_ref, bg2_ref, gnorm_ref, cw_ref, alog_ref, dtb_ref, dnorm_ref,
                             o_ref, sg_ref, sd_ref, tail_ref,
                             stg_ref, std_ref, prev_ref, la_ref, qkv_ref, dla_ref, beta_ref):
    i = pl.program_id(1)
    ns, tc = z_ref.shape[0], z_ref.shape[1]
    c = CHUNK_LIN

    @pl.when(i == 0)
    def _():
        stg_ref[...] = jnp.zeros_like(stg_ref)
        std_ref[...] = jnp.zeros_like(std_ref)
        prev_ref[...] = jnp.zeros_like(prev_ref)

    row8 = lax.broadcasted_iota(jnp.int32, (8, C_CONV), 0)
    qscale = jnp.where(lax.broadcasted_iota(jnp.int32, (1, 2 * KEY_C), 1) < KEY_C, DK_D ** -0.5, 1.0)
    for sq in range(ns):
        small = z_ref[sq, :, O_SM:O_SM + LANE]
        pre = jnp.dot(small, wg2_ref[...], precision=HI, preferred_element_type=F32) + bg2_ref[...]
        la_ref[sq] = _log_sigmoid(pre) / GATE_NORM
        dla_ref[sq] = -jnp.exp(alog_ref[...]) * _softplus(small + dtb_ref[...])
        beta_ref[sq] = jax.nn.sigmoid(small)

        x = z_ref[sq, :, O_DQKV:O_DQKV + C_CONV]
        x8 = x[0:8]
        p8 = prev_ref[sq]
        y = x * cw_ref[CONV_W - 1:CONV_W, :]
        y8 = x8 * cw_ref[CONV_W - 1:CONV_W, :]
        for s in range(1, CONV_W):
            wrow = cw_ref[CONV_W - 1 - s:CONV_W - s, :]
            y = y + pltpu.roll(x, s, 0) * wrow
            y8 = y8 + jnp.where(row8 < s, pltpu.roll(p8, s, 0), pltpu.roll(x8, s, 0)) * wrow
        prev_ref[sq] = x[tc - 8:tc]
        y = _silu(y)
        y8 = _silu(y8)
        yqk = y[:, 0:2 * KEY_C]
        nrm = lax.rsqrt(_group_sumsq(yqk, DK_D) + EPS)
        qkv_ref[sq, :, 0:2 * KEY_C] = yqk * nrm * qscale
        qkv_ref[sq, :, 2 * KEY_C:] = y[:, 2 * KEY_C:]
        yqk8 = y8[:, 0:2 * KEY_C]
        qkv_ref[sq, 0:8, 0:2 * KEY_C] = yqk8 * lax.rsqrt(_group_sumsq(yqk8, DK_D) + EPS) * qscale
        qkv_ref[sq, 0:8, 2 * KEY_C:] = y8[:, 2 * KEY_C:]

    ri = lax.broadcasted_iota(jnp.int32, (c, c), 0)
    ci = lax.broadcasted_iota(jnp.int32, (c, c), 1)
    tril = ri >= ci
    strict = ri > ci
    lower = tril.astype(BF16)
    upper = (ri <= ci).astype(BF16)
    lane_head = lax.broadcasted_iota(jnp.int32, (c, LANE), 1) // DK_C
    lane_head_row = lax.broadcasted_iota(jnp.int32, (1, LANE), 1) // DK_C

    def chunk(ch, carry):
        rows = pl.ds(pl.multiple_of(ch * c, c), c)
        n_pair = H_C // PAIR
        seqs = range(ns)
        units = [(sq, h) + divmod(h, PAIR) for sq in seqs for h in range(H_C)]
        pairs = [(sq, p) for sq in seqs for p in range(n_pair)]
        unit_at = lambda sq, p, hh: (sq * H_C) + p * PAIR + hh
        masks = [lane_head == hh for hh in range(PAIR)]
        bcum_all = [_select_dot(lower, la_ref[sq, rows, :], 3) for sq in seqs]
        dla_parts = [_split_bf16(dla_ref[sq, rows, :], 3) for sq in seqs]
        g_cols = [_dot(lower, dp[0]) + _dot(lower, dp[1]) + _dot(lower, dp[2]) for dp in dla_parts]
        g_rows = [_dot_tn(dp[0], upper) + _dot_tn(dp[1], upper) + _dot_tn(dp[2], upper) for dp in dla_parts]
        st_d = {sp: std_ref[sp[0], sp[1]] for sp in pairs}
        st_db = {sp: st_d[sp].astype(BF16) for sp in pairs}
        dn = []
        for sq, h, p, hh in units:
            g_col = jnp.broadcast_to(g_cols[sq][:, SM_DA + h:SM_DA + h + 1], (c, LANE))
            g_row = jnp.broadcast_to(g_rows[sq][SM_DA + h:SM_DA + h + 1, :], (c, c))
            beta_b = jnp.broadcast_to(beta_ref[sq, rows, SM_DB + h:SM_DB + h + 1], (c, LANE))
            km = jnp.where(masks[hh], qkv_ref[sq, rows, KEY_C + p * LANE:KEY_C + (p + 1) * LANE], 0.0)
            qm = jnp.where(masks[hh], qkv_ref[sq, rows, p * LANE:(p + 1) * LANE], 0.0)
            dn.append(dict(g_col=g_col, g_row=g_row, beta_b=beta_b, km=km, qm=qm, kmb=km.astype(BF16),
                           kb=km * beta_b, eg=jnp.exp(g_col), g_last=g_col[c - 1:c, :]))
        kk = [_dot_nt(d["kb"].astype(BF16), d["kmb"]) for d in dn]
        qk = [_dot_nt(d["qm"].astype(BF16), d["kmb"]) for d in dn]
        d_inter = [_dot_nt((d["qm"] * d["eg"]).astype(BF16), st_db[(sq, p)]) for d, (sq, h, p, hh) in zip(dn, units)]
        decay = [jnp.exp(jnp.where(tril, d["g_col"][:, 0:c] - d["g_row"], NEG)) for d in dn]
        a_mats = [jnp.where(strict, m * dc, 0.0) for m, dc in zip(kk, decay)]
        st_g = {sp: stg_ref[sp[0], sp[1]] for sp in pairs}
        st_gb = {sp: st_g[sp].astype(BF16) for sp in pairs}
        gl_pair = {}
        for sq, p in pairs:
            bcum = bcum_all[sq][:, p * LANE:(p + 1) * LANE]
            b_end = bcum[c - 1:c, :]
            k = z_ref[sq, rows, O_CK + p * LANE:O_CK + (p + 1) * LANE]
            gl_pair[(sq, p)] = dict(
                q_in=z_ref[sq, rows, O_CQ + p * LANE:O_CQ + (p + 1) * LANE] * (DK_C ** -0.5) * jnp.exp(bcum),
                k_in=(k * jnp.exp(-bcum)).astype(BF16), k_end=k * jnp.exp(b_end - bcum), b_end=b_end)
        g_qm = [jnp.where(masks[hh], gl_pair[(sq, p)]["q_in"], 0.0).astype(BF16) for sq, h, p, hh in units]
        g_ke = [jnp.where(masks[hh], gl_pair[(sq, p)]["k_end"], 0.0).astype(BF16) for sq, h, p, hh in units]
        g_vb = [z_ref[sq, rows, O_CV + h * DV_C:O_CV + (h + 1) * DV_C].astype(BF16) for sq, h, p, hh in units]
        g_sc = [_dot_nt(g_qm[u], gl_pair[(sq, p)]["k_in"]) for u, (sq, h, p, hh) in enumerate(units)]
        g_inter = [_dot_nt(g_qm[u], st_gb[(sq, p)]) for u, (sq, h, p, hh) in enumerate(units)]
        g_upd = [_dot_tn(g_vb[u], g_ke[u]) for u in range(len(units))]
        g_intra = [_dot(jnp.where(tril, g_sc[u], 0.0).astype(BF16), g_vb[u]) for u in range(len(units))]
        t_invs = _unit_lower_inverses(a_mats)
        for u, (sq, h, p, hh) in enumerate(units):
            gate = _silu(z_ref[sq, rows, O_CR + h * DV_C:O_CR + (h + 1) * DV_C])
            o_ref[sq, rows, h * DV_C:(h + 1) * DV_C] = (
                _rms(g_intra[u] + g_inter[u], gnorm_ref[...]) * gate).astype(BF16)
        for sq, p in pairs:
            stg_ref[sq, p] = (st_g[(sq, p)] * jnp.exp(gl_pair[(sq, p)]["b_end"])
                              + g_upd[unit_at(sq, p, 0)] + g_upd[unit_at(sq, p, 1)])
        rhs = [jnp.concatenate([qkv_ref[sq, rows, 2 * KEY_C + h * DV_D:2 * KEY_C + (h + 1) * DV_D] * d["beta_b"],
                                d["kb"] * d["eg"]], axis=1).astype(BF16) for d, (sq, h, p, hh) in zip(dn, units)]
        sol = [_dot(t.astype(BF16), r) for t, r in zip(t_invs, rhs)]
        w_st = [_dot_nt(s[:, DV_D:].astype(BF16), st_db[(sq, p)]) for s, (sq, h, p, hh) in zip(sol, units)]
        v_new = [(s[:, 0:DV_D] - ws).astype(BF16) for s, ws in zip(sol, w_st)]
        d_intra = [_dot(jnp.where(tril, q * dc, 0.0).astype(BF16), vn) for q, dc, vn in zip(qk, decay, v_new)]
        d_upd = [_dot_tn(vn, (d["km"] * jnp.exp(d["g_last"] - d["g_col"])).astype(BF16)) for vn, d in zip(v_new, dn)]
        for u, (sq, h, p, hh) in enumerate(units):
            gate = _silu(z_ref[sq, rows, O_DG + h * DV_D:O_DG + (h + 1) * DV_D])
            col = H_C * DV_C + h * DV_D
            o_ref[sq, rows, col:col + DV_D] = (_rms(d_inter[u] + d_intra[u], dnorm_ref[...]) * gate).astype(BF16)
        for sq, p in pairs:
            u0, u1 = unit_at(sq, p, 0), unit_at(sq, p, 1)
            dec_row = jnp.where(lane_head_row == 0, jnp.exp(dn[u0]["g_last"]), jnp.exp(dn[u1]["g_last"]))
            std_ref[sq, p] = st_d[(sq, p)] * dec_row + d_upd[u0] + d_upd[u1]
        return carry

    lax.fori_loop(0, tc // c, chunk, 0)

    @pl.when(i == pl.num_programs(1) - 1)
    def _():
        for sq in range(ns):
            tail_ref[sq] = prev_ref[sq]
            for p in range(H_C // PAIR):
                tg = stg_ref[sq, p].T
                td = std_ref[sq, p].T
                for hh in range(PAIR):
                    sg_ref[sq, p * PAIR + hh] = tg[hh * DK_C:(hh + 1) * DK_C, :]
                    sd_ref[sq, p * PAIR + hh] = td[hh * DK_D:(hh + 1) * DK_D, :]


def _odd_mixer_prompt(z, p, batch, seq, tc):
    nt = seq // tc
    ns = 2 if batch % 2 == 0 else 1
    full = lambda a: pl.BlockSpec(a.shape, lambda b, i: (0,) * a.ndim)
    consts = [p["w_g2_pad"], p["b_g2"], p["gla_norm"], p["conv_w"], p["alog_slab"], p["dtb_slab"], p["delta_norm"]]
    st_spec = pl.BlockSpec((ns, H_C, DK_C, DV_C), lambda b, i: (b, 0, 0, 0))
    mix, s_gla, s_delta, tail = pl.pallas_call(
        _odd_mixer_prompt_kernel,
        grid=(batch // ns, nt),
        in_specs=[pl.BlockSpec((ns, tc, IN_ODD_PAD), lambda b, i: (b, i, 0))] + [full(a) for a in consts],
        out_specs=[pl.BlockSpec((ns, tc, D_MODEL), lambda b, i: (b, i, 0)), st_spec, st_spec,
                   pl.BlockSpec((ns, 8, C_CONV), lambda b, i: (b, 0, 0))],
        out_shape=[jax.ShapeDtypeStruct((batch, seq, D_MODEL), BF16),
                   jax.ShapeDtypeStruct((batch, H_C, DK_C, DV_C), F32),
                   jax.ShapeDtypeStruct((batch, H_D, DK_D, DV_D), F32),
                   jax.ShapeDtypeStruct((batch, 8, C_CONV), F32)],
        scratch_shapes=[pltpu.VMEM((ns, H_C // PAIR, DV_C, LANE), F32),
                        pltpu.VMEM((ns, H_D // PAIR, DV_D, LANE), F32),
                        pltpu.VMEM((ns, 8, C_CONV), F32),
                        pltpu.VMEM((ns, tc, KEY_C), F32),
                        pltpu.VMEM((ns, tc, C_CONV), F32),
                        pltpu.VMEM((ns, tc, LANE), F32),
                        pltpu.VMEM((ns, tc, LANE), F32)],
        compiler_params=_cparams("parallel", "arbitrary"),
        name="odd_mixer_prompt",
    )(z.reshape(batch, seq, IN_ODD_PAD), *consts)
    return mix.reshape(batch * seq, D_MODEL), s_gla, s_delta, tail


M_E1, M_E2, M_R1, M_R2, M_G1, M_G2 = (N_EXPERTS + j for j in range(6))


def _proj_router_kernel(mix_ref, x_ref, wo_ref, g_ref, wr_ref, x1_ref, hn_ref, meta_ref, cnt_ref, carry_ref):
    tm = x_ref.shape[0]

    @pl.when(pl.program_id(0) == 0)
    def _():
        carry_ref[...] = jnp.zeros_like(carry_ref)

    x1 = x_ref[...] + _dot(mix_ref[...], wo_ref[...])
    x1_ref[...] = x1
    hn = _rms(x1, g_ref[...])
    hn_ref[...] = hn.astype(hn_ref.dtype)
    lane = lax.broadcasted_iota(jnp.int32, (tm, LANE), 1)
    hn_hi, hn_lo = _split_bf16(hn, 2)
    wr_hi, wr_lo = _split_bf16(wr_ref[...], 2)
    logits = _dot(hn_hi, wr_hi) + _dot(hn_hi, wr_lo) + _dot(hn_lo, wr_hi)
    logits = jnp.where(lane < N_EXPERTS, logits, NEG)
    m1 = jnp.max(logits, axis=1, keepdims=True)
    e1 = jnp.min(jnp.where(logits == m1, lane, LANE), axis=1, keepdims=True)
    rest = jnp.where(lane == e1, NEG, logits)
    m2 = jnp.max(rest, axis=1, keepdims=True)
    e2 = jnp.min(jnp.where(rest == m2, lane, LANE), axis=1, keepdims=True)
    t = jnp.exp(m2 - m1)
    g1 = 1.0 / (1.0 + t)
    g2 = t / (1.0 + t)
    oh1 = lane == e1
    oh2 = lane == e2
    member = (oh1 | oh2).astype(F32)
    ri = lax.broadcasted_iota(jnp.int32, (tm, tm), 0)
    ci = lax.broadcasted_iota(jnp.int32, (tm, tm), 1)
    before = _dot((ri > ci).astype(BF16), member.astype(BF16)) + carry_ref[...]
    r1 = jnp.sum(jnp.where(oh1, before, 0.0), axis=1, keepdims=True)
    r2 = jnp.sum(jnp.where(oh2, before, 0.0), axis=1, keepdims=True)
    carry_ref[...] = carry_ref[...] + jnp.sum(member, axis=0, keepdims=True)
    cnt_ref[...] = carry_ref[...]
    meta = jnp.where(oh1, g1, 0.0) + jnp.where(oh2, g2, 0.0)
    meta = jnp.where(lane == M_E1, e1.astype(F32), meta)
    meta = jnp.where(lane == M_E2, e2.astype(F32), meta)
    meta = jnp.where(lane == M_R1, r1, meta)
    meta = jnp.where(lane == M_R2, r2, meta)
    meta = jnp.where(lane == M_G1, g1, meta)
    meta = jnp.where(lane == M_G2, g2, meta)
    meta_ref[...] = meta


def _proj_router(mix, x, w_out, g, w_router_pad, tm, hn_dtype):
    m, d = x.shape
    row = lambda width: pl.BlockSpec((tm, width), lambda i: (i, 0))
    full = lambda a: pl.BlockSpec(a.shape, lambda i: (0,) * a.ndim)
    return pl.pallas_call(
        _proj_router_kernel,
        grid=(m // tm,),
        in_specs=[row(d), row(d), full(w_out), full(g), full(w_router_pad)],
        out_specs=[row(d), row(d), row(LANE), pl.BlockSpec((1, LANE), lambda i: (0, 0))],
        out_shape=[jax.ShapeDtypeStruct((m, d), F32), jax.ShapeDtypeStruct((m, d), hn_dtype),
                   jax.ShapeDtypeStruct((m, LANE), F32), jax.ShapeDtypeStruct((1, LANE), F32)],
        scratch_shapes=[pltpu.VMEM((1, LANE), F32)],
        compiler_params=_cparams("arbitrary"),
        name="proj_router",
    )(mix, x, w_out, g, w_router_pad)


def _moe_dense_kernel(hn_ref, x1_ref, meta_ref, wg_ref, wu_ref, wd_ref, gf_ref, o_ref):
    e = pl.program_id(1)
    f = pl.program_id(2)

    @pl.when((e == 0) & (f == 0))
    def _():
        o_ref[...] = x1_ref[...]

    h = hn_ref[...]
    act = (_silu(_dot(h, wg_ref[...].astype(BF16))) * _dot(h, wu_ref[...].astype(BF16))).astype(BF16)
    lane = lax.broadcasted_iota(jnp.int32, meta_ref.shape, 1)
    gate = jnp.sum(jnp.where(lane == e, meta_ref[...], 0.0), axis=1, keepdims=True)
    o_ref[...] += gate * _dot(act, wd_ref[...].astype(BF16))

    @pl.when((e == pl.num_programs(1) - 1) & (f == pl.num_programs(2) - 1))
    def _():
        o_ref[...] = _rms(o_ref[...], gf_ref[...])


def _moe_dense(hn, x1, meta, w_gate, w_up, w_down, g_final, tm):
    m, d = x1.shape
    n_e, _, ff = w_gate.shape
    tf = MOE_TF
    row = lambda width: pl.BlockSpec((tm, width), lambda i, e, f: (i, 0))
    return pl.pallas_call(
        _moe_dense_kernel,
        grid=(m // tm, n_e, ff // tf),
        in_specs=[row(d), row(d), row(LANE),
                  pl.BlockSpec((None, d, tf), lambda i, e, f: (e, 0, f)),
                  pl.BlockSpec((None, d, tf), lambda i, e, f: (e, 0, f)),
                  pl.BlockSpec((None, tf, d), lambda i, e, f: (e, f, 0)),
                  pl.BlockSpec((1, d), lambda i, e, f: (0, 0))],
        out_specs=row(d),
        out_shape=jax.ShapeDtypeStruct((m, d), F32),
        compiler_params=_cparams("parallel", "arbitrary", "arbitrary"),
        name="moe_dense",
    )(hn, x1, meta, w_gate, w_up, w_down, g_final)


def _moe_dispatch_kernel(dest_ref, hn_ref, xs_in_ref, xs_ref, sem):
    del xs_in_ref
    tm = hn_ref.shape[0]
    base = pl.program_id(0) * (TOP_K * tm)

    def row_copy(t, k):
        return pltpu.make_async_copy(hn_ref.at[pl.ds(t, 1)], xs_ref.at[pl.ds(dest_ref[base + k * tm + t], 1)], sem)

    def issue(t, carry):
        for k in range(TOP_K):
            row_copy(t, k).start(priority=k % 2)
        return carry

    def drain(t, carry):
        for k in range(TOP_K):
            row_copy(t, k).wait()
        return carry

    lax.fori_loop(0, tm, issue, 0, unroll=DMA_UNROLL)
    lax.fori_loop(0, tm, drain, 0, unroll=DMA_UNROLL)


def _moe_dispatch(dest, hn, n_rows, tm):
    m, d = hn.shape
    return pl.pallas_call(
        _moe_dispatch_kernel,
        grid_spec=pltpu.PrefetchScalarGridSpec(
            num_scalar_prefetch=1,
            grid=(m // tm,),
            in_specs=[pl.BlockSpec((tm, d), lambda i, dest: (i, 0)), pl.BlockSpec(memory_space=pl.ANY)],
            out_specs=pl.BlockSpec(memory_space=pl.ANY),
            scratch_shapes=[pltpu.SemaphoreType.DMA(())]),
        out_shape=jax.ShapeDtypeStruct((n_rows, d), hn.dtype),
        input_output_aliases={2: 0},
        compiler_params=_cparams("arbitrary"),
        name="moe_dispatch",
    )(dest, hn, jnp.zeros((n_rows, d), hn.dtype))


def _moe_grouped_kernel(te_ref, tr_ref, xs_ref, wg_ref, wu_ref, wd_ref, o_ref, xb_ref):
    del te_ref
    f = pl.program_id(1)
    used = tr_ref[pl.program_id(0)]
    half = xs_ref.shape[0] // 2

    @pl.when(f == 0)
    def _():
        o_ref[...] = jnp.zeros_like(o_ref)
        xb_ref[...] = xs_ref[...].astype(BF16)

    def swiglu_rows(rows):
        h = xb_ref[rows, :]
        act = (_silu(_dot(h, wg_ref[...].astype(BF16))) * _dot(h, wu_ref[...].astype(BF16))).astype(BF16)
        o_ref[rows, :] += _dot(act, wd_ref[...].astype(BF16))

    @pl.when(used > half)
    def _():
        swiglu_rows(slice(None))

    @pl.when((used > 0) & (used <= half))
    def _():
        swiglu_rows(slice(0, half))


def _moe_grouped(tile_expert, tile_rows, xs, w_gate, w_up, w_down, tg):
    rows, d = xs.shape
    tf = MOE_TF
    nf = w_gate.shape[2] // tf
    fidx = lambda r, f, tr: jnp.where(tr[r] > 0, f, nf - 1)
    return pl.pallas_call(
        _moe_grouped_kernel,
        grid_spec=pltpu.PrefetchScalarGridSpec(
            num_scalar_prefetch=2,
            grid=(rows // tg, nf),
            in_specs=[pl.BlockSpec((tg, d), lambda r, f, te, tr: (r, 0)),
                      pl.BlockSpec((None, d, tf), lambda r, f, te, tr: (te[r], 0, fidx(r, f, tr))),
                      pl.BlockSpec((None, d, tf), lambda r, f, te, tr: (te[r], 0, fidx(r, f, tr))),
                      pl.BlockSpec((None, tf, d), lambda r, f, te, tr: (te[r], fidx(r, f, tr), 0))],
            out_specs=pl.BlockSpec((tg, d), lambda r, f, te, tr: (r, 0)),
            scratch_shapes=[pltpu.VMEM((tg, d), BF16)]),
        out_shape=jax.ShapeDtypeStruct((rows, d), F32),
        compiler_params=_cparams("parallel", "arbitrary"),
        name="moe_grouped",
    )(tile_expert, tile_rows, xs, w_gate, w_up, w_down)


def _moe_combine_kernel(dest_ref, ys_ref, x1_ref, meta_ref, gf_ref, o_ref, buf_ref, sem):
    i = pl.program_id(0)
    tc = x1_ref.shape[0]
    rows = TOP_K * tc

    def row_copy(step, slot, j):
        return pltpu.make_async_copy(ys_ref.at[pl.ds(dest_ref[step * rows + j], 1)],
                                     buf_ref.at[slot, pl.ds(j, 1)], sem.at[slot])

    def issue(step, slot):
        def body(g, carry):
            for u in range(2):
                row_copy(step, slot, 2 * g + u).start(priority=u)
            return carry
        lax.fori_loop(0, rows // 2, body, 0, unroll=DMA_UNROLL // 2)

    def drain(step, slot):
        def body(j, carry):
            row_copy(step, slot, j).wait()
            return carry
        lax.fori_loop(0, rows, body, 0, unroll=DMA_UNROLL)

    slot = i % 2

    @pl.when(i == 0)
    def _():
        issue(0, 0)

    @pl.when(i + 1 < pl.num_programs(0))
    def _():
        issue(i + 1, 1 - slot)

    drain(i, slot)
    g1 = meta_ref[:, M_G1:M_G1 + 1]
    g2 = meta_ref[:, M_G2:M_G2 + 1]
    y = x1_ref[...] + g1 * buf_ref[slot, 0:tc] + g2 * buf_ref[slot, tc:rows]
    o_ref[...] = _rms(y, gf_ref[...])


def _moe_combine(dest, ys, x1, meta, g_final, tc):
    m, d = x1.shape
    row = lambda width: pl.BlockSpec((tc, width), lambda i, dest: (i, 0))
    return pl.pallas_call(
        _moe_combine_kernel,
        grid_spec=pltpu.PrefetchScalarGridSpec(
            num_scalar_prefetch=1,
            grid=(m // tc,),
            in_specs=[pl.BlockSpec(memory_space=pl.ANY), row(d), row(LANE),
                      pl.BlockSpec((1, d), lambda i, dest: (0, 0))],
            out_specs=row(d),
            scratch_shapes=[pltpu.VMEM((2, TOP_K * tc, d), F32), pltpu.SemaphoreType.DMA((2,))]),
        out_shape=jax.ShapeDtypeStruct((m, d), F32),
        compiler_params=_cparams("arbitrary"),
        name="moe_combine",
    )(dest, ys, x1, meta, g_final)


def _moe_routes(meta, counts, tile, tg):
    m = meta.shape[0]
    n_rows = TOP_K * m + N_EXPERTS * tg
    expert = meta[:, M_E1:M_E2 + 1].astype(jnp.int32)
    rank = meta[:, M_R1:M_R2 + 1].astype(jnp.int32)
    count = counts[0, :N_EXPERTS].astype(jnp.int32)
    padded = (count + tg - 1) // tg * tg
    ends = jnp.cumsum(padded)
    dest = (ends - padded)[expert] + rank
    dest = jnp.transpose(dest.reshape(m // tile, tile, TOP_K), (0, 2, 1)).reshape(-1)
    tile_start = jnp.arange(n_rows // tg, dtype=jnp.int32) * tg
    probe = jnp.minimum(tile_start, ends[-1] - 1)
    tile_expert = jnp.sum(ends[None, :] <= probe[:, None], axis=1).astype(jnp.int32)
    tile_rows = jnp.clip((ends - padded + count)[tile_expert] - tile_start, 0, tg).astype(jnp.int32)
    return dest, tile_expert, tile_rows, n_rows


def _odd_params(i, od_norm_mix, od_w_in, od_gla_w_gate2, od_gla_b_gate2, od_gla_norm, od_delta_conv,
                od_delta_a_log, od_delta_dt_bias, od_delta_norm, od_w_out, od_norm_ffn, od_router,
                od_w_gate, od_w_up, od_w_down):
    w = od_w_in[i]
    sizes = (KEY_C, KEY_C, H_C * DV_C, GATE_RANK, H_C * DV_C, C_CONV, H_D, H_D, H_D * DV_D)
    splits = tuple(sum(sizes[:j + 1]) for j in range(len(sizes) - 1))
    cq, ck, cv, c_lr, c_r, d_qkv, d_a, d_b, d_g = jnp.split(w, splits, axis=1)
    pad = jnp.zeros((w.shape[0], IN_ODD_PAD - w.shape[1]), w.dtype)
    w_in = jnp.concatenate([cq, ck, cv, c_r, d_qkv, d_g, c_lr, d_a, d_b, pad], axis=1).astype(BF16)
    slab = lambda v: jnp.zeros((1, LANE), F32).at[0, SM_DA:SM_DA + H_D].set(v)
    return dict(norm_mix=od_norm_mix[i][None], w_in=w_in,
                w_g2_pad=jnp.zeros((LANE, KEY_C), F32).at[:GATE_RANK].set(od_gla_w_gate2[i]),
                b_g2=od_gla_b_gate2[i][None], gla_norm=od_gla_norm[i][None], conv_w=od_delta_conv[i],
                alog_slab=slab(od_delta_a_log[i]), dtb_slab=slab(od_delta_dt_bias[i]),
                a_log=od_delta_a_log[i], dt_bias=od_delta_dt_bias[i],
                delta_norm=od_delta_norm[i][None], w_out=od_w_out[i].astype(BF16),
                norm_ffn=od_norm_ffn[i][None],
                router=jnp.zeros((D_MODEL, LANE), F32).at[:, :N_EXPERTS].set(od_router[i]),
                w_gate=od_w_gate[i], w_up=od_w_up[i], w_down=od_w_down[i])


MOE_GROUP_TILE = 1024
MOE_ROW_TILE = 512


def _moe(mix, x, p, g_final, tm):
    m = x.shape[0]
    if m < N_EXPERTS * MOE_GROUP_TILE:
        x1, hn, meta, _ = _proj_router(mix, x, p["w_out"], p["norm_ffn"], p["router"], tm, BF16)
        return _moe_dense(hn, x1, meta, p["w_gate"], p["w_up"], p["w_down"], g_final, tm)
    x1, hn, meta, counts = _proj_router(mix, x, p["w_out"], p["norm_ffn"], p["router"], tm, F32)
    dest, tile_expert, tile_rows, n_rows = _moe_routes(meta, counts, MOE_ROW_TILE, MOE_GROUP_TILE)
    xs = _moe_dispatch(dest, hn, n_rows, MOE_ROW_TILE)
    ys = _moe_grouped(tile_expert, tile_rows, xs, p["w_gate"], p["w_up"], p["w_down"], MOE_GROUP_TILE)
    return _moe_combine(dest, ys, x1, meta, g_final, MOE_ROW_TILE)


def _odd_layer_prompt(x, p, g_final, batch, seq):
    z = _norm_matmul(x, p["norm_mix"], p["w_in"], TOKEN_TILE)
    mix, s_gla, s_delta, tail = _odd_mixer_prompt(z, p, batch, seq, MIXER_TILE)
    y = _moe(mix, x, p, g_final, TOKEN_TILE)
    return y, s_gla, s_delta, tail[:, 8 - (CONV_W - 1):]


def _even_epilogue_sample_kernel(z_ref, c_ref, su_ref, sd_ref, lng_ref, lnb_ref, w0_ref, b0_ref,
                                 q_ref, kv_ref, gv_ref, bout_ref):
    c, su, sd = c_ref[...], su_ref[...], sd_ref[...]
    q_ref[...] = _rope(z_ref[:, 0:QKV_A], c, su, sd) * (D_A ** -0.5)
    kv_ref[:, 0:QKV_A] = _rope(z_ref[:, QKV_A:2 * QKV_A], c, su, sd)
    kv_ref[:, QKV_A:2 * QKV_A] = z_ref[:, 2 * QKV_A:3 * QKV_A]
    u = _gelu(z_ref[:, 3 * QKV_A:3 * QKV_A + D_B])
    gv = _layernorm(_gelu(z_ref[:, 3 * QKV_A + D_B:3 * QKV_A + 2 * D_B]), lng_ref[...], lnb_ref[...])
    gv_ref[...] = gv
    bout_ref[...] = (u * (gv * w0_ref[...] + b0_ref[...])).astype(BF16)


def _even_epilogue_sample(z, tables, ln_g, ln_b, w_sp, b_sp):
    m = z.shape[0]
    w0 = jnp.repeat(w_sp[:, 0, 0], D_BG)[None]
    b0 = jnp.repeat(b_sp[:, 0], D_BG)[None]
    return pl.pallas_call(
        _even_epilogue_sample_kernel,
        out_shape=[jax.ShapeDtypeStruct((m, QKV_A), F32), jax.ShapeDtypeStruct((m, 2 * QKV_A), F32),
                   jax.ShapeDtypeStruct((m, D_B), F32), jax.ShapeDtypeStruct((m, D_B), BF16)],
        compiler_params=pltpu.CompilerParams(vmem_limit_bytes=VMEM_LIMIT),
        name="even_epilogue_sample",
    )(z, *tables, ln_g, ln_b, w0, b0)


def _moba_sample_kernel(pt_ref, qt_ref, knt_ref, vnt_ref, *refs):
    del pt_ref
    n_pages = len(refs) - 2
    page_refs, o_ref, s_ref = refs[:n_pages], refs[n_pages], refs[n_pages + 1]
    pages_per_block = MOBA_BLOCK // PAGE_SIZE
    nb = n_pages // pages_per_block
    qt = qt_ref[0]
    for h in range(H_A):
        qcol = jnp.broadcast_to(qt[:, h:h + 1], (D_A, PAGE_SIZE))
        for j in range(n_pages):
            s_ref[h, j:j + 1, :] = jnp.sum(page_refs[j][0, 0, h] * qcol, axis=0, keepdims=True)
    lane = lax.broadcasted_iota(jnp.int32, (n_pages, LANE), 1)
    page_sums = jnp.zeros((n_pages, LANE), F32)
    for h in range(H_A):
        page_sums = jnp.where(lane == h, jnp.sum(s_ref[h], axis=1, keepdims=True), page_sums)
    pair = (lax.broadcasted_iota(jnp.int32, (nb, n_pages), 1) // pages_per_block
            == lax.broadcasted_iota(jnp.int32, (nb, n_pages), 0)).astype(F32)
    pair_t = (lax.broadcasted_iota(jnp.int32, (n_pages, nb), 0) // pages_per_block
              == lax.broadcasted_iota(jnp.int32, (n_pages, nb), 1)).astype(F32)
    gate = jnp.dot(pair, page_sums, precision=HI, preferred_element_type=F32)
    sel = _moba_select(gate, nb)
    sel_pages = jnp.dot(pair_t, sel, precision=HI, preferred_element_type=F32)
    own = jnp.sum(qt * knt_ref[0], axis=0, keepdims=True)
    vnt = vnt_ref[0]
    out_lane = lax.broadcasted_iota(jnp.int32, (D_A, LANE), 1)
    out = jnp.zeros((D_A, LANE), F32)
    for h in range(H_A):
        sm = jnp.where(sel_pages[:, h:h + 1] > 0.0, s_ref[h], NEG)
        s_own = own[:, h:h + 1]
        mx = jnp.maximum(jnp.max(jnp.max(sm, axis=1, keepdims=True), axis=0, keepdims=True), s_own)
        p = jnp.exp(sm - mx)
        p_own = jnp.exp(s_own - mx)
        denom = jnp.sum(jnp.sum(p, axis=1, keepdims=True), axis=0, keepdims=True) + p_own
        acc = jnp.zeros((D_A, PAGE_SIZE), F32)
        for j in range(n_pages):
            acc = acc + page_refs[j][0, 1, h] * p[j:j + 1, :]
        o = (jnp.sum(acc, axis=1, keepdims=True) + p_own * vnt[:, h:h + 1]) / denom
        out = jnp.where(out_lane == h, o, out)
    o_ref[0] = out


def _moba_sample(q, kv_new, cache, page_table):
    bs, n_pages = page_table.shape
    assert (n_pages * PAGE_SIZE) % MOBA_BLOCK == 0
    cache_t = jnp.transpose(cache, (0, 2, 3, 4, 1))
    page_spec = lambda j: pl.BlockSpec((1, 2, H_A, D_A, PAGE_SIZE),
                                       lambda b, pt: (pt[b * n_pages + j], 0, 0, 0, 0))
    col_spec = pl.BlockSpec((1, D_A, H_A), lambda b, pt: (b, 0, 0))
    heads_t = lambda a: jnp.transpose(a.reshape(bs, H_A, D_A), (0, 2, 1))
    out = pl.pallas_call(
        _moba_sample_kernel,
        grid_spec=pltpu.PrefetchScalarGridSpec(
            num_scalar_prefetch=1,
            grid=(bs,),
            in_specs=[col_spec, col_spec, col_spec] + [page_spec(j) for j in range(n_pages)],
            out_specs=pl.BlockSpec((1, D_A, LANE), lambda b, pt: (b, 0, 0)),
            scratch_shapes=[pltpu.VMEM((H_A, n_pages, PAGE_SIZE), F32)]),
        out_shape=jax.ShapeDtypeStruct((bs, D_A, LANE), F32),
        compiler_params=_cparams("parallel"),
        name="moba_sample",
    )(page_table.reshape(-1), heads_t(q), heads_t(kv_new[:, 0:QKV_A]), heads_t(kv_new[:, QKV_A:]),
      *([cache_t] * n_pages))
    return jnp.transpose(out[:, :, 0:H_A], (0, 2, 1)).reshape(bs, QKV_A).astype(BF16)


def _odd_mixer_sample_kernel(z_ref, buf_ref, sg_ref, sd_ref, wg2_ref, bg2_ref, gnorm_ref, cw_ref, alog_ref, dtb_ref,
                             dnorm_ref, o_ref, sgo_ref, sdo_ref):
    bt = z_ref.shape[0]
    stride = H_C * DK_C
    small = z_ref[:, O_SM:O_SM + LANE]
    pre = jnp.dot(small, wg2_ref[...], precision=HI, preferred_element_type=F32) + bg2_ref[...]
    a_all = jnp.exp(_log_sigmoid(pre) / GATE_NORM)
    g_all = -jnp.exp(alog_ref[...]) * _softplus(small + dtb_ref[...])
    beta_all = jax.nn.sigmoid(small)
    x = z_ref[:, O_DQKV:O_DQKV + C_CONV]
    y = x * cw_ref[CONV_W - 1:CONV_W, :]
    for j in range(CONV_W - 1):
        y = y + buf_ref[:, j, :] * cw_ref[j:j + 1, :]
    y = _silu(y)
    for h in range(H_C):
        ks = slice(h * DK_C, (h + 1) * DK_C)
        a = a_all[:, ks]
        q = z_ref[:, O_CQ + h * DK_C:O_CQ + (h + 1) * DK_C] * (DK_C ** -0.5)
        k = z_ref[:, O_CK + h * DK_C:O_CK + (h + 1) * DK_C]
        v = z_ref[:, O_CV + h * DV_C:O_CV + (h + 1) * DV_C]
        qa = q * a
        acc = jnp.sum(q * k, axis=1, keepdims=True) * v
        for kk in range(DK_C):
            rows = pl.ds(h * DK_C + kk, bt, stride=stride)
            srow = sg_ref[rows, :]
            acc = acc + qa[:, kk:kk + 1] * srow
            sgo_ref[rows, :] = a[:, kk:kk + 1] * srow + k[:, kk:kk + 1] * v
        gate = _silu(z_ref[:, O_CR + h * DV_C:O_CR + (h + 1) * DV_C])
        o_ref[:, h * DV_C:(h + 1) * DV_C] = (_rms(acc, gnorm_ref[...]) * gate).astype(BF16)
        yq = y[:, h * DK_D:(h + 1) * DK_D]
        yk = y[:, KEY_C + h * DK_D:KEY_C + (h + 1) * DK_D]
        dv = y[:, 2 * KEY_C + h * DV_D:2 * KEY_C + (h + 1) * DV_D]
        dq = yq * lax.rsqrt(jnp.sum(yq * yq, axis=1, keepdims=True) + EPS) * (DK_D ** -0.5)
        dk = yk * lax.rsqrt(jnp.sum(yk * yk, axis=1, keepdims=True) + EPS)
        beta = beta_all[:, SM_DB + h:SM_DB + h + 1]
        eg = jnp.exp(g_all[:, SM_DA + h:SM_DA + h + 1])
        w = dk * (beta * eg)
        qd = dq * eg
        ws = jnp.zeros((bt, DV_D), F32)
        qs = jnp.zeros((bt, DV_D), F32)
        for kk in range(DK_D):
            srow = sd_ref[pl.ds(h * DK_D + kk, bt, stride=stride), :]
            ws = ws + w[:, kk:kk + 1] * srow
            qs = qs + qd[:, kk:kk + 1] * srow
        v_new = dv * beta - ws
        o = qs + jnp.sum(dq * dk, axis=1, keepdims=True) * v_new
        for kk in range(DK_D):
            rows = pl.ds(h * DK_D + kk, bt, stride=stride)
            sdo_ref[rows, :] = sd_ref[rows, :] * eg + dk[:, kk:kk + 1] * v_new
        gate = _silu(z_ref[:, O_DG + h * DV_D:O_DG + (h + 1) * DV_D])
        col = H_C * DV_C + h * DV_D
        o_ref[:, col:col + DV_D] = (_rms(o, dnorm_ref[...]) * gate).astype(BF16)


def _odd_mixer_sample(z, conv_buf, s_gla, s_delta, p, bt):
    bs = z.shape[0]
    rows = H_C * DK_C
    full = lambda a: pl.BlockSpec(a.shape, lambda i: (0,) * a.ndim)
    consts = [p["w_g2_pad"], p["b_g2"], p["gla_norm"], p["conv_w"], p["alog_slab"], p["dtb_slab"], p["delta_norm"]]
    st_spec = pl.BlockSpec((bt * rows, DV_C), lambda i: (i, 0))
    mix, sg, sd = pl.pallas_call(
        _odd_mixer_sample_kernel,
        grid=(bs // bt,),
        in_specs=[pl.BlockSpec((bt, IN_ODD_PAD), lambda i: (i, 0)),
                  pl.BlockSpec((bt, CONV_W - 1, C_CONV), lambda i: (i, 0, 0)),
                  st_spec, st_spec] + [full(a) for a in consts],
        out_specs=[pl.BlockSpec((bt, D_MODEL), lambda i: (i, 0)), st_spec, st_spec],
        out_shape=[jax.ShapeDtypeStruct((bs, D_MODEL), BF16),
                   jax.ShapeDtypeStruct((bs * rows, DV_C), F32),
                   jax.ShapeDtypeStruct((bs * rows, DV_D), F32)],
        compiler_params=_cparams("parallel"),
        name="odd_mixer_sample",
    )(z, conv_buf, s_gla.reshape(bs * rows, DV_C), s_delta.reshape(bs * rows, DV_D), *consts)
    return mix, sg.reshape(s_gla.shape), sd.reshape(s_delta.shape)


def _sample_step(x, cache, page_table, s_gla, s_delta, conv_buf, ev, od, g_final):
    bs = x.shape[0]
    past = page_table.shape[1] * PAGE_SIZE
    z = _norm_matmul(x, ev["norm_mix"], ev["w_in"], bs)
    tables = _rope_tables(jnp.full((1,), past, jnp.int32))
    q, kv, gv, b_out = _even_epilogue_sample(z, tables, ev["ln_g"], ev["ln_b"], ev["w_sp"], ev["b_sp"])
    a_out = _moba_sample(q, kv, cache, page_table)
    x = _proj_ffn([a_out, b_out], x, ev["w_out"], ev["norm_ffn"], ev["w_gate"], ev["w_up"], ev["w_down"], bs)
    z = _norm_matmul(x, od["norm_mix"], od["w_in"], bs)
    mix, sg, sd = _odd_mixer_sample(z, conv_buf, s_gla, s_delta, od, SAMPLE_MIXER_TILE)
    conv_new = jnp.concatenate([conv_buf[:, 1:], z[:, None, O_DQKV:O_DQKV + C_CONV]], axis=1)
    y = _moe(mix, x, od, g_final, bs)
    return y, kv, gv, sg, sd, conv_new


def kernel(x_prompt, x_sample, cache_kv, state_gla, state_delta, state_conv, page_table, ev_norm_mix, ev_w_in, ev_gmlp_ln_g, ev_gmlp_ln_b, ev_w_spatial, ev_b_spatial, ev_w_out, ev_norm_ffn, ev_w_gate, ev_w_up, ev_w_down, od_norm_mix, od_w_in, od_gla_w_gate2, od_gla_b_gate2, od_gla_norm, od_delta_conv, od_delta_a_log, od_delta_dt_bias, od_delta_norm, od_w_out, od_norm_ffn, od_router, od_w_gate, od_w_up, od_w_down, norm_final):
    bp, tp, d = x_prompt.shape
    ev = _even_params(0, ev_norm_mix, ev_w_in, ev_gmlp_ln_g, ev_gmlp_ln_b, ev_w_spatial, ev_b_spatial, ev_w_out,
                      ev_norm_ffn, ev_w_gate, ev_w_up, ev_w_down)
    od = _odd_params(0, od_norm_mix, od_w_in, od_gla_w_gate2, od_gla_b_gate2, od_gla_norm, od_delta_conv,
                     od_delta_a_log, od_delta_dt_bias, od_delta_norm, od_w_out, od_norm_ffn, od_router,
                     od_w_gate, od_w_up, od_w_down)
    bs, ts, _ = x_sample.shape
    assert ts == 1 and cache_kv.shape[0] == 1 and state_gla.shape[0] == 1
    xp, kv_p, gv_p = _even_layer_prompt(x_prompt.reshape(bp * tp, d), ev, bp, tp)
    yp, gla_p, dl_p, cv_p = _odd_layer_prompt(xp, od, norm_final[None], bp, tp)
    ys, kv_s, gv_s, gla_s, dl_s, cv_s = _sample_step(
        x_sample.reshape(bs, d), cache_kv[0], page_table, state_gla[0], state_delta[0], state_conv[0],
        ev, od, norm_final[None])
    return (yp.reshape(bp, tp, d), ys.reshape(bs, ts, d),
            kv_p[None], kv_s.reshape(1, bs, ts, 2, H_A, D_A),
            gv_p[None], gv_s.reshape(1, bs, ts, D_B),
            gla_p[None], gla_s[None], dl_p[None], dl_s[None], cv_p[None], cv_s[None])
```
